```python
import jax, jax.numpy as jnp
from jax import lax
import numpy as np

D_MODEL = 1024
BATCH = 8
SEQ = 4096
DEPTH = 1

MEM_TOKENS = 256
MEM_HEADS = 4
MEM_HEAD_DIM = D_MODEL // MEM_HEADS
POOL_WIDTH = D_MODEL // 2
POOL_WINDOWS = (2, 4, 8, 16)
POOL_GROUPS = len(POOL_WINDOWS)
POOL_GROUP_DIM = POOL_WIDTH // POOL_GROUPS
ATT_HEADS = 8
ATT_HEAD_DIM = 64
ATT_WIDTH = ATT_HEADS * ATT_HEAD_DIM
KV_RANK = 128
IDX_HEADS = 8
IDX_HEAD_DIM = 64
TOPK_MAX = 256
Q_BLOCK = 128
MIX_WIDTH = POOL_WIDTH + ATT_WIDTH
IN_SPLITS = (POOL_WIDTH, ATT_WIDTH, KV_RANK, IDX_HEADS * IDX_HEAD_DIM, IDX_HEAD_DIM, IDX_HEADS)
IN_WIDTH = sum(IN_SPLITS)
N_EXPERTS = 32
TOP_K_EXPERTS = 4
D_FF = D_MODEL
SWIGLU_LIMIT = 7.0
SWIGLU_ALPHA = 1.702
MOE_BLOCK = 128
LN_EPS = 1e-5
RMS_EPS = 1e-6
DN_ALPHA = (2 * DEPTH) ** 0.25
DN_BETA = (8 * DEPTH) ** -0.25

kernel_name = 'hybrid_pool_dsa_moe_deepnorm'


def layer_norm(x, g, b):
    xf = x.astype(jnp.float32)
    mu = jnp.mean(xf, axis=-1, keepdims=True)
    var = jnp.mean(jnp.square(xf - mu), axis=-1, keepdims=True)
    y = (xf - mu) * lax.rsqrt(var + LN_EPS) * g.astype(jnp.float32) + b.astype(jnp.float32)
    return y.astype(x.dtype)


def rms_norm(x, g):
    xf = x.astype(jnp.float32)
    y = xf * lax.rsqrt(jnp.mean(jnp.square(xf), axis=-1, keepdims=True) + RMS_EPS) * g.astype(jnp.float32)
    return y.astype(x.dtype)


def pool_mixer(u, w_pool, pool_scale):
    B, L, _ = u.shape
    uf = u.astype(jnp.float32).reshape(B, L, POOL_GROUPS, POOL_GROUP_DIM)
    csum = jnp.concatenate([jnp.zeros_like(uf[:, :1]), jnp.cumsum(uf, axis=1)], axis=1)
    pos = jnp.arange(L)
    outs = []
    for g, w in enumerate(POOL_WINDOWS):
        lo = jnp.maximum(pos + 1 - w, 0)
        cnt = jnp.minimum(pos + 1, w).astype(jnp.float32)
        mean = (csum[:, 1:, g] - csum[:, lo, g]) / cnt[None, :, None]
        outs.append(mean - uf[:, :, g])
    d = jnp.stack(outs, axis=2).astype(u.dtype)
    y = jnp.einsum('blgc,gcd->blgd', d, w_pool).reshape(B, L, POOL_WIDTH)
    return y * pool_scale


def dsa_attention(q, c_kv, q_idx, k_idx, w_idx, w_uk, w_uv):
    B, L = q.shape[0], q.shape[1]
    top_k = min(TOPK_MAX, L // 4)
    nb = L // Q_BLOCK
    q_lat = jnp.einsum('blhd,hdr->blhr', q, w_uk)
    idx_scale = (IDX_HEAD_DIM ** -0.5) * (IDX_HEADS ** -0.5)
    att_scale = ATT_HEAD_DIM ** -0.5
    key_pos = jnp.arange(L)
    k_idx_f = k_idx.astype(jnp.float32)

    def to_blocks(a):
        return a.reshape((B, nb, Q_BLOCK) + a.shape[2:]).swapaxes(0, 1)

    def block(args):
        qi, wi, ql, start = args
        qpos = start + jnp.arange(Q_BLOCK)
        s = jnp.einsum('bqhd,bkd->bqhk', qi.astype(jnp.float32), k_idx_f)
        score = jnp.einsum('bqhk,bqh->bqk', jax.nn.relu(s), wi.astype(jnp.float32)) * idx_scale
        causal = key_pos[None, :] <= qpos[:, None]
        score = jnp.where(causal[None], score, -jnp.inf)
        _, sel = lax.top_k(score, top_k)
        valid = sel <= qpos[None, :, None]
        c_sel = jax.vmap(lambda c, i: c[i])(c_kv, sel)
        logits = jnp.einsum('bqhr,bqkr->bqhk', ql, c_sel).astype(jnp.float32) * att_scale
        logits = jnp.where(valid[:, :, None, :], logits, -jnp.inf)
        p = jax.nn.softmax(logits, axis=-1).astype(c_sel.dtype)
        return jnp.einsum('bqhk,bqkr->bqhr', p, c_sel)

    starts = jnp.arange(nb) * Q_BLOCK
    o_lat = lax.map(block, (to_blocks(q_idx), to_blocks(w_idx), to_blocks(q_lat), starts))
    o_lat = o_lat.swapaxes(0, 1).reshape(B, L, ATT_HEADS, KV_RANK)
    return jnp.einsum('blhr,hrd->blhd', o_lat, w_uv).reshape(B, L, ATT_WIDTH)


def memory_attention(h, mem, w_mq, w_mkv, w_mo):
    B, L, _ = h.shape
    M = mem.shape[1]
    q = (h @ w_mq).reshape(B, L, MEM_HEADS, MEM_HEAD_DIM)
    kv = (mem @ w_mkv).reshape(B, M, 2, MEM_HEADS, MEM_HEAD_DIM)
    k, v = kv[:, :, 0], kv[:, :, 1]
    logits = jnp.einsum('blhd,bmhd->bhlm', q, k).astype(jnp.float32) * (MEM_HEAD_DIM ** -0.5)
    p = jax.nn.softmax(logits, axis=-1).astype(v.dtype)
    o = jnp.einsum('bhlm,bmhd->blhd', p, v).reshape(B, L, D_MODEL)
    return o @ w_mo


def moe_ffn(h, w_router, b_router, w_gate_up, b_gate_up, w_down, b_down):
    B, L, D = h.shape
    hf = h.reshape(-1, D)
    T = hf.shape[0]
    logits = hf.astype(jnp.float32) @ w_router.astype(jnp.float32) + b_router.astype(jnp.float32)
    top_val, top_exp = lax.top_k(logits, TOP_K_EXPERTS)
    gates = jax.nn.softmax(top_val, axis=-1)
    M = T * TOP_K_EXPERTS
    flat_exp = top_exp.reshape(M)
    flat_tok = jnp.arange(M) // TOP_K_EXPERTS
    flat_gate = gates.reshape(M)
    order = jnp.argsort(flat_exp)
    sorted_exp = flat_exp[order]
    counts = jnp.zeros((N_EXPERTS,), jnp.int32).at[flat_exp].add(1)
    padded = (counts + MOE_BLOCK - 1) // MOE_BLOCK * MOE_BLOCK
    pad_end = jnp.cumsum(padded)
    pad_start = pad_end - padded
    start = jnp.cumsum(counts) - counts
    dest = pad_start[sorted_exp] + jnp.arange(M) - start[sorted_exp]
    P = M + N_EXPERTS * MOE_BLOCK
    n_blocks = P // MOE_BLOCK
    row_tok = jnp.zeros((P,), jnp.int32).at[dest].set(flat_tok[order])
    row_gate = jnp.zeros((P,), jnp.float32).at[dest].set(flat_gate[order])
    block_exp = jnp.minimum(jnp.searchsorted(pad_end, jnp.arange(n_blocks) * MOE_BLOCK, side='right'), N_EXPERTS - 1)

    def expert_block(args):
        tok, gate, e = args
        xb = hf[tok]
        gu = xb @ w_gate_up[e] + b_gate_up[e]
        g, u = jnp.split(gu, 2, axis=-1)
        g = jnp.minimum(g, SWIGLU_LIMIT)
        u = jnp.clip(u, -SWIGLU_LIMIT, SWIGLU_LIMIT)
        act = (u + 1.0) * g * jax.nn.sigmoid(SWIGLU_ALPHA * g)
        y = act @ w_down[e] + b_down[e]
        return y.astype(jnp.float32) * gate[:, None]

    y = lax.map(expert_block, (row_tok.reshape(n_blocks, MOE_BLOCK), row_gate.reshape(n_blocks, MOE_BLOCK), block_exp))
    out = jnp.zeros((T, D), jnp.float32).at[row_tok].add(y.reshape(P, D))
    return out.astype(h.dtype).reshape(B, L, D)


def setup_inputs(seed: int = 0) -> dict:
    key = jax.random.key(seed)
    ks = jax.random.split(key, 26)
    f32 = jnp.float32

    def nrm(k, shape, scale):
        return jax.random.normal(k, shape, f32) * scale

    def gain(k, shape):
        return 1.0 + 0.02 * jax.random.normal(k, shape, f32)

    Ly = DEPTH
    return {
        'x': nrm(ks[0], (BATCH, SEQ, D_MODEL), 1.0),
        'mem': nrm(ks[1], (BATCH, MEM_TOKENS, D_MODEL), 1.0),
        'w_in': nrm(ks[2], (Ly, D_MODEL, IN_WIDTH), D_MODEL ** -0.5),
        'w_pool': nrm(ks[3], (Ly, POOL_GROUPS, POOL_GROUP_DIM, POOL_GROUP_DIM), POOL_GROUP_DIM ** -0.5),
        'pool_scale': gain(ks[4], (Ly, POOL_WIDTH)),
        'idx_k_norm_g': gain(ks[5], (Ly, IDX_HEAD_DIM)),
        'idx_k_norm_b': nrm(ks[6], (Ly, IDX_HEAD_DIM), 0.02),
        'kv_norm_g': gain(ks[7], (Ly, KV_RANK)),
        'w_uk': nrm(ks[8], (Ly, ATT_HEADS, ATT_HEAD_DIM, KV_RANK), ATT_HEAD_DIM ** -0.5),
        'w_uv': nrm(ks[9], (Ly, ATT_HEADS, KV_RANK, ATT_HEAD_DIM), KV_RANK ** -0.5),
        'w_o': nrm(ks[10], (Ly, MIX_WIDTH, D_MODEL), DN_BETA * MIX_WIDTH ** -0.5),
        'ln1_g': gain(ks[11], (Ly, D_MODEL)),
        'ln1_b': nrm(ks[12], (Ly, D_MODEL), 0.02),
        'w_mq': nrm(ks[13], (Ly, D_MODEL, D_MODEL), D_MODEL ** -0.5),
        'w_mkv': nrm(ks[14], (Ly, D_MODEL, 2 * D_MODEL), D_MODEL ** -0.5),
        'w_mo': nrm(ks[15], (Ly, D_MODEL, D_MODEL), DN_BETA * D_MODEL ** -0.5),
        'ln2_g': gain(ks[16], (Ly, D_MODEL)),
        'ln2_b': nrm(ks[17], (Ly, D_MODEL), 0.02),
        'w_router': nrm(ks[18], (Ly, D_MODEL, N_EXPERTS), D_MODEL ** -0.5),
        'b_router': nrm(ks[19], (Ly, N_EXPERTS), 0.01),
        'w_gate_up': nrm(ks[20], (Ly, N_EXPERTS, D_MODEL, 2 * D_FF), D_MODEL ** -0.5),
        'b_gate_up': nrm(ks[21], (Ly, N_EXPERTS, 2 * D_FF), 0.01),
        'w_down': nrm(ks[22], (Ly, N_EXPERTS, D_FF, D_MODEL), DN_BETA * D_FF ** -0.5),
        'b_down': nrm(ks[23], (Ly, N_EXPERTS, D_MODEL), 0.01),
        'ln3_g': gain(ks[24], (Ly, D_MODEL)),
        'ln3_b': nrm(ks[25], (Ly, D_MODEL), 0.02),
    }


def reference(x, mem, w_in, w_pool, pool_scale, idx_k_norm_g, idx_k_norm_b, kv_norm_g, w_uk, w_uv, w_o,
              ln1_g, ln1_b, w_mq, w_mkv, w_mo, ln2_g, ln2_b, w_router, b_router, w_gate_up, b_gate_up,
              w_down, b_down, ln3_g, ln3_b):
    B, L, _ = x.shape
    split_points = np.cumsum(IN_SPLITS)[:-1].tolist()
    for l in range(DEPTH):
        proj = x @ w_in[l]
        u_pool, q, c_kv, q_idx, k_idx, w_idx = jnp.split(proj, split_points, axis=-1)
        q = q.reshape(B, L, ATT_HEADS, ATT_HEAD_DIM)
        c_kv = rms_norm(c_kv, kv_norm_g[l])
        q_idx = q_idx.reshape(B, L, IDX_HEADS, IDX_HEAD_DIM)
        k_idx = layer_norm(k_idx, idx_k_norm_g[l], idx_k_norm_b[l])
        y_pool = pool_mixer(u_pool, w_pool[l], pool_scale[l])
        y_att = dsa_attention(q, c_kv, q_idx, k_idx, w_idx, w_uk[l], w_uv[l])
        mix = jnp.concatenate([y_pool, y_att], axis=-1) @ w_o[l]
        x = layer_norm(DN_ALPHA * x + mix, ln1_g[l], ln1_b[l])
        x = layer_norm(DN_ALPHA * x + memory_attention(x, mem, w_mq[l], w_mkv[l], w_mo[l]), ln2_g[l], ln2_b[l])
        y_moe = moe_ffn(x, w_router[l], b_router[l], w_gate_up[l], b_gate_up[l], w_down[l], b_down[l])
        x = layer_norm(DN_ALPHA * x + y_moe, ln3_g[l], ln3_b[l])
    return x
```

```python
import functools

import jax
import jax.numpy as jnp
from jax import lax
from jax.experimental import pallas as pl
from jax.experimental.pallas import tpu as pltpu

F32 = jnp.float32
BF16 = jnp.bfloat16
I32 = jnp.int32

POOL_WINDOWS = (2, 4, 8, 16)
POOL_GROUP_DIM = 128
POOL_WIDTH = 512
ATT_HEADS = 8
ATT_HEAD_DIM = 64
ATT_WIDTH = 512
KV_RANK = 128
IDX_HEADS = 8
IDX_HEAD_DIM = 64
TOPK_MAX = 256
MEM_HEADS = 4
N_EXPERTS = 32
TOP_K_EXPERTS = 4
SWIGLU_LIMIT = 7.0
SWIGLU_ALPHA = 1.702
LN_EPS = 1e-5
RMS_EPS = 1e-6
DEPTH = 1
DN_ALPHA = (2 * DEPTH) ** 0.25

LANES = 128
MAX_POOL_WINDOW = 16
VMEM_LIMIT = 56 * 1024 * 1024

PROJ_ROWS = 512
DSA_QUERIES = 128
DSA_KEYS = 512
MIX_ROWS = 512
RANK_TOKENS = 512
DISPATCH_TOKENS = 1024
EXPERT_ROWS = 256
COMBINE_TOKENS = 256

NEG_BIG = -1e30
INT_MIN = -(2 ** 31)
NEG_INF_KEY = INT_MIN + 0x007FFFFF

_NT = (((1,), (1,)), ((), ()))


def _layer_norm(v, g, b):
    mu = jnp.mean(v, axis=-1, keepdims=True)
    d = v - mu
    var = jnp.mean(d * d, axis=-1, keepdims=True)
    return d * lax.rsqrt(var + LN_EPS) * g + b


_C_POOL = 0
_C_Q = 512
_C_CKV = 1024
_C_QIDX = 1152
_C_KIDX = 1664
_C_WIDX = 1792
_IN_PAD = 1920


def _inproj_kernel(x_ref, w_ref, wpool_ref, pscale_ref, kvg_ref, ikg_ref, ikb_ref, wuk_ref,
                   ypool_ref, qlat_ref, ckv_ref, qidx_ref, kidx_ref, widx_ref, ubuf):
    li = pl.program_id(1)
    tm = x_ref.shape[1]
    halo = MAX_POOL_WINDOW
    xb = x_ref[0].astype(BF16)

    u = jnp.dot(xb, w_ref[:, _C_POOL:_C_POOL + POOL_WIDTH], preferred_element_type=F32)

    @pl.when(li == 0)
    def _():
        ubuf[0:halo, :] = jnp.zeros((halo, POOL_WIDTH), F32)

    ubuf[halo:halo + tm, :] = u
    pos = li * tm + lax.broadcasted_iota(I32, (tm, 1), 0)
    for g, w in enumerate(POOL_WINDOWS):
        c0 = g * POOL_GROUP_DIM
        c1 = c0 + POOL_GROUP_DIM
        ug = ubuf[halo:halo + tm, c0:c1]
        s = ug
        for j in range(1, w):
            s = s + ubuf[halo - j:halo - j + tm, c0:c1]
        cnt = jnp.minimum(pos + 1, w).astype(F32)
        d = s / cnt - ug
        y = jnp.dot(d.astype(BF16), wpool_ref[g], preferred_element_type=F32) * pscale_ref[:, c0:c1]
        ypool_ref[0, :, c0:c1] = y.astype(BF16)
    ubuf[0:halo, :] = ubuf[tm:tm + halo, :]

    q = jnp.dot(xb, w_ref[:, _C_Q:_C_Q + ATT_WIDTH], preferred_element_type=F32)
    att_scale = ATT_HEAD_DIM ** -0.5
    for h in range(ATT_HEADS):
        qh = q[:, h * ATT_HEAD_DIM:(h + 1) * ATT_HEAD_DIM].astype(BF16)
        ql = jnp.dot(qh, wuk_ref[h], preferred_element_type=F32) * att_scale
        qlat_ref[0, h] = ql.astype(BF16)

    c = jnp.dot(xb, w_ref[:, _C_CKV:_C_CKV + KV_RANK], preferred_element_type=F32)
    c = c * lax.rsqrt(jnp.mean(c * c, axis=-1, keepdims=True) + RMS_EPS) * kvg_ref[...]
    ckv_ref[0] = c.astype(BF16)

    qi = jnp.dot(xb, w_ref[:, _C_QIDX:_C_QIDX + IDX_HEADS * IDX_HEAD_DIM], preferred_element_type=F32)
    for h in range(IDX_HEADS):
        qidx_ref[0, h] = qi[:, h * IDX_HEAD_DIM:(h + 1) * IDX_HEAD_DIM].astype(BF16)
    kw = jnp.dot(xb, w_ref[:, _C_KIDX:_C_KIDX + 2 * LANES], preferred_element_type=F32)
    kr = kw[:, 0:IDX_HEAD_DIM]
    kidx_ref[0] = _layer_norm(kr, ikg_ref[...], ikb_ref[...]).astype(BF16)
    widx_ref[0] = kw[:, LANES:2 * LANES]


def _inproj(x, w_in_pad, w_pool, pool_scale, kv_g, ik_g, ik_b, w_uk):
    B, L, D = x.shape
    tm = min(PROJ_ROWS, L)
    nl = L // tm
    full2 = lambda b, l: (0, 0)
    full3 = lambda b, l: (0, 0, 0)
    return pl.pallas_call(
        _inproj_kernel,
        grid=(B, nl),
        in_specs=[
            pl.BlockSpec((1, tm, D), lambda b, l: (b, l, 0)),
            pl.BlockSpec(w_in_pad.shape, full2),
            pl.BlockSpec(w_pool.shape, full3),
            pl.BlockSpec(pool_scale.shape, full2),
            pl.BlockSpec(kv_g.shape, full2),
            pl.BlockSpec(ik_g.shape, full2),
            pl.BlockSpec(ik_b.shape, full2),
            pl.BlockSpec(w_uk.shape, full3),
        ],
        out_specs=[
            pl.BlockSpec((1, tm, POOL_WIDTH), lambda b, l: (b, l, 0)),
            pl.BlockSpec((1, ATT_HEADS, tm, KV_RANK), lambda b, l: (b, 0, l, 0)),
            pl.BlockSpec((1, tm, KV_RANK), lambda b, l: (b, l, 0)),
            pl.BlockSpec((1, IDX_HEADS, tm, IDX_HEAD_DIM), lambda b, l: (b, 0, l, 0)),
            pl.BlockSpec((1, tm, IDX_HEAD_DIM), lambda b, l: (b, l, 0)),
            pl.BlockSpec((1, tm, LANES), lambda b, l: (b, l, 0)),
        ],
        out_shape=[
            jax.ShapeDtypeStruct((B, L, POOL_WIDTH), BF16),
            jax.ShapeDtypeStruct((B, ATT_HEADS, L, KV_RANK), BF16),
            jax.ShapeDtypeStruct((B, L, KV_RANK), BF16),
            jax.ShapeDtypeStruct((B, IDX_HEADS, L, IDX_HEAD_DIM), BF16),
            jax.ShapeDtypeStruct((B, L, IDX_HEAD_DIM), BF16),
            jax.ShapeDtypeStruct((B, L, LANES), F32),
        ],
        scratch_shapes=[pltpu.VMEM((MAX_POOL_WINDOW + tm, POOL_WIDTH), F32)],
        compiler_params=pltpu.CompilerParams(
            dimension_semantics=("arbitrary", "arbitrary"), vmem_limit_bytes=VMEM_LIMIT),
    )(x, w_in_pad, w_pool, pool_scale, kv_g, ik_g, ik_b, w_uk)


def _dsa_kernel(qidx_ref, widx_ref, qlat_ref, kidx_ref, ckv_ref, wuv_ref, o_ref,
                key_ref, wb_ref, p_ref, m_ref, l_ref, a_ref, acc_ref, tie_ref, *, top_k, idx_bits):
    qi = pl.program_id(1)
    H = ATT_HEADS
    qb = qidx_ref.shape[2]
    kc = wb_ref.shape[1]
    qstart = qi * qb
    n_ch = (qstart + qb + kc - 1) // kc
    idx_scale = (IDX_HEAD_DIM ** -0.5) * (IDX_HEADS ** -0.5)

    qpos = qstart + lax.broadcasted_iota(I32, (qb, 1), 0)
    lane_pos = lax.broadcasted_iota(I32, (1, kc), 1)

    qs = qidx_ref[0].reshape(H * qb, IDX_HEAD_DIM)
    for h in range(H):
        wb_ref[h * qb:(h + 1) * qb, :] = jnp.broadcast_to(widx_ref[0, :, h:h + 1], (qb, kc))

    def score_body(c, carry):
        off = pl.multiple_of(c * kc, kc)
        kk = kidx_ref[0, pl.ds(off, kc), :]
        s = lax.dot_general(qs, kk, _NT, preferred_element_type=F32)
        acc = None
        for h in range(H):
            term = jnp.maximum(s[h * qb:(h + 1) * qb, :], 0.0) * wb_ref[h * qb:(h + 1) * qb, :]
            acc = term if acc is None else acc + term
        score = acc * idx_scale
        score = jnp.where(off + lane_pos <= qpos, score, -jnp.inf)
        bits = pltpu.bitcast(score, I32)
        key_ref[c] = bits ^ ((bits >> 31) & 0x7FFFFFFF)
        return carry

    lax.fori_loop(0, n_ch, score_body, 0)

    def count(indicator):
        def body(c, cnt):
            off = pl.multiple_of(c * kc, kc)
            m = indicator(key_ref[c], off + lane_pos)
            part = m[:, 0:LANES]
            for j in range(1, kc // LANES):
                part = part + m[:, j * LANES:(j + 1) * LANES]
            return cnt + part
        cnt = lax.fori_loop(0, n_ch, body, jnp.zeros((qb, LANES), I32))
        return jnp.sum(cnt, axis=1, keepdims=True)

    def bit_body(i, t):
        cand = t + lax.shift_left(jnp.int32(1), 31 - i)
        n_ge = count(lambda k, _: jnp.where(k >= cand, 1, 0))
        return jnp.where(n_ge >= top_k, cand, t)

    thr = lax.fori_loop(0, 32, bit_body, jnp.full((qb, 1), INT_MIN, I32))

    n_gt = count(lambda k, _: jnp.where(k > thr, 1, 0))
    n_ge = count(lambda k, _: jnp.where(k >= thr, 1, 0))
    need = top_k - n_gt
    surplus = jnp.where(thr > NEG_INF_KEY, jnp.where(n_ge > top_k, 1, 0), 0)
    tie_ref[...] = jnp.broadcast_to(qpos, (qb, LANES))

    @pl.when(jnp.max(surplus) > 0)
    def _():
        def tie_body(i, m):
            cand = m + lax.shift_left(jnp.int32(1), idx_bits - 1 - i)
            n_before = count(lambda k, p: jnp.where(k == thr, jnp.where(p < cand, 1, 0), 0))
            return jnp.where(n_before < need, cand, m)
        last = lax.fori_loop(0, idx_bits, tie_body, jnp.zeros((qb, 1), I32))
        tie_ref[...] = jnp.broadcast_to(jnp.where(surplus > 0, jnp.minimum(last, qpos), qpos), (qb, LANES))

    tie_last = tie_ref[:, 0:1]

    ql = qlat_ref[0].reshape(H * qb, KV_RANK)
    m_ref[...] = jnp.full(m_ref.shape, NEG_BIG, F32)
    l_ref[...] = jnp.zeros(l_ref.shape, F32)
    acc_ref[...] = jnp.zeros(acc_ref.shape, F32)

    def att_body(c, carry):
        off = pl.multiple_of(c * kc, kc)
        ck = ckv_ref[0, pl.ds(off, kc), :]
        lg = lax.dot_general(ql, ck, _NT, preferred_element_type=F32)
        key = key_ref[c]
        kpos = off + lane_pos
        tie_bias = jnp.where(key == thr, jnp.where(kpos <= tie_last, 0.0, NEG_BIG), NEG_BIG)
        bias = jnp.where(key > thr, 0.0, tie_bias)
        for h in range(H):
            r = slice(h * qb, (h + 1) * qb)
            z = lg[r, :] + bias
            m_old = m_ref[r, :]
            m_new = jnp.maximum(m_old, jnp.max(z, axis=1, keepdims=True))
            p = jnp.exp(z - m_new[:, 0:1])
            alpha = jnp.exp(m_old - m_new)
            l_ref[r, :] = alpha * l_ref[r, :] + jnp.sum(p, axis=1, keepdims=True)
            m_ref[r, :] = m_new
            a_ref[r, :] = alpha
            p_ref[r, :] = p.astype(BF16)
        pv = jnp.dot(p_ref[...], ck, preferred_element_type=F32)
        acc_ref[...] = a_ref[...] * acc_ref[...] + pv
        return carry

    lax.fori_loop(0, n_ch, att_body, 0)

    o_lat = (acc_ref[...] / l_ref[...]).astype(BF16)
    y = None
    for h in range(H):
        t = jnp.dot(o_lat[h * qb:(h + 1) * qb, :], wuv_ref[h], preferred_element_type=F32)
        y = t if y is None else y + t
    o_ref[0] = y.astype(BF16)


def _dsa(qidx, widx, qlat, kidx, ckv, wuv_pad):
    B, H, L, _ = qidx.shape
    qb = min(DSA_QUERIES, L)
    kc = min(DSA_KEYS, L)
    top_k = min(TOPK_MAX, L // 4)
    idx_bits = max(1, (L - 1).bit_length())
    kern = functools.partial(_dsa_kernel, top_k=top_k, idx_bits=idx_bits)
    return pl.pallas_call(
        kern,
        grid=(B, L // qb),
        in_specs=[
            pl.BlockSpec((1, H, qb, IDX_HEAD_DIM), lambda b, q: (b, 0, q, 0)),
            pl.BlockSpec((1, qb, LANES), lambda b, q: (b, q, 0)),
            pl.BlockSpec((1, H, qb, KV_RANK), lambda b, q: (b, 0, q, 0)),
            pl.BlockSpec((1, L, IDX_HEAD_DIM), lambda b, q: (b, 0, 0)),
            pl.BlockSpec((1, L, KV_RANK), lambda b, q: (b, 0, 0)),
            pl.BlockSpec(wuv_pad.shape, lambda b, q: (0, 0, 0)),
        ],
        out_specs=pl.BlockSpec((1, qb, ATT_WIDTH), lambda b, q: (b, q, 0)),
        out_shape=jax.ShapeDtypeStruct((B, L, ATT_WIDTH), BF16),
        scratch_shapes=[
            pltpu.VMEM((L // kc, qb, kc), I32),
            pltpu.VMEM((H * qb, kc), F32),
            pltpu.VMEM((H * qb, kc), BF16),
            pltpu.VMEM((H * qb, LANES), F32),
            pltpu.VMEM((H * qb, LANES), F32),
            pltpu.VMEM((H * qb, LANES), F32),
            pltpu.VMEM((H * qb, KV_RANK), F32),
            pltpu.VMEM((qb, LANES), I32),
        ],
        compiler_params=pltpu.CompilerParams(
            dimension_semantics=("arbitrary", "arbitrary"), vmem_limit_bytes=VMEM_LIMIT),
    )(qidx, widx, qlat, kidx, ckv, wuv_pad)


def _memkv_kernel(mem_ref, w_ref, k_ref, v_ref):
    d = k_ref.shape[2]
    kv = jnp.dot(mem_ref[0].astype(BF16), w_ref[...], preferred_element_type=F32)
    k_ref[0] = kv[:, 0:d].astype(BF16)
    v_ref[0] = kv[:, d:2 * d].astype(BF16)


def _memkv(mem, w_mkv):
    B, M, D = mem.shape
    return pl.pallas_call(
        _memkv_kernel,
        grid=(B,),
        in_specs=[pl.BlockSpec((1, M, D), lambda b: (b, 0, 0)),
                  pl.BlockSpec(w_mkv.shape, lambda b: (0, 0))],
        out_specs=[pl.BlockSpec((1, M, D), lambda b: (b, 0, 0)),
                   pl.BlockSpec((1, M, D), lambda b: (b, 0, 0))],
        out_shape=[jax.ShapeDtypeStruct((B, M, D), BF16), jax.ShapeDtypeStruct((B, M, D), BF16)],
        compiler_params=pltpu.CompilerParams(
            dimension_semantics=("arbitrary",), vmem_limit_bytes=VMEM_LIMIT),
    )(mem, w_mkv)


def _split3(v):
    hi = v.astype(BF16)
    r1 = v - hi.astype(F32)
    mid = r1.astype(BF16)
    lo = (r1 - mid.astype(F32)).astype(BF16)
    return hi, mid, lo


def _mix_kernel(x_ref, yp_ref, ya_ref, wo_ref, g1_ref, b1_ref, km_ref, vm_ref, wq_ref, wmo_ref,
                g2_ref, b2_ref, wr_ref, br_ref, x2_ref, tope_ref, gate_ref):
    tm, d = x_ref.shape
    hd = d // MEM_HEADS
    pw = yp_ref.shape[1]
    mix = jnp.dot(yp_ref[...], wo_ref[0:pw, :], preferred_element_type=F32)
    mix = mix + jnp.dot(ya_ref[...], wo_ref[pw:, :], preferred_element_type=F32)
    x1 = _layer_norm(DN_ALPHA * x_ref[...] + mix, g1_ref[...], b1_ref[...])

    q = jnp.dot(x1.astype(BF16), wq_ref[...], preferred_element_type=F32).astype(BF16)
    scale = hd ** -0.5
    att = None
    for h in range(MEM_HEADS):
        c = slice(h * hd, (h + 1) * hd)
        lg = lax.dot_general(q[:, c], km_ref[0, :, c], _NT, preferred_element_type=F32) * scale
        p = jnp.exp(lg - jnp.max(lg, axis=-1, keepdims=True))
        p = p / jnp.sum(p, axis=-1, keepdims=True)
        oh = jnp.dot(p.astype(BF16), vm_ref[0, :, c], preferred_element_type=F32).astype(BF16)
        t = jnp.dot(oh, wmo_ref[c, :], preferred_element_type=F32)
        att = t if att is None else att + t
    x2 = _layer_norm(DN_ALPHA * x1 + att, g2_ref[...], b2_ref[...])
    x2_ref[...] = x2

    xs = _split3(x2)
    ws = _split3(wr_ref[...])
    lt = None
    for i, j in ((0, 0), (0, 1), (1, 0)):
        t = lax.dot_general(ws[j], xs[i], _NT, preferred_element_type=F32)
        lt = t if lt is None else lt + t
    lt = lt + br_ref[...]
    n_e = lt.shape[0]
    eidx = lax.broadcasted_iota(I32, lt.shape, 0)
    vals, idxs = [], []
    for _ in range(TOP_K_EXPERTS):
        mx = jnp.max(lt, axis=0, keepdims=True)
        ix = jnp.min(jnp.where(lt == mx, eidx, n_e), axis=0, keepdims=True)
        vals.append(mx)
        idxs.append(ix)
        lt = jnp.where(eidx == ix, -jnp.inf, lt)
    tope_ref[...] = jnp.concatenate(idxs, axis=0)
    ex = [jnp.exp(v - vals[0]) for v in vals]
    den = ex[0]
    for e_ in ex[1:]:
        den = den + e_
    gates = jnp.concatenate([e_ / den for e_ in ex] + [jnp.zeros((LANES - TOP_K_EXPERTS, tm), F32)], axis=0)
    gate_ref[...] = jnp.transpose(gates)


def _mix(x2d, ypool, yatt, w_o, g1, b1, k_mem, v_mem, w_mq, w_mo, g2, b2, w_rt, b_r, B, L):
    T, D = x2d.shape
    tm = min(MIX_ROWS, L)
    nl = L // tm
    M = k_mem.shape[1]
    row = lambda i: (i, 0)
    full = lambda i: (0, 0)
    return pl.pallas_call(
        _mix_kernel,
        grid=(T // tm,),
        in_specs=[
            pl.BlockSpec((tm, D), row),
            pl.BlockSpec((tm, ypool.shape[1]), row),
            pl.BlockSpec((tm, yatt.shape[1]), row),
            pl.BlockSpec(w_o.shape, full),
            pl.BlockSpec(g1.shape, full),
            pl.BlockSpec(b1.shape, full),
            pl.BlockSpec((1, M, D), lambda i: (i // nl, 0, 0)),
            pl.BlockSpec((1, M, D), lambda i: (i // nl, 0, 0)),
            pl.BlockSpec(w_mq.shape, full),
            pl.BlockSpec(w_mo.shape, full),
            pl.BlockSpec(g2.shape, full),
            pl.BlockSpec(b2.shape, full),
            pl.BlockSpec(w_rt.shape, full),
            pl.BlockSpec(b_r.shape, full),
        ],
        out_specs=[
            pl.BlockSpec((tm, D), row),
            pl.BlockSpec((TOP_K_EXPERTS, tm), lambda i: (0, i)),
            pl.BlockSpec((tm, LANES), row),
        ],
        out_shape=[
            jax.ShapeDtypeStruct((T, D), F32),
            jax.ShapeDtypeStruct((TOP_K_EXPERTS, T), I32),
            jax.ShapeDtypeStruct((T, LANES), F32),
        ],
        compiler_params=pltpu.CompilerParams(
            dimension_semantics=("arbitrary",), vmem_limit_bytes=VMEM_LIMIT),
    )(x2d, ypool, yatt, w_o, g1, b1, k_mem, v_mem, w_mq, w_mo, g2, b2, w_rt, b_r)


def _rank_kernel(tope_ref, rank_ref, cnt_ref, carry_ref):
    i = pl.program_id(0)
    tr = tope_ref.shape[1]

    @pl.when(i == 0)
    def _():
        carry_ref[...] = jnp.zeros(carry_ref.shape, F32)

    eidx = lax.broadcasted_iota(I32, (N_EXPERTS, tr), 0)
    onehot = jnp.zeros((N_EXPERTS, tr), F32)
    for k in range(TOP_K_EXPERTS):
        onehot = onehot + jnp.where(eidx == tope_ref[k:k + 1, :], 1.0, 0.0)
    before = jnp.where(lax.broadcasted_iota(I32, (tr, tr), 0) < lax.broadcasted_iota(I32, (tr, tr), 1), 1.0, 0.0)
    excl = jnp.dot(onehot.astype(BF16), before.astype(BF16), preferred_element_type=F32)
    rank_full = excl + carry_ref[:, 0:1]
    rows = []
    for k in range(TOP_K_EXPERTS):
        rows.append(jnp.sum(jnp.where(eidx == tope_ref[k:k + 1, :], rank_full, 0.0), axis=0, keepdims=True))
    rank_ref[...] = jnp.concatenate(rows, axis=0).astype(I32)
    carry_ref[...] = carry_ref[...] + jnp.sum(onehot, axis=1, keepdims=True)
    cnt_ref[...] = carry_ref[...].astype(I32)


def _rank(tope):
    K, T = tope.shape
    tr = min(RANK_TOKENS, T)
    return pl.pallas_call(
        _rank_kernel,
        grid=(T // tr,),
        in_specs=[pl.BlockSpec((K, tr), lambda i: (0, i))],
        out_specs=[pl.BlockSpec((K, tr), lambda i: (0, i)),
                   pl.BlockSpec((N_EXPERTS, LANES), lambda i: (0, 0))],
        out_shape=[jax.ShapeDtypeStruct((K, T), I32), jax.ShapeDtypeStruct((N_EXPERTS, LANES), I32)],
        scratch_shapes=[pltpu.VMEM((N_EXPERTS, LANES), F32)],
        compiler_params=pltpu.CompilerParams(dimension_semantics=("arbitrary",)),
    )(tope)


def _dispatch_kernel(tope_ref, rank_ref, start_ref, cnt_ref, padded_ref, x_hbm, xs_hbm, zrow, sem, zsem):
    i = pl.program_id(0)
    td = tope_ref.shape[1]
    base = i * td

    def row_copy(t, dst):
        return pltpu.make_async_copy(x_hbm.at[pl.ds(t, 1), :], xs_hbm.at[pl.ds(dst, 1), :], sem)

    def zero_copy(dst):
        return pltpu.make_async_copy(zrow, xs_hbm.at[pl.ds(dst, 1), :], zsem)

    @pl.when(i == 0)
    def _():
        zrow[...] = jnp.zeros(zrow.shape, F32)
        for e in range(N_EXPERTS):
            first = start_ref[e] + cnt_ref[e]
            n_pad = padded_ref[e] - cnt_ref[e]

            def zstart(r, c):
                zero_copy(first + r).start()
                return c
            lax.fori_loop(0, n_pad, zstart, 0)

            def zwait(r, c):
                zero_copy(first + r).wait()
                return c
            lax.fori_loop(0, n_pad, zwait, 0)

    def issue(j, c):
        for k in range(TOP_K_EXPERTS):
            row_copy(base + j, start_ref[tope_ref[k, j]] + rank_ref[k, j]).start()
        return c
    lax.fori_loop(0, td, issue, 0)

    def drain(j, c):
        for k in range(TOP_K_EXPERTS):
            row_copy(base + j, start_ref[tope_ref[k, j]] + rank_ref[k, j]).wait()
        return c
    lax.fori_loop(0, td, drain, 0)


def _dispatch(tope, rank, pad_start, counts, padded, x2, n_rows):
    K, T = tope.shape
    D = x2.shape[1]
    td = min(DISPATCH_TOKENS, T)
    smem_tok = pl.BlockSpec((K, td), lambda i: (0, i), memory_space=pltpu.SMEM)
    smem_full = pl.BlockSpec(memory_space=pltpu.SMEM)
    return pl.pallas_call(
        _dispatch_kernel,
        grid=(T // td,),
        in_specs=[smem_tok, smem_tok, smem_full, smem_full, smem_full,
                  pl.BlockSpec(memory_space=pl.ANY)],
        out_specs=pl.BlockSpec(memory_space=pl.ANY),
        out_shape=jax.ShapeDtypeStruct((n_rows, D), F32),
        scratch_shapes=[pltpu.VMEM((1, D), F32), pltpu.SemaphoreType.DMA(()), pltpu.SemaphoreType.DMA(())],
        compiler_params=pltpu.CompilerParams(dimension_semantics=("arbitrary",)),
    )(tope, rank, pad_start, counts, padded, x2)


def _expert_kernel(bexp_ref, nused_ref, xs_ref, wgu_ref, bgu_ref, wd_ref, bd_ref, y_ref, wgu_bf, wd_bf):
    j = pl.program_id(0)
    d_ff = wd_ref.shape[1]

    @pl.when(j < nused_ref[0])
    def _():
        prev = bexp_ref[jnp.maximum(j - 1, 0)]

        @pl.when(jnp.logical_or(j == 0, bexp_ref[j] != prev))
        def _():
            wgu_bf[...] = wgu_ref[0].astype(BF16)
            wd_bf[...] = wd_ref[0].astype(BF16)

        xb = xs_ref[...].astype(BF16)
        gu = jnp.dot(xb, wgu_bf[...], preferred_element_type=F32) + bgu_ref[0]
        g = jnp.minimum(gu[:, 0:d_ff], SWIGLU_LIMIT)
        u = jnp.clip(gu[:, d_ff:2 * d_ff], -SWIGLU_LIMIT, SWIGLU_LIMIT)
        act = (u + 1.0) * g * (1.0 / (1.0 + jnp.exp(-SWIGLU_ALPHA * g)))
        y_ref[...] = jnp.dot(act.astype(BF16), wd_bf[...], preferred_element_type=F32) + bd_ref[0]


def _experts(block_exp, n_used, xs, w_gate_up, b_gate_up, w_down, b_down):
    P, D = xs.shape
    blk = EXPERT_ROWS
    nblk = P // blk
    E, _, F2 = w_gate_up.shape
    d_ff = w_down.shape[1]
    row = lambda j, be, nu: (jnp.minimum(j, nu[0] - 1), 0)
    exp3 = lambda j, be, nu: (be[j], 0, 0)
    grid_spec = pltpu.PrefetchScalarGridSpec(
        num_scalar_prefetch=2,
        grid=(nblk,),
        in_specs=[
            pl.BlockSpec((blk, D), row),
            pl.BlockSpec((1, D, F2), exp3),
            pl.BlockSpec((1, 1, F2), exp3),
            pl.BlockSpec((1, d_ff, D), exp3),
            pl.BlockSpec((1, 1, D), exp3),
        ],
        out_specs=pl.BlockSpec((blk, D), row),
        scratch_shapes=[pltpu.VMEM((D, F2), BF16), pltpu.VMEM((d_ff, D), BF16)],
    )
    return pl.pallas_call(
        _expert_kernel,
        grid_spec=grid_spec,
        out_shape=jax.ShapeDtypeStruct((P, D), F32),
        compiler_params=pltpu.CompilerParams(
            dimension_semantics=("arbitrary",), vmem_limit_bytes=VMEM_LIMIT),
    )(block_exp, n_used, xs, w_gate_up, b_gate_up.reshape(E, 1, F2), w_down, b_down.reshape(E, 1, D))


def _combine_kernel(tope_ref, rank_ref, start_ref, x2_ref, gate_ref, g3_ref, b3_ref, y_hbm, o_ref, ybuf, sem):
    tc = x2_ref.shape[0]

    def row_copy(j, k):
        src = start_ref[tope_ref[k, j]] + rank_ref[k, j]
        return pltpu.make_async_copy(y_hbm.at[pl.ds(src, 1), :], ybuf.at[k, pl.ds(j, 1), :], sem)

    def issue(j, c):
        for k in range(TOP_K_EXPERTS):
            row_copy(j, k).start()
        return c
    lax.fori_loop(0, tc, issue, 0)

    def drain(j, c):
        for k in range(TOP_K_EXPERTS):
            row_copy(j, k).wait()
        return c
    lax.fori_loop(0, tc, drain, 0)

    moe = None
    for k in range(TOP_K_EXPERTS):
        t = ybuf[k] * gate_ref[:, k:k + 1]
        moe = t if moe is None else moe + t
    o_ref[...] = _layer_norm(DN_ALPHA * x2_ref[...] + moe, g3_ref[...], b3_ref[...])


def _combine(tope, rank, pad_start, x2, gates, g3, b3, y):
    K, T = tope.shape
    D = x2.shape[1]
    tc = min(COMBINE_TOKENS, T)
    smem_tok = pl.BlockSpec((K, tc), lambda i: (0, i), memory_space=pltpu.SMEM)
    row = lambda i: (i, 0)
    full = lambda i: (0, 0)
    return pl.pallas_call(
        _combine_kernel,
        grid=(T // tc,),
        in_specs=[smem_tok, smem_tok, pl.BlockSpec(memory_space=pltpu.SMEM),
                  pl.BlockSpec((tc, D), row), pl.BlockSpec((tc, LANES), row),
                  pl.BlockSpec(g3.shape, full), pl.BlockSpec(b3.shape, full),
                  pl.BlockSpec(memory_space=pl.ANY)],
        out_specs=pl.BlockSpec((tc, D), row),
        out_shape=jax.ShapeDtypeStruct((T, D), F32),
        scratch_shapes=[pltpu.VMEM((K, tc, D), F32), pltpu.SemaphoreType.DMA(())],
        compiler_params=pltpu.CompilerParams(
            dimension_semantics=("arbitrary",), vmem_limit_bytes=VMEM_LIMIT),
    )(tope, rank, pad_start, x2, gates, g3, b3, y)


def _pad_cols(w, width):
    return jnp.pad(w, ((0, 0), (0, width - w.shape[1])))


def _layer(x, mem, w_in, w_pool, pool_scale, ik_g, ik_b, kv_g, w_uk, w_uv, w_o, ln1_g, ln1_b,
           w_mq, w_mkv, w_mo, ln2_g, ln2_b, w_router, b_router, w_gate_up, b_gate_up, w_down, b_down,
           ln3_g, ln3_b):
    B, L, D = x.shape
    T = B * L
    row = lambda v: v.reshape(1, -1)

    o = 0
    pieces = []
    for width in (POOL_WIDTH, ATT_WIDTH, KV_RANK, IDX_HEADS * IDX_HEAD_DIM, IDX_HEAD_DIM, IDX_HEADS):
        pieces.append(w_in[:, o:o + width])
        o += width
    pieces[4] = _pad_cols(pieces[4], LANES)
    pieces[5] = _pad_cols(pieces[5], LANES)
    w_in_pad = jnp.concatenate(pieces, axis=1).astype(BF16)
    wuv_pad = jnp.zeros((ATT_HEADS, KV_RANK, ATT_WIDTH), F32)
    for h in range(ATT_HEADS):
        wuv_pad = wuv_pad.at[h, :, h * ATT_HEAD_DIM:(h + 1) * ATT_HEAD_DIM].set(w_uv[h])

    ypool, qlat, ckv, qidx, kidx, widx = _inproj(
        x, w_in_pad, w_pool.astype(BF16), row(pool_scale), row(kv_g), row(ik_g), row(ik_b), w_uk.astype(BF16))
    yatt = _dsa(qidx, widx, qlat, kidx, ckv, wuv_pad.astype(BF16))
    k_mem, v_mem = _memkv(mem, w_mkv.astype(BF16))
    x2, tope, gates = _mix(
        x.reshape(T, D), ypool.reshape(T, -1), yatt.reshape(T, -1), w_o.astype(BF16), row(ln1_g), row(ln1_b),
        k_mem, v_mem, w_mq.astype(BF16), w_mo.astype(BF16), row(ln2_g), row(ln2_b),
        jnp.transpose(w_router), b_router.reshape(-1, 1), B, L)

    rank, cnt = _rank(tope)
    counts = cnt[:, 0]
    blk = EXPERT_ROWS
    padded = (counts + blk - 1) // blk * blk
    pad_end = jnp.cumsum(padded)
    pad_start = pad_end - padded
    n_rows = T * TOP_K_EXPERTS + N_EXPERTS * blk
    nblk = n_rows // blk
    n_used = (pad_end[-1] // blk).astype(I32)
    blk_first = jnp.minimum(jnp.arange(nblk, dtype=I32), n_used - 1) * blk
    block_exp = jnp.minimum(jnp.searchsorted(pad_end, blk_first, side='right'), N_EXPERTS - 1).astype(I32)

    xs = _dispatch(tope, rank, pad_start.astype(I32), counts, padded.astype(I32), x2, n_rows)
    y = _experts(block_exp, n_used.reshape(1), xs, w_gate_up, b_gate_up, w_down, b_down)
    out = _combine(tope, rank, pad_start.astype(I32), x2, gates, row(ln3_g), row(ln3_b), y)
    return out.reshape(B, L, D)


def kernel(x, mem, w_in, w_pool, pool_scale, idx_k_norm_g, idx_k_norm_b, kv_norm_g, w_uk, w_uv, w_o, ln1_g, ln1_b, w_mq, w_mkv, w_mo, ln2_g, ln2_b, w_router, b_router, w_gate_up, b_gate_up, w_down, b_down, ln3_g, ln3_b):
    assert w_in.shape[0] == DEPTH
    return _layer(x, mem, w_in[0], w_pool[0], pool_scale[0], idx_k_norm_g[0], idx_k_norm_b[0], kv_norm_g[0],
                  w_uk[0], w_uv[0], w_o[0], ln1_g[0], ln1_b[0], w_mq[0], w_mkv[0], w_mo[0], ln2_g[0], ln2_b[0],
                  w_router[0], b_router[0], w_gate_up[0], b_gate_up[0], w_down[0], b_down[0], ln3_g[0], ln3_b[0])
```

```python
import functools

import jax
import jax.numpy as jnp
from jax import lax
from jax.experimental import pallas as pl
from jax.experimental.pallas import tpu as pltpu

F32 = jnp.float32
BF16 = jnp.bfloat16
I32 = jnp.int32

POOL_WINDOWS = (2, 4, 8, 16)
POOL_GROUP_DIM = 128
POOL_WIDTH = 512
ATT_HEADS = 8
ATT_HEAD_DIM = 64
ATT_WIDTH = 512
KV_RANK = 128
IDX_HEADS = 8
IDX_HEAD_DIM = 64
TOPK_MAX = 256
MEM_HEADS = 4
N_EXPERTS = 32
TOP_K_EXPERTS = 4
SWIGLU_LIMIT = 7.0
SWIGLU_ALPHA = 1.702
LN_EPS = 1e-5
RMS_EPS = 1e-6
DEPTH = 1
DN_ALPHA = (2 * DEPTH) ** 0.25

LANES = 128
MAX_POOL_WINDOW = 16
VMEM_LIMIT = 56 * 1024 * 1024

PROJ_ROWS = 512
DSA_QUERIES = 128
DSA_KEYS = 512
MIX_ROWS = 512
RANK_TOKENS = 512
DISPATCH_TOKENS = 512
EXPERT_ROWS = 256
COMBINE_TOKENS = 256

NEG_BIG = -1e30
INT_MIN = -(2 ** 31)
NEG_INF_KEY = INT_MIN + 0x007FFFFF

_NT = (((1,), (1,)), ((), ()))


def _layer_norm(v, g, b):
    mu = jnp.mean(v, axis=-1, keepdims=True)
    d = v - mu
    var = jnp.mean(d * d, axis=-1, keepdims=True)
    return d * lax.rsqrt(var + LN_EPS) * g + b


_C_POOL = 0
_C_Q = 512
_C_CKV = 1024
_C_QIDX = 1152
_C_KIDX = 1664
_C_WIDX = 1792
_IN_PAD = 1920


def _inproj_kernel(x_ref, w_ref, wpool_ref, pscale_ref, kvg_ref, ikg_ref, ikb_ref, wuk_ref,
                   ypool_ref, qlat_ref, ckv_ref, qidx_ref, kidx_ref, widx_ref, ubuf):
    li = pl.program_id(1)
    tm = x_ref.shape[1]
    halo = MAX_POOL_WINDOW
    xb = x_ref[0].astype(BF16)

    u = jnp.dot(xb, w_ref[:, _C_POOL:_C_POOL + POOL_WIDTH], preferred_element_type=F32)

    @pl.when(li == 0)
    def _():
        ubuf[0:halo, :] = jnp.zeros((halo, POOL_WIDTH), F32)

    ubuf[halo:halo + tm, :] = u
    pos = li * tm + lax.broadcasted_iota(I32, (tm, 1), 0)
    for g, w in enumerate(POOL_WINDOWS):
        c0 = g * POOL_GROUP_DIM
        c1 = c0 + POOL_GROUP_DIM
        ug = ubuf[halo:halo + tm, c0:c1]
        s = ug
        for j in range(1, w):
            s = s + ubuf[halo - j:halo - j + tm, c0:c1]
        cnt = jnp.minimum(pos + 1, w).astype(F32)
        d = s / cnt - ug
        y = jnp.dot(d.astype(BF16), wpool_ref[g], preferred_element_type=F32) * pscale_ref[:, c0:c1]
        ypool_ref[0, :, c0:c1] = y.astype(BF16)
    ubuf[0:halo, :] = ubuf[tm:tm + halo, :]

    q = jnp.dot(xb, w_ref[:, _C_Q:_C_Q + ATT_WIDTH], preferred_element_type=F32)
    att_scale = ATT_HEAD_DIM ** -0.5
    for h in range(ATT_HEADS):
        qh = q[:, h * ATT_HEAD_DIM:(h + 1) * ATT_HEAD_DIM].astype(BF16)
        ql = jnp.dot(qh, wuk_ref[h], preferred_element_type=F32) * att_scale
        qlat_ref[0, h] = ql.astype(BF16)

    c = jnp.dot(xb, w_ref[:, _C_CKV:_C_CKV + KV_RANK], preferred_element_type=F32)
    c = c * lax.rsqrt(jnp.mean(c * c, axis=-1, keepdims=True) + RMS_EPS) * kvg_ref[...]
    ckv_ref[0] = c.astype(BF16)

    qi = jnp.dot(xb, w_ref[:, _C_QIDX:_C_QIDX + IDX_HEADS * IDX_HEAD_DIM], preferred_element_type=F32)
    for h in range(IDX_HEADS):
        qidx_ref[0, h] = qi[:, h * IDX_HEAD_DIM:(h + 1) * IDX_HEAD_DIM].astype(BF16)
    kw = jnp.dot(xb, w_ref[:, _C_KIDX:_C_KIDX + 2 * LANES], preferred_element_type=F32)
    kr = kw[:, 0:IDX_HEAD_DIM]
    kidx_ref[0] = _layer_norm(kr, ikg_ref[...], ikb_ref[...]).astype(BF16)
    widx_ref[0] = kw[:, LANES:2 * LANES]


def _inproj(x, w_in_pad, w_pool, pool_scale, kv_g, ik_g, ik_b, w_uk):
    B, L, D = x.shape
    tm = min(PROJ_ROWS, L)
    nl = L // tm
    full2 = lambda b, l: (0, 0)
    full3 = lambda b, l: (0, 0, 0)
    return pl.pallas_call(
        _inproj_kernel,
        grid=(B, nl),
        in_specs=[
            pl.BlockSpec((1, tm, D), lambda b, l: (b, l, 0)),
            pl.BlockSpec(w_in_pad.shape, full2),
            pl.BlockSpec(w_pool.shape, full3),
            pl.BlockSpec(pool_scale.shape, full2),
            pl.BlockSpec(kv_g.shape, full2),
            pl.BlockSpec(ik_g.shape, full2),
            pl.BlockSpec(ik_b.shape, full2),
            pl.BlockSpec(w_uk.shape, full3),
        ],
        out_specs=[
            pl.BlockSpec((1, tm, POOL_WIDTH), lambda b, l: (b, l, 0)),
            pl.BlockSpec((1, ATT_HEADS, tm, KV_RANK), lambda b, l: (b, 0, l, 0)),
            pl.BlockSpec((1, tm, KV_RANK), lambda b, l: (b, l, 0)),
            pl.BlockSpec((1, IDX_HEADS, tm, IDX_HEAD_DIM), lambda b, l: (b, 0, l, 0)),
            pl.BlockSpec((1, tm, IDX_HEAD_DIM), lambda b, l: (b, l, 0)),
            pl.BlockSpec((1, tm, LANES), lambda b, l: (b, l, 0)),
        ],
        out_shape=[
            jax.ShapeDtypeStruct((B, L, POOL_WIDTH), BF16),
            jax.ShapeDtypeStruct((B, ATT_HEADS, L, KV_RANK), BF16),
            jax.ShapeDtypeStruct((B, L, KV_RANK), BF16),
            jax.ShapeDtypeStruct((B, IDX_HEADS, L, IDX_HEAD_DIM), BF16),
            jax.ShapeDtypeStruct((B, L, IDX_HEAD_DIM), BF16),
            jax.ShapeDtypeStruct((B, L, LANES), F32),
        ],
        scratch_shapes=[pltpu.VMEM((MAX_POOL_WINDOW + tm, POOL_WIDTH), F32)],
        compiler_params=pltpu.CompilerParams(
            dimension_semantics=("arbitrary", "arbitrary"), vmem_limit_bytes=VMEM_LIMIT),
    )(x, w_in_pad, w_pool, pool_scale, kv_g, ik_g, ik_b, w_uk)


def _dsa_kernel(qidx_ref, widx_ref, qlat_ref, kidx_ref, ckv_ref, wuv_ref, o_ref,
                key_ref, wb_ref, p_ref, m_ref, l_ref, a_ref, acc_ref, tie_ref, *, top_k, idx_bits):
    qi = pl.program_id(1)
    H = ATT_HEADS
    qb = qidx_ref.shape[2]
    kc = wb_ref.shape[1]
    qstart = qi * qb
    n_ch = (qstart + qb + kc - 1) // kc
    idx_scale = (IDX_HEAD_DIM ** -0.5) * (IDX_HEADS ** -0.5)

    qpos = qstart + lax.broadcasted_iota(I32, (qb, 1), 0)
    lane_pos = lax.broadcasted_iota(I32, (1, kc), 1)

    qs = qidx_ref[0].reshape(H * qb, IDX_HEAD_DIM)
    for h in range(H):
        wb_ref[h * qb:(h + 1) * qb, :] = jnp.broadcast_to(widx_ref[0, :, h:h + 1], (qb, kc))

    def score_body(c, carry):
        off = pl.multiple_of(c * kc, kc)
        kk = kidx_ref[0, pl.ds(off, kc), :]
        s = lax.dot_general(qs, kk, _NT, preferred_element_type=F32)
        acc = None
        for h in range(H):
            term = jnp.maximum(s[h * qb:(h + 1) * qb, :], 0.0) * wb_ref[h * qb:(h + 1) * qb, :]
            acc = term if acc is None else acc + term
        score = acc * idx_scale
        score = jnp.where(off + lane_pos <= qpos, score, -jnp.inf)
        bits = pltpu.bitcast(score, I32)
        key_ref[c] = bits ^ ((bits >> 31) & 0x7FFFFFFF)
        return carry

    lax.fori_loop(0, n_ch, score_body, 0)

    def count(indicator):
        def body(c, cnt):
            off = pl.multiple_of(c * kc, kc)
            m = indicator(key_ref[c], off + lane_pos)
            part = m[:, 0:LANES]
            for j in range(1, kc // LANES):
                part = part + m[:, j * LANES:(j + 1) * LANES]
            return cnt + part
        cnt = lax.fori_loop(0, n_ch, body, jnp.zeros((qb, LANES), I32))
        return jnp.sum(cnt, axis=1, keepdims=True)

    def bit_body(i, t):
        cand = t + lax.shift_left(jnp.int32(1), 31 - i)
        n_ge = count(lambda k, _: jnp.where(k >= cand, 1, 0))
        return jnp.where(n_ge >= top_k, cand, t)

    thr = lax.fori_loop(0, 32, bit_body, jnp.full((qb, 1), INT_MIN, I32))

    n_gt = count(lambda k, _: jnp.where(k > thr, 1, 0))
    n_ge = count(lambda k, _: jnp.where(k >= thr, 1, 0))
    need = top_k - n_gt
    surplus = jnp.where(thr > NEG_INF_KEY, jnp.where(n_ge > top_k, 1, 0), 0)
    tie_ref[...] = jnp.broadcast_to(qpos, (qb, LANES))

    @pl.when(jnp.max(surplus) > 0)
    def _():
        def tie_body(i, m):
            cand = m + lax.shift_left(jnp.int32(1), idx_bits - 1 - i)
            n_before = count(lambda k, p: jnp.where(k == thr, jnp.where(p < cand, 1, 0), 0))
            return jnp.where(n_before < need, cand, m)
        last = lax.fori_loop(0, idx_bits, tie_body, jnp.zeros((qb, 1), I32))
        tie_ref[...] = jnp.broadcast_to(jnp.where(surplus > 0, jnp.minimum(last, qpos), qpos), (qb, LANES))

    tie_last = tie_ref[:, 0:1]

    ql = qlat_ref[0].reshape(H * qb, KV_RANK)
    m_ref[...] = jnp.full(m_ref.shape, NEG_BIG, F32)
    l_ref[...] = jnp.zeros(l_ref.shape, F32)
    acc_ref[...] = jnp.zeros(acc_ref.shape, F32)

    def att_body(c, carry):
        off = pl.multiple_of(c * kc, kc)
        ck = ckv_ref[0, pl.ds(off, kc), :]
        lg = lax.dot_general(ql, ck, _NT, preferred_element_type=F32)
        key = key_ref[c]
        kpos = off + lane_pos
        tie_bias = jnp.where(key == thr, jnp.where(kpos <= tie_last, 0.0, NEG_BIG), NEG_BIG)
        bias = jnp.where(key > thr, 0.0, tie_bias)
        for h in range(H):
            r = slice(h * qb, (h + 1) * qb)
            z = lg[r, :] + bias
            m_old = m_ref[r, :]
            m_new = jnp.maximum(m_old, jnp.max(z, axis=1, keepdims=True))
            p = jnp.exp(z - m_new[:, 0:1])
            alpha = jnp.exp(m_old - m_new)
            l_ref[r, :] = alpha * l_ref[r, :] + jnp.sum(p, axis=1, keepdims=True)
            m_ref[r, :] = m_new
            a_ref[r, :] = alpha
            p_ref[r, :] = p.astype(BF16)
        pv = jnp.dot(p_ref[...], ck, preferred_element_type=F32)
        acc_ref[...] = a_ref[...] * acc_ref[...] + pv
        return carry

    lax.fori_loop(0, n_ch, att_body, 0)

    o_lat = (acc_ref[...] / l_ref[...]).astype(BF16)
    y = None
    for h in range(H):
        t = jnp.dot(o_lat[h * qb:(h + 1) * qb, :], wuv_ref[h], preferred_element_type=F32)
        y = t if y is None else y + t
    o_ref[0] = y.astype(BF16)


def _dsa(qidx, widx, qlat, kidx, ckv, wuv_pad):
    B, H, L, _ = qidx.shape
    qb = min(DSA_QUERIES, L)
    kc = min(DSA_KEYS, L)
    top_k = min(TOPK_MAX, L // 4)
    idx_bits = max(1, (L - 1).bit_length())
    kern = functools.partial(_dsa_kernel, top_k=top_k, idx_bits=idx_bits)
    return pl.pallas_call(
        kern,
        grid=(B, L // qb),
        in_specs=[
            pl.BlockSpec((1, H, qb, IDX_HEAD_DIM), lambda b, q: (b, 0, q, 0)),
            pl.BlockSpec((1, qb, LANES), lambda b, q: (b, q, 0)),
            pl.BlockSpec((1, H, qb, KV_RANK), lambda b, q: (b, 0, q, 0)),
            pl.BlockSpec((1, L, IDX_HEAD_DIM), lambda b, q: (b, 0, 0)),
            pl.BlockSpec((1, L, KV_RANK), lambda b, q: (b, 0, 0)),
            pl.BlockSpec(wuv_pad.shape, lambda b, q: (0, 0, 0)),
        ],
        out_specs=pl.BlockSpec((1, qb, ATT_WIDTH), lambda b, q: (b, q, 0)),
        out_shape=jax.ShapeDtypeStruct((B, L, ATT_WIDTH), BF16),
        scratch_shapes=[
            pltpu.VMEM((L // kc, qb, kc), I32),
            pltpu.VMEM((H * qb, kc), F32),
            pltpu.VMEM((H * qb, kc), BF16),
            pltpu.VMEM((H * qb, LANES), F32),
            pltpu.VMEM((H * qb, LANES), F32),
            pltpu.VMEM((H * qb, LANES), F32),
            pltpu.VMEM((H * qb, KV_RANK), F32),
            pltpu.VMEM((qb, LANES), I32),
        ],
        compiler_params=pltpu.CompilerParams(
            dimension_semantics=("arbitrary", "arbitrary"), vmem_limit_bytes=VMEM_LIMIT),
    )(qidx, widx, qlat, kidx, ckv, wuv_pad)


def _memkv_kernel(mem_ref, w_ref, k_ref, v_ref):
    d = k_ref.shape[2]
    kv = jnp.dot(mem_ref[0].astype(BF16), w_ref[...], preferred_element_type=F32)
    k_ref[0] = kv[:, 0:d].astype(BF16)
    v_ref[0] = kv[:, d:2 * d].astype(BF16)


def _memkv(mem, w_mkv):
    B, M, D = mem.shape
    return pl.pallas_call(
        _memkv_kernel,
        grid=(B,),
        in_specs=[pl.BlockSpec((1, M, D), lambda b: (b, 0, 0)),
                  pl.BlockSpec(w_mkv.shape, lambda b: (0, 0))],
        out_specs=[pl.BlockSpec((1, M, D), lambda b: (b, 0, 0)),
                   pl.BlockSpec((1, M, D), lambda b: (b, 0, 0))],
        out_shape=[jax.ShapeDtypeStruct((B, M, D), BF16), jax.ShapeDtypeStruct((B, M, D), BF16)],
        compiler_params=pltpu.CompilerParams(
            dimension_semantics=("arbitrary",), vmem_limit_bytes=VMEM_LIMIT),
    )(mem, w_mkv)


def _split3(v):
    hi = v.astype(BF16)
    r1 = v - hi.astype(F32)
    mid = r1.astype(BF16)
    lo = (r1 - mid.astype(F32)).astype(BF16)
    return hi, mid, lo


def _mix_kernel(x_ref, yp_ref, ya_ref, wo_ref, g1_ref, b1_ref, km_ref, vm_ref, wq_ref, wmo_ref,
                g2_ref, b2_ref, wr_ref, br_ref, x2_ref, tope_ref, gate_ref):
    tm, d = x_ref.shape
    hd = d // MEM_HEADS
    pw = yp_ref.shape[1]
    mix = jnp.dot(yp_ref[...], wo_ref[0:pw, :], preferred_element_type=F32)
    mix = mix + jnp.dot(ya_ref[...], wo_ref[pw:, :], preferred_element_type=F32)
    x1 = _layer_norm(DN_ALPHA * x_ref[...] + mix, g1_ref[...], b1_ref[...])

    q = jnp.dot(x1.astype(BF16), wq_ref[...], preferred_element_type=F32).astype(BF16)
    scale = hd ** -0.5
    att = None
    for h in range(MEM_HEADS):
        c = slice(h * hd, (h + 1) * hd)
        lg = lax.dot_general(q[:, c], km_ref[0, :, c], _NT, preferred_element_type=F32) * scale
        p = jnp.exp(lg - jnp.max(lg, axis=-1, keepdims=True))
        p = p / jnp.sum(p, axis=-1, keepdims=True)
        oh = jnp.dot(p.astype(BF16), vm_ref[0, :, c], preferred_element_type=F32).astype(BF16)
        t = jnp.dot(oh, wmo_ref[c, :], preferred_element_type=F32)
        att = t if att is None else att + t
    x2 = _layer_norm(DN_ALPHA * x1 + att, g2_ref[...], b2_ref[...])
    x2_ref[...] = x2

    xs = _split3(x2)
    ws = _split3(wr_ref[...])
    lt = None
    for i, j in ((0, 0), (0, 1), (1, 0)):
        t = lax.dot_general(ws[j], xs[i], _NT, preferred_element_type=F32)
        lt = t if lt is None else lt + t
    lt = lt + br_ref[...]
    n_e = lt.shape[0]
    eidx = lax.broadcasted_iota(I32, lt.shape, 0)
    vals, idxs = [], []
    for _ in range(TOP_K_EXPERTS):
        mx = jnp.max(lt, axis=0, keepdims=True)
        ix = jnp.min(jnp.where(lt == mx, eidx, n_e), axis=0, keepdims=True)
        vals.append(mx)
        idxs.append(ix)
        lt = jnp.where(eidx == ix, -jnp.inf, lt)
    tope_ref[...] = jnp.concatenate(idxs, axis=0)
    ex = [jnp.exp(v - vals[0]) for v in vals]
    den = ex[0]
    for e_ in ex[1:]:
        den = den + e_
    gates = jnp.concatenate([e_ / den for e_ in ex] + [jnp.zeros((LANES - TOP_K_EXPERTS, tm), F32)], axis=0)
    gate_ref[...] = jnp.transpose(gates)


def _mix(x2d, ypool, yatt, w_o, g1, b1, k_mem, v_mem, w_mq, w_mo, g2, b2, w_rt, b_r, B, L):
    T, D = x2d.shape
    tm = min(MIX_ROWS, L)
    nl = L // tm
    M = k_mem.shape[1]
    row = lambda i: (i, 0)
    full = lambda i: (0, 0)
    return pl.pallas_call(
        _mix_kernel,
        grid=(T // tm,),
        in_specs=[
            pl.BlockSpec((tm, D), row),
            pl.BlockSpec((tm, ypool.shape[1]), row),
            pl.BlockSpec((tm, yatt.shape[1]), row),
            pl.BlockSpec(w_o.shape, full),
            pl.BlockSpec(g1.shape, full),
            pl.BlockSpec(b1.shape, full),
            pl.BlockSpec((1, M, D), lambda i: (i // nl, 0, 0)),
            pl.BlockSpec((1, M, D), lambda i: (i // nl, 0, 0)),
            pl.BlockSpec(w_mq.shape, full),
            pl.BlockSpec(w_mo.shape, full),
            pl.BlockSpec(g2.shape, full),
            pl.BlockSpec(b2.shape, full),
            pl.BlockSpec(w_rt.shape, full),
            pl.BlockSpec(b_r.shape, full),
        ],
        out_specs=[
            pl.BlockSpec((tm, D), row),
            pl.BlockSpec((TOP_K_EXPERTS, tm), lambda i: (0, i)),
            pl.BlockSpec((tm, LANES), row),
        ],
        out_shape=[
            jax.ShapeDtypeStruct((T, D), F32),
            jax.ShapeDtypeStruct((TOP_K_EXPERTS, T), I32),
            jax.ShapeDtypeStruct((T, LANES), F32),
        ],
        compiler_params=pltpu.CompilerParams(
            dimension_semantics=("arbitrary",), vmem_limit_bytes=VMEM_LIMIT),
    )(x2d, ypool, yatt, w_o, g1, b1, k_mem, v_mem, w_mq, w_mo, g2, b2, w_rt, b_r)


def _rank_kernel(tope_ref, rank_ref, cnt_ref, carry_ref):
    i = pl.program_id(0)
    tr = tope_ref.shape[1]

    @pl.when(i == 0)
    def _():
        carry_ref[...] = jnp.zeros(carry_ref.shape, F32)

    eidx = lax.broadcasted_iota(I32, (N_EXPERTS, tr), 0)
    onehot = jnp.zeros((N_EXPERTS, tr), F32)
    for k in range(TOP_K_EXPERTS):
        onehot = onehot + jnp.where(eidx == tope_ref[k:k + 1, :], 1.0, 0.0)
    before = jnp.where(lax.broadcasted_iota(I32, (tr, tr), 0) < lax.broadcasted_iota(I32, (tr, tr), 1), 1.0, 0.0)
    excl = jnp.dot(onehot.astype(BF16), before.astype(BF16), preferred_element_type=F32)
    rank_full = excl + carry_ref[:, 0:1]
    rows = []
    for k in range(TOP_K_EXPERTS):
        rows.append(jnp.sum(jnp.where(eidx == tope_ref[k:k + 1, :], rank_full, 0.0), axis=0, keepdims=True))
    rank_ref[...] = jnp.concatenate(rows, axis=0).astype(I32)
    carry_ref[...] = carry_ref[...] + jnp.sum(onehot, axis=1, keepdims=True)
    cnt_ref[...] = carry_ref[...].astype(I32)


def _rank(tope):
    K, T = tope.shape
    tr = min(RANK_TOKENS, T)
    return pl.pallas_call(
        _rank_kernel,
        grid=(T // tr,),
        in_specs=[pl.BlockSpec((K, tr), lambda i: (0, i))],
        out_specs=[pl.BlockSpec((K, tr), lambda i: (0, i)),
                   pl.BlockSpec((N_EXPERTS, LANES), lambda i: (0, 0))],
        out_shape=[jax.ShapeDtypeStruct((K, T), I32), jax.ShapeDtypeStruct((N_EXPERTS, LANES), I32)],
        scratch_shapes=[pltpu.VMEM((N_EXPERTS, LANES), F32)],
        compiler_params=pltpu.CompilerParams(dimension_semantics=("arbitrary",)),
    )(tope)


def _dispatch_kernel(tope_ref, rank_ref, start_ref, cnt_ref, padded_ref, x_ref, xs_hbm, zrow, sem, zsem):
    i = pl.program_id(0)
    td = tope_ref.shape[1]

    def row_copy(j, dst):
        return pltpu.make_async_copy(x_ref.at[pl.ds(j, 1), :], xs_hbm.at[pl.ds(dst, 1), :], sem)

    def zero_copy(dst):
        return pltpu.make_async_copy(zrow, xs_hbm.at[pl.ds(dst, 1), :], zsem)

    @pl.when(i == 0)
    def _():
        zrow[...] = jnp.zeros(zrow.shape, F32)
        for e in range(N_EXPERTS):
            first = start_ref[e] + cnt_ref[e]
            n_pad = padded_ref[e] - cnt_ref[e]

            def zstart(r, c):
                zero_copy(first + r).start()
                return c
            lax.fori_loop(0, n_pad, zstart, 0)

            def zwait(r, c):
                zero_copy(first + r).wait()
                return c
            lax.fori_loop(0, n_pad, zwait, 0)

    def issue(j, c):
        for k in range(TOP_K_EXPERTS):
            row_copy(j, start_ref[tope_ref[k, j]] + rank_ref[k, j]).start()
        return c
    lax.fori_loop(0, td, issue, 0)

    def drain(j, c):
        for k in range(TOP_K_EXPERTS):
            row_copy(j, start_ref[tope_ref[k, j]] + rank_ref[k, j]).wait()
        return c
    lax.fori_loop(0, td, drain, 0)


def _dispatch(tope, rank, pad_start, counts, padded, x2, n_rows):
    K, T = tope.shape
    D = x2.shape[1]
    td = min(DISPATCH_TOKENS, T)
    smem_tok = pl.BlockSpec((K, td), lambda i: (0, i), memory_space=pltpu.SMEM)
    smem_full = pl.BlockSpec(memory_space=pltpu.SMEM)
    return pl.pallas_call(
        _dispatch_kernel,
        grid=(T // td,),
        in_specs=[smem_tok, smem_tok, smem_full, smem_full, smem_full,
                  pl.BlockSpec((td, D), lambda i: (i, 0))],
        out_specs=pl.BlockSpec(memory_space=pl.ANY),
        out_shape=jax.ShapeDtypeStruct((n_rows, D), F32),
        scratch_shapes=[pltpu.VMEM((1, D), F32), pltpu.SemaphoreType.DMA(()), pltpu.SemaphoreType.DMA(())],
        compiler_params=pltpu.CompilerParams(
            dimension_semantics=("arbitrary",), vmem_limit_bytes=VMEM_LIMIT),
    )(tope, rank, pad_start, counts, padded, x2)


def _expert_kernel(bexp_ref, nused_ref, xs_ref, wgu_ref, bgu_ref, wd_ref, bd_ref, y_ref, wgu_bf, wd_bf):
    j = pl.program_id(0)
    d_ff = wd_ref.shape[1]

    @pl.when(j < nused_ref[0])
    def _():
        prev = bexp_ref[jnp.maximum(j - 1, 0)]

        @pl.when(jnp.logical_or(j == 0, bexp_ref[j] != prev))
        def _():
            wgu_bf[...] = wgu_ref[0].astype(BF16)
            wd_bf[...] = wd_ref[0].astype(BF16)

        xb = xs_ref[...].astype(BF16)
        gu = jnp.dot(xb, wgu_bf[...], preferred_element_type=F32) + bgu_ref[0]
        g = jnp.minimum(gu[:, 0:d_ff], SWIGLU_LIMIT)
        u = jnp.clip(gu[:, d_ff:2 * d_ff], -SWIGLU_LIMIT, SWIGLU_LIMIT)
        act = (u + 1.0) * g * (1.0 / (1.0 + jnp.exp(-SWIGLU_ALPHA * g)))
        y_ref[...] = jnp.dot(act.astype(BF16), wd_bf[...], preferred_element_type=F32) + bd_ref[0]


def _experts(block_exp, n_used, xs, w_gate_up, b_gate_up, w_down, b_down):
    P, D = xs.shape
    blk = EXPERT_ROWS
    nblk = P // blk
    E, _, F2 = w_gate_up.shape
    d_ff = w_down.shape[1]
    row = lambda j, be, nu: (jnp.minimum(j, nu[0] - 1), 0)
    exp3 = lambda j, be, nu: (be[j], 0, 0)
    grid_spec = pltpu.PrefetchScalarGridSpec(
        num_scalar_prefetch=2,
        grid=(nblk,),
        in_specs=[
            pl.BlockSpec((blk, D), row),
            pl.BlockSpec((1, D, F2), exp3),
            pl.BlockSpec((1, 1, F2), exp3),
            pl.BlockSpec((1, d_ff, D), exp3),
            pl.BlockSpec((1, 1, D), exp3),
        ],
        out_specs=pl.BlockSpec((blk, D), row),
        scratch_shapes=[pltpu.VMEM((D, F2), BF16), pltpu.VMEM((d_ff, D), BF16)],
    )
    return pl.pallas_call(
        _expert_kernel,
        grid_spec=grid_spec,
        out_shape=jax.ShapeDtypeStruct((P, D), F32),
        compiler_params=pltpu.CompilerParams(
            dimension_semantics=("arbitrary",), vmem_limit_bytes=VMEM_LIMIT),
    )(block_exp, n_used, xs, w_gate_up, b_gate_up.reshape(E, 1, F2), w_down, b_down.reshape(E, 1, D))


def _combine_kernel(tope_ref, rank_ref, start_ref, x2_ref, gate_ref, g3_ref, b3_ref, y_hbm, o_ref, ybuf, sem):
    tc = x2_ref.shape[0]

    def row_copy(j, k):
        src = start_ref[tope_ref[k, j]] + rank_ref[k, j]
        return pltpu.make_async_copy(y_hbm.at[pl.ds(src, 1), :], ybuf.at[k, pl.ds(j, 1), :], sem)

    def issue(j, c):
        for k in range(TOP_K_EXPERTS):
            row_copy(j, k).start()
        return c
    lax.fori_loop(0, tc, issue, 0)

    def drain(j, c):
        for k in range(TOP_K_EXPERTS):
            row_copy(j, k).wait()
        return c
    lax.fori_loop(0, tc, drain, 0)

    moe = None
    for k in range(TOP_K_EXPERTS):
        t = ybuf[k] * gate_ref[:, k:k + 1]
        moe = t if moe is None else moe + t
    o_ref[...] = _layer_norm(DN_ALPHA * x2_ref[...] + moe, g3_ref[...], b3_ref[...])


def _combine(tope, rank, pad_start, x2, gates, g3, b3, y):
    K, T = tope.shape
    D = x2.shape[1]
    tc = min(COMBINE_TOKENS, T)
    smem_tok = pl.BlockSpec((K, tc), lambda i: (0, i), memory_space=pltpu.SMEM)
    row = lambda i: (i, 0)
    full = lambda i: (0, 0)
    return pl.pallas_call(
        _combine_kernel,
        grid=(T // tc,),
        in_specs=[smem_tok, smem_tok, pl.BlockSpec(memory_space=pltpu.SMEM),
                  pl.BlockSpec((tc, D), row), pl.BlockSpec((tc, LANES), row),
                  pl.BlockSpec(g3.shape, full), pl.BlockSpec(b3.shape, full),
                  pl.BlockSpec(memory_space=pl.ANY)],
        out_specs=pl.BlockSpec((tc, D), row),
        out_shape=jax.ShapeDtypeStruct((T, D), F32),
        scratch_shapes=[pltpu.VMEM((K, tc, D), F32), pltpu.SemaphoreType.DMA(())],
        compiler_params=pltpu.CompilerParams(
            dimension_semantics=("arbitrary",), vmem_limit_bytes=VMEM_LIMIT),
    )(tope, rank, pad_start, x2, gates, g3, b3, y)


def _pad_cols(w, width):
    return jnp.pad(w, ((0, 0), (0, width - w.shape[1])))


def _layer(x, mem, w_in, w_pool, pool_scale, ik_g, ik_b, kv_g, w_uk, w_uv, w_o, ln1_g, ln1_b,
           w_mq, w_mkv, w_mo, ln2_g, ln2_b, w_router, b_router, w_gate_up, b_gate_up, w_down, b_down,
           ln3_g, ln3_b):
    B, L, D = x.shape
    T = B * L
    row = lambda v: v.reshape(1, -1)

    o = 0
    pieces = []
    for width in (POOL_WIDTH, ATT_WIDTH, KV_RANK, IDX_HEADS * IDX_HEAD_DIM, IDX_HEAD_DIM, IDX_HEADS):
        pieces.append(w_in[:, o:o + width])
        o += width
    pieces[4] = _pad_cols(pieces[4], LANES)
    pieces[5] = _pad_cols(pieces[5], LANES)
    w_in_pad = jnp.concatenate(pieces, axis=1).astype(BF16)
    wuv_pad = jnp.zeros((ATT_HEADS, KV_RANK, ATT_WIDTH), F32)
    for h in range(ATT_HEADS):
        wuv_pad = wuv_pad.at[h, :, h * ATT_HEAD_DIM:(h + 1) * ATT_HEAD_DIM].set(w_uv[h])

    ypool, qlat, ckv, qidx, kidx, widx = _inproj(
        x, w_in_pad, w_pool.astype(BF16), row(pool_scale), row(kv_g), row(ik_g), row(ik_b), w_uk.astype(BF16))
    yatt = _dsa(qidx, widx, qlat, kidx, ckv, wuv_pad.astype(BF16))
    k_mem, v_mem = _memkv(mem, w_mkv.astype(BF16))
    x2, tope, gates = _mix(
        x.reshape(T, D), ypool.reshape(T, -1), yatt.reshape(T, -1), w_o.astype(BF16), row(ln1_g), row(ln1_b),
        k_mem, v_mem, w_mq.astype(BF16), w_mo.astype(BF16), row(ln2_g), row(ln2_b),
        jnp.transpose(w_router), b_router.reshape(-1, 1), B, L)

    rank, cnt = _rank(tope)
    counts = cnt[:, 0]
    blk = EXPERT_ROWS
    padded = (counts + blk - 1) // blk * blk
    pad_end = jnp.cumsum(padded)
    pad_start = pad_end - padded
    n_rows = T * TOP_K_EXPERTS + N_EXPERTS * blk
    nblk = n_rows // blk
    n_used = (pad_end[-1] // blk).astype(I32)
    blk_first = jnp.minimum(jnp.arange(nblk, dtype=I32), n_used - 1) * blk
    n_ended = jnp.sum((pad_end[None, :] <= blk_first[:, None]).astype(I32), axis=1)
    block_exp = jnp.minimum(n_ended, N_EXPERTS - 1).astype(I32)

    xs = _dispatch(tope, rank, pad_start.astype(I32), counts, padded.astype(I32), x2, n_rows)
    y = _experts(block_exp, n_used.reshape(1), xs, w_gate_up, b_gate_up, w_down, b_down)
    out = _combine(tope, rank, pad_start.astype(I32), x2, gates, row(ln3_g), row(ln3_b), y)
    return out.reshape(B, L, D)


def kernel(x, mem, w_in, w_pool, pool_scale, idx_k_norm_g, idx_k_norm_b, kv_norm_g, w_uk, w_uv, w_o, ln1_g, ln1_b, w_mq, w_mkv, w_mo, ln2_g, ln2_b, w_router, b_router, w_gate_up, b_gate_up, w_down, b_down, ln3_g, ln3_b):
    assert w_in.shape[0] == DEPTH
    return _layer(x, mem, w_in[0], w_pool[0], pool_scale[0], idx_k_norm_g[0], idx_k_norm_b[0], kv_norm_g[0],
                  w_uk[0], w_uv[0], w_o[0], ln1_g[0], ln1_b[0], w_mq[0], w_mkv[0], w_mo[0], ln2_g[0], ln2_b[0],
                  w_router[0], b_router[0], w_gate_up[0], b_gate_up[0], w_down[0], b_down[0], ln3_g[0], ln3_b[0])
```

```python
import functools

import jax
import jax.numpy as jnp
from jax import lax
from jax.experimental import pallas as pl
from jax.experimental.pallas import tpu as pltpu

F32 = jnp.float32
BF16 = jnp.bfloat16
I32 = jnp.int32

POOL_WINDOWS = (2, 4, 8, 16)
POOL_GROUP_DIM = 128
POOL_WIDTH = 512
ATT_HEADS = 8
ATT_HEAD_DIM = 64
ATT_WIDTH = 512
KV_RANK = 128
IDX_HEADS = 8
IDX_HEAD_DIM = 64
TOPK_MAX = 256
MEM_HEADS = 4
N_EXPERTS = 32
TOP_K_EXPERTS = 4
SWIGLU_LIMIT = 7.0
SWIGLU_ALPHA = 1.702
LN_EPS = 1e-5
RMS_EPS = 1e-6
DEPTH = 1
DN_ALPHA = (2 * DEPTH) ** 0.25

LANES = 128
MAX_POOL_WINDOW = 16
VMEM_LIMIT = 56 * 1024 * 1024

PROJ_ROWS = 512
DSA_QUERIES = 128
DSA_KEYS = 512
MIX_ROWS = 512
RANK_TOKENS = 512
DISPATCH_TOKENS = 512
EXPERT_ROWS = 256
COMBINE_TOKENS = 256

NEG_BIG = -1e30
INT_MIN = -(2 ** 31)
NEG_INF_KEY = INT_MIN + 0x007FFFFF

_NT = (((1,), (1,)), ((), ()))


def _layer_norm(v, g, b):
    mu = jnp.mean(v, axis=-1, keepdims=True)
    d = v - mu
    var = jnp.mean(d * d, axis=-1, keepdims=True)
    return d * lax.rsqrt(var + LN_EPS) * g + b


_C_POOL = 0
_C_Q = 512
_C_CKV = 1024
_C_QIDX = 1152
_C_KIDX = 1664
_C_WIDX = 1792
_IN_PAD = 1920


def _inproj_kernel(x_ref, w_ref, wpool_ref, pscale_ref, kvg_ref, ikg_ref, ikb_ref, wuk_ref,
                   ypool_ref, qlat_ref, ckv_ref, qidx_ref, kidx_ref, widx_ref, ubuf):
    li = pl.program_id(1)
    tm = x_ref.shape[1]
    halo = MAX_POOL_WINDOW
    xb = x_ref[0].astype(BF16)

    u = jnp.dot(xb, w_ref[:, _C_POOL:_C_POOL + POOL_WIDTH], preferred_element_type=F32)

    @pl.when(li == 0)
    def _():
        ubuf[0:halo, :] = jnp.zeros((halo, POOL_WIDTH), F32)

    ubuf[halo:halo + tm, :] = u
    pos = li * tm + lax.broadcasted_iota(I32, (tm, 1), 0)
    for g, w in enumerate(POOL_WINDOWS):
        c0 = g * POOL_GROUP_DIM
        c1 = c0 + POOL_GROUP_DIM
        ug = ubuf[halo:halo + tm, c0:c1]
        s = ug
        for j in range(1, w):
            s = s + ubuf[halo - j:halo - j + tm, c0:c1]
        cnt = jnp.minimum(pos + 1, w).astype(F32)
        d = s / cnt - ug
        y = jnp.dot(d.astype(BF16), wpool_ref[g], preferred_element_type=F32) * pscale_ref[:, c0:c1]
        ypool_ref[0, :, c0:c1] = y.astype(BF16)
    ubuf[0:halo, :] = ubuf[tm:tm + halo, :]

    q = jnp.dot(xb, w_ref[:, _C_Q:_C_Q + ATT_WIDTH], preferred_element_type=F32)
    att_scale = ATT_HEAD_DIM ** -0.5
    for h in range(ATT_HEADS):
        qh = q[:, h * ATT_HEAD_DIM:(h + 1) * ATT_HEAD_DIM].astype(BF16)
        ql = jnp.dot(qh, wuk_ref[h], preferred_element_type=F32) * att_scale
        qlat_ref[0, h] = ql.astype(BF16)

    c = jnp.dot(xb, w_ref[:, _C_CKV:_C_CKV + KV_RANK], preferred_element_type=F32)
    c = c * lax.rsqrt(jnp.mean(c * c, axis=-1, keepdims=True) + RMS_EPS) * kvg_ref[...]
    ckv_ref[0, :, 0:KV_RANK] = c.astype(BF16)
    ckv_ref[0, :, KV_RANK:2 * KV_RANK] = jnp.ones((tm, KV_RANK), BF16)

    qi = jnp.dot(xb, w_ref[:, _C_QIDX:_C_QIDX + IDX_HEADS * IDX_HEAD_DIM], preferred_element_type=F32)
    for h in range(IDX_HEADS):
        qidx_ref[0, h] = qi[:, h * IDX_HEAD_DIM:(h + 1) * IDX_HEAD_DIM].astype(BF16)
    kw = jnp.dot(xb, w_ref[:, _C_KIDX:_C_KIDX + 2 * LANES], preferred_element_type=F32)
    kr = kw[:, 0:IDX_HEAD_DIM]
    kidx_ref[0] = _layer_norm(kr, ikg_ref[...], ikb_ref[...]).astype(BF16)
    widx_ref[0] = kw[:, LANES:2 * LANES]


def _inproj(x, w_in_pad, w_pool, pool_scale, kv_g, ik_g, ik_b, w_uk):
    B, L, D = x.shape
    tm = min(PROJ_ROWS, L)
    nl = L // tm
    full2 = lambda b, l: (0, 0)
    full3 = lambda b, l: (0, 0, 0)
    return pl.pallas_call(
        _inproj_kernel,
        grid=(B, nl),
        in_specs=[
            pl.BlockSpec((1, tm, D), lambda b, l: (b, l, 0)),
            pl.BlockSpec(w_in_pad.shape, full2),
            pl.BlockSpec(w_pool.shape, full3),
            pl.BlockSpec(pool_scale.shape, full2),
            pl.BlockSpec(kv_g.shape, full2),
            pl.BlockSpec(ik_g.shape, full2),
            pl.BlockSpec(ik_b.shape, full2),
            pl.BlockSpec(w_uk.shape, full3),
        ],
        out_specs=[
            pl.BlockSpec((1, tm, POOL_WIDTH), lambda b, l: (b, l, 0)),
            pl.BlockSpec((1, ATT_HEADS, tm, KV_RANK), lambda b, l: (b, 0, l, 0)),
            pl.BlockSpec((1, tm, 2 * KV_RANK), lambda b, l: (b, l, 0)),
            pl.BlockSpec((1, IDX_HEADS, tm, IDX_HEAD_DIM), lambda b, l: (b, 0, l, 0)),
            pl.BlockSpec((1, tm, IDX_HEAD_DIM), lambda b, l: (b, l, 0)),
            pl.BlockSpec((1, tm, LANES), lambda b, l: (b, l, 0)),
        ],
        out_shape=[
            jax.ShapeDtypeStruct((B, L, POOL_WIDTH), BF16),
            jax.ShapeDtypeStruct((B, ATT_HEADS, L, KV_RANK), BF16),
            jax.ShapeDtypeStruct((B, L, 2 * KV_RANK), BF16),
            jax.ShapeDtypeStruct((B, IDX_HEADS, L, IDX_HEAD_DIM), BF16),
            jax.ShapeDtypeStruct((B, L, IDX_HEAD_DIM), BF16),
            jax.ShapeDtypeStruct((B, L, LANES), F32),
        ],
        scratch_shapes=[pltpu.VMEM((MAX_POOL_WINDOW + tm, POOL_WIDTH), F32)],
        compiler_params=pltpu.CompilerParams(
            dimension_semantics=("arbitrary", "arbitrary"), vmem_limit_bytes=VMEM_LIMIT),
    )(x, w_in_pad, w_pool, pool_scale, kv_g, ik_g, ik_b, w_uk)


def _sortable(score):
    bits = pltpu.bitcast(score, I32)
    return bits ^ ((bits >> 31) & 0x7FFFFFFF)


def _dsa_kernel(qidx_ref, widx_ref, qlat_ref, kidx_ref, ckv_ref, wuv_ref, o_ref,
                key_ref, keyt_ref, wb_ref, p_ref, m_ref, acc_ref, sel_ref, *, top_k, idx_bits):
    qi = pl.program_id(1)
    H = ATT_HEADS
    qb = qidx_ref.shape[2]
    kc = wb_ref.shape[1]
    qstart = qi * qb
    n_ch = (qstart + qb + kc - 1) // kc
    idx_scale = (IDX_HEAD_DIM ** -0.5) * (IDX_HEADS ** -0.5)

    qpos = qstart + lax.broadcasted_iota(I32, (qb, 1), 0)
    qpos_t = qstart + lax.broadcasted_iota(I32, (1, qb), 1)
    lane_pos = lax.broadcasted_iota(I32, (1, kc), 1)
    row_pos = lax.broadcasted_iota(I32, (kc, 1), 0)

    qs = qidx_ref[0].reshape(H * qb, IDX_HEAD_DIM)
    for h in range(H):
        wb_ref[h * qb:(h + 1) * qb, :] = jnp.broadcast_to(widx_ref[0, :, h:h + 1], (qb, kc))

    def score_body(c, carry):
        off = pl.multiple_of(c * kc, kc)
        kk = kidx_ref[0, pl.ds(off, kc), :]
        s = lax.dot_general(qs, kk, _NT, preferred_element_type=F32)
        acc = None
        for h in range(H):
            term = jnp.maximum(s[h * qb:(h + 1) * qb, :], 0.0) * wb_ref[h * qb:(h + 1) * qb, :]
            acc = term if acc is None else acc + term
        score = acc * idx_scale
        score = jnp.where(off + lane_pos <= qpos, score, -jnp.inf)
        key_ref[c] = _sortable(score)
        keyt_ref[c] = _sortable(jnp.transpose(score))
        return carry

    lax.fori_loop(0, n_ch, score_body, 0)

    def count(indicator):
        def body(c, cnt):
            m = indicator(keyt_ref[c], c * kc + row_pos)
            parts = [m[j * 8:(j + 1) * 8, :] for j in range(kc // 8)]
            while len(parts) > 1:
                parts = [parts[j] + parts[j + 1] for j in range(0, len(parts), 2)]
            return cnt + parts[0]
        cnt = lax.fori_loop(0, n_ch, body, jnp.zeros((8, qb), I32))
        return jnp.sum(cnt, axis=0, keepdims=True)

    def bit_body(i, t):
        cand = t + lax.shift_left(jnp.int32(1), 31 - i)
        n_ge = count(lambda k, _: jnp.where(k >= cand, 1, 0))
        return jnp.where(n_ge >= top_k, cand, t)

    thr_t = lax.fori_loop(0, 32, bit_body, jnp.full((1, qb), INT_MIN, I32))

    n_gt = count(lambda k, _: jnp.where(k > thr_t, 1, 0))
    n_ge = count(lambda k, _: jnp.where(k >= thr_t, 1, 0))
    need = top_k - n_gt
    surplus = jnp.where(thr_t > NEG_INF_KEY, jnp.where(n_ge > top_k, 1, 0), 0)
    sel_ref[0:1, :] = thr_t
    sel_ref[1:2, :] = qpos_t

    @pl.when(jnp.max(surplus) > 0)
    def _():
        def tie_body(i, m):
            cand = m + lax.shift_left(jnp.int32(1), idx_bits - 1 - i)
            n_before = count(lambda k, p: jnp.where(k == thr_t, jnp.where(p < cand, 1, 0), 0))
            return jnp.where(n_before < need, cand, m)
        last = lax.fori_loop(0, idx_bits, tie_body, jnp.zeros((1, qb), I32))
        sel_ref[1:2, :] = jnp.where(surplus > 0, jnp.minimum(last, qpos_t), qpos_t)

    thr = jnp.transpose(jnp.broadcast_to(sel_ref[0:1, :], (qb, qb)))[:, 0:1]
    tie_last = jnp.transpose(jnp.broadcast_to(sel_ref[1:2, :], (qb, qb)))[:, 0:1]

    def chunk_bias(c):
        key = key_ref[c]
        kpos = c * kc + lane_pos
        tie_bias = jnp.where(key == thr, jnp.where(kpos <= tie_last, 0.0, NEG_BIG), NEG_BIG)
        return jnp.where(key > thr, 0.0, tie_bias)

    ql = qlat_ref[0].reshape(H * qb, KV_RANK)
    n_lt = kc // LANES

    m_ref[...] = jnp.full(m_ref.shape, NEG_BIG, F32)

    def max_body(c, carry):
        off = pl.multiple_of(c * kc, kc)
        ck = ckv_ref[0, pl.ds(off, kc), 0:KV_RANK]
        lg = lax.dot_general(ql, ck, _NT, preferred_element_type=F32)
        bias = chunk_bias(c)
        for h in range(H):
            r = slice(h * qb, (h + 1) * qb)
            zm = lg[r, 0:LANES] + bias[:, 0:LANES]
            for j in range(1, n_lt):
                zm = jnp.maximum(zm, lg[r, j * LANES:(j + 1) * LANES] + bias[:, j * LANES:(j + 1) * LANES])
            m_ref[r, :] = jnp.maximum(m_ref[r, :], zm)
        return carry

    lax.fori_loop(0, n_ch, max_body, 0)
    m_ref[...] = jnp.broadcast_to(jnp.max(m_ref[...], axis=1, keepdims=True), m_ref.shape)

    acc_ref[...] = jnp.zeros(acc_ref.shape, F32)

    def att_body(c, carry):
        off = pl.multiple_of(c * kc, kc)
        ckx = ckv_ref[0, pl.ds(off, kc), :]
        lg = lax.dot_general(ql, ckx[:, 0:KV_RANK], _NT, preferred_element_type=F32)
        bias = chunk_bias(c)
        for h in range(H):
            r = slice(h * qb, (h + 1) * qb)
            mh = m_ref[r, :]
            for j in range(n_lt):
                cs = slice(j * LANES, (j + 1) * LANES)
                p_ref[r, cs] = jnp.exp(lg[r, cs] + bias[:, cs] - mh).astype(BF16)
        acc_ref[...] = acc_ref[...] + jnp.dot(p_ref[...], ckx, preferred_element_type=F32)
        return carry

    lax.fori_loop(0, n_ch, att_body, 0)

    o_lat = (acc_ref[:, 0:KV_RANK] / acc_ref[:, KV_RANK:2 * KV_RANK]).astype(BF16)
    y = None
    for h in range(H):
        t = jnp.dot(o_lat[h * qb:(h + 1) * qb, :], wuv_ref[h], preferred_element_type=F32)
        y = t if y is None else y + t
    o_ref[0] = y.astype(BF16)


def _dsa(qidx, widx, qlat, kidx, ckv, wuv_pad):
    B, H, L, _ = qidx.shape
    qb = min(DSA_QUERIES, L)
    kc = min(DSA_KEYS, L)
    top_k = min(TOPK_MAX, L // 4)
    idx_bits = max(1, (L - 1).bit_length())
    kern = functools.partial(_dsa_kernel, top_k=top_k, idx_bits=idx_bits)
    return pl.pallas_call(
        kern,
        grid=(B, L // qb),
        in_specs=[
            pl.BlockSpec((1, H, qb, IDX_HEAD_DIM), lambda b, q: (b, 0, q, 0)),
            pl.BlockSpec((1, qb, LANES), lambda b, q: (b, q, 0)),
            pl.BlockSpec((1, H, qb, KV_RANK), lambda b, q: (b, 0, q, 0)),
            pl.BlockSpec((1, L, IDX_HEAD_DIM), lambda b, q: (b, 0, 0)),
            pl.BlockSpec((1, L, 2 * KV_RANK), lambda b, q: (b, 0, 0)),
            pl.BlockSpec(wuv_pad.shape, lambda b, q: (0, 0, 0)),
        ],
        out_specs=pl.BlockSpec((1, qb, ATT_WIDTH), lambda b, q: (b, q, 0)),
        out_shape=jax.ShapeDtypeStruct((B, L, ATT_WIDTH), BF16),
        scratch_shapes=[
            pltpu.VMEM((L // kc, qb, kc), I32),
            pltpu.VMEM((L // kc, kc, qb), I32),
            pltpu.VMEM((H * qb, kc), F32),
            pltpu.VMEM((H * qb, kc), BF16),
            pltpu.VMEM((H * qb, LANES), F32),
            pltpu.VMEM((H * qb, 2 * KV_RANK), F32),
            pltpu.VMEM((8, qb), I32),
        ],
        compiler_params=pltpu.CompilerParams(
            dimension_semantics=("arbitrary", "arbitrary"), vmem_limit_bytes=VMEM_LIMIT),
    )(qidx, widx, qlat, kidx, ckv, wuv_pad)


def _memkv_kernel(mem_ref, w_ref, k_ref, v_ref):
    d = k_ref.shape[2]
    kv = jnp.dot(mem_ref[0].astype(BF16), w_ref[...], preferred_element_type=F32)
    k_ref[0] = kv[:, 0:d].astype(BF16)
    v_ref[0] = kv[:, d:2 * d].astype(BF16)


def _memkv(mem, w_mkv):
    B, M, D = mem.shape
    return pl.pallas_call(
        _memkv_kernel,
        grid=(B,),
        in_specs=[pl.BlockSpec((1, M, D), lambda b: (b, 0, 0)),
                  pl.BlockSpec(w_mkv.shape, lambda b: (0, 0))],
        out_specs=[pl.BlockSpec((1, M, D), lambda b: (b, 0, 0)),
                   pl.BlockSpec((1, M, D), lambda b: (b, 0, 0))],
        out_shape=[jax.ShapeDtypeStruct((B, M, D), BF16), jax.ShapeDtypeStruct((B, M, D), BF16)],
        compiler_params=pltpu.CompilerParams(
            dimension_semantics=("arbitrary",), vmem_limit_bytes=VMEM_LIMIT),
    )(mem, w_mkv)


def _split3(v):
    hi = v.astype(BF16)
    r1 = v - hi.astype(F32)
    mid = r1.astype(BF16)
    lo = (r1 - mid.astype(F32)).astype(BF16)
    return hi, mid, lo


def _mix_kernel(x_ref, yp_ref, ya_ref, wo_ref, g1_ref, b1_ref, km_ref, vm_ref, wq_ref, wmo_ref,
                g2_ref, b2_ref, wr_ref, br_ref, x2_ref, tope_ref, gate_ref):
    tm, d = x_ref.shape
    hd = d // MEM_HEADS
    pw = yp_ref.shape[1]
    mix = jnp.dot(yp_ref[...], wo_ref[0:pw, :], preferred_element_type=F32)
    mix = mix + jnp.dot(ya_ref[...], wo_ref[pw:, :], preferred_element_type=F32)
    x1 = _layer_norm(DN_ALPHA * x_ref[...] + mix, g1_ref[...], b1_ref[...])

    q = jnp.dot(x1.astype(BF16), wq_ref[...], preferred_element_type=F32).astype(BF16)
    scale = hd ** -0.5
    att = None
    for h in range(MEM_HEADS):
        c = slice(h * hd, (h + 1) * hd)
        lg = lax.dot_general(q[:, c], km_ref[0, :, c], _NT, preferred_element_type=F32) * scale
        p = jnp.exp(lg - jnp.max(lg, axis=-1, keepdims=True))
        p = p / jnp.sum(p, axis=-1, keepdims=True)
        oh = jnp.dot(p.astype(BF16), vm_ref[0, :, c], preferred_element_type=F32).astype(BF16)
        t = jnp.dot(oh, wmo_ref[c, :], preferred_element_type=F32)
        att = t if att is None else att + t
    x2 = _layer_norm(DN_ALPHA * x1 + att, g2_ref[...], b2_ref[...])
    x2_ref[...] = x2

    xs = _split3(x2)
    ws = _split3(wr_ref[...])
    lt = None
    for i, j in ((0, 0), (0, 1), (1, 0)):
        t = lax.dot_general(ws[j], xs[i], _NT, preferred_element_type=F32)
        lt = t if lt is None else lt + t
    lt = lt + br_ref[...]
    n_e = lt.shape[0]
    eidx = lax.broadcasted_iota(I32, lt.shape, 0)
    vals, idxs = [], []
    for _ in range(TOP_K_EXPERTS):
        mx = jnp.max(lt, axis=0, keepdims=True)
        ix = jnp.min(jnp.where(lt == mx, eidx, n_e), axis=0, keepdims=True)
        vals.append(mx)
        idxs.append(ix)
        lt = jnp.where(eidx == ix, -jnp.inf, lt)
    tope_ref[...] = jnp.concatenate(idxs, axis=0)
    ex = [jnp.exp(v - vals[0]) for v in vals]
    den = ex[0]
    for e_ in ex[1:]:
        den = den + e_
    gates = jnp.concatenate([e_ / den for e_ in ex] + [jnp.zeros((LANES - TOP_K_EXPERTS, tm), F32)], axis=0)
    gate_ref[...] = jnp.transpose(gates)


def _mix(x2d, ypool, yatt, w_o, g1, b1, k_mem, v_mem, w_mq, w_mo, g2, b2, w_rt, b_r, B, L):
    T, D = x2d.shape
    tm = min(MIX_ROWS, L)
    nl = L // tm
    M = k_mem.shape[1]
    row = lambda i: (i, 0)
    full = lambda i: (0, 0)
    return pl.pallas_call(
        _mix_kernel,
        grid=(T // tm,),
        in_specs=[
            pl.BlockSpec((tm, D), row),
            pl.BlockSpec((tm, ypool.shape[1]), row),
            pl.BlockSpec((tm, yatt.shape[1]), row),
            pl.BlockSpec(w_o.shape, full),
            pl.BlockSpec(g1.shape, full),
            pl.BlockSpec(b1.shape, full),
            pl.BlockSpec((1, M, D), lambda i: (i // nl, 0, 0)),
            pl.BlockSpec((1, M, D), lambda i: (i // nl, 0, 0)),
            pl.BlockSpec(w_mq.shape, full),
            pl.BlockSpec(w_mo.shape, full),
            pl.BlockSpec(g2.shape, full),
            pl.BlockSpec(b2.shape, full),
            pl.BlockSpec(w_rt.shape, full),
            pl.BlockSpec(b_r.shape, full),
        ],
        out_specs=[
            pl.BlockSpec((tm, D), row),
            pl.BlockSpec((TOP_K_EXPERTS, tm), lambda i: (0, i)),
            pl.BlockSpec((tm, LANES), row),
        ],
        out_shape=[
            jax.ShapeDtypeStruct((T, D), F32),
            jax.ShapeDtypeStruct((TOP_K_EXPERTS, T), I32),
            jax.ShapeDtypeStruct((T, LANES), F32),
        ],
        compiler_params=pltpu.CompilerParams(
            dimension_semantics=("arbitrary",), vmem_limit_bytes=VMEM_LIMIT),
    )(x2d, ypool, yatt, w_o, g1, b1, k_mem, v_mem, w_mq, w_mo, g2, b2, w_rt, b_r)


def _rank_kernel(tope_ref, rank_ref, cnt_ref, carry_ref):
    i = pl.program_id(0)
    tr = tope_ref.shape[1]

    @pl.when(i == 0)
    def _():
        carry_ref[...] = jnp.zeros(carry_ref.shape, F32)

    eidx = lax.broadcasted_iota(I32, (N_EXPERTS, tr), 0)
    onehot = jnp.zeros((N_EXPERTS, tr), F32)
    for k in range(TOP_K_EXPERTS):
        onehot = onehot + jnp.where(eidx == tope_ref[k:k + 1, :], 1.0, 0.0)
    before = jnp.where(lax.broadcasted_iota(I32, (tr, tr), 0) < lax.broadcasted_iota(I32, (tr, tr), 1), 1.0, 0.0)
    excl = jnp.dot(onehot.astype(BF16), before.astype(BF16), preferred_element_type=F32)
    rank_full = excl + carry_ref[:, 0:1]
    rows = []
    for k in range(TOP_K_EXPERTS):
        rows.append(jnp.sum(jnp.where(eidx == tope_ref[k:k + 1, :], rank_full, 0.0), axis=0, keepdims=True))
    rank_ref[...] = jnp.concatenate(rows, axis=0).astype(I32)
    carry_ref[...] = carry_ref[...] + jnp.sum(onehot, axis=1, keepdims=True)
    cnt_ref[...] = carry_ref[...].astype(I32)


def _rank(tope):
    K, T = tope.shape
    tr = min(RANK_TOKENS, T)
    return pl.pallas_call(
        _rank_kernel,
        grid=(T // tr,),
        in_specs=[pl.BlockSpec((K, tr), lambda i: (0, i))],
        out_specs=[pl.BlockSpec((K, tr), lambda i: (0, i)),
                   pl.BlockSpec((N_EXPERTS, LANES), lambda i: (0, 0))],
        out_shape=[jax.ShapeDtypeStruct((K, T), I32), jax.ShapeDtypeStruct((N_EXPERTS, LANES), I32)],
        scratch_shapes=[pltpu.VMEM((N_EXPERTS, LANES), F32)],
        compiler_params=pltpu.CompilerParams(dimension_semantics=("arbitrary",)),
    )(tope)


def _dispatch_kernel(tope_ref, rank_ref, start_ref, cnt_ref, padded_ref, x_ref, xs_hbm, zrow, sem, zsem):
    i = pl.program_id(0)
    td = tope_ref.shape[1]

    def row_copy(j, dst):
        return pltpu.make_async_copy(x_ref.at[pl.ds(j, 1), :], xs_hbm.at[pl.ds(dst, 1), :], sem)

    def zero_copy(dst):
        return pltpu.make_async_copy(zrow, xs_hbm.at[pl.ds(dst, 1), :], zsem)

    @pl.when(i == 0)
    def _():
        zrow[...] = jnp.zeros(zrow.shape, F32)
        for e in range(N_EXPERTS):
            first = start_ref[e] + cnt_ref[e]
            n_pad = padded_ref[e] - cnt_ref[e]

            def zstart(r, c):
                zero_copy(first + r).start()
                return c
            lax.fori_loop(0, n_pad, zstart, 0)

            def zwait(r, c):
                zero_copy(first + r).wait()
                return c
            lax.fori_loop(0, n_pad, zwait, 0)

    def issue(j, c):
        for k in range(TOP_K_EXPERTS):
            row_copy(j, start_ref[tope_ref[k, j]] + rank_ref[k, j]).start()
        return c
    lax.fori_loop(0, td, issue, 0)

    def drain(j, c):
        for k in range(TOP_K_EXPERTS):
            row_copy(j, start_ref[tope_ref[k, j]] + rank_ref[k, j]).wait()
        return c
    lax.fori_loop(0, td, drain, 0)


def _dispatch(tope, rank, pad_start, counts, padded, x2, n_rows):
    K, T = tope.shape
    D = x2.shape[1]
    td = min(DISPATCH_TOKENS, T)
    smem_tok = pl.BlockSpec((K, td), lambda i: (0, i), memory_space=pltpu.SMEM)
    smem_full = pl.BlockSpec(memory_space=pltpu.SMEM)
    return pl.pallas_call(
        _dispatch_kernel,
        grid=(T // td,),
        in_specs=[smem_tok, smem_tok, smem_full, smem_full, smem_full,
                  pl.BlockSpec((td, D), lambda i: (i, 0))],
        out_specs=pl.BlockSpec(memory_space=pl.ANY),
        out_shape=jax.ShapeDtypeStruct((n_rows, D), F32),
        scratch_shapes=[pltpu.VMEM((1, D), F32), pltpu.SemaphoreType.DMA(()), pltpu.SemaphoreType.DMA(())],
        compiler_params=pltpu.CompilerParams(
            dimension_semantics=("arbitrary",), vmem_limit_bytes=VMEM_LIMIT),
    )(tope, rank, pad_start, counts, padded, x2)


def _expert_kernel(bexp_ref, nused_ref, xs_ref, wgu_ref, bgu_ref, wd_ref, bd_ref, y_ref, wgu_bf, wd_bf):
    j = pl.program_id(0)
    d_ff = wd_ref.shape[1]

    @pl.when(j < nused_ref[0])
    def _():
        prev = bexp_ref[jnp.maximum(j - 1, 0)]

        @pl.when(jnp.logical_or(j == 0, bexp_ref[j] != prev))
        def _():
            wgu_bf[...] = wgu_ref[0].astype(BF16)
            wd_bf[...] = wd_ref[0].astype(BF16)

        xb = xs_ref[...].astype(BF16)
        gu = jnp.dot(xb, wgu_bf[...], preferred_element_type=F32) + bgu_ref[0]
        g = jnp.minimum(gu[:, 0:d_ff], SWIGLU_LIMIT)
        u = jnp.clip(gu[:, d_ff:2 * d_ff], -SWIGLU_LIMIT, SWIGLU_LIMIT)
        act = (u + 1.0) * g * (1.0 / (1.0 + jnp.exp(-SWIGLU_ALPHA * g)))
        y_ref[...] = jnp.dot(act.astype(BF16), wd_bf[...], preferred_element_type=F32) + bd_ref[0]


def _experts(block_exp, n_used, xs, w_gate_up, b_gate_up, w_down, b_down):
    P, D = xs.shape
    blk = EXPERT_ROWS
    nblk = P // blk
    E, _, F2 = w_gate_up.shape
    d_ff = w_down.shape[1]
    row = lambda j, be, nu: (jnp.minimum(j, nu[0] - 1), 0)
    exp3 = lambda j, be, nu: (be[j], 0, 0)
    grid_spec = pltpu.PrefetchScalarGridSpec(
        num_scalar_prefetch=2,
        grid=(nblk,),
        in_specs=[
            pl.BlockSpec((blk, D), row),
            pl.BlockSpec((1, D, F2), exp3),
            pl.BlockSpec((1, 1, F2), exp3),
            pl.BlockSpec((1, d_ff, D), exp3),
            pl.BlockSpec((1, 1, D), exp3),
        ],
        out_specs=pl.BlockSpec((blk, D), row),
        scratch_shapes=[pltpu.VMEM((D, F2), BF16), pltpu.VMEM((d_ff, D), BF16)],
    )
    return pl.pallas_call(
        _expert_kernel,
        grid_spec=grid_spec,
        out_shape=jax.ShapeDtypeStruct((P, D), F32),
        compiler_params=pltpu.CompilerParams(
            dimension_semantics=("arbitrary",), vmem_limit_bytes=VMEM_LIMIT),
    )(block_exp, n_used, xs, w_gate_up, b_gate_up.reshape(E, 1, F2), w_down, b_down.reshape(E, 1, D))


def _combine_kernel(tope_ref, rank_ref, start_ref, x2_ref, gate_ref, g3_ref, b3_ref, y_hbm, o_ref, ybuf, sem):
    tc = x2_ref.shape[0]

    def row_copy(j, k):
        src = start_ref[tope_ref[k, j]] + rank_ref[k, j]
        return pltpu.make_async_copy(y_hbm.at[pl.ds(src, 1), :], ybuf.at[k, pl.ds(j, 1), :], sem)

    def issue(j, c):
        for k in range(TOP_K_EXPERTS):
            row_copy(j, k).start()
        return c
    lax.fori_loop(0, tc, issue, 0)

    def drain(j, c):
        for k in range(TOP_K_EXPERTS):
            row_copy(j, k).wait()
        return c
    lax.fori_loop(0, tc, drain, 0)

    moe = None
    for k in range(TOP_K_EXPERTS):
        t = ybuf[k] * gate_ref[:, k:k + 1]
        moe = t if moe is None else moe + t
    o_ref[...] = _layer_norm(DN_ALPHA * x2_ref[...] + moe, g3_ref[...], b3_ref[...])


def _combine(tope, rank, pad_start, x2, gates, g3, b3, y):
    K, T = tope.shape
    D = x2.shape[1]
    tc = min(COMBINE_TOKENS, T)
    smem_tok = pl.BlockSpec((K, tc), lambda i: (0, i), memory_space=pltpu.SMEM)
    row = lambda i: (i, 0)
    full = lambda i: (0, 0)
    return pl.pallas_call(
        _combine_kernel,
        grid=(T // tc,),
        in_specs=[smem_tok, smem_tok, pl.BlockSpec(memory_space=pltpu.SMEM),
                  pl.BlockSpec((tc, D), row), pl.BlockSpec((tc, LANES), row),
                  pl.BlockSpec(g3.shape, full), pl.BlockSpec(b3.shape, full),
                  pl.BlockSpec(memory_space=pl.ANY)],
        out_specs=pl.BlockSpec((tc, D), row),
        out_shape=jax.ShapeDtypeStruct((T, D), F32),
        scratch_shapes=[pltpu.VMEM((K, tc, D), F32), pltpu.SemaphoreType.DMA(())],
        compiler_params=pltpu.CompilerParams(
            dimension_semantics=("arbitrary",), vmem_limit_bytes=VMEM_LIMIT),
    )(tope, rank, pad_start, x2, gates, g3, b3, y)


def _pad_cols(w, width):
    return jnp.pad(w, ((0, 0), (0, width - w.shape[1])))


def _layer(x, mem, w_in, w_pool, pool_scale, ik_g, ik_b, kv_g, w_uk, w_uv, w_o, ln1_g, ln1_b,
           w_mq, w_mkv, w_mo, ln2_g, ln2_b, w_router, b_router, w_gate_up, b_gate_up, w_down, b_down,
           ln3_g, ln3_b):
    B, L, D = x.shape
    T = B * L
    row = lambda v: v.reshape(1, -1)

    o = 0
    pieces = []
    for width in (POOL_WIDTH, ATT_WIDTH, KV_RANK, IDX_HEADS * IDX_HEAD_DIM, IDX_HEAD_DIM, IDX_HEADS):
        pieces.append(w_in[:, o:o + width])
        o += width
    pieces[4] = _pad_cols(pieces[4], LANES)
    pieces[5] = _pad_cols(pieces[5], LANES)
    w_in_pad = jnp.concatenate(pieces, axis=1).astype(BF16)
    wuv_pad = jnp.zeros((ATT_HEADS, KV_RANK, ATT_WIDTH), F32)
    for h in range(ATT_HEADS):
        wuv_pad = wuv_pad.at[h, :, h * ATT_HEAD_DIM:(h + 1) * ATT_HEAD_DIM].set(w_uv[h])

    ypool, qlat, ckv, qidx, kidx, widx = _inproj(
        x, w_in_pad, w_pool.astype(BF16), row(pool_scale), row(kv_g), row(ik_g), row(ik_b), w_uk.astype(BF16))
    yatt = _dsa(qidx, widx, qlat, kidx, ckv, wuv_pad.astype(BF16))
    k_mem, v_mem = _memkv(mem, w_mkv.astype(BF16))
    x2, tope, gates = _mix(
        x.reshape(T, D), ypool.reshape(T, -1), yatt.reshape(T, -1), w_o.astype(BF16), row(ln1_g), row(ln1_b),
        k_mem, v_mem, w_mq.astype(BF16), w_mo.astype(BF16), row(ln2_g), row(ln2_b),
        jnp.transpose(w_router), b_router.reshape(-1, 1), B, L)

    rank, cnt = _rank(tope)
    counts = cnt[:, 0]
    blk = EXPERT_ROWS
    padded = (counts + blk - 1) // blk * blk
    pad_end = jnp.cumsum(padded)
    pad_start = pad_end - padded
    n_rows = T * TOP_K_EXPERTS + N_EXPERTS * blk
    nblk = n_rows // blk
    n_used = (pad_end[-1] // blk).astype(I32)
    blk_first = jnp.minimum(jnp.arange(nblk, dtype=I32), n_used - 1) * blk
    n_ended = jnp.sum((pad_end[None, :] <= blk_first[:, None]).astype(I32), axis=1)
    block_exp = jnp.minimum(n_ended, N_EXPERTS - 1).astype(I32)

    xs = _dispatch(tope, rank, pad_start.astype(I32), counts, padded.astype(I32), x2, n_rows)
    y = _experts(block_exp, n_used.reshape(1), xs, w_gate_up, b_gate_up, w_down, b_down)
    out = _combine(tope, rank, pad_start.astype(I32), x2, gates, row(ln3_g), row(ln3_b), y)
    return out.reshape(B, L, D)


def kernel(x, mem, w_in, w_pool, pool_scale, idx_k_norm_g, idx_k_norm_b, kv_norm_g, w_uk, w_uv, w_o, ln1_g, ln1_b, w_mq, w_mkv, w_mo, ln2_g, ln2_b, w_router, b_router, w_gate_up, b_gate_up, w_down, b_down, ln3_g, ln3_b):
    assert w_in.shape[0] == DEPTH
    return _layer(x, mem, w_in[0], w_pool[0], pool_scale[0], idx_k_norm_g[0], idx_k_norm_b[0], kv_norm_g[0],
                  w_uk[0], w_uv[0], w_o[0], ln1_g[0], ln1_b[0], w_mq[0], w_mkv[0], w_mo[0], ln2_g[0], ln2_b[0],
                  w_router[0], b_router[0], w_gate_up[0], b_gate_up[0], w_down[0], b_down[0], ln3_g[0], ln3_b[0])
```

```python
import functools

import jax
import jax.numpy as jnp
from jax import lax
from jax.experimental import pallas as pl
from jax.experimental.pallas import tpu as pltpu

F32 = jnp.float32
BF16 = jnp.bfloat16
I32 = jnp.int32

POOL_WINDOWS = (2, 4, 8, 16)
POOL_GROUP_DIM = 128
POOL_WIDTH = 512
ATT_HEADS = 8
ATT_HEAD_DIM = 64
ATT_WIDTH = 512
KV_RANK = 128
IDX_HEADS = 8
IDX_HEAD_DIM = 64
TOPK_MAX = 256
MEM_HEADS = 4
N_EXPERTS = 32
TOP_K_EXPERTS = 4
SWIGLU_LIMIT = 7.0
SWIGLU_ALPHA = 1.702
LN_EPS = 1e-5
RMS_EPS = 1e-6
DEPTH = 1
DN_ALPHA = (2 * DEPTH) ** 0.25

LANES = 128
MAX_POOL_WINDOW = 16
VMEM_LIMIT = 56 * 1024 * 1024

PROJ_ROWS = 512
DSA_QUERIES = 128
DSA_KEYS = 512
MIX_ROWS = 512
RANK_TOKENS = 512
DEST_TOKENS = 2048
DISPATCH_TOKENS = 512
ISSUE_UNROLL = 4
EXPERT_ROWS = 256
COMBINE_TOKENS = 256

NEG_BIG = -1e30
INT_MIN = -(2 ** 31)
NEG_INF_KEY = INT_MIN + 0x007FFFFF

_NT = (((1,), (1,)), ((), ()))


def _layer_norm(v, g, b):
    mu = jnp.mean(v, axis=-1, keepdims=True)
    d = v - mu
    var = jnp.mean(d * d, axis=-1, keepdims=True)
    return d * lax.rsqrt(var + LN_EPS) * g + b


_C_POOL = 0
_C_Q = 512
_C_CKV = 1024
_C_QIDX = 1152
_C_KIDX = 1664
_C_WIDX = 1792
_IN_PAD = 1920


def _inproj_kernel(x_ref, w_ref, wpool_ref, pscale_ref, kvg_ref, ikg_ref, ikb_ref, wuk_ref,
                   ypool_ref, qlat_ref, ckv_ref, qidx_ref, kidx_ref, widx_ref, ubuf):
    li = pl.program_id(1)
    tm = x_ref.shape[1]
    halo = MAX_POOL_WINDOW
    xb = x_ref[0].astype(BF16)

    u = jnp.dot(xb, w_ref[:, _C_POOL:_C_POOL + POOL_WIDTH], preferred_element_type=F32)

    @pl.when(li == 0)
    def _():
        ubuf[0:halo, :] = jnp.zeros((halo, POOL_WIDTH), F32)

    ubuf[halo:halo + tm, :] = u
    pos = li * tm + lax.broadcasted_iota(I32, (tm, 1), 0)
    for g, w in enumerate(POOL_WINDOWS):
        c0 = g * POOL_GROUP_DIM
        c1 = c0 + POOL_GROUP_DIM
        ug = ubuf[halo:halo + tm, c0:c1]
        s = ug
        for j in range(1, w):
            s = s + ubuf[halo - j:halo - j + tm, c0:c1]
        cnt = jnp.minimum(pos + 1, w).astype(F32)
        d = s / cnt - ug
        y = jnp.dot(d.astype(BF16), wpool_ref[g], preferred_element_type=F32) * pscale_ref[:, c0:c1]
        ypool_ref[0, :, c0:c1] = y.astype(BF16)
    ubuf[0:halo, :] = ubuf[tm:tm + halo, :]

    q = jnp.dot(xb, w_ref[:, _C_Q:_C_Q + ATT_WIDTH], preferred_element_type=F32)
    att_scale = ATT_HEAD_DIM ** -0.5
    for h in range(ATT_HEADS):
        qh = q[:, h * ATT_HEAD_DIM:(h + 1) * ATT_HEAD_DIM].astype(BF16)
        ql = jnp.dot(qh, wuk_ref[h], preferred_element_type=F32) * att_scale
        qlat_ref[0, h] = ql.astype(BF16)

    c = jnp.dot(xb, w_ref[:, _C_CKV:_C_CKV + KV_RANK], preferred_element_type=F32)
    c = c * lax.rsqrt(jnp.mean(c * c, axis=-1, keepdims=True) + RMS_EPS) * kvg_ref[...]
    ckv_ref[0, :, 0:KV_RANK] = c.astype(BF16)
    ckv_ref[0, :, KV_RANK:2 * KV_RANK] = jnp.ones((tm, KV_RANK), BF16)

    qi = jnp.dot(xb, w_ref[:, _C_QIDX:_C_QIDX + IDX_HEADS * IDX_HEAD_DIM], preferred_element_type=F32)
    for h in range(IDX_HEADS):
        qidx_ref[0, h] = qi[:, h * IDX_HEAD_DIM:(h + 1) * IDX_HEAD_DIM].astype(BF16)
    kw = jnp.dot(xb, w_ref[:, _C_KIDX:_C_KIDX + 2 * LANES], preferred_element_type=F32)
    kr = kw[:, 0:IDX_HEAD_DIM]
    kidx_ref[0] = _layer_norm(kr, ikg_ref[...], ikb_ref[...]).astype(BF16)
    widx_ref[0] = kw[:, LANES:2 * LANES]


def _inproj(x, w_in_pad, w_pool, pool_scale, kv_g, ik_g, ik_b, w_uk):
    B, L, D = x.shape
    tm = min(PROJ_ROWS, L)
    nl = L // tm
    full2 = lambda b, l: (0, 0)
    full3 = lambda b, l: (0, 0, 0)
    return pl.pallas_call(
        _inproj_kernel,
        grid=(B, nl),
        in_specs=[
            pl.BlockSpec((1, tm, D), lambda b, l: (b, l, 0)),
            pl.BlockSpec(w_in_pad.shape, full2),
            pl.BlockSpec(w_pool.shape, full3),
            pl.BlockSpec(pool_scale.shape, full2),
            pl.BlockSpec(kv_g.shape, full2),
            pl.BlockSpec(ik_g.shape, full2),
            pl.BlockSpec(ik_b.shape, full2),
            pl.BlockSpec(w_uk.shape, full3),
        ],
        out_specs=[
            pl.BlockSpec((1, tm, POOL_WIDTH), lambda b, l: (b, l, 0)),
            pl.BlockSpec((1, ATT_HEADS, tm, KV_RANK), lambda b, l: (b, 0, l, 0)),
            pl.BlockSpec((1, tm, 2 * KV_RANK), lambda b, l: (b, l, 0)),
            pl.BlockSpec((1, IDX_HEADS, tm, IDX_HEAD_DIM), lambda b, l: (b, 0, l, 0)),
            pl.BlockSpec((1, tm, IDX_HEAD_DIM), lambda b, l: (b, l, 0)),
            pl.BlockSpec((1, tm, LANES), lambda b, l: (b, l, 0)),
        ],
        out_shape=[
            jax.ShapeDtypeStruct((B, L, POOL_WIDTH), BF16),
            jax.ShapeDtypeStruct((B, ATT_HEADS, L, KV_RANK), BF16),
            jax.ShapeDtypeStruct((B, L, 2 * KV_RANK), BF16),
            jax.ShapeDtypeStruct((B, IDX_HEADS, L, IDX_HEAD_DIM), BF16),
            jax.ShapeDtypeStruct((B, L, IDX_HEAD_DIM), BF16),
            jax.ShapeDtypeStruct((B, L, LANES), F32),
        ],
        scratch_shapes=[pltpu.VMEM((MAX_POOL_WINDOW + tm, POOL_WIDTH), F32)],
        compiler_params=pltpu.CompilerParams(
            dimension_semantics=("arbitrary", "arbitrary"), vmem_limit_bytes=VMEM_LIMIT),
    )(x, w_in_pad, w_pool, pool_scale, kv_g, ik_g, ik_b, w_uk)


def _sortable(score):
    bits = pltpu.bitcast(score, I32)
    return bits ^ ((bits >> 31) & 0x7FFFFFFF)


def _dsa_kernel(qidx_ref, widx_ref, qlat_ref, kidx_ref, ckv_ref, wuv_ref, o_ref,
                key_ref, keyt_ref, wb_ref, p_ref, m_ref, acc_ref, sel_ref, *, top_k, idx_bits):
    qi = pl.program_id(1)
    H = ATT_HEADS
    qb = qidx_ref.shape[2]
    kc = wb_ref.shape[1]
    qstart = qi * qb
    n_ch = (qstart + qb + kc - 1) // kc
    idx_scale = (IDX_HEAD_DIM ** -0.5) * (IDX_HEADS ** -0.5)

    qpos = qstart + lax.broadcasted_iota(I32, (qb, 1), 0)
    qpos_t = qstart + lax.broadcasted_iota(I32, (1, qb), 1)
    lane_pos = lax.broadcasted_iota(I32, (1, kc), 1)
    row_pos = lax.broadcasted_iota(I32, (kc, 1), 0)

    qs = qidx_ref[0].reshape(H * qb, IDX_HEAD_DIM)
    for h in range(H):
        wb_ref[h * qb:(h + 1) * qb, :] = jnp.broadcast_to(widx_ref[0, :, h:h + 1], (qb, kc))

    def score_body(c, carry):
        off = pl.multiple_of(c * kc, kc)
        kk = kidx_ref[0, pl.ds(off, kc), :]
        s = lax.dot_general(qs, kk, _NT, preferred_element_type=F32)
        acc = None
        for h in range(H):
            term = jnp.maximum(s[h * qb:(h + 1) * qb, :], 0.0) * wb_ref[h * qb:(h + 1) * qb, :]
            acc = term if acc is None else acc + term
        score = acc * idx_scale
        score = jnp.where(off + lane_pos <= qpos, score, -jnp.inf)
        key_ref[c] = _sortable(score)
        keyt_ref[c] = _sortable(jnp.transpose(score))
        return carry

    lax.fori_loop(0, n_ch, score_body, 0)

    def count(indicator):
        def body(c, cnt):
            m = indicator(keyt_ref[c], c * kc + row_pos)
            parts = [m[j * 8:(j + 1) * 8, :] for j in range(kc // 8)]
            while len(parts) > 1:
                parts = [parts[j] + parts[j + 1] for j in range(0, len(parts), 2)]
            return cnt + parts[0]
        cnt = lax.fori_loop(0, n_ch, body, jnp.zeros((8, qb), I32))
        return jnp.sum(cnt, axis=0, keepdims=True)

    def bit_body(i, t):
        cand = t + lax.shift_left(jnp.int32(1), 31 - i)
        n_ge = count(lambda k, _: jnp.where(k >= cand, 1, 0))
        return jnp.where(n_ge >= top_k, cand, t)

    thr_t = lax.fori_loop(0, 32, bit_body, jnp.full((1, qb), INT_MIN, I32))

    n_gt = count(lambda k, _: jnp.where(k > thr_t, 1, 0))
    n_ge = count(lambda k, _: jnp.where(k >= thr_t, 1, 0))
    need = top_k - n_gt
    surplus = jnp.where(thr_t > NEG_INF_KEY, jnp.where(n_ge > top_k, 1, 0), 0)
    sel_ref[0:1, :] = thr_t
    sel_ref[1:2, :] = qpos_t

    @pl.when(jnp.max(surplus) > 0)
    def _():
        def tie_body(i, m):
            cand = m + lax.shift_left(jnp.int32(1), idx_bits - 1 - i)
            n_before = count(lambda k, p: jnp.where(k == thr_t, jnp.where(p < cand, 1, 0), 0))
            return jnp.where(n_before < need, cand, m)
        last = lax.fori_loop(0, idx_bits, tie_body, jnp.zeros((1, qb), I32))
        sel_ref[1:2, :] = jnp.where(surplus > 0, jnp.minimum(last, qpos_t), qpos_t)

    thr = jnp.transpose(jnp.broadcast_to(sel_ref[0:1, :], (qb, qb)))[:, 0:1]
    tie_last = jnp.transpose(jnp.broadcast_to(sel_ref[1:2, :], (qb, qb)))[:, 0:1]

    def chunk_bias(c):
        key = key_ref[c]
        kpos = c * kc + lane_pos
        tie_bias = jnp.where(key == thr, jnp.where(kpos <= tie_last, 0.0, NEG_BIG), NEG_BIG)
        return jnp.where(key > thr, 0.0, tie_bias)

    ql = qlat_ref[0].reshape(H * qb, KV_RANK)
    n_lt = kc // LANES

    m_ref[...] = jnp.full(m_ref.shape, NEG_BIG, F32)

    def max_body(c, carry):
        off = pl.multiple_of(c * kc, kc)
        ck = ckv_ref[0, pl.ds(off, kc), 0:KV_RANK]
        lg = lax.dot_general(ql, ck, _NT, preferred_element_type=F32)
        bias = chunk_bias(c)
        for h in range(H):
            r = slice(h * qb, (h + 1) * qb)
            zm = lg[r, 0:LANES] + bias[:, 0:LANES]
            for j in range(1, n_lt):
                zm = jnp.maximum(zm, lg[r, j * LANES:(j + 1) * LANES] + bias[:, j * LANES:(j + 1) * LANES])
            m_ref[r, :] = jnp.maximum(m_ref[r, :], zm)
        return carry

    lax.fori_loop(0, n_ch, max_body, 0)
    m_ref[...] = jnp.broadcast_to(jnp.max(m_ref[...], axis=1, keepdims=True), m_ref.shape)

    acc_ref[...] = jnp.zeros(acc_ref.shape, F32)

    def att_body(c, carry):
        off = pl.multiple_of(c * kc, kc)
        ckx = ckv_ref[0, pl.ds(off, kc), :]
        lg = lax.dot_general(ql, ckx[:, 0:KV_RANK], _NT, preferred_element_type=F32)
        bias = chunk_bias(c)
        for h in range(H):
            r = slice(h * qb, (h + 1) * qb)
            mh = m_ref[r, :]
            for j in range(n_lt):
                cs = slice(j * LANES, (j + 1) * LANES)
                p_ref[r, cs] = jnp.exp(lg[r, cs] + bias[:, cs] - mh).astype(BF16)
        acc_ref[...] = acc_ref[...] + jnp.dot(p_ref[...], ckx, preferred_element_type=F32)
        return carry

    lax.fori_loop(0, n_ch, att_body, 0)

    o_lat = (acc_ref[:, 0:KV_RANK] / acc_ref[:, KV_RANK:2 * KV_RANK]).astype(BF16)
    y = None
    for h in range(H):
        t = jnp.dot(o_lat[h * qb:(h + 1) * qb, :], wuv_ref[h], preferred_element_type=F32)
        y = t if y is None else y + t
    o_ref[0] = y.astype(BF16)


def _dsa(qidx, widx, qlat, kidx, ckv, wuv_pad):
    B, H, L, _ = qidx.shape
    qb = min(DSA_QUERIES, L)
    kc = min(DSA_KEYS, L)
    top_k = min(TOPK_MAX, L // 4)
    idx_bits = max(1, (L - 1).bit_length())
    kern = functools.partial(_dsa_kernel, top_k=top_k, idx_bits=idx_bits)
    return pl.pallas_call(
        kern,
        grid=(B, L // qb),
        in_specs=[
            pl.BlockSpec((1, H, qb, IDX_HEAD_DIM), lambda b, q: (b, 0, q, 0)),
            pl.BlockSpec((1, qb, LANES), lambda b, q: (b, q, 0)),
            pl.BlockSpec((1, H, qb, KV_RANK), lambda b, q: (b, 0, q, 0)),
            pl.BlockSpec((1, L, IDX_HEAD_DIM), lambda b, q: (b, 0, 0)),
            pl.BlockSpec((1, L, 2 * KV_RANK), lambda b, q: (b, 0, 0)),
            pl.BlockSpec(wuv_pad.shape, lambda b, q: (0, 0, 0)),
        ],
        out_specs=pl.BlockSpec((1, qb, ATT_WIDTH), lambda b, q: (b, q, 0)),
        out_shape=jax.ShapeDtypeStruct((B, L, ATT_WIDTH), BF16),
        scratch_shapes=[
            pltpu.VMEM((L // kc, qb, kc), I32),
            pltpu.VMEM((L // kc, kc, qb), I32),
            pltpu.VMEM((H * qb, kc), F32),
            pltpu.VMEM((H * qb, kc), BF16),
            pltpu.VMEM((H * qb, LANES), F32),
            pltpu.VMEM((H * qb, 2 * KV_RANK), F32),
            pltpu.VMEM((8, qb), I32),
        ],
        compiler_params=pltpu.CompilerParams(
            dimension_semantics=("arbitrary", "arbitrary"), vmem_limit_bytes=VMEM_LIMIT),
    )(qidx, widx, qlat, kidx, ckv, wuv_pad)


def _memkv_kernel(mem_ref, w_ref, k_ref, v_ref):
    d = k_ref.shape[2]
    kv = jnp.dot(mem_ref[0].astype(BF16), w_ref[...], preferred_element_type=F32)
    k_ref[0] = kv[:, 0:d].astype(BF16)
    v_ref[0] = kv[:, d:2 * d].astype(BF16)


def _memkv(mem, w_mkv):
    B, M, D = mem.shape
    return pl.pallas_call(
        _memkv_kernel,
        grid=(B,),
        in_specs=[pl.BlockSpec((1, M, D), lambda b: (b, 0, 0)),
                  pl.BlockSpec(w_mkv.shape, lambda b: (0, 0))],
        out_specs=[pl.BlockSpec((1, M, D), lambda b: (b, 0, 0)),
                   pl.BlockSpec((1, M, D), lambda b: (b, 0, 0))],
        out_shape=[jax.ShapeDtypeStruct((B, M, D), BF16), jax.ShapeDtypeStruct((B, M, D), BF16)],
        compiler_params=pltpu.CompilerParams(
            dimension_semantics=("arbitrary",), vmem_limit_bytes=VMEM_LIMIT),
    )(mem, w_mkv)


def _split3(v):
    hi = v.astype(BF16)
    r1 = v - hi.astype(F32)
    mid = r1.astype(BF16)
    lo = (r1 - mid.astype(F32)).astype(BF16)
    return hi, mid, lo


def _mix_kernel(x_ref, yp_ref, ya_ref, wo_ref, g1_ref, b1_ref, km_ref, vm_ref, wq_ref, wmo_ref,
                g2_ref, b2_ref, wr_ref, br_ref, x2_ref, tope_ref, gate_ref):
    tm, d = x_ref.shape
    hd = d // MEM_HEADS
    pw = yp_ref.shape[1]
    mix = jnp.dot(yp_ref[...], wo_ref[0:pw, :], preferred_element_type=F32)
    mix = mix + jnp.dot(ya_ref[...], wo_ref[pw:, :], preferred_element_type=F32)
    x1 = _layer_norm(DN_ALPHA * x_ref[...] + mix, g1_ref[...], b1_ref[...])

    q = jnp.dot(x1.astype(BF16), wq_ref[...], preferred_element_type=F32).astype(BF16)
    scale = hd ** -0.5
    att = None
    for h in range(MEM_HEADS):
        c = slice(h * hd, (h + 1) * hd)
        lg = lax.dot_general(q[:, c], km_ref[0, :, c], _NT, preferred_element_type=F32) * scale
        p = jnp.exp(lg - jnp.max(lg, axis=-1, keepdims=True))
        p = p / jnp.sum(p, axis=-1, keepdims=True)
        oh = jnp.dot(p.astype(BF16), vm_ref[0, :, c], preferred_element_type=F32).astype(BF16)
        t = jnp.dot(oh, wmo_ref[c, :], preferred_element_type=F32)
        att = t if att is None else att + t
    x2 = _layer_norm(DN_ALPHA * x1 + att, g2_ref[...], b2_ref[...])
    x2_ref[...] = x2

    xs = _split3(x2)
    ws = _split3(wr_ref[...])
    lt = None
    for i, j in ((0, 0), (0, 1), (1, 0)):
        t = lax.dot_general(ws[j], xs[i], _NT, preferred_element_type=F32)
        lt = t if lt is None else lt + t
    lt = lt + br_ref[...]
    n_e = lt.shape[0]
    eidx = lax.broadcasted_iota(I32, lt.shape, 0)
    vals, idxs = [], []
    for _ in range(TOP_K_EXPERTS):
        mx = jnp.max(lt, axis=0, keepdims=True)
        ix = jnp.min(jnp.where(lt == mx, eidx, n_e), axis=0, keepdims=True)
        vals.append(mx)
        idxs.append(ix)
        lt = jnp.where(eidx == ix, -jnp.inf, lt)
    tope_ref[...] = jnp.concatenate(idxs, axis=0)
    ex = [jnp.exp(v - vals[0]) for v in vals]
    den = ex[0]
    for e_ in ex[1:]:
        den = den + e_
    gates = jnp.concatenate([e_ / den for e_ in ex] + [jnp.zeros((LANES - TOP_K_EXPERTS, tm), F32)], axis=0)
    gate_ref[...] = jnp.transpose(gates)


def _mix(x2d, ypool, yatt, w_o, g1, b1, k_mem, v_mem, w_mq, w_mo, g2, b2, w_rt, b_r, B, L):
    T, D = x2d.shape
    tm = min(MIX_ROWS, L)
    nl = L // tm
    M = k_mem.shape[1]
    row = lambda i: (i, 0)
    full = lambda i: (0, 0)
    return pl.pallas_call(
        _mix_kernel,
        grid=(T // tm,),
        in_specs=[
            pl.BlockSpec((tm, D), row),
            pl.BlockSpec((tm, ypool.shape[1]), row),
            pl.BlockSpec((tm, yatt.shape[1]), row),
            pl.BlockSpec(w_o.shape, full),
            pl.BlockSpec(g1.shape, full),
            pl.BlockSpec(b1.shape, full),
            pl.BlockSpec((1, M, D), lambda i: (i // nl, 0, 0)),
            pl.BlockSpec((1, M, D), lambda i: (i // nl, 0, 0)),
            pl.BlockSpec(w_mq.shape, full),
            pl.BlockSpec(w_mo.shape, full),
            pl.BlockSpec(g2.shape, full),
            pl.BlockSpec(b2.shape, full),
            pl.BlockSpec(w_rt.shape, full),
            pl.BlockSpec(b_r.shape, full),
        ],
        out_specs=[
            pl.BlockSpec((tm, D), row),
            pl.BlockSpec((TOP_K_EXPERTS, tm), lambda i: (0, i)),
            pl.BlockSpec((tm, LANES), row),
        ],
        out_shape=[
            jax.ShapeDtypeStruct((T, D), F32),
            jax.ShapeDtypeStruct((TOP_K_EXPERTS, T), I32),
            jax.ShapeDtypeStruct((T, LANES), F32),
        ],
        compiler_params=pltpu.CompilerParams(
            dimension_semantics=("arbitrary",), vmem_limit_bytes=VMEM_LIMIT),
    )(x2d, ypool, yatt, w_o, g1, b1, k_mem, v_mem, w_mq, w_mo, g2, b2, w_rt, b_r)


def _rank_kernel(tope_ref, rank_ref, cnt_ref, carry_ref):
    i = pl.program_id(0)
    tr = tope_ref.shape[1]

    @pl.when(i == 0)
    def _():
        carry_ref[...] = jnp.zeros(carry_ref.shape, F32)

    eidx = lax.broadcasted_iota(I32, (N_EXPERTS, tr), 0)
    onehot = jnp.zeros((N_EXPERTS, tr), F32)
    for k in range(TOP_K_EXPERTS):
        onehot = onehot + jnp.where(eidx == tope_ref[k:k + 1, :], 1.0, 0.0)
    before = jnp.where(lax.broadcasted_iota(I32, (tr, tr), 0) < lax.broadcasted_iota(I32, (tr, tr), 1), 1.0, 0.0)
    excl = jnp.dot(onehot.astype(BF16), before.astype(BF16), preferred_element_type=F32)
    rank_full = excl + carry_ref[:, 0:1]
    rows = []
    for k in range(TOP_K_EXPERTS):
        rows.append(jnp.sum(jnp.where(eidx == tope_ref[k:k + 1, :], rank_full, 0.0), axis=0, keepdims=True))
    rank_ref[...] = jnp.concatenate(rows, axis=0).astype(I32)
    carry_ref[...] = carry_ref[...] + jnp.sum(onehot, axis=1, keepdims=True)
    cnt_ref[...] = carry_ref[...].astype(I32)


def _rank(tope):
    K, T = tope.shape
    tr = min(RANK_TOKENS, T)
    return pl.pallas_call(
        _rank_kernel,
        grid=(T // tr,),
        in_specs=[pl.BlockSpec((K, tr), lambda i: (0, i))],
        out_specs=[pl.BlockSpec((K, tr), lambda i: (0, i)),
                   pl.BlockSpec((N_EXPERTS, LANES), lambda i: (0, 0))],
        out_shape=[jax.ShapeDtypeStruct((K, T), I32), jax.ShapeDtypeStruct((N_EXPERTS, LANES), I32)],
        scratch_shapes=[pltpu.VMEM((N_EXPERTS, LANES), F32)],
        compiler_params=pltpu.CompilerParams(dimension_semantics=("arbitrary",)),
    )(tope)


def _dest_kernel(start_ref, tope_ref, rank_ref, dest_ref):
    tope = tope_ref[...]
    base = jnp.zeros(tope.shape, I32)
    for e in range(N_EXPERTS):
        base = jnp.where(tope == e, start_ref[e], base)
    dest_ref[...] = base + rank_ref[...]


def _dest(pad_start, tope, rank):
    K, T = tope.shape
    tt = min(DEST_TOKENS, T)
    blk = pl.BlockSpec((K, tt), lambda i: (0, i))
    return pl.pallas_call(
        _dest_kernel,
        grid=(T // tt,),
        in_specs=[pl.BlockSpec(memory_space=pltpu.SMEM), blk, blk],
        out_specs=blk,
        out_shape=jax.ShapeDtypeStruct((K, T), I32),
        compiler_params=pltpu.CompilerParams(dimension_semantics=("arbitrary",)),
    )(pad_start, tope, rank)


def _dispatch_kernel(dest_ref, start_ref, cnt_ref, padded_ref, x_ref, xs_hbm, zrow, sem, zsem):
    i = pl.program_id(0)
    td = dest_ref.shape[1]

    def row_copy(j, dst):
        return pltpu.make_async_copy(x_ref.at[pl.ds(j, 1), :], xs_hbm.at[pl.ds(dst, 1), :], sem)

    def zero_copy(dst):
        return pltpu.make_async_copy(zrow, xs_hbm.at[pl.ds(dst, 1), :], zsem)

    @pl.when(i == 0)
    def _():
        zrow[...] = jnp.zeros(zrow.shape, F32)
        for e in range(N_EXPERTS):
            first = start_ref[e] + cnt_ref[e]
            n_pad = padded_ref[e] - cnt_ref[e]

            def zstart(r, c):
                zero_copy(first + r).start()
                return c
            lax.fori_loop(0, n_pad, zstart, 0)

            def zwait(r, c):
                zero_copy(first + r).wait()
                return c
            lax.fori_loop(0, n_pad, zwait, 0)

    def issue(jj, c):
        for u in range(ISSUE_UNROLL):
            j = jj * ISSUE_UNROLL + u
            for k in range(TOP_K_EXPERTS):
                row_copy(j, dest_ref[k, j]).start()
        return c
    lax.fori_loop(0, td // ISSUE_UNROLL, issue, 0)

    def drain(jj, c):
        for _ in range(ISSUE_UNROLL * TOP_K_EXPERTS):
            row_copy(0, 0).wait()
        return c
    lax.fori_loop(0, td // ISSUE_UNROLL, drain, 0)


def _dispatch(dest, pad_start, counts, padded, x2, n_rows):
    K, T = dest.shape
    D = x2.shape[1]
    td = min(DISPATCH_TOKENS, T)
    smem_tok = pl.BlockSpec((K, td), lambda i: (0, i), memory_space=pltpu.SMEM)
    smem_full = pl.BlockSpec(memory_space=pltpu.SMEM)
    return pl.pallas_call(
        _dispatch_kernel,
        grid=(T // td,),
        in_specs=[smem_tok, smem_full, smem_full, smem_full,
                  pl.BlockSpec((td, D), lambda i: (i, 0))],
        out_specs=pl.BlockSpec(memory_space=pl.ANY),
        out_shape=jax.ShapeDtypeStruct((n_rows, D), F32),
        scratch_shapes=[pltpu.VMEM((1, D), F32), pltpu.SemaphoreType.DMA(()), pltpu.SemaphoreType.DMA(())],
        compiler_params=pltpu.CompilerParams(
            dimension_semantics=("arbitrary",), vmem_limit_bytes=VMEM_LIMIT),
    )(dest, pad_start, counts, padded, x2)


def _expert_kernel(bexp_ref, nused_ref, xs_ref, wgu_ref, bgu_ref, wd_ref, bd_ref, y_ref, wgu_bf, wd_bf):
    j = pl.program_id(0)
    d_ff = wd_ref.shape[1]

    @pl.when(j < nused_ref[0])
    def _():
        prev = bexp_ref[jnp.maximum(j - 1, 0)]

        @pl.when(jnp.logical_or(j == 0, bexp_ref[j] != prev))
        def _():
            wgu_bf[...] = wgu_ref[0].astype(BF16)
            wd_bf[...] = wd_ref[0].astype(BF16)

        xb = xs_ref[...].astype(BF16)
        gu = jnp.dot(xb, wgu_bf[...], preferred_element_type=F32) + bgu_ref[0]
        g = jnp.minimum(gu[:, 0:d_ff], SWIGLU_LIMIT)
        u = jnp.clip(gu[:, d_ff:2 * d_ff], -SWIGLU_LIMIT, SWIGLU_LIMIT)
        act = (u + 1.0) * g * (1.0 / (1.0 + jnp.exp(-SWIGLU_ALPHA * g)))
        y_ref[...] = jnp.dot(act.astype(BF16), wd_bf[...], preferred_element_type=F32) + bd_ref[0]


def _experts(block_exp, n_used, xs, w_gate_up, b_gate_up, w_down, b_down):
    P, D = xs.shape
    blk = EXPERT_ROWS
    nblk = P // blk
    E, _, F2 = w_gate_up.shape
    d_ff = w_down.shape[1]
    row = lambda j, be, nu: (jnp.minimum(j, nu[0] - 1), 0)
    exp3 = lambda j, be, nu: (be[j], 0, 0)
    grid_spec = pltpu.PrefetchScalarGridSpec(
        num_scalar_prefetch=2,
        grid=(nblk,),
        in_specs=[
            pl.BlockSpec((blk, D), row),
            pl.BlockSpec((1, D, F2), exp3),
            pl.BlockSpec((1, 1, F2), exp3),
            pl.BlockSpec((1, d_ff, D), exp3),
            pl.BlockSpec((1, 1, D), exp3),
        ],
        out_specs=pl.BlockSpec((blk, D), row),
        scratch_shapes=[pltpu.VMEM((D, F2), BF16), pltpu.VMEM((d_ff, D), BF16)],
    )
    return pl.pallas_call(
        _expert_kernel,
        grid_spec=grid_spec,
        out_shape=jax.ShapeDtypeStruct((P, D), F32),
        compiler_params=pltpu.CompilerParams(
            dimension_semantics=("arbitrary",), vmem_limit_bytes=VMEM_LIMIT),
    )(block_exp, n_used, xs, w_gate_up, b_gate_up.reshape(E, 1, F2), w_down, b_down.reshape(E, 1, D))


def _combine_kernel(dest_ref, dest_next_ref, x2_ref, gate_ref, g3_ref, b3_ref, y_hbm, o_ref, ybuf, sem):
    i = pl.program_id(0)
    n = pl.num_programs(0)
    tc = x2_ref.shape[0]
    slot = i % 2

    def row_copy(src_ref, s, j, k):
        return pltpu.make_async_copy(y_hbm.at[pl.ds(src_ref[k, j], 1), :], ybuf.at[s, k, pl.ds(j, 1), :], sem.at[s])

    def gather(src_ref, s):
        def issue(jj, c):
            for u in range(ISSUE_UNROLL):
                for k in range(TOP_K_EXPERTS):
                    row_copy(src_ref, s, jj * ISSUE_UNROLL + u, k).start()
            return c
        lax.fori_loop(0, tc // ISSUE_UNROLL, issue, 0)

    @pl.when(i == 0)
    def _():
        gather(dest_ref, 0)

    @pl.when(i + 1 < n)
    def _():
        gather(dest_next_ref, 1 - slot)

    def drain(jj, c):
        for _ in range(ISSUE_UNROLL * TOP_K_EXPERTS):
            pltpu.make_async_copy(y_hbm.at[pl.ds(0, 1), :], ybuf.at[slot, 0, pl.ds(0, 1), :], sem.at[slot]).wait()
        return c
    lax.fori_loop(0, tc // ISSUE_UNROLL, drain, 0)

    moe = None
    for k in range(TOP_K_EXPERTS):
        t = ybuf[slot, k] * gate_ref[:, k:k + 1]
        moe = t if moe is None else moe + t
    o_ref[...] = _layer_norm(DN_ALPHA * x2_ref[...] + moe, g3_ref[...], b3_ref[...])


def _combine(dest, x2, gates, g3, b3, y):
    K, T = dest.shape
    D = x2.shape[1]
    tc = min(COMBINE_TOKENS, T)
    n = T // tc
    row = lambda i: (i, 0)
    full = lambda i: (0, 0)
    return pl.pallas_call(
        _combine_kernel,
        grid=(n,),
        in_specs=[pl.BlockSpec((K, tc), lambda i: (0, i), memory_space=pltpu.SMEM),
                  pl.BlockSpec((K, tc), lambda i: (0, jnp.minimum(i + 1, n - 1)), memory_space=pltpu.SMEM),
                  pl.BlockSpec((tc, D), row), pl.BlockSpec((tc, LANES), row),
                  pl.BlockSpec(g3.shape, full), pl.BlockSpec(b3.shape, full),
                  pl.BlockSpec(memory_space=pl.ANY)],
        out_specs=pl.BlockSpec((tc, D), row),
        out_shape=jax.ShapeDtypeStruct((T, D), F32),
        scratch_shapes=[pltpu.VMEM((2, K, tc, D), F32), pltpu.SemaphoreType.DMA((2,))],
        compiler_params=pltpu.CompilerParams(
            dimension_semantics=("arbitrary",), vmem_limit_bytes=VMEM_LIMIT),
    )(dest, dest, x2, gates, g3, b3, y)


def _pad_cols(w, width):
    return jnp.pad(w, ((0, 0), (0, width - w.shape[1])))


def _layer(x, mem, w_in, w_pool, pool_scale, ik_g, ik_b, kv_g, w_uk, w_uv, w_o, ln1_g, ln1_b,
           w_mq, w_mkv, w_mo, ln2_g, ln2_b, w_router, b_router, w_gate_up, b_gate_up, w_down, b_down,
           ln3_g, ln3_b):
    B, L, D = x.shape
    T = B * L
    row = lambda v: v.reshape(1, -1)

    o = 0
    pieces = []
    for width in (POOL_WIDTH, ATT_WIDTH, KV_RANK, IDX_HEADS * IDX_HEAD_DIM, IDX_HEAD_DIM, IDX_HEADS):
        pieces.append(w_in[:, o:o + width])
        o += width
    pieces[4] = _pad_cols(pieces[4], LANES)
    pieces[5] = _pad_cols(pieces[5], LANES)
    w_in_pad = jnp.concatenate(pieces, axis=1).astype(BF16)
    wuv_pad = jnp.zeros((ATT_HEADS, KV_RANK, ATT_WIDTH), F32)
    for h in range(ATT_HEADS):
        wuv_pad = wuv_pad.at[h, :, h * ATT_HEAD_DIM:(h + 1) * ATT_HEAD_DIM].set(w_uv[h])

    ypool, qlat, ckv, qidx, kidx, widx = _inproj(
        x, w_in_pad, w_pool.astype(BF16), row(pool_scale), row(kv_g), row(ik_g), row(ik_b), w_uk.astype(BF16))
    yatt = _dsa(qidx, widx, qlat, kidx, ckv, wuv_pad.astype(BF16))
    k_mem, v_mem = _memkv(mem, w_mkv.astype(BF16))
    x2, tope, gates = _mix(
        x.reshape(T, D), ypool.reshape(T, -1), yatt.reshape(T, -1), w_o.astype(BF16), row(ln1_g), row(ln1_b),
        k_mem, v_mem, w_mq.astype(BF16), w_mo.astype(BF16), row(ln2_g), row(ln2_b),
        jnp.transpose(w_router), b_router.reshape(-1, 1), B, L)

    rank, cnt = _rank(tope)
    counts = cnt[:, 0]
    blk = EXPERT_ROWS
    padded = (counts + blk - 1) // blk * blk
    pad_end = jnp.cumsum(padded)
    pad_start = pad_end - padded
    n_rows = T * TOP_K_EXPERTS + N_EXPERTS * blk
    nblk = n_rows // blk
    n_used = (pad_end[-1] // blk).astype(I32)
    blk_first = jnp.minimum(jnp.arange(nblk, dtype=I32), n_used - 1) * blk
    n_ended = jnp.sum((pad_end[None, :] <= blk_first[:, None]).astype(I32), axis=1)
    block_exp = jnp.minimum(n_ended, N_EXPERTS - 1).astype(I32)

    pad_start = pad_start.astype(I32)
    dest = _dest(pad_start, tope, rank)
    xs = _dispatch(dest, pad_start, counts, padded.astype(I32), x2, n_rows)
    y = _experts(block_exp, n_used.reshape(1), xs, w_gate_up, b_gate_up, w_down, b_down)
    out = _combine(dest, x2, gates, row(ln3_g), row(ln3_b), y)
    return out.reshape(B, L, D)


def kernel(x, mem, w_in, w_pool, pool_scale, idx_k_norm_g, idx_k_norm_b, kv_norm_g, w_uk, w_uv, w_o, ln1_g, ln1_b, w_mq, w_mkv, w_mo, ln2_g, ln2_b, w_router, b_router, w_gate_up, b_gate_up, w_down, b_down, ln3_g, ln3_b):
    assert w_in.shape[0] == DEPTH
    return _layer(x, mem, w_in[0], w_pool[0], pool_scale[0], idx_k_norm_g[0], idx_k_norm_b[0], kv_norm_g[0],
                  w_uk[0], w_uv[0], w_o[0], ln1_g[0], ln1_b[0], w_mq[0], w_mkv[0], w_mo[0], ln2_g[0], ln2_b[0],
                  w_router[0], b_router[0], w_gate_up[0], b_gate_up[0], w_down[0], b_down[0], ln3_g[0], ln3_b[0])
```

```python
import functools

import jax
import jax.numpy as jnp
from jax import lax
from jax.experimental import pallas as pl
from jax.experimental.pallas import tpu as pltpu

F32 = jnp.float32
BF16 = jnp.bfloat16
I32 = jnp.int32

POOL_WINDOWS = (2, 4, 8, 16)
POOL_GROUP_DIM = 128
POOL_WIDTH = 512
ATT_HEADS = 8
ATT_HEAD_DIM = 64
ATT_WIDTH = 512
KV_RANK = 128
IDX_HEADS = 8
IDX_HEAD_DIM = 64
TOPK_MAX = 256
MEM_HEADS = 4
N_EXPERTS = 32
TOP_K_EXPERTS = 4
SWIGLU_LIMIT = 7.0
SWIGLU_ALPHA = 1.702
LN_EPS = 1e-5
RMS_EPS = 1e-6
DEPTH = 1
DN_ALPHA = (2 * DEPTH) ** 0.25

LANES = 128
MAX_POOL_WINDOW = 16
VMEM_LIMIT = 56 * 1024 * 1024

PROJ_ROWS = 512
DSA_QUERIES = 128
DSA_KEYS = 512
MIX_ROWS = 512
RANK_TOKENS = 512
DEST_TOKENS = 2048
DISPATCH_TOKENS = 512
ISSUE_UNROLL = 4
EXPERT_ROWS = 256
COMBINE_TOKENS = 256

NEG_BIG = -1e30
INT_MIN = -(2 ** 31)
NEG_INF_KEY = INT_MIN + 0x007FFFFF

_NT = (((1,), (1,)), ((), ()))


def _layer_norm(v, g, b):
    mu = jnp.mean(v, axis=-1, keepdims=True)
    d = v - mu
    var = jnp.mean(d * d, axis=-1, keepdims=True)
    return d * lax.rsqrt(var + LN_EPS) * g + b


_C_POOL = 0
_C_Q = 512
_C_CKV = 1024
_C_QIDX = 1152
_C_KIDX = 1664
_C_WIDX = 1792
_IN_PAD = 1920


def _inproj_kernel(x_ref, w_ref, wpool_ref, pscale_ref, kvg_ref, ikg_ref, ikb_ref, wuk_ref,
                   ypool_ref, qlat_ref, ckv_ref, qidx_ref, kidx_ref, widx_ref, ubuf):
    li = pl.program_id(1)
    tm = x_ref.shape[1]
    halo = MAX_POOL_WINDOW
    xb = x_ref[0].astype(BF16)

    u = jnp.dot(xb, w_ref[:, _C_POOL:_C_POOL + POOL_WIDTH], preferred_element_type=F32)

    @pl.when(li == 0)
    def _():
        ubuf[0:halo, :] = jnp.zeros((halo, POOL_WIDTH), F32)

    ubuf[halo:halo + tm, :] = u
    pos = li * tm + lax.broadcasted_iota(I32, (tm, 1), 0)
    for g, w in enumerate(POOL_WINDOWS):
        c0 = g * POOL_GROUP_DIM
        c1 = c0 + POOL_GROUP_DIM
        ug = ubuf[halo:halo + tm, c0:c1]
        s = ug
        for j in range(1, w):
            s = s + ubuf[halo - j:halo - j + tm, c0:c1]
        cnt = jnp.minimum(pos + 1, w).astype(F32)
        d = s / cnt - ug
        y = jnp.dot(d.astype(BF16), wpool_ref[g], preferred_element_type=F32) * pscale_ref[:, c0:c1]
        ypool_ref[0, :, c0:c1] = y.astype(BF16)
    ubuf[0:halo, :] = ubuf[tm:tm + halo, :]

    q = jnp.dot(xb, w_ref[:, _C_Q:_C_Q + ATT_WIDTH], preferred_element_type=F32)
    att_scale = ATT_HEAD_DIM ** -0.5
    for h in range(ATT_HEADS):
        qh = q[:, h * ATT_HEAD_DIM:(h + 1) * ATT_HEAD_DIM].astype(BF16)
        ql = jnp.dot(qh, wuk_ref[h], preferred_element_type=F32) * att_scale
        qlat_ref[0, h] = ql.astype(BF16)

    c = jnp.dot(xb, w_ref[:, _C_CKV:_C_CKV + KV_RANK], preferred_element_type=F32)
    c = c * lax.rsqrt(jnp.mean(c * c, axis=-1, keepdims=True) + RMS_EPS) * kvg_ref[...]
    ckv_ref[0, :, 0:KV_RANK] = c.astype(BF16)
    ckv_ref[0, :, KV_RANK:2 * KV_RANK] = jnp.ones((tm, KV_RANK), BF16)

    qi = jnp.dot(xb, w_ref[:, _C_QIDX:_C_QIDX + IDX_HEADS * IDX_HEAD_DIM], preferred_element_type=F32)
    for h in range(IDX_HEADS):
        qidx_ref[0, h] = qi[:, h * IDX_HEAD_DIM:(h + 1) * IDX_HEAD_DIM].astype(BF16)
    kw = jnp.dot(xb, w_ref[:, _C_KIDX:_C_KIDX + 2 * LANES], preferred_element_type=F32)
    kr = kw[:, 0:IDX_HEAD_DIM]
    kidx_ref[0] = _layer_norm(kr, ikg_ref[...], ikb_ref[...]).astype(BF16)
    widx_ref[0] = kw[:, LANES:2 * LANES]


def _inproj(x, w_in_pad, w_pool, pool_scale, kv_g, ik_g, ik_b, w_uk):
    B, L, D = x.shape
    tm = min(PROJ_ROWS, L)
    nl = L // tm
    full2 = lambda b, l: (0, 0)
    full3 = lambda b, l: (0, 0, 0)
    return pl.pallas_call(
        _inproj_kernel,
        grid=(B, nl),
        in_specs=[
            pl.BlockSpec((1, tm, D), lambda b, l: (b, l, 0)),
            pl.BlockSpec(w_in_pad.shape, full2),
            pl.BlockSpec(w_pool.shape, full3),
            pl.BlockSpec(pool_scale.shape, full2),
            pl.BlockSpec(kv_g.shape, full2),
            pl.BlockSpec(ik_g.shape, full2),
            pl.BlockSpec(ik_b.shape, full2),
            pl.BlockSpec(w_uk.shape, full3),
        ],
        out_specs=[
            pl.BlockSpec((1, tm, POOL_WIDTH), lambda b, l: (b, l, 0)),
            pl.BlockSpec((1, ATT_HEADS, tm, KV_RANK), lambda b, l: (b, 0, l, 0)),
            pl.BlockSpec((1, tm, 2 * KV_RANK), lambda b, l: (b, l, 0)),
            pl.BlockSpec((1, IDX_HEADS, tm, IDX_HEAD_DIM), lambda b, l: (b, 0, l, 0)),
            pl.BlockSpec((1, tm, IDX_HEAD_DIM), lambda b, l: (b, l, 0)),
            pl.BlockSpec((1, tm, LANES), lambda b, l: (b, l, 0)),
        ],
        out_shape=[
            jax.ShapeDtypeStruct((B, L, POOL_WIDTH), BF16),
            jax.ShapeDtypeStruct((B, ATT_HEADS, L, KV_RANK), BF16),
            jax.ShapeDtypeStruct((B, L, 2 * KV_RANK), BF16),
            jax.ShapeDtypeStruct((B, IDX_HEADS, L, IDX_HEAD_DIM), BF16),
            jax.ShapeDtypeStruct((B, L, IDX_HEAD_DIM), BF16),
            jax.ShapeDtypeStruct((B, L, LANES), F32),
        ],
        scratch_shapes=[pltpu.VMEM((MAX_POOL_WINDOW + tm, POOL_WIDTH), F32)],
        compiler_params=pltpu.CompilerParams(
            dimension_semantics=("arbitrary", "arbitrary"), vmem_limit_bytes=VMEM_LIMIT),
    )(x, w_in_pad, w_pool, pool_scale, kv_g, ik_g, ik_b, w_uk)


def _sortable(score):
    bits = pltpu.bitcast(score, I32)
    return bits ^ ((bits >> 31) & 0x7FFFFFFF)


def _dsa_kernel(qidx_ref, widx_ref, qlat_ref, kidx_ref, ckv_ref, wuv_ref, o_ref,
                key_ref, keyt_ref, wb_ref, p_ref, p2_ref, m_ref, acc_ref, sel_ref, *, top_k, idx_bits):
    qi = pl.program_id(1)
    H = ATT_HEADS
    qb = qidx_ref.shape[2]
    kc = wb_ref.shape[1]
    qstart = qi * qb
    n_ch = (qstart + qb + kc - 1) // kc
    idx_scale = (IDX_HEAD_DIM ** -0.5) * (IDX_HEADS ** -0.5)

    qpos = qstart + lax.broadcasted_iota(I32, (qb, 1), 0)
    qpos_t = qstart + lax.broadcasted_iota(I32, (1, qb), 1)
    lane_pos = lax.broadcasted_iota(I32, (1, kc), 1)
    row_pos = lax.broadcasted_iota(I32, (kc, 1), 0)

    for h in range(H):
        wb_ref[h * qb:(h + 1) * qb, :] = jnp.broadcast_to(widx_ref[0, :, h:h + 1], (qb, kc))

    def score_body(c, carry):
        off = pl.multiple_of(c * kc, kc)
        kk = kidx_ref[0, pl.ds(off, kc), :]
        acc = None
        for h in range(H):
            r = slice(h * qb, (h + 1) * qb)
            s = lax.dot_general(qidx_ref[0, h], kk, _NT, preferred_element_type=F32)
            term = jnp.maximum(s, 0.0) * wb_ref[r, :]
            acc = term if acc is None else acc + term
        score = acc * idx_scale
        score = jnp.where(off + lane_pos <= qpos, score, -jnp.inf)
        key_ref[c] = _sortable(score)
        keyt_ref[c] = _sortable(jnp.transpose(score))
        return carry

    lax.fori_loop(0, n_ch, score_body, 0)

    def count(indicator):
        def body(c, cnt):
            m = indicator(keyt_ref[c], c * kc + row_pos)
            parts = [m[j * 8:(j + 1) * 8, :] for j in range(kc // 8)]
            while len(parts) > 1:
                parts = [parts[j] + parts[j + 1] for j in range(0, len(parts), 2)]
            return cnt + parts[0]
        cnt = lax.fori_loop(0, n_ch, body, jnp.zeros((8, qb), I32))
        return jnp.sum(cnt, axis=0, keepdims=True)

    def bit_body(i, t):
        cand = t + lax.shift_left(jnp.int32(1), 31 - i)
        n_ge = count(lambda k, _: jnp.where(k >= cand, 1, 0))
        return jnp.where(n_ge >= top_k, cand, t)

    thr_t = lax.fori_loop(0, 32, bit_body, jnp.full((1, qb), INT_MIN, I32))

    n_gt = count(lambda k, _: jnp.where(k > thr_t, 1, 0))
    n_ge = count(lambda k, _: jnp.where(k >= thr_t, 1, 0))
    need = top_k - n_gt
    surplus = jnp.where(thr_t > NEG_INF_KEY, jnp.where(n_ge > top_k, 1, 0), 0)
    sel_ref[0:1, :] = thr_t
    sel_ref[1:2, :] = qpos_t

    @pl.when(jnp.max(surplus) > 0)
    def _():
        def tie_body(i, m):
            cand = m + lax.shift_left(jnp.int32(1), idx_bits - 1 - i)
            n_before = count(lambda k, p: jnp.where(k == thr_t, jnp.where(p < cand, 1, 0), 0))
            return jnp.where(n_before < need, cand, m)
        last = lax.fori_loop(0, idx_bits, tie_body, jnp.zeros((1, qb), I32))
        sel_ref[1:2, :] = jnp.where(surplus > 0, jnp.minimum(last, qpos_t), qpos_t)

    thr = jnp.transpose(jnp.broadcast_to(sel_ref[0:1, :], (qb, qb)))[:, 0:1]
    tie_last = jnp.transpose(jnp.broadcast_to(sel_ref[1:2, :], (qb, qb)))[:, 0:1]

    def chunk_bias(c):
        key = key_ref[c]
        kpos = c * kc + lane_pos
        tie_bias = jnp.where(key == thr, jnp.where(kpos <= tie_last, 0.0, NEG_BIG), NEG_BIG)
        return jnp.where(key > thr, 0.0, tie_bias)

    n_lt = kc // LANES

    def head_logits(h, ck):
        return lax.dot_general(qlat_ref[0, h], ck, _NT, preferred_element_type=F32)

    m_ref[...] = jnp.full(m_ref.shape, NEG_BIG, F32)

    def max_body(c, carry):
        off = pl.multiple_of(c * kc, kc)
        ck = ckv_ref[0, pl.ds(off, kc), 0:KV_RANK]
        bias = chunk_bias(c)
        for h in range(H):
            r = slice(h * qb, (h + 1) * qb)
            lg = head_logits(h, ck)
            zm = lg[:, 0:LANES] + bias[:, 0:LANES]
            for j in range(1, n_lt):
                zm = jnp.maximum(zm, lg[:, j * LANES:(j + 1) * LANES] + bias[:, j * LANES:(j + 1) * LANES])
            m_ref[r, :] = jnp.maximum(m_ref[r, :], zm)
        return carry

    lax.fori_loop(0, n_ch, max_body, 0)
    m_ref[...] = jnp.broadcast_to(jnp.max(m_ref[...], axis=1, keepdims=True), m_ref.shape)

    acc_ref[...] = jnp.zeros(acc_ref.shape, F32)

    def numerators(c, buf):
        off = pl.multiple_of(c * kc, kc)
        ck = ckv_ref[0, pl.ds(off, kc), 0:KV_RANK]
        bias = chunk_bias(c)
        for h in range(H):
            r = slice(h * qb, (h + 1) * qb)
            lg = head_logits(h, ck)
            mh = m_ref[r, :]
            for j in range(n_lt):
                cs = slice(j * LANES, (j + 1) * LANES)
                buf[r, cs] = jnp.exp(lg[:, cs] + bias[:, cs] - mh).astype(BF16)

    def accumulate(c, buf):
        off = pl.multiple_of(c * kc, kc)
        ckx = ckv_ref[0, pl.ds(off, kc), :]
        acc_ref[...] = acc_ref[...] + jnp.dot(buf[...], ckx, preferred_element_type=F32)

    numerators(0, p_ref)
    n_pairs = (n_ch - 1) // 2

    def pair_body(i, carry):
        c = 2 * i
        accumulate(c, p_ref)
        numerators(c + 1, p2_ref)
        accumulate(c + 1, p2_ref)
        numerators(c + 2, p_ref)
        return carry

    lax.fori_loop(0, n_pairs, pair_body, 0)
    last = 2 * n_pairs

    @pl.when(n_ch - last == 1)
    def _():
        accumulate(last, p_ref)

    @pl.when(n_ch - last == 2)
    def _():
        accumulate(last, p_ref)
        numerators(last + 1, p2_ref)
        accumulate(last + 1, p2_ref)

    o_lat = (acc_ref[:, 0:KV_RANK] / acc_ref[:, KV_RANK:2 * KV_RANK]).astype(BF16)
    for h in range(0, H, 2):
        t = jnp.dot(o_lat[h * qb:(h + 1) * qb, :], wuv_ref[h], preferred_element_type=F32)
        t = t + jnp.dot(o_lat[(h + 1) * qb:(h + 2) * qb, :], wuv_ref[h + 1], preferred_element_type=F32)
        o_ref[0, :, h * ATT_HEAD_DIM:(h + 2) * ATT_HEAD_DIM] = t.astype(BF16)


def _dsa(qidx, widx, qlat, kidx, ckv, wuv_pad):
    B, H, L, _ = qidx.shape
    qb = min(DSA_QUERIES, L)
    kc = min(DSA_KEYS, L)
    top_k = min(TOPK_MAX, L // 4)
    idx_bits = max(1, (L - 1).bit_length())
    kern = functools.partial(_dsa_kernel, top_k=top_k, idx_bits=idx_bits)
    return pl.pallas_call(
        kern,
        grid=(B, L // qb),
        in_specs=[
            pl.BlockSpec((1, H, qb, IDX_HEAD_DIM), lambda b, q: (b, 0, q, 0)),
            pl.BlockSpec((1, qb, LANES), lambda b, q: (b, q, 0)),
            pl.BlockSpec((1, H, qb, KV_RANK), lambda b, q: (b, 0, q, 0)),
            pl.BlockSpec((1, L, IDX_HEAD_DIM), lambda b, q: (b, 0, 0)),
            pl.BlockSpec((1, L, 2 * KV_RANK), lambda b, q: (b, 0, 0)),
            pl.BlockSpec(wuv_pad.shape, lambda b, q: (0, 0, 0)),
        ],
        out_specs=pl.BlockSpec((1, qb, ATT_WIDTH), lambda b, q: (b, q, 0)),
        out_shape=jax.ShapeDtypeStruct((B, L, ATT_WIDTH), BF16),
        scratch_shapes=[
            pltpu.VMEM((L // kc, qb, kc), I32),
            pltpu.VMEM((L // kc, kc, qb), I32),
            pltpu.VMEM((H * qb, kc), F32),
            pltpu.VMEM((H * qb, kc), BF16),
            pltpu.VMEM((H * qb, kc), BF16),
            pltpu.VMEM((H * qb, LANES), F32),
            pltpu.VMEM((H * qb, 2 * KV_RANK), F32),
            pltpu.VMEM((8, qb), I32),
        ],
        compiler_params=pltpu.CompilerParams(
            dimension_semantics=("arbitrary", "arbitrary"), vmem_limit_bytes=VMEM_LIMIT),
    )(qidx, widx, qlat, kidx, ckv, wuv_pad)


def _memkv_kernel(mem_ref, w_ref, k_ref, v_ref):
    d = k_ref.shape[2]
    kv = jnp.dot(mem_ref[0].astype(BF16), w_ref[...], preferred_element_type=F32)
    k_ref[0] = kv[:, 0:d].astype(BF16)
    v_ref[0] = kv[:, d:2 * d].astype(BF16)


def _memkv(mem, w_mkv):
    B, M, D = mem.shape
    return pl.pallas_call(
        _memkv_kernel,
        grid=(B,),
        in_specs=[pl.BlockSpec((1, M, D), lambda b: (b, 0, 0)),
                  pl.BlockSpec(w_mkv.shape, lambda b: (0, 0))],
        out_specs=[pl.BlockSpec((1, M, D), lambda b: (b, 0, 0)),
                   pl.BlockSpec((1, M, D), lambda b: (b, 0, 0))],
        out_shape=[jax.ShapeDtypeStruct((B, M, D), BF16), jax.ShapeDtypeStruct((B, M, D), BF16)],
        compiler_params=pltpu.CompilerParams(
            dimension_semantics=("arbitrary",), vmem_limit_bytes=VMEM_LIMIT),
    )(mem, w_mkv)


def _split3(v):
    hi = v.astype(BF16)
    r1 = v - hi.astype(F32)
    mid = r1.astype(BF16)
    lo = (r1 - mid.astype(F32)).astype(BF16)
    return hi, mid, lo


def _mix_kernel(x_ref, yp_ref, ya_ref, wo_ref, g1_ref, b1_ref, km_ref, vm_ref, wq_ref, wmo_ref,
                g2_ref, b2_ref, wr_ref, br_ref, x2_ref, tope_ref, gate_ref):
    tm, d = x_ref.shape
    hd = d // MEM_HEADS
    pw = yp_ref.shape[1]
    mix = jnp.dot(yp_ref[...], wo_ref[0:pw, :], preferred_element_type=F32)
    mix = mix + jnp.dot(ya_ref[...], wo_ref[pw:, :], preferred_element_type=F32)
    x1 = _layer_norm(DN_ALPHA * x_ref[...] + mix, g1_ref[...], b1_ref[...])

    q = jnp.dot(x1.astype(BF16), wq_ref[...], preferred_element_type=F32).astype(BF16)
    scale = hd ** -0.5
    att = None
    for h in range(MEM_HEADS):
        c = slice(h * hd, (h + 1) * hd)
        lg = lax.dot_general(q[:, c], km_ref[0, :, c], _NT, preferred_element_type=F32) * scale
        p = jnp.exp(lg - jnp.max(lg, axis=-1, keepdims=True))
        p = p / jnp.sum(p, axis=-1, keepdims=True)
        oh = jnp.dot(p.astype(BF16), vm_ref[0, :, c], preferred_element_type=F32).astype(BF16)
        t = jnp.dot(oh, wmo_ref[c, :], preferred_element_type=F32)
        att = t if att is None else att + t
    x2 = _layer_norm(DN_ALPHA * x1 + att, g2_ref[...], b2_ref[...])
    x2_ref[...] = x2

    xs = _split3(x2)
    ws = _split3(wr_ref[...])
    lt = None
    for i, j in ((0, 0), (0, 1), (1, 0)):
        t = lax.dot_general(ws[j], xs[i], _NT, preferred_element_type=F32)
        lt = t if lt is None else lt + t
    lt = lt + br_ref[...]
    n_e = lt.shape[0]
    eidx = lax.broadcasted_iota(I32, lt.shape, 0)
    vals, idxs = [], []
    for _ in range(TOP_K_EXPERTS):
        mx = jnp.max(lt, axis=0, keepdims=True)
        ix = jnp.min(jnp.where(lt == mx, eidx, n_e), axis=0, keepdims=True)
        vals.append(mx)
        idxs.append(ix)
        lt = jnp.where(eidx == ix, -jnp.inf, lt)
    tope_ref[...] = jnp.concatenate(idxs, axis=0)
    ex = [jnp.exp(v - vals[0]) for v in vals]
    den = ex[0]
    for e_ in ex[1:]:
        den = den + e_
    gates = jnp.concatenate([e_ / den for e_ in ex] + [jnp.zeros((LANES - TOP_K_EXPERTS, tm), F32)], axis=0)
    gate_ref[...] = jnp.transpose(gates)


def _mix(x2d, ypool, yatt, w_o, g1, b1, k_mem, v_mem, w_mq, w_mo, g2, b2, w_rt, b_r, B, L):
    T, D = x2d.shape
    tm = min(MIX_ROWS, L)
    nl = L // tm
    M = k_mem.shape[1]
    row = lambda i: (i, 0)
    full = lambda i: (0, 0)
    return pl.pallas_call(
        _mix_kernel,
        grid=(T // tm,),
        in_specs=[
            pl.BlockSpec((tm, D), row),
            pl.BlockSpec((tm, ypool.shape[1]), row),
            pl.BlockSpec((tm, yatt.shape[1]), row),
            pl.BlockSpec(w_o.shape, full),
            pl.BlockSpec(g1.shape, full),
            pl.BlockSpec(b1.shape, full),
            pl.BlockSpec((1, M, D), lambda i: (i // nl, 0, 0)),
            pl.BlockSpec((1, M, D), lambda i: (i // nl, 0, 0)),
            pl.BlockSpec(w_mq.shape, full),
            pl.BlockSpec(w_mo.shape, full),
            pl.BlockSpec(g2.shape, full),
            pl.BlockSpec(b2.shape, full),
            pl.BlockSpec(w_rt.shape, full),
            pl.BlockSpec(b_r.shape, full),
        ],
        out_specs=[
            pl.BlockSpec((tm, D), row),
            pl.BlockSpec((TOP_K_EXPERTS, tm), lambda i: (0, i)),
            pl.BlockSpec((tm, LANES), row),
        ],
        out_shape=[
            jax.ShapeDtypeStruct((T, D), F32),
            jax.ShapeDtypeStruct((TOP_K_EXPERTS, T), I32),
            jax.ShapeDtypeStruct((T, LANES), F32),
        ],
        compiler_params=pltpu.CompilerParams(
            dimension_semantics=("arbitrary",), vmem_limit_bytes=VMEM_LIMIT),
    )(x2d, ypool, yatt, w_o, g1, b1, k_mem, v_mem, w_mq, w_mo, g2, b2, w_rt, b_r)


def _rank_kernel(tope_ref, rank_ref, cnt_ref, carry_ref):
    i = pl.program_id(0)
    tr = tope_ref.shape[1]

    @pl.when(i == 0)
    def _():
        carry_ref[...] = jnp.zeros(carry_ref.shape, F32)

    eidx = lax.broadcasted_iota(I32, (N_EXPERTS, tr), 0)
    onehot = jnp.zeros((N_EXPERTS, tr), F32)
    for k in range(TOP_K_EXPERTS):
        onehot = onehot + jnp.where(eidx == tope_ref[k:k + 1, :], 1.0, 0.0)
    before = jnp.where(lax.broadcasted_iota(I32, (tr, tr), 0) < lax.broadcasted_iota(I32, (tr, tr), 1), 1.0, 0.0)
    excl = jnp.dot(onehot.astype(BF16), before.astype(BF16), preferred_element_type=F32)
    rank_full = excl + carry_ref[:, 0:1]
    rows = []
    for k in range(TOP_K_EXPERTS):
        rows.append(jnp.sum(jnp.where(eidx == tope_ref[k:k + 1, :], rank_full, 0.0), axis=0, keepdims=True))
    rank_ref[...] = jnp.concatenate(rows, axis=0).astype(I32)
    carry_ref[...] = carry_ref[...] + jnp.sum(onehot, axis=1, keepdims=True)
    cnt_ref[...] = carry_ref[...].astype(I32)


def _rank(tope):
    K, T = tope.shape
    tr = min(RANK_TOKENS, T)
    return pl.pallas_call(
        _rank_kernel,
        grid=(T // tr,),
        in_specs=[pl.BlockSpec((K, tr), lambda i: (0, i))],
        out_specs=[pl.BlockSpec((K, tr), lambda i: (0, i)),
                   pl.BlockSpec((N_EXPERTS, LANES), lambda i: (0, 0))],
        out_shape=[jax.ShapeDtypeStruct((K, T), I32), jax.ShapeDtypeStruct((N_EXPERTS, LANES), I32)],
        scratch_shapes=[pltpu.VMEM((N_EXPERTS, LANES), F32)],
        compiler_params=pltpu.CompilerParams(dimension_semantics=("arbitrary",)),
    )(tope)


def _dest_kernel(start_ref, tope_ref, rank_ref, dest_ref):
    tope = tope_ref[...]
    base = jnp.zeros(tope.shape, I32)
    for e in range(N_EXPERTS):
        base = jnp.where(tope == e, start_ref[e], base)
    dest_ref[...] = base + rank_ref[...]


def _dest(pad_start, tope, rank):
    K, T = tope.shape
    tt = min(DEST_TOKENS, T)
    blk = pl.BlockSpec((K, tt), lambda i: (0, i))
    return pl.pallas_call(
        _dest_kernel,
        grid=(T // tt,),
        in_specs=[pl.BlockSpec(memory_space=pltpu.SMEM), blk, blk],
        out_specs=blk,
        out_shape=jax.ShapeDtypeStruct((K, T), I32),
        compiler_params=pltpu.CompilerParams(dimension_semantics=("arbitrary",)),
    )(pad_start, tope, rank)


def _dispatch_kernel(dest_ref, start_ref, cnt_ref, padded_ref, x_ref, xs_hbm, zrow, sem, zsem):
    i = pl.program_id(0)
    td = dest_ref.shape[1]

    def row_copy(j, dst):
        return pltpu.make_async_copy(x_ref.at[pl.ds(j, 1), :], xs_hbm.at[pl.ds(dst, 1), :], sem)

    def zero_copy(dst):
        return pltpu.make_async_copy(zrow, xs_hbm.at[pl.ds(dst, 1), :], zsem)

    @pl.when(i == 0)
    def _():
        zrow[...] = jnp.zeros(zrow.shape, F32)
        for e in range(N_EXPERTS):
            first = start_ref[e] + cnt_ref[e]
            n_pad = padded_ref[e] - cnt_ref[e]

            def zstart(r, c):
                zero_copy(first + r).start()
                return c
            lax.fori_loop(0, n_pad, zstart, 0)

            def zwait(r, c):
                zero_copy(first + r).wait()
                return c
            lax.fori_loop(0, n_pad, zwait, 0)

    def issue(jj, c):
        for u in range(ISSUE_UNROLL):
            j = jj * ISSUE_UNROLL + u
            for k in range(TOP_K_EXPERTS):
                row_copy(j, dest_ref[k, j]).start()
        return c
    lax.fori_loop(0, td // ISSUE_UNROLL, issue, 0)

    def drain(jj, c):
        for _ in range(ISSUE_UNROLL * TOP_K_EXPERTS):
            row_copy(0, 0).wait()
        return c
    lax.fori_loop(0, td // ISSUE_UNROLL, drain, 0)


def _dispatch(dest, pad_start, counts, padded, x2, n_rows):
    K, T = dest.shape
    D = x2.shape[1]
    td = min(DISPATCH_TOKENS, T)
    smem_tok = pl.BlockSpec((K, td), lambda i: (0, i), memory_space=pltpu.SMEM)
    smem_full = pl.BlockSpec(memory_space=pltpu.SMEM)
    return pl.pallas_call(
        _dispatch_kernel,
        grid=(T // td,),
        in_specs=[smem_tok, smem_full, smem_full, smem_full,
                  pl.BlockSpec((td, D), lambda i: (i, 0))],
        out_specs=pl.BlockSpec(memory_space=pl.ANY),
        out_shape=jax.ShapeDtypeStruct((n_rows, D), F32),
        scratch_shapes=[pltpu.VMEM((1, D), F32), pltpu.SemaphoreType.DMA(()), pltpu.SemaphoreType.DMA(())],
        compiler_params=pltpu.CompilerParams(
            dimension_semantics=("arbitrary",), vmem_limit_bytes=VMEM_LIMIT),
    )(dest, pad_start, counts, padded, x2)


def _expert_kernel(bexp_ref, nused_ref, xs_ref, wgu_ref, bgu_ref, wd_ref, bd_ref, y_ref, wgu_bf, wd_bf):
    j = pl.program_id(0)
    d_ff = wd_ref.shape[1]

    @pl.when(j < nused_ref[0])
    def _():
        prev = bexp_ref[jnp.maximum(j - 1, 0)]

        @pl.when(jnp.logical_or(j == 0, bexp_ref[j] != prev))
        def _():
            wgu_bf[...] = wgu_ref[0].astype(BF16)
            wd_bf[...] = wd_ref[0].astype(BF16)

        xb = xs_ref[...].astype(BF16)
        gu = jnp.dot(xb, wgu_bf[...], preferred_element_type=F32) + bgu_ref[0]
        g = jnp.minimum(gu[:, 0:d_ff], SWIGLU_LIMIT)
        u = jnp.clip(gu[:, d_ff:2 * d_ff], -SWIGLU_LIMIT, SWIGLU_LIMIT)
        act = (u + 1.0) * g * (1.0 / (1.0 + jnp.exp(-SWIGLU_ALPHA * g)))
        y_ref[...] = jnp.dot(act.astype(BF16), wd_bf[...], preferred_element_type=F32) + bd_ref[0]


def _experts(block_exp, n_used, xs, w_gate_up, b_gate_up, w_down, b_down):
    P, D = xs.shape
    blk = EXPERT_ROWS
    nblk = P // blk
    E, _, F2 = w_gate_up.shape
    d_ff = w_down.shape[1]
    row = lambda j, be, nu: (jnp.minimum(j, nu[0] - 1), 0)
    exp3 = lambda j, be, nu: (be[j], 0, 0)
    grid_spec = pltpu.PrefetchScalarGridSpec(
        num_scalar_prefetch=2,
        grid=(nblk,),
        in_specs=[
            pl.BlockSpec((blk, D), row),
            pl.BlockSpec((1, D, F2), exp3),
            pl.BlockSpec((1, 1, F2), exp3),
            pl.BlockSpec((1, d_ff, D), exp3),
            pl.BlockSpec((1, 1, D), exp3),
        ],
        out_specs=pl.BlockSpec((blk, D), row),
        scratch_shapes=[pltpu.VMEM((D, F2), BF16), pltpu.VMEM((d_ff, D), BF16)],
    )
    return pl.pallas_call(
        _expert_kernel,
        grid_spec=grid_spec,
        out_shape=jax.ShapeDtypeStruct((P, D), F32),
        compiler_params=pltpu.CompilerParams(
            dimension_semantics=("arbitrary",), vmem_limit_bytes=VMEM_LIMIT),
    )(block_exp, n_used, xs, w_gate_up, b_gate_up.reshape(E, 1, F2), w_down, b_down.reshape(E, 1, D))


def _combine_kernel(dest_ref, dest_next_ref, x2_ref, gate_ref, g3_ref, b3_ref, y_hbm, o_ref, ybuf, sem):
    i = pl.program_id(0)
    n = pl.num_programs(0)
    tc = x2_ref.shape[0]
    slot = i % 2

    def row_copy(src_ref, s, j, k):
        return pltpu.make_async_copy(y_hbm.at[pl.ds(src_ref[k, j], 1), :], ybuf.at[s, k, pl.ds(j, 1), :], sem.at[s])

    def gather(src_ref, s):
        def issue(jj, c):
            for u in range(ISSUE_UNROLL):
                for k in range(TOP_K_EXPERTS):
                    row_copy(src_ref, s, jj * ISSUE_UNROLL + u, k).start()
            return c
        lax.fori_loop(0, tc // ISSUE_UNROLL, issue, 0)

    @pl.when(i == 0)
    def _():
        gather(dest_ref, 0)

    @pl.when(i + 1 < n)
    def _():
        gather(dest_next_ref, 1 - slot)

    def drain(jj, c):
        for _ in range(ISSUE_UNROLL * TOP_K_EXPERTS):
            pltpu.make_async_copy(y_hbm.at[pl.ds(0, 1), :], ybuf.at[slot, 0, pl.ds(0, 1), :], sem.at[slot]).wait()
        return c
    lax.fori_loop(0, tc // ISSUE_UNROLL, drain, 0)

    moe = None
    for k in range(TOP_K_EXPERTS):
        t = ybuf[slot, k] * gate_ref[:, k:k + 1]
        moe = t if moe is None else moe + t
    o_ref[...] = _layer_norm(DN_ALPHA * x2_ref[...] + moe, g3_ref[...], b3_ref[...])


def _combine(dest, x2, gates, g3, b3, y):
    K, T = dest.shape
    D = x2.shape[1]
    tc = min(COMBINE_TOKENS, T)
    n = T // tc
    row = lambda i: (i, 0)
    full = lambda i: (0, 0)
    return pl.pallas_call(
        _combine_kernel,
        grid=(n,),
        in_specs=[pl.BlockSpec((K, tc), lambda i: (0, i), memory_space=pltpu.SMEM),
                  pl.BlockSpec((K, tc), lambda i: (0, jnp.minimum(i + 1, n - 1)), memory_space=pltpu.SMEM),
                  pl.BlockSpec((tc, D), row), pl.BlockSpec((tc, LANES), row),
                  pl.BlockSpec(g3.shape, full), pl.BlockSpec(b3.shape, full),
                  pl.BlockSpec(memory_space=pl.ANY)],
        out_specs=pl.BlockSpec((tc, D), row),
        out_shape=jax.ShapeDtypeStruct((T, D), F32),
        scratch_shapes=[pltpu.VMEM((2, K, tc, D), F32), pltpu.SemaphoreType.DMA((2,))],
        compiler_params=pltpu.CompilerParams(
            dimension_semantics=("arbitrary",), vmem_limit_bytes=VMEM_LIMIT),
    )(dest, dest, x2, gates, g3, b3, y)


def _pad_cols(w, width):
    return jnp.pad(w, ((0, 0), (0, width - w.shape[1])))


def _layer(x, mem, w_in, w_pool, pool_scale, ik_g, ik_b, kv_g, w_uk, w_uv, w_o, ln1_g, ln1_b,
           w_mq, w_mkv, w_mo, ln2_g, ln2_b, w_router, b_router, w_gate_up, b_gate_up, w_down, b_down,
           ln3_g, ln3_b):
    B, L, D = x.shape
    T = B * L
    row = lambda v: v.reshape(1, -1)

    o = 0
    pieces = []
    for width in (POOL_WIDTH, ATT_WIDTH, KV_RANK, IDX_HEADS * IDX_HEAD_DIM, IDX_HEAD_DIM, IDX_HEADS):
        pieces.append(w_in[:, o:o + width])
        o += width
    pieces[4] = _pad_cols(pieces[4], LANES)
    pieces[5] = _pad_cols(pieces[5], LANES)
    w_in_pad = jnp.concatenate(pieces, axis=1).astype(BF16)
    wuv_pad = jnp.zeros((ATT_HEADS, KV_RANK, 2 * ATT_HEAD_DIM), F32)
    for h in range(ATT_HEADS):
        wuv_pad = wuv_pad.at[h, :, (h % 2) * ATT_HEAD_DIM:(h % 2 + 1) * ATT_HEAD_DIM].set(w_uv[h])

    ypool, qlat, ckv, qidx, kidx, widx = _inproj(
        x, w_in_pad, w_pool.astype(BF16), row(pool_scale), row(kv_g), row(ik_g), row(ik_b), w_uk.astype(BF16))
    yatt = _dsa(qidx, widx, qlat, kidx, ckv, wuv_pad.astype(BF16))
    k_mem, v_mem = _memkv(mem, w_mkv.astype(BF16))
    x2, tope, gates = _mix(
        x.reshape(T, D), ypool.reshape(T, -1), yatt.reshape(T, -1), w_o.astype(BF16), row(ln1_g), row(ln1_b),
        k_mem, v_mem, w_mq.astype(BF16), w_mo.astype(BF16), row(ln2_g), row(ln2_b),
        jnp.transpose(w_router), b_router.reshape(-1, 1), B, L)

    rank, cnt = _rank(tope)
    counts = cnt[:, 0]
    blk = EXPERT_ROWS
    padded = (counts + blk - 1) // blk * blk
    pad_end = jnp.cumsum(padded)
    pad_start = pad_end - padded
    n_rows = T * TOP_K_EXPERTS + N_EXPERTS * blk
    nblk = n_rows // blk
    n_used = (pad_end[-1] // blk).astype(I32)
    blk_first = jnp.minimum(jnp.arange(nblk, dtype=I32), n_used - 1) * blk
    n_ended = jnp.sum((pad_end[None, :] <= blk_first[:, None]).astype(I32), axis=1)
    block_exp = jnp.minimum(n_ended, N_EXPERTS - 1).astype(I32)

    pad_start = pad_start.astype(I32)
    dest = _dest(pad_start, tope, rank)
    xs = _dispatch(dest, pad_start, counts, padded.astype(I32), x2, n_rows)
    y = _experts(block_exp, n_used.reshape(1), xs, w_gate_up, b_gate_up, w_down, b_down)
    out = _combine(dest, x2, gates, row(ln3_g), row(ln3_b), y)
    return out.reshape(B, L, D)


def kernel(x, mem, w_in, w_pool, pool_scale, idx_k_norm_g, idx_k_norm_b, kv_norm_g, w_uk, w_uv, w_o, ln1_g, ln1_b, w_mq, w_mkv, w_mo, ln2_g, ln2_b, w_router, b_router, w_gate_up, b_gate_up, w_down, b_down, ln3_g, ln3_b):
    assert w_in.shape[0] == DEPTH
    return _layer(x, mem, w_in[0], w_pool[0], pool_scale[0], idx_k_norm_g[0], idx_k_norm_b[0], kv_norm_g[0],
                  w_uk[0], w_uv[0], w_o[0], ln1_g[0], ln1_b[0], w_mq[0], w_mkv[0], w_mo[0], ln2_g[0], ln2_b[0],
                  w_router[0], b_router[0], w_gate_up[0], b_gate_up[0], w_down[0], b_down[0], ln3_g[0], ln3_b[0])
```

```python
import functools

import jax
import jax.numpy as jnp
from jax import lax
from jax.experimental import pallas as pl
from jax.experimental.pallas import tpu as pltpu

F32 = jnp.float32
BF16 = jnp.bfloat16
I32 = jnp.int32

POOL_WINDOWS = (2, 4, 8, 16)
POOL_GROUP_DIM = 128
POOL_WIDTH = 512
ATT_HEADS = 8
ATT_HEAD_DIM = 64
ATT_WIDTH = 512
KV_RANK = 128
IDX_HEADS = 8
IDX_HEAD_DIM = 64
TOPK_MAX = 256
MEM_HEADS = 4
N_EXPERTS = 32
TOP_K_EXPERTS = 4
SWIGLU_LIMIT = 7.0
SWIGLU_ALPHA = 1.702
LN_EPS = 1e-5
RMS_EPS = 1e-6
DEPTH = 1
DN_ALPHA = (2 * DEPTH) ** 0.25

LANES = 128
MAX_POOL_WINDOW = 16
VMEM_LIMIT = 56 * 1024 * 1024

PROJ_ROWS = 512
DSA_QUERIES = 128
DSA_KEYS = 512
MIX_ROWS = 512
MIX_ROW_GROUPS = 1
RANK_TOKENS = 512
DEST_TOKENS = 2048
DISPATCH_TOKENS = 512
ISSUE_UNROLL = 4
EXPERT_ROWS = 256
COMBINE_TOKENS = 256

NEG_BIG = -1e30
INT_MIN = -(2 ** 31)
NEG_INF_KEY = INT_MIN + 0x007FFFFF

_NT = (((1,), (1,)), ((), ()))


def _layer_norm(v, g, b):
    mu = jnp.mean(v, axis=-1, keepdims=True)
    d = v - mu
    var = jnp.mean(d * d, axis=-1, keepdims=True)
    return d * lax.rsqrt(var + LN_EPS) * g + b


_C_POOL = 0
_C_Q = 512
_C_CKV = 1024
_C_QIDX = 1152
_C_KIDX = 1664
_C_WIDX = 1792
_IN_PAD = 1920


def _inproj_kernel(x_ref, w_ref, wpool_ref, pscale_ref, kvg_ref, ikg_ref, ikb_ref, wuk_ref,
                   ypool_ref, qlat_ref, ckv_ref, qidx_ref, kidx_ref, widx_ref, ubuf):
    li = pl.program_id(1)
    tm = x_ref.shape[1]
    halo = MAX_POOL_WINDOW
    xb = x_ref[0].astype(BF16)

    u = jnp.dot(xb, w_ref[:, _C_POOL:_C_POOL + POOL_WIDTH], preferred_element_type=F32)

    @pl.when(li == 0)
    def _():
        ubuf[0:halo, :] = jnp.zeros((halo, POOL_WIDTH), F32)

    ubuf[halo:halo + tm, :] = u
    pos = li * tm + lax.broadcasted_iota(I32, (tm, 1), 0)
    for g, w in enumerate(POOL_WINDOWS):
        c0 = g * POOL_GROUP_DIM
        c1 = c0 + POOL_GROUP_DIM
        ug = ubuf[halo:halo + tm, c0:c1]
        s = ug
        for j in range(1, w):
            s = s + ubuf[halo - j:halo - j + tm, c0:c1]
        cnt = jnp.minimum(pos + 1, w).astype(F32)
        d = s / cnt - ug
        y = jnp.dot(d.astype(BF16), wpool_ref[g], preferred_element_type=F32) * pscale_ref[:, c0:c1]
        ypool_ref[0, :, c0:c1] = y.astype(BF16)
    ubuf[0:halo, :] = ubuf[tm:tm + halo, :]

    q = jnp.dot(xb, w_ref[:, _C_Q:_C_Q + ATT_WIDTH], preferred_element_type=F32)
    att_scale = ATT_HEAD_DIM ** -0.5
    for h in range(ATT_HEADS):
        qh = q[:, h * ATT_HEAD_DIM:(h + 1) * ATT_HEAD_DIM].astype(BF16)
        ql = jnp.dot(qh, wuk_ref[h], preferred_element_type=F32) * att_scale
        qlat_ref[0, h] = ql.astype(BF16)

    c = jnp.dot(xb, w_ref[:, _C_CKV:_C_CKV + KV_RANK], preferred_element_type=F32)
    c = c * lax.rsqrt(jnp.mean(c * c, axis=-1, keepdims=True) + RMS_EPS) * kvg_ref[...]
    ckv_ref[0, :, 0:KV_RANK] = c.astype(BF16)
    ckv_ref[0, :, KV_RANK:2 * KV_RANK] = jnp.ones((tm, KV_RANK), BF16)

    qi = jnp.dot(xb, w_ref[:, _C_QIDX:_C_QIDX + IDX_HEADS * IDX_HEAD_DIM], preferred_element_type=F32)
    for h in range(IDX_HEADS):
        qidx_ref[0, h] = qi[:, h * IDX_HEAD_DIM:(h + 1) * IDX_HEAD_DIM].astype(BF16)
    kw = jnp.dot(xb, w_ref[:, _C_KIDX:_C_KIDX + 2 * LANES], preferred_element_type=F32)
    kr = kw[:, 0:IDX_HEAD_DIM]
    kidx_ref[0] = _layer_norm(kr, ikg_ref[...], ikb_ref[...]).astype(BF16)
    widx_ref[0] = kw[:, LANES:2 * LANES]


def _inproj(x, w_in_pad, w_pool, pool_scale, kv_g, ik_g, ik_b, w_uk):
    B, L, D = x.shape
    tm = min(PROJ_ROWS, L)
    nl = L // tm
    full2 = lambda b, l: (0, 0)
    full3 = lambda b, l: (0, 0, 0)
    return pl.pallas_call(
        _inproj_kernel,
        grid=(B, nl),
        in_specs=[
            pl.BlockSpec((1, tm, D), lambda b, l: (b, l, 0)),
            pl.BlockSpec(w_in_pad.shape, full2),
            pl.BlockSpec(w_pool.shape, full3),
            pl.BlockSpec(pool_scale.shape, full2),
            pl.BlockSpec(kv_g.shape, full2),
            pl.BlockSpec(ik_g.shape, full2),
            pl.BlockSpec(ik_b.shape, full2),
            pl.BlockSpec(w_uk.shape, full3),
        ],
        out_specs=[
            pl.BlockSpec((1, tm, POOL_WIDTH), lambda b, l: (b, l, 0)),
            pl.BlockSpec((1, ATT_HEADS, tm, KV_RANK), lambda b, l: (b, 0, l, 0)),
            pl.BlockSpec((1, tm, 2 * KV_RANK), lambda b, l: (b, l, 0)),
            pl.BlockSpec((1, IDX_HEADS, tm, IDX_HEAD_DIM), lambda b, l: (b, 0, l, 0)),
            pl.BlockSpec((1, tm, IDX_HEAD_DIM), lambda b, l: (b, l, 0)),
            pl.BlockSpec((1, tm, LANES), lambda b, l: (b, l, 0)),
        ],
        out_shape=[
            jax.ShapeDtypeStruct((B, L, POOL_WIDTH), BF16),
            jax.ShapeDtypeStruct((B, ATT_HEADS, L, KV_RANK), BF16),
            jax.ShapeDtypeStruct((B, L, 2 * KV_RANK), BF16),
            jax.ShapeDtypeStruct((B, IDX_HEADS, L, IDX_HEAD_DIM), BF16),
            jax.ShapeDtypeStruct((B, L, IDX_HEAD_DIM), BF16),
            jax.ShapeDtypeStruct((B, L, LANES), F32),
        ],
        scratch_shapes=[pltpu.VMEM((MAX_POOL_WINDOW + tm, POOL_WIDTH), F32)],
        compiler_params=pltpu.CompilerParams(
            dimension_semantics=("arbitrary", "arbitrary"), vmem_limit_bytes=VMEM_LIMIT),
    )(x, w_in_pad, w_pool, pool_scale, kv_g, ik_g, ik_b, w_uk)


def _sortable(score):
    bits = pltpu.bitcast(score, I32)
    return bits ^ ((bits >> 31) & 0x7FFFFFFF)


def _dsa_kernel(qidx_ref, widx_ref, qlat_ref, kidx_ref, ckv_ref, wuv_ref, o_ref,
                key_ref, keyt_ref, wb_ref, s_ref, s2_ref, p_ref, p2_ref, m_ref, acc_ref, sel_ref,
                *, top_k, idx_bits):
    qi = pl.program_id(1)
    H = ATT_HEADS
    qb = qidx_ref.shape[2]
    kc = s_ref.shape[1]
    qstart = qi * qb
    n_ch = (qstart + qb + kc - 1) // kc
    idx_scale = (IDX_HEAD_DIM ** -0.5) * (IDX_HEADS ** -0.5)

    qpos = qstart + lax.broadcasted_iota(I32, (qb, 1), 0)
    qpos_t = qstart + lax.broadcasted_iota(I32, (1, qb), 1)
    lane_pos = lax.broadcasted_iota(I32, (1, kc), 1)
    row_pos = lax.broadcasted_iota(I32, (kc, 1), 0)

    for h in range(H):
        wb_ref[h * qb:(h + 1) * qb, :] = jnp.broadcast_to(widx_ref[0, :, h:h + 1], (qb, LANES))

    def pipelined(produce, consume, buf_a, buf_b):
        produce(0, buf_a)
        n_pairs = (n_ch - 1) // 2

        def pair_body(i, carry):
            c = 2 * i
            consume(c, buf_a)
            produce(c + 1, buf_b)
            consume(c + 1, buf_b)
            produce(c + 2, buf_a)
            return carry

        lax.fori_loop(0, n_pairs, pair_body, 0)
        last = 2 * n_pairs

        @pl.when(n_ch - last == 1)
        def _():
            consume(last, buf_a)

        @pl.when(n_ch - last == 2)
        def _():
            consume(last, buf_a)
            produce(last + 1, buf_b)
            consume(last + 1, buf_b)

    def head_scores(c, buf):
        off = pl.multiple_of(c * kc, kc)
        kk = kidx_ref[0, pl.ds(off, kc), :]
        qs = qidx_ref[0].reshape(H * qb, IDX_HEAD_DIM)
        buf[...] = lax.dot_general(qs, kk, _NT, preferred_element_type=F32)

    def chunk_keys(c, buf):
        for j in range(kc // LANES):
            cs = slice(j * LANES, (j + 1) * LANES)
            acc = None
            for h in range(H):
                r = slice(h * qb, (h + 1) * qb)
                term = jnp.maximum(buf[r, cs], 0.0) * wb_ref[r, 0:LANES]
                acc = term if acc is None else acc + term
            score = acc * idx_scale
            score = jnp.where(c * kc + j * LANES + lane_pos[:, 0:LANES] <= qpos, score, -jnp.inf)
            key_ref[c, :, cs] = _sortable(score)
            keyt_ref[c, cs, :] = _sortable(jnp.transpose(score))

    pipelined(head_scores, chunk_keys, s_ref, s2_ref)

    def count(indicator):
        def body(c, cnt):
            m = indicator(keyt_ref[c], c * kc + row_pos)
            parts = [m[j * 8:(j + 1) * 8, :] for j in range(kc // 8)]
            while len(parts) > 1:
                parts = [parts[j] + parts[j + 1] for j in range(0, len(parts), 2)]
            return cnt + parts[0]
        cnt = lax.fori_loop(0, n_ch, body, jnp.zeros((8, qb), I32))
        return jnp.sum(cnt, axis=0, keepdims=True)

    def bit_body(i, t):
        cand = t + lax.shift_left(jnp.int32(1), 31 - i)
        n_ge = count(lambda k, _: jnp.where(k >= cand, 1, 0))
        return jnp.where(n_ge >= top_k, cand, t)

    thr_t = lax.fori_loop(0, 32, bit_body, jnp.full((1, qb), INT_MIN, I32))

    n_gt = count(lambda k, _: jnp.where(k > thr_t, 1, 0))
    n_ge = count(lambda k, _: jnp.where(k >= thr_t, 1, 0))
    need = top_k - n_gt
    surplus = jnp.where(thr_t > NEG_INF_KEY, jnp.where(n_ge > top_k, 1, 0), 0)
    sel_ref[0:1, :] = thr_t
    sel_ref[1:2, :] = qpos_t

    @pl.when(jnp.max(surplus) > 0)
    def _():
        def tie_body(i, m):
            cand = m + lax.shift_left(jnp.int32(1), idx_bits - 1 - i)
            n_before = count(lambda k, p: jnp.where(k == thr_t, jnp.where(p < cand, 1, 0), 0))
            return jnp.where(n_before < need, cand, m)
        last = lax.fori_loop(0, idx_bits, tie_body, jnp.zeros((1, qb), I32))
        sel_ref[1:2, :] = jnp.where(surplus > 0, jnp.minimum(last, qpos_t), qpos_t)

    thr = jnp.transpose(jnp.broadcast_to(sel_ref[0:1, :], (qb, qb)))[:, 0:1]
    tie_last = jnp.transpose(jnp.broadcast_to(sel_ref[1:2, :], (qb, qb)))[:, 0:1]

    def chunk_bias(c):
        key = key_ref[c]
        kpos = c * kc + lane_pos
        tie_bias = jnp.where(key == thr, jnp.where(kpos <= tie_last, 0.0, NEG_BIG), NEG_BIG)
        return jnp.where(key > thr, 0.0, tie_bias)

    n_lt = kc // LANES

    def head_logits(h, ck):
        return lax.dot_general(qlat_ref[0, h], ck, _NT, preferred_element_type=F32)

    m_ref[...] = jnp.full(m_ref.shape, NEG_BIG, F32)

    def all_logits(c, buf):
        off = pl.multiple_of(c * kc, kc)
        ck = ckv_ref[0, pl.ds(off, kc), 0:KV_RANK]
        for h in range(H):
            buf[h * qb:(h + 1) * qb, :] = head_logits(h, ck)

    def running_max(c, buf):
        bias = chunk_bias(c)
        for h in range(H):
            r = slice(h * qb, (h + 1) * qb)
            zm = buf[r, 0:LANES] + bias[:, 0:LANES]
            for j in range(1, n_lt):
                zm = jnp.maximum(zm, buf[r, j * LANES:(j + 1) * LANES] + bias[:, j * LANES:(j + 1) * LANES])
            m_ref[r, :] = jnp.maximum(m_ref[r, :], zm)

    pipelined(all_logits, running_max, s_ref, s2_ref)
    m_ref[...] = jnp.broadcast_to(jnp.max(m_ref[...], axis=1, keepdims=True), m_ref.shape)

    acc_ref[...] = jnp.zeros(acc_ref.shape, F32)

    def numerators(c, buf):
        off = pl.multiple_of(c * kc, kc)
        ck = ckv_ref[0, pl.ds(off, kc), 0:KV_RANK]
        bias = chunk_bias(c)
        for h in range(H):
            r = slice(h * qb, (h + 1) * qb)
            lg = head_logits(h, ck)
            mh = m_ref[r, :]
            for j in range(n_lt):
                cs = slice(j * LANES, (j + 1) * LANES)
                buf[r, cs] = jnp.exp(lg[:, cs] + bias[:, cs] - mh).astype(BF16)

    def accumulate(c, buf):
        off = pl.multiple_of(c * kc, kc)
        ckx = ckv_ref[0, pl.ds(off, kc), :]
        acc_ref[...] = acc_ref[...] + jnp.dot(buf[...], ckx, preferred_element_type=F32)

    pipelined(numerators, accumulate, p_ref, p2_ref)

    o_lat = (acc_ref[:, 0:KV_RANK] / acc_ref[:, KV_RANK:2 * KV_RANK]).astype(BF16)
    for h in range(0, H, 2):
        t = jnp.dot(o_lat[h * qb:(h + 1) * qb, :], wuv_ref[h], preferred_element_type=F32)
        t = t + jnp.dot(o_lat[(h + 1) * qb:(h + 2) * qb, :], wuv_ref[h + 1], preferred_element_type=F32)
        o_ref[0, :, h * ATT_HEAD_DIM:(h + 2) * ATT_HEAD_DIM] = t.astype(BF16)


def _dsa(qidx, widx, qlat, kidx, ckv, wuv_pad):
    B, H, L, _ = qidx.shape
    qb = min(DSA_QUERIES, L)
    kc = min(DSA_KEYS, L)
    top_k = min(TOPK_MAX, L // 4)
    idx_bits = max(1, (L - 1).bit_length())
    kern = functools.partial(_dsa_kernel, top_k=top_k, idx_bits=idx_bits)
    return pl.pallas_call(
        kern,
        grid=(B, L // qb),
        in_specs=[
            pl.BlockSpec((1, H, qb, IDX_HEAD_DIM), lambda b, q: (b, 0, q, 0)),
            pl.BlockSpec((1, qb, LANES), lambda b, q: (b, q, 0)),
            pl.BlockSpec((1, H, qb, KV_RANK), lambda b, q: (b, 0, q, 0)),
            pl.BlockSpec((1, L, IDX_HEAD_DIM), lambda b, q: (b, 0, 0)),
            pl.BlockSpec((1, L, 2 * KV_RANK), lambda b, q: (b, 0, 0)),
            pl.BlockSpec(wuv_pad.shape, lambda b, q: (0, 0, 0)),
        ],
        out_specs=pl.BlockSpec((1, qb, ATT_WIDTH), lambda b, q: (b, q, 0)),
        out_shape=jax.ShapeDtypeStruct((B, L, ATT_WIDTH), BF16),
        scratch_shapes=[
            pltpu.VMEM((L // kc, qb, kc), I32),
            pltpu.VMEM((L // kc, kc, qb), I32),
            pltpu.VMEM((H * qb, LANES), F32),
            pltpu.VMEM((H * qb, kc), F32),
            pltpu.VMEM((H * qb, kc), F32),
            pltpu.VMEM((H * qb, kc), BF16),
            pltpu.VMEM((H * qb, kc), BF16),
            pltpu.VMEM((H * qb, LANES), F32),
            pltpu.VMEM((H * qb, 2 * KV_RANK), F32),
            pltpu.VMEM((8, qb), I32),
        ],
        compiler_params=pltpu.CompilerParams(
            dimension_semantics=("arbitrary", "arbitrary"), vmem_limit_bytes=VMEM_LIMIT),
    )(qidx, widx, qlat, kidx, ckv, wuv_pad)


def _memkv_kernel(mem_ref, w_ref, k_ref, v_ref):
    d = k_ref.shape[2]
    kv = jnp.dot(mem_ref[0].astype(BF16), w_ref[...], preferred_element_type=F32)
    k_ref[0] = kv[:, 0:d].astype(BF16)
    v_ref[0] = kv[:, d:2 * d].astype(BF16)


def _memkv(mem, w_mkv):
    B, M, D = mem.shape
    return pl.pallas_call(
        _memkv_kernel,
        grid=(B,),
        in_specs=[pl.BlockSpec((1, M, D), lambda b: (b, 0, 0)),
                  pl.BlockSpec(w_mkv.shape, lambda b: (0, 0))],
        out_specs=[pl.BlockSpec((1, M, D), lambda b: (b, 0, 0)),
                   pl.BlockSpec((1, M, D), lambda b: (b, 0, 0))],
        out_shape=[jax.ShapeDtypeStruct((B, M, D), BF16), jax.ShapeDtypeStruct((B, M, D), BF16)],
        compiler_params=pltpu.CompilerParams(
            dimension_semantics=("arbitrary",), vmem_limit_bytes=VMEM_LIMIT),
    )(mem, w_mkv)


def _split3(v):
    hi = v.astype(BF16)
    r1 = v - hi.astype(F32)
    mid = r1.astype(BF16)
    lo = (r1 - mid.astype(F32)).astype(BF16)
    return hi, mid, lo


def _mix_kernel(x_ref, yp_ref, ya_ref, wo_ref, g1_ref, b1_ref, km_ref, vm_ref, wq_ref, wmo_ref,
                g2_ref, b2_ref, wr_ref, br_ref, x2_ref, tope_ref, gate_ref):
    tm, d = x_ref.shape
    hd = d // MEM_HEADS
    pw = yp_ref.shape[1]
    ws = _split3(wr_ref[...])

    def rows(rs):
        n = rs.stop - rs.start
        mix = jnp.dot(yp_ref[rs, :], wo_ref[0:pw, :], preferred_element_type=F32)
        mix = mix + jnp.dot(ya_ref[rs, :], wo_ref[pw:, :], preferred_element_type=F32)
        x1 = _layer_norm(DN_ALPHA * x_ref[rs, :] + mix, g1_ref[...], b1_ref[...])

        q = jnp.dot(x1.astype(BF16), wq_ref[...], preferred_element_type=F32).astype(BF16)
        scale = hd ** -0.5
        att = None
        for h in range(MEM_HEADS):
            c = slice(h * hd, (h + 1) * hd)
            lg = lax.dot_general(q[:, c], km_ref[0, :, c], _NT, preferred_element_type=F32) * scale
            p = jnp.exp(lg - jnp.max(lg, axis=-1, keepdims=True))
            p = p / jnp.sum(p, axis=-1, keepdims=True)
            oh = jnp.dot(p.astype(BF16), vm_ref[0, :, c], preferred_element_type=F32).astype(BF16)
            t = jnp.dot(oh, wmo_ref[c, :], preferred_element_type=F32)
            att = t if att is None else att + t
        x2 = _layer_norm(DN_ALPHA * x1 + att, g2_ref[...], b2_ref[...])
        x2_ref[rs, :] = x2

        xs = _split3(x2)
        lt = None
        for i, j in ((0, 0), (0, 1), (1, 0)):
            t = lax.dot_general(ws[j], xs[i], _NT, preferred_element_type=F32)
            lt = t if lt is None else lt + t
        lt = lt + br_ref[...]
        n_e = lt.shape[0]
        eidx = lax.broadcasted_iota(I32, lt.shape, 0)
        vals, idxs = [], []
        for _ in range(TOP_K_EXPERTS):
            mx = jnp.max(lt, axis=0, keepdims=True)
            ix = jnp.min(jnp.where(lt == mx, eidx, n_e), axis=0, keepdims=True)
            vals.append(mx)
            idxs.append(ix)
            lt = jnp.where(eidx == ix, -jnp.inf, lt)
        tope_ref[:, rs] = jnp.concatenate(idxs, axis=0)
        ex = [jnp.exp(v - vals[0]) for v in vals]
        den = ex[0]
        for e_ in ex[1:]:
            den = den + e_
        gates = jnp.concatenate([e_ / den for e_ in ex] + [jnp.zeros((LANES - TOP_K_EXPERTS, n), F32)], axis=0)
        gate_ref[rs, :] = jnp.transpose(gates)

    n_groups = MIX_ROW_GROUPS if tm % (MIX_ROW_GROUPS * LANES) == 0 else 1
    for g in range(n_groups):
        rows(slice(g * (tm // n_groups), (g + 1) * (tm // n_groups)))


def _mix(x2d, ypool, yatt, w_o, g1, b1, k_mem, v_mem, w_mq, w_mo, g2, b2, w_rt, b_r, B, L):
    T, D = x2d.shape
    tm = min(MIX_ROWS, L)
    nl = L // tm
    M = k_mem.shape[1]
    row = lambda i: (i, 0)
    full = lambda i: (0, 0)
    return pl.pallas_call(
        _mix_kernel,
        grid=(T // tm,),
        in_specs=[
            pl.BlockSpec((tm, D), row),
            pl.BlockSpec((tm, ypool.shape[1]), row),
            pl.BlockSpec((tm, yatt.shape[1]), row),
            pl.BlockSpec(w_o.shape, full),
            pl.BlockSpec(g1.shape, full),
            pl.BlockSpec(b1.shape, full),
            pl.BlockSpec((1, M, D), lambda i: (i // nl, 0, 0)),
            pl.BlockSpec((1, M, D), lambda i: (i // nl, 0, 0)),
            pl.BlockSpec(w_mq.shape, full),
            pl.BlockSpec(w_mo.shape, full),
            pl.BlockSpec(g2.shape, full),
            pl.BlockSpec(b2.shape, full),
            pl.BlockSpec(w_rt.shape, full),
            pl.BlockSpec(b_r.shape, full),
        ],
        out_specs=[
            pl.BlockSpec((tm, D), row),
            pl.BlockSpec((TOP_K_EXPERTS, tm), lambda i: (0, i)),
            pl.BlockSpec((tm, LANES), row),
        ],
        out_shape=[
            jax.ShapeDtypeStruct((T, D), F32),
            jax.ShapeDtypeStruct((TOP_K_EXPERTS, T), I32),
            jax.ShapeDtypeStruct((T, LANES), F32),
        ],
        compiler_params=pltpu.CompilerParams(
            dimension_semantics=("arbitrary",), vmem_limit_bytes=VMEM_LIMIT),
    )(x2d, ypool, yatt, w_o, g1, b1, k_mem, v_mem, w_mq, w_mo, g2, b2, w_rt, b_r)


def _rank_kernel(tope_ref, rank_ref, cnt_ref, carry_ref):
    i = pl.program_id(0)
    tr = tope_ref.shape[1]

    @pl.when(i == 0)
    def _():
        carry_ref[...] = jnp.zeros(carry_ref.shape, F32)

    eidx = lax.broadcasted_iota(I32, (N_EXPERTS, tr), 0)
    onehot = jnp.zeros((N_EXPERTS, tr), F32)
    for k in range(TOP_K_EXPERTS):
        onehot = onehot + jnp.where(eidx == tope_ref[k:k + 1, :], 1.0, 0.0)
    before = jnp.where(lax.broadcasted_iota(I32, (tr, tr), 0) < lax.broadcasted_iota(I32, (tr, tr), 1), 1.0, 0.0)
    excl = jnp.dot(onehot.astype(BF16), before.astype(BF16), preferred_element_type=F32)
    rank_full = excl + carry_ref[:, 0:1]
    rows = []
    for k in range(TOP_K_EXPERTS):
        rows.append(jnp.sum(jnp.where(eidx == tope_ref[k:k + 1, :], rank_full, 0.0), axis=0, keepdims=True))
    rank_ref[...] = jnp.concatenate(rows, axis=0).astype(I32)
    carry_ref[...] = carry_ref[...] + jnp.sum(onehot, axis=1, keepdims=True)
    cnt_ref[...] = carry_ref[...].astype(I32)


def _rank(tope):
    K, T = tope.shape
    tr = min(RANK_TOKENS, T)
    return pl.pallas_call(
        _rank_kernel,
        grid=(T // tr,),
        in_specs=[pl.BlockSpec((K, tr), lambda i: (0, i))],
        out_specs=[pl.BlockSpec((K, tr), lambda i: (0, i)),
                   pl.BlockSpec((N_EXPERTS, LANES), lambda i: (0, 0))],
        out_shape=[jax.ShapeDtypeStruct((K, T), I32), jax.ShapeDtypeStruct((N_EXPERTS, LANES), I32)],
        scratch_shapes=[pltpu.VMEM((N_EXPERTS, LANES), F32)],
        compiler_params=pltpu.CompilerParams(dimension_semantics=("arbitrary",)),
    )(tope)


def _dest_kernel(start_ref, tope_ref, rank_ref, dest_ref):
    tope = tope_ref[...]
    base = jnp.zeros(tope.shape, I32)
    for e in range(N_EXPERTS):
        base = jnp.where(tope == e, start_ref[e], base)
    dest_ref[...] = base + rank_ref[...]


def _dest(pad_start, tope, rank):
    K, T = tope.shape
    tt = min(DEST_TOKENS, T)
    blk = pl.BlockSpec((K, tt), lambda i: (0, i))
    return pl.pallas_call(
        _dest_kernel,
        grid=(T // tt,),
        in_specs=[pl.BlockSpec(memory_space=pltpu.SMEM), blk, blk],
        out_specs=blk,
        out_shape=jax.ShapeDtypeStruct((K, T), I32),
        compiler_params=pltpu.CompilerParams(dimension_semantics=("arbitrary",)),
    )(pad_start, tope, rank)


def _dispatch_kernel(dest_ref, start_ref, cnt_ref, padded_ref, x_ref, xs_hbm, zrow, sem, zsem):
    i = pl.program_id(0)
    td = dest_ref.shape[1]

    def row_copy(j, dst):
        return pltpu.make_async_copy(x_ref.at[pl.ds(j, 1), :], xs_hbm.at[pl.ds(dst, 1), :], sem)

    def zero_copy(dst):
        return pltpu.make_async_copy(zrow, xs_hbm.at[pl.ds(dst, 1), :], zsem)

    @pl.when(i == 0)
    def _():
        zrow[...] = jnp.zeros(zrow.shape, F32)
        for e in range(N_EXPERTS):
            first = start_ref[e] + cnt_ref[e]
            n_pad = padded_ref[e] - cnt_ref[e]

            def zstart(r, c):
                zero_copy(first + r).start()
                return c
            lax.fori_loop(0, n_pad, zstart, 0)

            def zwait(r, c):
                zero_copy(first + r).wait()
                return c
            lax.fori_loop(0, n_pad, zwait, 0)

    def issue(jj, c):
        for u in range(ISSUE_UNROLL):
            j = jj * ISSUE_UNROLL + u
            for k in range(TOP_K_EXPERTS):
                row_copy(j, dest_ref[k, j]).start()
        return c
    lax.fori_loop(0, td // ISSUE_UNROLL, issue, 0)

    def drain(jj, c):
        for _ in range(ISSUE_UNROLL * TOP_K_EXPERTS):
            row_copy(0, 0).wait()
        return c
    lax.fori_loop(0, td // ISSUE_UNROLL, drain, 0)


def _dispatch(dest, pad_start, counts, padded, x2, n_rows):
    K, T = dest.shape
    D = x2.shape[1]
    td = min(DISPATCH_TOKENS, T)
    smem_tok = pl.BlockSpec((K, td), lambda i: (0, i), memory_space=pltpu.SMEM)
    smem_full = pl.BlockSpec(memory_space=pltpu.SMEM)
    return pl.pallas_call(
        _dispatch_kernel,
        grid=(T // td,),
        in_specs=[smem_tok, smem_full, smem_full, smem_full,
                  pl.BlockSpec((td, D), lambda i: (i, 0))],
        out_specs=pl.BlockSpec(memory_space=pl.ANY),
        out_shape=jax.ShapeDtypeStruct((n_rows, D), F32),
        scratch_shapes=[pltpu.VMEM((1, D), F32), pltpu.SemaphoreType.DMA(()), pltpu.SemaphoreType.DMA(())],
        compiler_params=pltpu.CompilerParams(
            dimension_semantics=("arbitrary",), vmem_limit_bytes=VMEM_LIMIT),
    )(dest, pad_start, counts, padded, x2)


def _expert_kernel(bexp_ref, nused_ref, xs_ref, wgu_ref, bgu_ref, wd_ref, bd_ref, y_ref, wgu_bf, wd_bf):
    j = pl.program_id(0)
    d_ff = wd_ref.shape[1]

    @pl.when(j < nused_ref[0])
    def _():
        prev = bexp_ref[jnp.maximum(j - 1, 0)]

        @pl.when(jnp.logical_or(j == 0, bexp_ref[j] != prev))
        def _():
            wgu_bf[...] = wgu_ref[0].astype(BF16)
            wd_bf[...] = wd_ref[0].astype(BF16)

        xb = xs_ref[...].astype(BF16)
        gu = jnp.dot(xb, wgu_bf[...], preferred_element_type=F32) + bgu_ref[0]
        g = jnp.minimum(gu[:, 0:d_ff], SWIGLU_LIMIT)
        u = jnp.clip(gu[:, d_ff:2 * d_ff], -SWIGLU_LIMIT, SWIGLU_LIMIT)
        act = (u + 1.0) * g * (1.0 / (1.0 + jnp.exp(-SWIGLU_ALPHA * g)))
        y_ref[...] = jnp.dot(act.astype(BF16), wd_bf[...], preferred_element_type=F32) + bd_ref[0]


def _experts(block_exp, n_used, xs, w_gate_up, b_gate_up, w_down, b_down):
    P, D = xs.shape
    blk = EXPERT_ROWS
    nblk = P // blk
    E, _, F2 = w_gate_up.shape
    d_ff = w_down.shape[1]
    row = lambda j, be, nu: (jnp.minimum(j, nu[0] - 1), 0)
    exp3 = lambda j, be, nu: (be[j], 0, 0)
    grid_spec = pltpu.PrefetchScalarGridSpec(
        num_scalar_prefetch=2,
        grid=(nblk,),
        in_specs=[
            pl.BlockSpec((blk, D), row),
            pl.BlockSpec((1, D, F2), exp3),
            pl.BlockSpec((1, 1, F2), exp3),
            pl.BlockSpec((1, d_ff, D), exp3),
            pl.BlockSpec((1, 1, D), exp3),
        ],
        out_specs=pl.BlockSpec((blk, D), row),
        scratch_shapes=[pltpu.VMEM((D, F2), BF16), pltpu.VMEM((d_ff, D), BF16)],
    )
    return pl.pallas_call(
        _expert_kernel,
        grid_spec=grid_spec,
        out_shape=jax.ShapeDtypeStruct((P, D), F32),
        compiler_params=pltpu.CompilerParams(
            dimension_semantics=("arbitrary",), vmem_limit_bytes=VMEM_LIMIT),
    )(block_exp, n_used, xs, w_gate_up, b_gate_up.reshape(E, 1, F2), w_down, b_down.reshape(E, 1, D))


def _combine_kernel(dest_ref, dest_next_ref, x2_ref, gate_ref, g3_ref, b3_ref, y_hbm, o_ref, ybuf, sem):
    i = pl.program_id(0)
    n = pl.num_programs(0)
    tc = x2_ref.shape[0]
    slot = i % 2

    def row_copy(src_ref, s, j, k):
        return pltpu.make_async_copy(y_hbm.at[pl.ds(src_ref[k, j], 1), :], ybuf.at[s, k, pl.ds(j, 1), :], sem.at[s])

    def gather(src_ref, s):
        def issue(jj, c):
            for u in range(ISSUE_UNROLL):
                for k in range(TOP_K_EXPERTS):
                    row_copy(src_ref, s, jj * ISSUE_UNROLL + u, k).start()
            return c
        lax.fori_loop(0, tc // ISSUE_UNROLL, issue, 0)

    @pl.when(i == 0)
    def _():
        gather(dest_ref, 0)

    @pl.when(i + 1 < n)
    def _():
        gather(dest_next_ref, 1 - slot)

    def drain(jj, c):
        for _ in range(ISSUE_UNROLL * TOP_K_EXPERTS):
            pltpu.make_async_copy(y_hbm.at[pl.ds(0, 1), :], ybuf.at[slot, 0, pl.ds(0, 1), :], sem.at[slot]).wait()
        return c
    lax.fori_loop(0, tc // ISSUE_UNROLL, drain, 0)

    moe = None
    for k in range(TOP_K_EXPERTS):
        t = ybuf[slot, k] * gate_ref[:, k:k + 1]
        moe = t if moe is None else moe + t
    o_ref[...] = _layer_norm(DN_ALPHA * x2_ref[...] + moe, g3_ref[...], b3_ref[...])


def _combine(dest, x2, gates, g3, b3, y):
    K, T = dest.shape
    D = x2.shape[1]
    tc = min(COMBINE_TOKENS, T)
    n = T // tc
    row = lambda i: (i, 0)
    full = lambda i: (0, 0)
    return pl.pallas_call(
        _combine_kernel,
        grid=(n,),
        in_specs=[pl.BlockSpec((K, tc), lambda i: (0, i), memory_space=pltpu.SMEM),
                  pl.BlockSpec((K, tc), lambda i: (0, jnp.minimum(i + 1, n - 1)), memory_space=pltpu.SMEM),
                  pl.BlockSpec((tc, D), row), pl.BlockSpec((tc, LANES), row),
                  pl.BlockSpec(g3.shape, full), pl.BlockSpec(b3.shape, full),
                  pl.BlockSpec(memory_space=pl.ANY)],
        out_specs=pl.BlockSpec((tc, D), row),
        out_shape=jax.ShapeDtypeStruct((T, D), F32),
        scratch_shapes=[pltpu.VMEM((2, K, tc, D), F32), pltpu.SemaphoreType.DMA((2,))],
        compiler_params=pltpu.CompilerParams(
            dimension_semantics=("arbitrary",), vmem_limit_bytes=VMEM_LIMIT),
    )(dest, dest, x2, gates, g3, b3, y)


def _pad_cols(w, width):
    return jnp.pad(w, ((0, 0), (0, width - w.shape[1])))


def _layer(x, mem, w_in, w_pool, pool_scale, ik_g, ik_b, kv_g, w_uk, w_uv, w_o, ln1_g, ln1_b,
           w_mq, w_mkv, w_mo, ln2_g, ln2_b, w_router, b_router, w_gate_up, b_gate_up, w_down, b_down,
           ln3_g, ln3_b):
    B, L, D = x.shape
    T = B * L
    row = lambda v: v.reshape(1, -1)

    o = 0
    pieces = []
    for width in (POOL_WIDTH, ATT_WIDTH, KV_RANK, IDX_HEADS * IDX_HEAD_DIM, IDX_HEAD_DIM, IDX_HEADS):
        pieces.append(w_in[:, o:o + width])
        o += width
    pieces[4] = _pad_cols(pieces[4], LANES)
    pieces[5] = _pad_cols(pieces[5], LANES)
    w_in_pad = jnp.concatenate(pieces, axis=1).astype(BF16)
    wuv_pad = jnp.zeros((ATT_HEADS, KV_RANK, 2 * ATT_HEAD_DIM), F32)
    for h in range(ATT_HEADS):
        wuv_pad = wuv_pad.at[h, :, (h % 2) * ATT_HEAD_DIM:(h % 2 + 1) * ATT_HEAD_DIM].set(w_uv[h])

    ypool, qlat, ckv, qidx, kidx, widx = _inproj(
        x, w_in_pad, w_pool.astype(BF16), row(pool_scale), row(kv_g), row(ik_g), row(ik_b), w_uk.astype(BF16))
    yatt = _dsa(qidx, widx, qlat, kidx, ckv, wuv_pad.astype(BF16))
    k_mem, v_mem = _memkv(mem, w_mkv.astype(BF16))
    x2, tope, gates = _mix(
        x.reshape(T, D), ypool.reshape(T, -1), yatt.reshape(T, -1), w_o.astype(BF16), row(ln1_g), row(ln1_b),
        k_mem, v_mem, w_mq.astype(BF16), w_mo.astype(BF16), row(ln2_g), row(ln2_b),
        jnp.transpose(w_router), b_router.reshape(-1, 1), B, L)

    rank, cnt = _rank(tope)
    counts = cnt[:, 0]
    blk = EXPERT_ROWS
    padded = (counts + blk - 1) // blk * blk
    pad_end = jnp.cumsum(padded)
    pad_start = pad_end - padded
    n_rows = T * TOP_K_EXPERTS + N_EXPERTS * blk
    nblk = n_rows // blk
    n_used = (pad_end[-1] // blk).astype(I32)
    blk_first = jnp.minimum(jnp.arange(nblk, dtype=I32), n_used - 1) * blk
    n_ended = jnp.sum((pad_end[None, :] <= blk_first[:, None]).astype(I32), axis=1)
    block_exp = jnp.minimum(n_ended, N_EXPERTS - 1).astype(I32)

    pad_start = pad_start.astype(I32)
    dest = _dest(pad_start, tope, rank)
    xs = _dispatch(dest, pad_start, counts, padded.astype(I32), x2, n_rows)
    y = _experts(block_exp, n_used.reshape(1), xs, w_gate_up, b_gate_up, w_down, b_down)
    out = _combine(dest, x2, gates, row(ln3_g), row(ln3_b), y)
    return out.reshape(B, L, D)


def kernel(x, mem, w_in, w_pool, pool_scale, idx_k_norm_g, idx_k_norm_b, kv_norm_g, w_uk, w_uv, w_o, ln1_g, ln1_b, w_mq, w_mkv, w_mo, ln2_g, ln2_b, w_router, b_router, w_gate_up, b_gate_up, w_down, b_down, ln3_g, ln3_b):
    assert w_in.shape[0] == DEPTH
    return _layer(x, mem, w_in[0], w_pool[0], pool_scale[0], idx_k_norm_g[0], idx_k_norm_b[0], kv_norm_g[0],
                  w_uk[0], w_uv[0], w_o[0], ln1_g[0], ln1_b[0], w_mq[0], w_mkv[0], w_mo[0], ln2_g[0], ln2_b[0],
                  w_router[0], b_router[0], w_gate_up[0], b_gate_up[0], w_down[0], b_down[0], ln3_g[0], ln3_b[0])
```

```python
import functools

import jax
import jax.numpy as jnp
from jax import lax
from jax.experimental import pallas as pl
from jax.experimental.pallas import tpu as pltpu

F32 = jnp.float32
BF16 = jnp.bfloat16
I32 = jnp.int32
I16 = jnp.int16

POOL_WINDOWS = (2, 4, 8, 16)
POOL_GROUP_DIM = 128
POOL_WIDTH = 512
ATT_HEADS = 8
ATT_HEAD_DIM = 64
ATT_WIDTH = 512
KV_RANK = 128
IDX_HEADS = 8
IDX_HEAD_DIM = 64
TOPK_MAX = 256
MEM_HEADS = 4
N_EXPERTS = 32
TOP_K_EXPERTS = 4
SWIGLU_LIMIT = 7.0
SWIGLU_ALPHA = 1.702
LN_EPS = 1e-5
RMS_EPS = 1e-6
DEPTH = 1
DN_ALPHA = (2 * DEPTH) ** 0.25

LANES = 128
MAX_POOL_WINDOW = 16
VMEM_LIMIT = 56 * 1024 * 1024

PROJ_ROWS = 512
DSA_QUERIES = 128
DSA_KEYS = 512
MIX_ROWS = 512
MIX_ROW_GROUPS = 1
RANK_TOKENS = 512
DEST_TOKENS = 2048
DISPATCH_TOKENS = 512
ISSUE_UNROLL = 4
EXPERT_ROWS = 256
COMBINE_TOKENS = 256

NEG_BIG = -1e30
INT_MIN = -(2 ** 31)
NEG_INF_KEY = INT_MIN + 0x007FFFFF
HALF_BIAS = 1 << 15

_NT = (((1,), (1,)), ((), ()))


def _layer_norm(v, g, b):
    mu = jnp.mean(v, axis=-1, keepdims=True)
    d = v - mu
    var = jnp.mean(d * d, axis=-1, keepdims=True)
    return d * lax.rsqrt(var + LN_EPS) * g + b


_C_POOL = 0
_C_Q = 512
_C_CKV = 1024
_C_QIDX = 1152
_C_KIDX = 1664
_C_WIDX = 1792
_IN_PAD = 1920


def _inproj_kernel(x_ref, w_ref, wpool_ref, pscale_ref, kvg_ref, ikg_ref, ikb_ref, wuk_ref,
                   ypool_ref, qlat_ref, ckv_ref, qidx_ref, kidx_ref, widx_ref, ubuf):
    li = pl.program_id(1)
    tm = x_ref.shape[1]
    halo = MAX_POOL_WINDOW
    xb = x_ref[0].astype(BF16)

    u = jnp.dot(xb, w_ref[:, _C_POOL:_C_POOL + POOL_WIDTH], preferred_element_type=F32)

    @pl.when(li == 0)
    def _():
        ubuf[0:halo, :] = jnp.zeros((halo, POOL_WIDTH), F32)

    ubuf[halo:halo + tm, :] = u
    pos = li * tm + lax.broadcasted_iota(I32, (tm, 1), 0)
    for g, w in enumerate(POOL_WINDOWS):
        c0 = g * POOL_GROUP_DIM
        c1 = c0 + POOL_GROUP_DIM
        ug = ubuf[halo:halo + tm, c0:c1]
        s = ug
        for j in range(1, w):
            s = s + ubuf[halo - j:halo - j + tm, c0:c1]
        cnt = jnp.minimum(pos + 1, w).astype(F32)
        d = s / cnt - ug
        y = jnp.dot(d.astype(BF16), wpool_ref[g], preferred_element_type=F32) * pscale_ref[:, c0:c1]
        ypool_ref[0, :, c0:c1] = y.astype(BF16)
    ubuf[0:halo, :] = ubuf[tm:tm + halo, :]

    q = jnp.dot(xb, w_ref[:, _C_Q:_C_Q + ATT_WIDTH], preferred_element_type=F32)
    att_scale = ATT_HEAD_DIM ** -0.5
    for h in range(ATT_HEADS):
        qh = q[:, h * ATT_HEAD_DIM:(h + 1) * ATT_HEAD_DIM].astype(BF16)
        ql = jnp.dot(qh, wuk_ref[h], preferred_element_type=F32) * att_scale
        qlat_ref[0, h] = ql.astype(BF16)

    c = jnp.dot(xb, w_ref[:, _C_CKV:_C_CKV + KV_RANK], preferred_element_type=F32)
    c = c * lax.rsqrt(jnp.mean(c * c, axis=-1, keepdims=True) + RMS_EPS) * kvg_ref[...]
    ckv_ref[0, :, 0:KV_RANK] = c.astype(BF16)
    ckv_ref[0, :, KV_RANK:2 * KV_RANK] = jnp.ones((tm, KV_RANK), BF16)

    qi = jnp.dot(xb, w_ref[:, _C_QIDX:_C_QIDX + IDX_HEADS * IDX_HEAD_DIM], preferred_element_type=F32)
    for h in range(IDX_HEADS):
        qidx_ref[0, h] = qi[:, h * IDX_HEAD_DIM:(h + 1) * IDX_HEAD_DIM].astype(BF16)
    kw = jnp.dot(xb, w_ref[:, _C_KIDX:_C_KIDX + 2 * LANES], preferred_element_type=F32)
    kr = kw[:, 0:IDX_HEAD_DIM]
    kidx_ref[0] = _layer_norm(kr, ikg_ref[...], ikb_ref[...]).astype(BF16)
    widx_ref[0] = kw[:, LANES:2 * LANES]


def _inproj(x, w_in_pad, w_pool, pool_scale, kv_g, ik_g, ik_b, w_uk):
    B, L, D = x.shape
    tm = min(PROJ_ROWS, L)
    nl = L // tm
    full2 = lambda b, l: (0, 0)
    full3 = lambda b, l: (0, 0, 0)
    return pl.pallas_call(
        _inproj_kernel,
        grid=(B, nl),
        in_specs=[
            pl.BlockSpec((1, tm, D), lambda b, l: (b, l, 0)),
            pl.BlockSpec(w_in_pad.shape, full2),
            pl.BlockSpec(w_pool.shape, full3),
            pl.BlockSpec(pool_scale.shape, full2),
            pl.BlockSpec(kv_g.shape, full2),
            pl.BlockSpec(ik_g.shape, full2),
            pl.BlockSpec(ik_b.shape, full2),
            pl.BlockSpec(w_uk.shape, full3),
        ],
        out_specs=[
            pl.BlockSpec((1, tm, POOL_WIDTH), lambda b, l: (b, l, 0)),
            pl.BlockSpec((1, ATT_HEADS, tm, KV_RANK), lambda b, l: (b, 0, l, 0)),
            pl.BlockSpec((1, tm, 2 * KV_RANK), lambda b, l: (b, l, 0)),
            pl.BlockSpec((1, IDX_HEADS, tm, IDX_HEAD_DIM), lambda b, l: (b, 0, l, 0)),
            pl.BlockSpec((1, tm, IDX_HEAD_DIM), lambda b, l: (b, l, 0)),
            pl.BlockSpec((1, tm, LANES), lambda b, l: (b, l, 0)),
        ],
        out_shape=[
            jax.ShapeDtypeStruct((B, L, POOL_WIDTH), BF16),
            jax.ShapeDtypeStruct((B, ATT_HEADS, L, KV_RANK), BF16),
            jax.ShapeDtypeStruct((B, L, 2 * KV_RANK), BF16),
            jax.ShapeDtypeStruct((B, IDX_HEADS, L, IDX_HEAD_DIM), BF16),
            jax.ShapeDtypeStruct((B, L, IDX_HEAD_DIM), BF16),
            jax.ShapeDtypeStruct((B, L, LANES), F32),
        ],
        scratch_shapes=[pltpu.VMEM((MAX_POOL_WINDOW + tm, POOL_WIDTH), F32)],
        compiler_params=pltpu.CompilerParams(
            dimension_semantics=("arbitrary", "arbitrary"), vmem_limit_bytes=VMEM_LIMIT),
    )(x, w_in_pad, w_pool, pool_scale, kv_g, ik_g, ik_b, w_uk)


def _sortable(score):
    bits = pltpu.bitcast(score, I32)
    return bits ^ ((bits >> 31) & 0x7FFFFFFF)


def _dsa_kernel(qidx_ref, widx_ref, qlat_ref, kidx_ref, ckv_ref, wuv_ref, o_ref,
                key_ref, keyt_ref, hi_ref, lo_ref, wb_ref, s_ref, s2_ref, p_ref, p2_ref, m_ref, acc_ref, sel_ref,
                *, top_k, idx_bits):
    qi = pl.program_id(1)
    H = ATT_HEADS
    qb = qidx_ref.shape[2]
    kc = s_ref.shape[1]
    qstart = qi * qb
    n_ch = (qstart + qb + kc - 1) // kc
    idx_scale = (IDX_HEAD_DIM ** -0.5) * (IDX_HEADS ** -0.5)

    qpos = qstart + lax.broadcasted_iota(I32, (qb, 1), 0)
    qpos_t = qstart + lax.broadcasted_iota(I32, (1, qb), 1)
    lane_pos = lax.broadcasted_iota(I32, (1, kc), 1)
    row_pos = lax.broadcasted_iota(I32, (kc, 1), 0)

    for h in range(H):
        wb_ref[h * qb:(h + 1) * qb, :] = jnp.broadcast_to(widx_ref[0, :, h:h + 1], (qb, LANES))

    def pipelined(produce, consume, buf_a, buf_b):
        produce(0, buf_a)
        n_pairs = (n_ch - 1) // 2

        def pair_body(i, carry):
            c = 2 * i
            consume(c, buf_a)
            produce(c + 1, buf_b)
            consume(c + 1, buf_b)
            produce(c + 2, buf_a)
            return carry

        lax.fori_loop(0, n_pairs, pair_body, 0)
        last = 2 * n_pairs

        @pl.when(n_ch - last == 1)
        def _():
            consume(last, buf_a)

        @pl.when(n_ch - last == 2)
        def _():
            consume(last, buf_a)
            produce(last + 1, buf_b)
            consume(last + 1, buf_b)

    def head_scores(c, buf):
        off = pl.multiple_of(c * kc, kc)
        kk = kidx_ref[0, pl.ds(off, kc), :]
        qs = qidx_ref[0].reshape(H * qb, IDX_HEAD_DIM)
        buf[...] = lax.dot_general(qs, kk, _NT, preferred_element_type=F32)

    def chunk_keys(c, buf):
        for j in range(kc // LANES):
            cs = slice(j * LANES, (j + 1) * LANES)
            acc = None
            for h in range(H):
                r = slice(h * qb, (h + 1) * qb)
                term = jnp.maximum(buf[r, cs], 0.0) * wb_ref[r, 0:LANES]
                acc = term if acc is None else acc + term
            score = acc * idx_scale
            score = jnp.where(c * kc + j * LANES + lane_pos[:, 0:LANES] <= qpos, score, -jnp.inf)
            key_ref[c, :, cs] = _sortable(score)
            kt = _sortable(jnp.transpose(score))
            keyt_ref[c, cs, :] = kt
            hi_ref[c, cs, :] = (kt >> 16).astype(I16)
            lo_ref[c, cs, :] = ((kt & 0xFFFF) - HALF_BIAS).astype(I16)

    pipelined(head_scores, chunk_keys, s_ref, s2_ref)

    def count16(ref, indicator):
        def body(c, cnt):
            m = indicator(ref[c])
            parts = [m[j * 16:(j + 1) * 16, :] for j in range(kc // 16)]
            while len(parts) > 1:
                parts = [parts[j] + parts[j + 1] for j in range(0, len(parts), 2)]
            return cnt + parts[0]
        cnt = lax.fori_loop(0, n_ch, body, jnp.zeros((16, qb), I16))
        return jnp.sum(cnt.astype(I32), axis=0, keepdims=True)

    one16 = jnp.int16(1)
    zero16 = jnp.int16(0)

    def ge16(cand):
        c16 = cand.astype(I16)
        return lambda v: jnp.where(v >= c16, one16, zero16)

    def half_search(ref, n_above):
        def body(i, t):
            cand = t + lax.shift_left(jnp.int32(1), 15 - i)
            n_ge = n_above + count16(ref, ge16(cand))
            return jnp.where(n_ge >= top_k, cand, t)
        return lax.fori_loop(0, 16, body, jnp.full((1, qb), -HALF_BIAS, I32))

    t_hi = half_search(hi_ref, jnp.zeros((1, qb), I32))
    t_hi16 = t_hi.astype(I16)
    n_above = count16(hi_ref, lambda v: jnp.where(v > t_hi16, one16, zero16))

    def mask_body(c, carry):
        lo_ref[c] = jnp.where(hi_ref[c] == t_hi16, lo_ref[c], jnp.int16(-HALF_BIAS))
        return carry

    lax.fori_loop(0, n_ch, mask_body, 0)
    t_lo = half_search(lo_ref, n_above)
    thr_t = lax.shift_left(t_hi, 16) | (t_lo + HALF_BIAS)

    def count(indicator):
        def body(c, cnt):
            m = indicator(keyt_ref[c], c * kc + row_pos)
            parts = [m[j * 8:(j + 1) * 8, :] for j in range(kc // 8)]
            while len(parts) > 1:
                parts = [parts[j] + parts[j + 1] for j in range(0, len(parts), 2)]
            return cnt + parts[0]
        cnt = lax.fori_loop(0, n_ch, body, jnp.zeros((8, qb), I32))
        return jnp.sum(cnt, axis=0, keepdims=True)

    n_gt = count(lambda k, _: jnp.where(k > thr_t, 1, 0))
    n_ge = count(lambda k, _: jnp.where(k >= thr_t, 1, 0))
    need = top_k - n_gt
    surplus = jnp.where(thr_t > NEG_INF_KEY, jnp.where(n_ge > top_k, 1, 0), 0)
    sel_ref[0:1, :] = thr_t
    sel_ref[1:2, :] = qpos_t

    @pl.when(jnp.max(surplus) > 0)
    def _():
        def tie_body(i, m):
            cand = m + lax.shift_left(jnp.int32(1), idx_bits - 1 - i)
            n_before = count(lambda k, p: jnp.where(k == thr_t, jnp.where(p < cand, 1, 0), 0))
            return jnp.where(n_before < need, cand, m)
        last = lax.fori_loop(0, idx_bits, tie_body, jnp.zeros((1, qb), I32))
        sel_ref[1:2, :] = jnp.where(surplus > 0, jnp.minimum(last, qpos_t), qpos_t)

    thr = jnp.transpose(jnp.broadcast_to(sel_ref[0:1, :], (qb, qb)))[:, 0:1]
    tie_last = jnp.transpose(jnp.broadcast_to(sel_ref[1:2, :], (qb, qb)))[:, 0:1]

    def chunk_bias(c):
        key = key_ref[c]
        kpos = c * kc + lane_pos
        tie_bias = jnp.where(key == thr, jnp.where(kpos <= tie_last, 0.0, NEG_BIG), NEG_BIG)
        return jnp.where(key > thr, 0.0, tie_bias)

    n_lt = kc // LANES

    def head_logits(h, ck):
        return lax.dot_general(qlat_ref[0, h], ck, _NT, preferred_element_type=F32)

    m_ref[...] = jnp.full(m_ref.shape, NEG_BIG, F32)

    def all_logits(c, buf):
        off = pl.multiple_of(c * kc, kc)
        ck = ckv_ref[0, pl.ds(off, kc), 0:KV_RANK]
        for h in range(H):
            buf[h * qb:(h + 1) * qb, :] = head_logits(h, ck)

    def running_max(c, buf):
        bias = chunk_bias(c)
        for h in range(H):
            r = slice(h * qb, (h + 1) * qb)
            zm = buf[r, 0:LANES] + bias[:, 0:LANES]
            for j in range(1, n_lt):
                zm = jnp.maximum(zm, buf[r, j * LANES:(j + 1) * LANES] + bias[:, j * LANES:(j + 1) * LANES])
            m_ref[r, :] = jnp.maximum(m_ref[r, :], zm)

    pipelined(all_logits, running_max, s_ref, s2_ref)
    m_ref[...] = jnp.broadcast_to(jnp.max(m_ref[...], axis=1, keepdims=True), m_ref.shape)

    acc_ref[...] = jnp.zeros(acc_ref.shape, F32)

    def numerators(c, buf):
        off = pl.multiple_of(c * kc, kc)
        ck = ckv_ref[0, pl.ds(off, kc), 0:KV_RANK]
        bias = chunk_bias(c)
        for h in range(H):
            r = slice(h * qb, (h + 1) * qb)
            lg = head_logits(h, ck)
            mh = m_ref[r, :]
            for j in range(n_lt):
                cs = slice(j * LANES, (j + 1) * LANES)
                buf[r, cs] = jnp.exp(lg[:, cs] + bias[:, cs] - mh).astype(BF16)

    def accumulate(c, buf):
        off = pl.multiple_of(c * kc, kc)
        ckx = ckv_ref[0, pl.ds(off, kc), :]
        acc_ref[...] = acc_ref[...] + jnp.dot(buf[...], ckx, preferred_element_type=F32)

    pipelined(numerators, accumulate, p_ref, p2_ref)

    o_lat = (acc_ref[:, 0:KV_RANK] / acc_ref[:, KV_RANK:2 * KV_RANK]).astype(BF16)
    for h in range(0, H, 2):
        t = jnp.dot(o_lat[h * qb:(h + 1) * qb, :], wuv_ref[h], preferred_element_type=F32)
        t = t + jnp.dot(o_lat[(h + 1) * qb:(h + 2) * qb, :], wuv_ref[h + 1], preferred_element_type=F32)
        o_ref[0, :, h * ATT_HEAD_DIM:(h + 2) * ATT_HEAD_DIM] = t.astype(BF16)


def _dsa(qidx, widx, qlat, kidx, ckv, wuv_pad):
    B, H, L, _ = qidx.shape
    qb = min(DSA_QUERIES, L)
    kc = min(DSA_KEYS, L)
    top_k = min(TOPK_MAX, L // 4)
    idx_bits = max(1, (L - 1).bit_length())
    kern = functools.partial(_dsa_kernel, top_k=top_k, idx_bits=idx_bits)
    return pl.pallas_call(
        kern,
        grid=(B, L // qb),
        in_specs=[
            pl.BlockSpec((1, H, qb, IDX_HEAD_DIM), lambda b, q: (b, 0, q, 0)),
            pl.BlockSpec((1, qb, LANES), lambda b, q: (b, q, 0)),
            pl.BlockSpec((1, H, qb, KV_RANK), lambda b, q: (b, 0, q, 0)),
            pl.BlockSpec((1, L, IDX_HEAD_DIM), lambda b, q: (b, 0, 0)),
            pl.BlockSpec((1, L, 2 * KV_RANK), lambda b, q: (b, 0, 0)),
            pl.BlockSpec(wuv_pad.shape, lambda b, q: (0, 0, 0)),
        ],
        out_specs=pl.BlockSpec((1, qb, ATT_WIDTH), lambda b, q: (b, q, 0)),
        out_shape=jax.ShapeDtypeStruct((B, L, ATT_WIDTH), BF16),
        scratch_shapes=[
            pltpu.VMEM((L // kc, qb, kc), I32),
            pltpu.VMEM((L // kc, kc, qb), I32),
            pltpu.VMEM((L // kc, kc, qb), I16),
            pltpu.VMEM((L // kc, kc, qb), I16),
            pltpu.VMEM((H * qb, LANES), F32),
            pltpu.VMEM((H * qb, kc), F32),
            pltpu.VMEM((H * qb, kc), F32),
            pltpu.VMEM((H * qb, kc), BF16),
            pltpu.VMEM((H * qb, kc), BF16),
            pltpu.VMEM((H * qb, LANES), F32),
            pltpu.VMEM((H * qb, 2 * KV_RANK), F32),
            pltpu.VMEM((8, qb), I32),
        ],
        compiler_params=pltpu.CompilerParams(
            dimension_semantics=("arbitrary", "arbitrary"), vmem_limit_bytes=VMEM_LIMIT),
    )(qidx, widx, qlat, kidx, ckv, wuv_pad)


def _memkv_kernel(mem_ref, w_ref, k_ref, v_ref):
    d = k_ref.shape[2]
    kv = jnp.dot(mem_ref[0].astype(BF16), w_ref[...], preferred_element_type=F32)
    k_ref[0] = kv[:, 0:d].astype(BF16)
    v_ref[0] = kv[:, d:2 * d].astype(BF16)


def _memkv(mem, w_mkv):
    B, M, D = mem.shape
    return pl.pallas_call(
        _memkv_kernel,
        grid=(B,),
        in_specs=[pl.BlockSpec((1, M, D), lambda b: (b, 0, 0)),
                  pl.BlockSpec(w_mkv.shape, lambda b: (0, 0))],
        out_specs=[pl.BlockSpec((1, M, D), lambda b: (b, 0, 0)),
                   pl.BlockSpec((1, M, D), lambda b: (b, 0, 0))],
        out_shape=[jax.ShapeDtypeStruct((B, M, D), BF16), jax.ShapeDtypeStruct((B, M, D), BF16)],
        compiler_params=pltpu.CompilerParams(
            dimension_semantics=("arbitrary",), vmem_limit_bytes=VMEM_LIMIT),
    )(mem, w_mkv)


def _split3(v):
    hi = v.astype(BF16)
    r1 = v - hi.astype(F32)
    mid = r1.astype(BF16)
    lo = (r1 - mid.astype(F32)).astype(BF16)
    return hi, mid, lo


def _mix_kernel(x_ref, yp_ref, ya_ref, wo_ref, g1_ref, b1_ref, km_ref, vm_ref, wq_ref, wmo_ref,
                g2_ref, b2_ref, wr_ref, br_ref, x2_ref, tope_ref, gate_ref):
    tm, d = x_ref.shape
    hd = d // MEM_HEADS
    pw = yp_ref.shape[1]
    ws = _split3(wr_ref[...])

    def rows(rs):
        n = rs.stop - rs.start
        mix = jnp.dot(yp_ref[rs, :], wo_ref[0:pw, :], preferred_element_type=F32)
        mix = mix + jnp.dot(ya_ref[rs, :], wo_ref[pw:, :], preferred_element_type=F32)
        x1 = _layer_norm(DN_ALPHA * x_ref[rs, :] + mix, g1_ref[...], b1_ref[...])

        q = jnp.dot(x1.astype(BF16), wq_ref[...], preferred_element_type=F32).astype(BF16)
        scale = hd ** -0.5
        att = None
        for h in range(MEM_HEADS):
            c = slice(h * hd, (h + 1) * hd)
            lg = lax.dot_general(q[:, c], km_ref[0, :, c], _NT, preferred_element_type=F32) * scale
            p = jnp.exp(lg - jnp.max(lg, axis=-1, keepdims=True))
            p = p / jnp.sum(p, axis=-1, keepdims=True)
            oh = jnp.dot(p.astype(BF16), vm_ref[0, :, c], preferred_element_type=F32).astype(BF16)
            t = jnp.dot(oh, wmo_ref[c, :], preferred_element_type=F32)
            att = t if att is None else att + t
        x2 = _layer_norm(DN_ALPHA * x1 + att, g2_ref[...], b2_ref[...])
        x2_ref[rs, :] = x2

        xs = _split3(x2)
        lt = None
        for i, j in ((0, 0), (0, 1), (1, 0)):
            t = lax.dot_general(ws[j], xs[i], _NT, preferred_element_type=F32)
            lt = t if lt is None else lt + t
        lt = lt + br_ref[...]
        n_e = lt.shape[0]
        eidx = lax.broadcasted_iota(I32, lt.shape, 0)
        vals, idxs = [], []
        for _ in range(TOP_K_EXPERTS):
            mx = jnp.max(lt, axis=0, keepdims=True)
            ix = jnp.min(jnp.where(lt == mx, eidx, n_e), axis=0, keepdims=True)
            vals.append(mx)
            idxs.append(ix)
            lt = jnp.where(eidx == ix, -jnp.inf, lt)
        tope_ref[:, rs] = jnp.concatenate(idxs, axis=0)
        ex = [jnp.exp(v - vals[0]) for v in vals]
        den = ex[0]
        for e_ in ex[1:]:
            den = den + e_
        gates = jnp.concatenate([e_ / den for e_ in ex] + [jnp.zeros((LANES - TOP_K_EXPERTS, n), F32)], axis=0)
        gate_ref[rs, :] = jnp.transpose(gates)

    n_groups = MIX_ROW_GROUPS if tm % (MIX_ROW_GROUPS * LANES) == 0 else 1
    for g in range(n_groups):
        rows(slice(g * (tm // n_groups), (g + 1) * (tm // n_groups)))


def _mix(x2d, ypool, yatt, w_o, g1, b1, k_mem, v_mem, w_mq, w_mo, g2, b2, w_rt, b_r, B, L):
    T, D = x2d.shape
    tm = min(MIX_ROWS, L)
    nl = L // tm
    M = k_mem.shape[1]
    row = lambda i: (i, 0)
    full = lambda i: (0, 0)
    return pl.pallas_call(
        _mix_kernel,
        grid=(T // tm,),
        in_specs=[
            pl.BlockSpec((tm, D), row),
            pl.BlockSpec((tm, ypool.shape[1]), row),
            pl.BlockSpec((tm, yatt.shape[1]), row),
            pl.BlockSpec(w_o.shape, full),
            pl.BlockSpec(g1.shape, full),
            pl.BlockSpec(b1.shape, full),
            pl.BlockSpec((1, M, D), lambda i: (i // nl, 0, 0)),
            pl.BlockSpec((1, M, D), lambda i: (i // nl, 0, 0)),
            pl.BlockSpec(w_mq.shape, full),
            pl.BlockSpec(w_mo.shape, full),
            pl.BlockSpec(g2.shape, full),
            pl.BlockSpec(b2.shape, full),
            pl.BlockSpec(w_rt.shape, full),
            pl.BlockSpec(b_r.shape, full),
        ],
        out_specs=[
            pl.BlockSpec((tm, D), row),
            pl.BlockSpec((TOP_K_EXPERTS, tm), lambda i: (0, i)),
            pl.BlockSpec((tm, LANES), row),
        ],
        out_shape=[
            jax.ShapeDtypeStruct((T, D), F32),
            jax.ShapeDtypeStruct((TOP_K_EXPERTS, T), I32),
            jax.ShapeDtypeStruct((T, LANES), F32),
        ],
        compiler_params=pltpu.CompilerParams(
            dimension_semantics=("arbitrary",), vmem_limit_bytes=VMEM_LIMIT),
    )(x2d, ypool, yatt, w_o, g1, b1, k_mem, v_mem, w_mq, w_mo, g2, b2, w_rt, b_r)


def _rank_kernel(tope_ref, rank_ref, cnt_ref, carry_ref):
    i = pl.program_id(0)
    tr = tope_ref.shape[1]

    @pl.when(i == 0)
    def _():
        carry_ref[...] = jnp.zeros(carry_ref.shape, F32)

    eidx = lax.broadcasted_iota(I32, (N_EXPERTS, tr), 0)
    onehot = jnp.zeros((N_EXPERTS, tr), F32)
    for k in range(TOP_K_EXPERTS):
        onehot = onehot + jnp.where(eidx == tope_ref[k:k + 1, :], 1.0, 0.0)
    before = jnp.where(lax.broadcasted_iota(I32, (tr, tr), 0) < lax.broadcasted_iota(I32, (tr, tr), 1), 1.0, 0.0)
    excl = jnp.dot(onehot.astype(BF16), before.astype(BF16), preferred_element_type=F32)
    rank_full = excl + carry_ref[:, 0:1]
    rows = []
    for k in range(TOP_K_EXPERTS):
        rows.append(jnp.sum(jnp.where(eidx == tope_ref[k:k + 1, :], rank_full, 0.0), axis=0, keepdims=True))
    rank_ref[...] = jnp.concatenate(rows, axis=0).astype(I32)
    carry_ref[...] = carry_ref[...] + jnp.sum(onehot, axis=1, keepdims=True)
    cnt_ref[...] = carry_ref[...].astype(I32)


def _rank(tope):
    K, T = tope.shape
    tr = min(RANK_TOKENS, T)
    return pl.pallas_call(
        _rank_kernel,
        grid=(T // tr,),
        in_specs=[pl.BlockSpec((K, tr), lambda i: (0, i))],
        out_specs=[pl.BlockSpec((K, tr), lambda i: (0, i)),
                   pl.BlockSpec((N_EXPERTS, LANES), lambda i: (0, 0))],
        out_shape=[jax.ShapeDtypeStruct((K, T), I32), jax.ShapeDtypeStruct((N_EXPERTS, LANES), I32)],
        scratch_shapes=[pltpu.VMEM((N_EXPERTS, LANES), F32)],
        compiler_params=pltpu.CompilerParams(dimension_semantics=("arbitrary",)),
    )(tope)


def _dest_kernel(start_ref, tope_ref, rank_ref, dest_ref):
    tope = tope_ref[...]
    base = jnp.zeros(tope.shape, I32)
    for e in range(N_EXPERTS):
        base = jnp.where(tope == e, start_ref[e], base)
    dest_ref[...] = base + rank_ref[...]


def _dest(pad_start, tope, rank):
    K, T = tope.shape
    tt = min(DEST_TOKENS, T)
    blk = pl.BlockSpec((K, tt), lambda i: (0, i))
    return pl.pallas_call(
        _dest_kernel,
        grid=(T // tt,),
        in_specs=[pl.BlockSpec(memory_space=pltpu.SMEM), blk, blk],
        out_specs=blk,
        out_shape=jax.ShapeDtypeStruct((K, T), I32),
        compiler_params=pltpu.CompilerParams(dimension_semantics=("arbitrary",)),
    )(pad_start, tope, rank)


def _dispatch_kernel(dest_ref, start_ref, cnt_ref, padded_ref, x_ref, xs_hbm, zrow, sem, zsem):
    i = pl.program_id(0)
    td = dest_ref.shape[1]

    def row_copy(j, dst):
        return pltpu.make_async_copy(x_ref.at[pl.ds(j, 1), :], xs_hbm.at[pl.ds(dst, 1), :], sem)

    def zero_copy(dst):
        return pltpu.make_async_copy(zrow, xs_hbm.at[pl.ds(dst, 1), :], zsem)

    @pl.when(i == 0)
    def _():
        zrow[...] = jnp.zeros(zrow.shape, F32)
        for e in range(N_EXPERTS):
            first = start_ref[e] + cnt_ref[e]
            n_pad = padded_ref[e] - cnt_ref[e]

            def zstart(r, c):
                zero_copy(first + r).start()
                return c
            lax.fori_loop(0, n_pad, zstart, 0)

            def zwait(r, c):
                zero_copy(first + r).wait()
                return c
            lax.fori_loop(0, n_pad, zwait, 0)

    def issue(jj, c):
        for u in range(ISSUE_UNROLL):
            j = jj * ISSUE_UNROLL + u
            for k in range(TOP_K_EXPERTS):
                row_copy(j, dest_ref[k, j]).start()
        return c
    lax.fori_loop(0, td // ISSUE_UNROLL, issue, 0)

    def drain(jj, c):
        for _ in range(ISSUE_UNROLL * TOP_K_EXPERTS):
            row_copy(0, 0).wait()
        return c
    lax.fori_loop(0, td // ISSUE_UNROLL, drain, 0)


def _dispatch(dest, pad_start, counts, padded, x2, n_rows):
    K, T = dest.shape
    D = x2.shape[1]
    td = min(DISPATCH_TOKENS, T)
    smem_tok = pl.BlockSpec((K, td), lambda i: (0, i), memory_space=pltpu.SMEM)
    smem_full = pl.BlockSpec(memory_space=pltpu.SMEM)
    return pl.pallas_call(
        _dispatch_kernel,
        grid=(T // td,),
        in_specs=[smem_tok, smem_full, smem_full, smem_full,
                  pl.BlockSpec((td, D), lambda i: (i, 0))],
        out_specs=pl.BlockSpec(memory_space=pl.ANY),
        out_shape=jax.ShapeDtypeStruct((n_rows, D), F32),
        scratch_shapes=[pltpu.VMEM((1, D), F32), pltpu.SemaphoreType.DMA(()), pltpu.SemaphoreType.DMA(())],
        compiler_params=pltpu.CompilerParams(
            dimension_semantics=("arbitrary",), vmem_limit_bytes=VMEM_LIMIT),
    )(dest, pad_start, counts, padded, x2)


def _expert_kernel(bexp_ref, nused_ref, xs_ref, wgu_ref, bgu_ref, wd_ref, bd_ref, y_ref, wgu_bf, wd_bf):
    j = pl.program_id(0)
    d_ff = wd_ref.shape[1]

    @pl.when(j < nused_ref[0])
    def _():
        prev = bexp_ref[jnp.maximum(j - 1, 0)]

        @pl.when(jnp.logical_or(j == 0, bexp_ref[j] != prev))
        def _():
            wgu_bf[...] = wgu_ref[0].astype(BF16)
            wd_bf[...] = wd_ref[0].astype(BF16)

        xb = xs_ref[...].astype(BF16)
        gu = jnp.dot(xb, wgu_bf[...], preferred_element_type=F32) + bgu_ref[0]
        g = jnp.minimum(gu[:, 0:d_ff], SWIGLU_LIMIT)
        u = jnp.clip(gu[:, d_ff:2 * d_ff], -SWIGLU_LIMIT, SWIGLU_LIMIT)
        act = (u + 1.0) * g * (1.0 / (1.0 + jnp.exp(-SWIGLU_ALPHA * g)))
        y_ref[...] = jnp.dot(act.astype(BF16), wd_bf[...], preferred_element_type=F32) + bd_ref[0]


def _experts(block_exp, n_used, xs, w_gate_up, b_gate_up, w_down, b_down):
    P, D = xs.shape
    blk = EXPERT_ROWS
    nblk = P // blk
    E, _, F2 = w_gate_up.shape
    d_ff = w_down.shape[1]
    row = lambda j, be, nu: (jnp.minimum(j, nu[0] - 1), 0)
    exp3 = lambda j, be, nu: (be[j], 0, 0)
    grid_spec = pltpu.PrefetchScalarGridSpec(
        num_scalar_prefetch=2,
        grid=(nblk,),
        in_specs=[
            pl.BlockSpec((blk, D), row),
            pl.BlockSpec((1, D, F2), exp3),
            pl.BlockSpec((1, 1, F2), exp3),
            pl.BlockSpec((1, d_ff, D), exp3),
            pl.BlockSpec((1, 1, D), exp3),
        ],
        out_specs=pl.BlockSpec((blk, D), row),
        scratch_shapes=[pltpu.VMEM((D, F2), BF16), pltpu.VMEM((d_ff, D), BF16)],
    )
    return pl.pallas_call(
        _expert_kernel,
        grid_spec=grid_spec,
        out_shape=jax.ShapeDtypeStruct((P, D), F32),
        compiler_params=pltpu.CompilerParams(
            dimension_semantics=("arbitrary",), vmem_limit_bytes=VMEM_LIMIT),
    )(block_exp, n_used, xs, w_gate_up, b_gate_up.reshape(E, 1, F2), w_down, b_down.reshape(E, 1, D))


def _combine_kernel(dest_ref, dest_next_ref, x2_ref, gate_ref, g3_ref, b3_ref, y_hbm, o_ref, ybuf, sem):
    i = pl.program_id(0)
    n = pl.num_programs(0)
    tc = x2_ref.shape[0]
    slot = i % 2

    def row_copy(src_ref, s, j, k):
        return pltpu.make_async_copy(y_hbm.at[pl.ds(src_ref[k, j], 1), :], ybuf.at[s, k, pl.ds(j, 1), :], sem.at[s])

    def gather(src_ref, s):
        def issue(jj, c):
            for u in range(ISSUE_UNROLL):
                for k in range(TOP_K_EXPERTS):
                    row_copy(src_ref, s, jj * ISSUE_UNROLL + u, k).start()
            return c
        lax.fori_loop(0, tc // ISSUE_UNROLL, issue, 0)

    @pl.when(i == 0)
    def _():
        gather(dest_ref, 0)

    @pl.when(i + 1 < n)
    def _():
        gather(dest_next_ref, 1 - slot)

    def drain(jj, c):
        for _ in range(ISSUE_UNROLL * TOP_K_EXPERTS):
            pltpu.make_async_copy(y_hbm.at[pl.ds(0, 1), :], ybuf.at[slot, 0, pl.ds(0, 1), :], sem.at[slot]).wait()
        return c
    lax.fori_loop(0, tc // ISSUE_UNROLL, drain, 0)

    moe = None
    for k in range(TOP_K_EXPERTS):
        t = ybuf[slot, k] * gate_ref[:, k:k + 1]
        moe = t if moe is None else moe + t
    o_ref[...] = _layer_norm(DN_ALPHA * x2_ref[...] + moe, g3_ref[...], b3_ref[...])


def _combine(dest, x2, gates, g3, b3, y):
    K, T = dest.shape
    D = x2.shape[1]
    tc = min(COMBINE_TOKENS, T)
    n = T // tc
    row = lambda i: (i, 0)
    full = lambda i: (0, 0)
    return pl.pallas_call(
        _combine_kernel,
        grid=(n,),
        in_specs=[pl.BlockSpec((K, tc), lambda i: (0, i), memory_space=pltpu.SMEM),
                  pl.BlockSpec((K, tc), lambda i: (0, jnp.minimum(i + 1, n - 1)), memory_space=pltpu.SMEM),
                  pl.BlockSpec((tc, D), row), pl.BlockSpec((tc, LANES), row),
                  pl.BlockSpec(g3.shape, full), pl.BlockSpec(b3.shape, full),
                  pl.BlockSpec(memory_space=pl.ANY)],
        out_specs=pl.BlockSpec((tc, D), row),
        out_shape=jax.ShapeDtypeStruct((T, D), F32),
        scratch_shapes=[pltpu.VMEM((2, K, tc, D), F32), pltpu.SemaphoreType.DMA((2,))],
        compiler_params=pltpu.CompilerParams(
            dimension_semantics=("arbitrary",), vmem_limit_bytes=VMEM_LIMIT),
    )(dest, dest, x2, gates, g3, b3, y)


def _pad_cols(w, width):
    return jnp.pad(w, ((0, 0), (0, width - w.shape[1])))


def _layer(x, mem, w_in, w_pool, pool_scale, ik_g, ik_b, kv_g, w_uk, w_uv, w_o, ln1_g, ln1_b,
           w_mq, w_mkv, w_mo, ln2_g, ln2_b, w_router, b_router, w_gate_up, b_gate_up, w_down, b_down,
           ln3_g, ln3_b):
    B, L, D = x.shape
    T = B * L
    row = lambda v: v.reshape(1, -1)

    o = 0
    pieces = []
    for width in (POOL_WIDTH, ATT_WIDTH, KV_RANK, IDX_HEADS * IDX_HEAD_DIM, IDX_HEAD_DIM, IDX_HEADS):
        pieces.append(w_in[:, o:o + width])
        o += width
    pieces[4] = _pad_cols(pieces[4], LANES)
    pieces[5] = _pad_cols(pieces[5], LANES)
    w_in_pad = jnp.concatenate(pieces, axis=1).astype(BF16)
    wuv_pad = jnp.zeros((ATT_HEADS, KV_RANK, 2 * ATT_HEAD_DIM), F32)
    for h in range(ATT_HEADS):
        wuv_pad = wuv_pad.at[h, :, (h % 2) * ATT_HEAD_DIM:(h % 2 + 1) * ATT_HEAD_DIM].set(w_uv[h])

    ypool, qlat, ckv, qidx, kidx, widx = _inproj(
        x, w_in_pad, w_pool.astype(BF16), row(pool_scale), row(kv_g), row(ik_g), row(ik_b), w_uk.astype(BF16))
    yatt = _dsa(qidx, widx, qlat, kidx, ckv, wuv_pad.astype(BF16))
    k_mem, v_mem = _memkv(mem, w_mkv.astype(BF16))
    x2, tope, gates = _mix(
        x.reshape(T, D), ypool.reshape(T, -1), yatt.reshape(T, -1), w_o.astype(BF16), row(ln1_g), row(ln1_b),
        k_mem, v_mem, w_mq.astype(BF16), w_mo.astype(BF16), row(ln2_g), row(ln2_b),
        jnp.transpose(w_router), b_router.reshape(-1, 1), B, L)

    rank, cnt = _rank(tope)
    counts = cnt[:, 0]
    blk = EXPERT_ROWS
    padded = (counts + blk - 1) // blk * blk
    pad_end = jnp.cumsum(padded)
    pad_start = pad_end - padded
    n_rows = T * TOP_K_EXPERTS + N_EXPERTS * blk
    nblk = n_rows // blk
    n_used = (pad_end[-1] // blk).astype(I32)
    blk_first = jnp.minimum(jnp.arange(nblk, dtype=I32), n_used - 1) * blk
    n_ended = jnp.sum((pad_end[None, :] <= blk_first[:, None]).astype(I32), axis=1)
    block_exp = jnp.minimum(n_ended, N_EXPERTS - 1).astype(I32)

    pad_start = pad_start.astype(I32)
    dest = _dest(pad_start, tope, rank)
    xs = _dispatch(dest, pad_start, counts, padded.astype(I32), x2, n_rows)
    y = _experts(block_exp, n_used.reshape(1), xs, w_gate_up, b_gate_up, w_down, b_down)
    out = _combine(dest, x2, gates, row(ln3_g), row(ln3_b), y)
    return out.reshape(B, L, D)


def kernel(x, mem, w_in, w_pool, pool_scale, idx_k_norm_g, idx_k_norm_b, kv_norm_g, w_uk, w_uv, w_o, ln1_g, ln1_b, w_mq, w_mkv, w_mo, ln2_g, ln2_b, w_router, b_router, w_gate_up, b_gate_up, w_down, b_down, ln3_g, ln3_b):
    assert w_in.shape[0] == DEPTH
    return _layer(x, mem, w_in[0], w_pool[0], pool_scale[0], idx_k_norm_g[0], idx_k_norm_b[0], kv_norm_g[0],
                  w_uk[0], w_uv[0], w_o[0], ln1_g[0], ln1_b[0], w_mq[0], w_mkv[0], w_mo[0], ln2_g[0], ln2_b[0],
                  w_router[0], b_router[0], w_gate_up[0], b_gate_up[0], w_down[0], b_down[0], ln3_g[0], ln3_b[0])
```

```python
import functools

import jax
import jax.numpy as jnp
from jax import lax
from jax.experimental import pallas as pl
from jax.experimental.pallas import tpu as pltpu

F32 = jnp.float32
BF16 = jnp.bfloat16
I32 = jnp.int32

POOL_WINDOWS = (2, 4, 8, 16)
POOL_GROUP_DIM = 128
POOL_WIDTH = 512
ATT_HEADS = 8
ATT_HEAD_DIM = 64
ATT_WIDTH = 512
KV_RANK = 128
IDX_HEADS = 8
IDX_HEAD_DIM = 64
TOPK_MAX = 256
MEM_HEADS = 4
N_EXPERTS = 32
TOP_K_EXPERTS = 4
SWIGLU_LIMIT = 7.0
SWIGLU_ALPHA = 1.702
LN_EPS = 1e-5
RMS_EPS = 1e-6
DEPTH = 1
DN_ALPHA = (2 * DEPTH) ** 0.25

LANES = 128
MAX_POOL_WINDOW = 16
VMEM_LIMIT = 56 * 1024 * 1024

PROJ_ROWS = 512
DSA_QUERIES = 128
DSA_KEYS = 512
MIX_ROWS = 1024
MIX_ROW_GROUPS = 1
RANK_TOKENS = 512
DEST_TOKENS = 2048
DISPATCH_TOKENS = 512
ISSUE_UNROLL = 4
EXPERT_ROWS = 256
COMBINE_TOKENS = 256

NEG_BIG = -1e30
INT_MIN = -(2 ** 31)
NEG_INF_KEY = INT_MIN + 0x007FFFFF

_NT = (((1,), (1,)), ((), ()))


def _layer_norm(v, g, b):
    mu = jnp.mean(v, axis=-1, keepdims=True)
    d = v - mu
    var = jnp.mean(d * d, axis=-1, keepdims=True)
    return d * lax.rsqrt(var + LN_EPS) * g + b


_C_POOL = 0
_C_Q = 512
_C_CKV = 1024
_C_QIDX = 1152
_C_KIDX = 1664
_C_WIDX = 1792
_IN_PAD = 1920


def _inproj_kernel(x_ref, w_ref, wpool_ref, pscale_ref, kvg_ref, ikg_ref, ikb_ref, wuk_ref,
                   ypool_ref, qlat_ref, ckv_ref, qidx_ref, kidx_ref, widx_ref, ubuf):
    li = pl.program_id(1)
    tm = x_ref.shape[1]
    halo = MAX_POOL_WINDOW
    xb = x_ref[0].astype(BF16)

    u = jnp.dot(xb, w_ref[:, _C_POOL:_C_POOL + POOL_WIDTH], preferred_element_type=F32)

    @pl.when(li == 0)
    def _():
        ubuf[0:halo, :] = jnp.zeros((halo, POOL_WIDTH), F32)

    ubuf[halo:halo + tm, :] = u
    pos = li * tm + lax.broadcasted_iota(I32, (tm, 1), 0)
    for g, w in enumerate(POOL_WINDOWS):
        c0 = g * POOL_GROUP_DIM
        c1 = c0 + POOL_GROUP_DIM
        ug = ubuf[halo:halo + tm, c0:c1]
        s = ug
        for j in range(1, w):
            s = s + ubuf[halo - j:halo - j + tm, c0:c1]
        cnt = jnp.minimum(pos + 1, w).astype(F32)
        d = s / cnt - ug
        y = jnp.dot(d.astype(BF16), wpool_ref[g], preferred_element_type=F32) * pscale_ref[:, c0:c1]
        ypool_ref[0, :, c0:c1] = y.astype(BF16)
    ubuf[0:halo, :] = ubuf[tm:tm + halo, :]

    q = jnp.dot(xb, w_ref[:, _C_Q:_C_Q + ATT_WIDTH], preferred_element_type=F32)
    att_scale = ATT_HEAD_DIM ** -0.5
    for h in range(ATT_HEADS):
        qh = q[:, h * ATT_HEAD_DIM:(h + 1) * ATT_HEAD_DIM].astype(BF16)
        ql = jnp.dot(qh, wuk_ref[h], preferred_element_type=F32) * att_scale
        qlat_ref[0, h] = ql.astype(BF16)

    c = jnp.dot(xb, w_ref[:, _C_CKV:_C_CKV + KV_RANK], preferred_element_type=F32)
    c = c * lax.rsqrt(jnp.mean(c * c, axis=-1, keepdims=True) + RMS_EPS) * kvg_ref[...]
    ckv_ref[0, :, 0:KV_RANK] = c.astype(BF16)
    ckv_ref[0, :, KV_RANK:2 * KV_RANK] = jnp.ones((tm, KV_RANK), BF16)

    qi = jnp.dot(xb, w_ref[:, _C_QIDX:_C_QIDX + IDX_HEADS * IDX_HEAD_DIM], preferred_element_type=F32)
    for h in range(IDX_HEADS):
        qidx_ref[0, h] = qi[:, h * IDX_HEAD_DIM:(h + 1) * IDX_HEAD_DIM].astype(BF16)
    kw = jnp.dot(xb, w_ref[:, _C_KIDX:_C_KIDX + 2 * LANES], preferred_element_type=F32)
    kr = kw[:, 0:IDX_HEAD_DIM]
    kidx_ref[0] = _layer_norm(kr, ikg_ref[...], ikb_ref[...]).astype(BF16)
    widx_ref[0] = kw[:, LANES:2 * LANES]


def _inproj(x, w_in_pad, w_pool, pool_scale, kv_g, ik_g, ik_b, w_uk):
    B, L, D = x.shape
    tm = min(PROJ_ROWS, L)
    nl = L // tm
    full2 = lambda b, l: (0, 0)
    full3 = lambda b, l: (0, 0, 0)
    return pl.pallas_call(
        _inproj_kernel,
        grid=(B, nl),
        in_specs=[
            pl.BlockSpec((1, tm, D), lambda b, l: (b, l, 0)),
            pl.BlockSpec(w_in_pad.shape, full2),
            pl.BlockSpec(w_pool.shape, full3),
            pl.BlockSpec(pool_scale.shape, full2),
            pl.BlockSpec(kv_g.shape, full2),
            pl.BlockSpec(ik_g.shape, full2),
            pl.BlockSpec(ik_b.shape, full2),
            pl.BlockSpec(w_uk.shape, full3),
        ],
        out_specs=[
            pl.BlockSpec((1, tm, POOL_WIDTH), lambda b, l: (b, l, 0)),
            pl.BlockSpec((1, ATT_HEADS, tm, KV_RANK), lambda b, l: (b, 0, l, 0)),
            pl.BlockSpec((1, tm, 2 * KV_RANK), lambda b, l: (b, l, 0)),
            pl.BlockSpec((1, IDX_HEADS, tm, IDX_HEAD_DIM), lambda b, l: (b, 0, l, 0)),
            pl.BlockSpec((1, tm, IDX_HEAD_DIM), lambda b, l: (b, l, 0)),
            pl.BlockSpec((1, tm, LANES), lambda b, l: (b, l, 0)),
        ],
        out_shape=[
            jax.ShapeDtypeStruct((B, L, POOL_WIDTH), BF16),
            jax.ShapeDtypeStruct((B, ATT_HEADS, L, KV_RANK), BF16),
            jax.ShapeDtypeStruct((B, L, 2 * KV_RANK), BF16),
            jax.ShapeDtypeStruct((B, IDX_HEADS, L, IDX_HEAD_DIM), BF16),
            jax.ShapeDtypeStruct((B, L, IDX_HEAD_DIM), BF16),
            jax.ShapeDtypeStruct((B, L, LANES), F32),
        ],
        scratch_shapes=[pltpu.VMEM((MAX_POOL_WINDOW + tm, POOL_WIDTH), F32)],
        compiler_params=pltpu.CompilerParams(
            dimension_semantics=("arbitrary", "arbitrary"), vmem_limit_bytes=VMEM_LIMIT),
    )(x, w_in_pad, w_pool, pool_scale, kv_g, ik_g, ik_b, w_uk)


def _sortable(score):
    bits = pltpu.bitcast(score, I32)
    return bits ^ ((bits >> 31) & 0x7FFFFFFF)


def _dsa_kernel(qidx_ref, widx_ref, qlat_ref, kidx_ref, ckv_ref, wuv_ref, o_ref,
                key_ref, keyt_ref, wb_ref, s_ref, s2_ref, p_ref, p2_ref, m_ref, acc_ref, sel_ref,
                *, top_k, idx_bits):
    qi = pl.program_id(1)
    H = ATT_HEADS
    qb = qidx_ref.shape[2]
    kc = s_ref.shape[1]
    qstart = qi * qb
    n_ch = (qstart + qb + kc - 1) // kc
    idx_scale = (IDX_HEAD_DIM ** -0.5) * (IDX_HEADS ** -0.5)

    qpos = qstart + lax.broadcasted_iota(I32, (qb, 1), 0)
    qpos_t = qstart + lax.broadcasted_iota(I32, (1, qb), 1)
    lane_pos = lax.broadcasted_iota(I32, (1, kc), 1)
    row_pos = lax.broadcasted_iota(I32, (kc, 1), 0)

    for h in range(H):
        wb_ref[h * qb:(h + 1) * qb, :] = jnp.broadcast_to(widx_ref[0, :, h:h + 1], (qb, LANES))

    def pipelined(produce, consume, buf_a, buf_b):
        produce(0, buf_a)
        n_pairs = (n_ch - 1) // 2

        def pair_body(i, carry):
            c = 2 * i
            consume(c, buf_a)
            produce(c + 1, buf_b)
            consume(c + 1, buf_b)
            produce(c + 2, buf_a)
            return carry

        lax.fori_loop(0, n_pairs, pair_body, 0)
        last = 2 * n_pairs

        @pl.when(n_ch - last == 1)
        def _():
            consume(last, buf_a)

        @pl.when(n_ch - last == 2)
        def _():
            consume(last, buf_a)
            produce(last + 1, buf_b)
            consume(last + 1, buf_b)

    def head_scores(c, buf):
        off = pl.multiple_of(c * kc, kc)
        kk = kidx_ref[0, pl.ds(off, kc), :]
        qs = qidx_ref[0].reshape(H * qb, IDX_HEAD_DIM)
        buf[...] = lax.dot_general(qs, kk, _NT, preferred_element_type=F32)

    def chunk_keys(c, buf):
        for j in range(kc // LANES):
            cs = slice(j * LANES, (j + 1) * LANES)
            acc = None
            for h in range(H):
                r = slice(h * qb, (h + 1) * qb)
                term = jnp.maximum(buf[r, cs], 0.0) * wb_ref[r, 0:LANES]
                acc = term if acc is None else acc + term
            score = acc * idx_scale
            score = jnp.where(c * kc + j * LANES + lane_pos[:, 0:LANES] <= qpos, score, -jnp.inf)
            key_ref[c, :, cs] = _sortable(score)
            keyt_ref[c, cs, :] = _sortable(jnp.transpose(score))

    pipelined(head_scores, chunk_keys, s_ref, s2_ref)

    def count(indicator):
        def body(c, cnt):
            m = indicator(keyt_ref[c], c * kc + row_pos)
            parts = [m[j * 8:(j + 1) * 8, :] for j in range(kc // 8)]
            while len(parts) > 1:
                parts = [parts[j] + parts[j + 1] for j in range(0, len(parts), 2)]
            return cnt + parts[0]
        cnt = lax.fori_loop(0, n_ch, body, jnp.zeros((8, qb), I32))
        return jnp.sum(cnt, axis=0, keepdims=True)

    def bit_body(i, t):
        cand = t + lax.shift_left(jnp.int32(1), 31 - i)
        n_ge = count(lambda k, _: jnp.where(k >= cand, 1, 0))
        return jnp.where(n_ge >= top_k, cand, t)

    thr_t = lax.fori_loop(0, 32, bit_body, jnp.full((1, qb), INT_MIN, I32))

    n_gt = count(lambda k, _: jnp.where(k > thr_t, 1, 0))
    n_ge = count(lambda k, _: jnp.where(k >= thr_t, 1, 0))
    need = top_k - n_gt
    surplus = jnp.where(thr_t > NEG_INF_KEY, jnp.where(n_ge > top_k, 1, 0), 0)
    sel_ref[0:1, :] = thr_t
    sel_ref[1:2, :] = qpos_t

    @pl.when(jnp.max(surplus) > 0)
    def _():
        def tie_body(i, m):
            cand = m + lax.shift_left(jnp.int32(1), idx_bits - 1 - i)
            n_before = count(lambda k, p: jnp.where(k == thr_t, jnp.where(p < cand, 1, 0), 0))
            return jnp.where(n_before < need, cand, m)
        last = lax.fori_loop(0, idx_bits, tie_body, jnp.zeros((1, qb), I32))
        sel_ref[1:2, :] = jnp.where(surplus > 0, jnp.minimum(last, qpos_t), qpos_t)

    thr = jnp.transpose(jnp.broadcast_to(sel_ref[0:1, :], (qb, qb)))[:, 0:1]
    tie_last = jnp.transpose(jnp.broadcast_to(sel_ref[1:2, :], (qb, qb)))[:, 0:1]

    def chunk_bias(c):
        key = key_ref[c]
        kpos = c * kc + lane_pos
        tie_bias = jnp.where(key == thr, jnp.where(kpos <= tie_last, 0.0, NEG_BIG), NEG_BIG)
        return jnp.where(key > thr, 0.0, tie_bias)

    n_lt = kc // LANES

    def head_logits(h, ck):
        return lax.dot_general(qlat_ref[0, h], ck, _NT, preferred_element_type=F32)

    m_ref[...] = jnp.full(m_ref.shape, NEG_BIG, F32)

    def all_logits(c, buf):
        off = pl.multiple_of(c * kc, kc)
        ck = ckv_ref[0, pl.ds(off, kc), 0:KV_RANK]
        for h in range(H):
            buf[h * qb:(h + 1) * qb, :] = head_logits(h, ck)

    def running_max(c, buf):
        bias = chunk_bias(c)
        for h in range(H):
            r = slice(h * qb, (h + 1) * qb)
            zm = buf[r, 0:LANES] + bias[:, 0:LANES]
            for j in range(1, n_lt):
                zm = jnp.maximum(zm, buf[r, j * LANES:(j + 1) * LANES] + bias[:, j * LANES:(j + 1) * LANES])
            m_ref[r, :] = jnp.maximum(m_ref[r, :], zm)

    pipelined(all_logits, running_max, s_ref, s2_ref)
    m_ref[...] = jnp.broadcast_to(jnp.max(m_ref[...], axis=1, keepdims=True), m_ref.shape)

    acc_ref[...] = jnp.zeros(acc_ref.shape, F32)

    def numerators(c, buf):
        off = pl.multiple_of(c * kc, kc)
        ck = ckv_ref[0, pl.ds(off, kc), 0:KV_RANK]
        bias = chunk_bias(c)
        for h in range(H):
            r = slice(h * qb, (h + 1) * qb)
            lg = head_logits(h, ck)
            mh = m_ref[r, :]
            for j in range(n_lt):
                cs = slice(j * LANES, (j + 1) * LANES)
                buf[r, cs] = jnp.exp(lg[:, cs] + bias[:, cs] - mh).astype(BF16)

    def accumulate(c, buf):
        off = pl.multiple_of(c * kc, kc)
        ckx = ckv_ref[0, pl.ds(off, kc), :]
        acc_ref[...] = acc_ref[...] + jnp.dot(buf[...], ckx, preferred_element_type=F32)

    pipelined(numerators, accumulate, p_ref, p2_ref)

    o_lat = (acc_ref[:, 0:KV_RANK] / acc_ref[:, KV_RANK:2 * KV_RANK]).astype(BF16)
    for h in range(0, H, 2):
        t = jnp.dot(o_lat[h * qb:(h + 1) * qb, :], wuv_ref[h], preferred_element_type=F32)
        t = t + jnp.dot(o_lat[(h + 1) * qb:(h + 2) * qb, :], wuv_ref[h + 1], preferred_element_type=F32)
        o_ref[0, :, h * ATT_HEAD_DIM:(h + 2) * ATT_HEAD_DIM] = t.astype(BF16)


def _dsa(qidx, widx, qlat, kidx, ckv, wuv_pad):
    B, H, L, _ = qidx.shape
    qb = min(DSA_QUERIES, L)
    kc = min(DSA_KEYS, L)
    top_k = min(TOPK_MAX, L // 4)
    idx_bits = max(1, (L - 1).bit_length())
    kern = functools.partial(_dsa_kernel, top_k=top_k, idx_bits=idx_bits)
    return pl.pallas_call(
        kern,
        grid=(B, L // qb),
        in_specs=[
            pl.BlockSpec((1, H, qb, IDX_HEAD_DIM), lambda b, q: (b, 0, q, 0)),
            pl.BlockSpec((1, qb, LANES), lambda b, q: (b, q, 0)),
            pl.BlockSpec((1, H, qb, KV_RANK), lambda b, q: (b, 0, q, 0)),
            pl.BlockSpec((1, L, IDX_HEAD_DIM), lambda b, q: (b, 0, 0)),
            pl.BlockSpec((1, L, 2 * KV_RANK), lambda b, q: (b, 0, 0)),
            pl.BlockSpec(wuv_pad.shape, lambda b, q: (0, 0, 0)),
        ],
        out_specs=pl.BlockSpec((1, qb, ATT_WIDTH), lambda b, q: (b, q, 0)),
        out_shape=jax.ShapeDtypeStruct((B, L, ATT_WIDTH), BF16),
        scratch_shapes=[
            pltpu.VMEM((L // kc, qb, kc), I32),
            pltpu.VMEM((L // kc, kc, qb), I32),
            pltpu.VMEM((H * qb, LANES), F32),
            pltpu.VMEM((H * qb, kc), F32),
            pltpu.VMEM((H * qb, kc), F32),
            pltpu.VMEM((H * qb, kc), BF16),
            pltpu.VMEM((H * qb, kc), BF16),
            pltpu.VMEM((H * qb, LANES), F32),
            pltpu.VMEM((H * qb, 2 * KV_RANK), F32),
            pltpu.VMEM((8, qb), I32),
        ],
        compiler_params=pltpu.CompilerParams(
            dimension_semantics=("arbitrary", "arbitrary"), vmem_limit_bytes=VMEM_LIMIT),
    )(qidx, widx, qlat, kidx, ckv, wuv_pad)


def _memkv_kernel(mem_ref, w_ref, k_ref, v_ref):
    d = k_ref.shape[2]
    kv = jnp.dot(mem_ref[0].astype(BF16), w_ref[...], preferred_element_type=F32)
    k_ref[0] = kv[:, 0:d].astype(BF16)
    v_ref[0] = kv[:, d:2 * d].astype(BF16)


def _memkv(mem, w_mkv):
    B, M, D = mem.shape
    return pl.pallas_call(
        _memkv_kernel,
        grid=(B,),
        in_specs=[pl.BlockSpec((1, M, D), lambda b: (b, 0, 0)),
                  pl.BlockSpec(w_mkv.shape, lambda b: (0, 0))],
        out_specs=[pl.BlockSpec((1, M, D), lambda b: (b, 0, 0)),
                   pl.BlockSpec((1, M, D), lambda b: (b, 0, 0))],
        out_shape=[jax.ShapeDtypeStruct((B, M, D), BF16), jax.ShapeDtypeStruct((B, M, D), BF16)],
        compiler_params=pltpu.CompilerParams(
            dimension_semantics=("arbitrary",), vmem_limit_bytes=VMEM_LIMIT),
    )(mem, w_mkv)


def _split3(v):
    hi = v.astype(BF16)
    r1 = v - hi.astype(F32)
    mid = r1.astype(BF16)
    lo = (r1 - mid.astype(F32)).astype(BF16)
    return hi, mid, lo


def _mix_kernel(x_ref, yp_ref, ya_ref, wo_ref, g1_ref, b1_ref, km_ref, vm_ref, wq_ref, wmo_ref,
                g2_ref, b2_ref, wr_ref, br_ref, x2_ref, tope_ref, gate_ref):
    tm, d = x_ref.shape
    hd = d // MEM_HEADS
    pw = yp_ref.shape[1]
    ws = _split3(wr_ref[...])

    def stage_mix(rs):
        mix = jnp.dot(yp_ref[rs, :], wo_ref[0:pw, :], preferred_element_type=F32)
        mix = mix + jnp.dot(ya_ref[rs, :], wo_ref[pw:, :], preferred_element_type=F32)
        return _layer_norm(DN_ALPHA * x_ref[rs, :] + mix, g1_ref[...], b1_ref[...])

    def stage_mem(x1):
        q = jnp.dot(x1.astype(BF16), wq_ref[...], preferred_element_type=F32).astype(BF16)
        scale = hd ** -0.5
        att = None
        for h in range(MEM_HEADS):
            c = slice(h * hd, (h + 1) * hd)
            lg = lax.dot_general(q[:, c], km_ref[0, :, c], _NT, preferred_element_type=F32) * scale
            p = jnp.exp(lg - jnp.max(lg, axis=-1, keepdims=True))
            p = p / jnp.sum(p, axis=-1, keepdims=True)
            oh = jnp.dot(p.astype(BF16), vm_ref[0, :, c], preferred_element_type=F32).astype(BF16)
            t = jnp.dot(oh, wmo_ref[c, :], preferred_element_type=F32)
            att = t if att is None else att + t
        return att

    def stage_route(rs, x1, att):
        n = rs.stop - rs.start
        x2 = _layer_norm(DN_ALPHA * x1 + att, g2_ref[...], b2_ref[...])
        x2_ref[rs, :] = x2

        xs = _split3(x2)
        lt = None
        for i, j in ((0, 0), (0, 1), (1, 0)):
            t = lax.dot_general(ws[j], xs[i], _NT, preferred_element_type=F32)
            lt = t if lt is None else lt + t
        lt = lt + br_ref[...]
        n_e = lt.shape[0]
        eidx = lax.broadcasted_iota(I32, lt.shape, 0)
        vals, idxs = [], []
        for _ in range(TOP_K_EXPERTS):
            mx = jnp.max(lt, axis=0, keepdims=True)
            ix = jnp.min(jnp.where(lt == mx, eidx, n_e), axis=0, keepdims=True)
            vals.append(mx)
            idxs.append(ix)
            lt = jnp.where(eidx == ix, -jnp.inf, lt)
        tope_ref[:, rs] = jnp.concatenate(idxs, axis=0)
        ex = [jnp.exp(v - vals[0]) for v in vals]
        den = ex[0]
        for e_ in ex[1:]:
            den = den + e_
        gates = jnp.concatenate([e_ / den for e_ in ex] + [jnp.zeros((LANES - TOP_K_EXPERTS, n), F32)], axis=0)
        gate_ref[rs, :] = jnp.transpose(gates)

    n_groups = MIX_ROW_GROUPS if tm % (MIX_ROW_GROUPS * LANES) == 0 else 1
    groups = [slice(g * (tm // n_groups), (g + 1) * (tm // n_groups)) for g in range(n_groups)]
    x1s = [stage_mix(rs) for rs in groups]
    atts = [stage_mem(x1) for x1 in x1s]
    for rs, x1, att in zip(groups, x1s, atts):
        stage_route(rs, x1, att)


def _mix(x2d, ypool, yatt, w_o, g1, b1, k_mem, v_mem, w_mq, w_mo, g2, b2, w_rt, b_r, B, L):
    T, D = x2d.shape
    tm = min(MIX_ROWS, L)
    nl = L // tm
    M = k_mem.shape[1]
    row = lambda i: (i, 0)
    full = lambda i: (0, 0)
    return pl.pallas_call(
        _mix_kernel,
        grid=(T // tm,),
        in_specs=[
            pl.BlockSpec((tm, D), row),
            pl.BlockSpec((tm, ypool.shape[1]), row),
            pl.BlockSpec((tm, yatt.shape[1]), row),
            pl.BlockSpec(w_o.shape, full),
            pl.BlockSpec(g1.shape, full),
            pl.BlockSpec(b1.shape, full),
            pl.BlockSpec((1, M, D), lambda i: (i // nl, 0, 0)),
            pl.BlockSpec((1, M, D), lambda i: (i // nl, 0, 0)),
            pl.BlockSpec(w_mq.shape, full),
            pl.BlockSpec(w_mo.shape, full),
            pl.BlockSpec(g2.shape, full),
            pl.BlockSpec(b2.shape, full),
            pl.BlockSpec(w_rt.shape, full),
            pl.BlockSpec(b_r.shape, full),
        ],
        out_specs=[
            pl.BlockSpec((tm, D), row),
            pl.BlockSpec((TOP_K_EXPERTS, tm), lambda i: (0, i)),
            pl.BlockSpec((tm, LANES), row),
        ],
        out_shape=[
            jax.ShapeDtypeStruct((T, D), F32),
            jax.ShapeDtypeStruct((TOP_K_EXPERTS, T), I32),
            jax.ShapeDtypeStruct((T, LANES), F32),
        ],
        compiler_params=pltpu.CompilerParams(
            dimension_semantics=("arbitrary",), vmem_limit_bytes=VMEM_LIMIT),
    )(x2d, ypool, yatt, w_o, g1, b1, k_mem, v_mem, w_mq, w_mo, g2, b2, w_rt, b_r)


def _rank_kernel(tope_ref, rank_ref, cnt_ref, carry_ref):
    i = pl.program_id(0)
    tr = tope_ref.shape[1]

    @pl.when(i == 0)
    def _():
        carry_ref[...] = jnp.zeros(carry_ref.shape, F32)

    eidx = lax.broadcasted_iota(I32, (N_EXPERTS, tr), 0)
    onehot = jnp.zeros((N_EXPERTS, tr), F32)
    for k in range(TOP_K_EXPERTS):
        onehot = onehot + jnp.where(eidx == tope_ref[k:k + 1, :], 1.0, 0.0)
    before = jnp.where(lax.broadcasted_iota(I32, (tr, tr), 0) < lax.broadcasted_iota(I32, (tr, tr), 1), 1.0, 0.0)
    excl = jnp.dot(onehot.astype(BF16), before.astype(BF16), preferred_element_type=F32)
    rank_full = excl + carry_ref[:, 0:1]
    rows = []
    for k in range(TOP_K_EXPERTS):
        rows.append(jnp.sum(jnp.where(eidx == tope_ref[k:k + 1, :], rank_full, 0.0), axis=0, keepdims=True))
    rank_ref[...] = jnp.concatenate(rows, axis=0).astype(I32)
    carry_ref[...] = carry_ref[...] + jnp.sum(onehot, axis=1, keepdims=True)
    cnt_ref[...] = carry_ref[...].astype(I32)


def _rank(tope):
    K, T = tope.shape
    tr = min(RANK_TOKENS, T)
    return pl.pallas_call(
        _rank_kernel,
        grid=(T // tr,),
        in_specs=[pl.BlockSpec((K, tr), lambda i: (0, i))],
        out_specs=[pl.BlockSpec((K, tr), lambda i: (0, i)),
                   pl.BlockSpec((N_EXPERTS, LANES), lambda i: (0, 0))],
        out_shape=[jax.ShapeDtypeStruct((K, T), I32), jax.ShapeDtypeStruct((N_EXPERTS, LANES), I32)],
        scratch_shapes=[pltpu.VMEM((N_EXPERTS, LANES), F32)],
        compiler_params=pltpu.CompilerParams(dimension_semantics=("arbitrary",)),
    )(tope)


def _dest_kernel(start_ref, tope_ref, rank_ref, dest_ref):
    tope = tope_ref[...]
    base = jnp.zeros(tope.shape, I32)
    for e in range(N_EXPERTS):
        base = jnp.where(tope == e, start_ref[e], base)
    dest_ref[...] = base + rank_ref[...]


def _dest(pad_start, tope, rank):
    K, T = tope.shape
    tt = min(DEST_TOKENS, T)
    blk = pl.BlockSpec((K, tt), lambda i: (0, i))
    return pl.pallas_call(
        _dest_kernel,
        grid=(T // tt,),
        in_specs=[pl.BlockSpec(memory_space=pltpu.SMEM), blk, blk],
        out_specs=blk,
        out_shape=jax.ShapeDtypeStruct((K, T), I32),
        compiler_params=pltpu.CompilerParams(dimension_semantics=("arbitrary",)),
    )(pad_start, tope, rank)


def _dispatch_kernel(dest_ref, start_ref, cnt_ref, padded_ref, x_ref, xs_hbm, zrow, sem, zsem):
    i = pl.program_id(0)
    td = dest_ref.shape[1]

    def row_copy(j, dst):
        return pltpu.make_async_copy(x_ref.at[pl.ds(j, 1), :], xs_hbm.at[pl.ds(dst, 1), :], sem)

    def zero_copy(dst):
        return pltpu.make_async_copy(zrow, xs_hbm.at[pl.ds(dst, 1), :], zsem)

    @pl.when(i == 0)
    def _():
        zrow[...] = jnp.zeros(zrow.shape, F32)
        for e in range(N_EXPERTS):
            first = start_ref[e] + cnt_ref[e]
            n_pad = padded_ref[e] - cnt_ref[e]

            def zstart(r, c):
                zero_copy(first + r).start()
                return c
            lax.fori_loop(0, n_pad, zstart, 0)

            def zwait(r, c):
                zero_copy(first + r).wait()
                return c
            lax.fori_loop(0, n_pad, zwait, 0)

    def issue(jj, c):
        for u in range(ISSUE_UNROLL):
            j = jj * ISSUE_UNROLL + u
            for k in range(TOP_K_EXPERTS):
                row_copy(j, dest_ref[k, j]).start()
        return c
    lax.fori_loop(0, td // ISSUE_UNROLL, issue, 0)

    def drain(jj, c):
        for _ in range(ISSUE_UNROLL * TOP_K_EXPERTS):
            row_copy(0, 0).wait()
        return c
    lax.fori_loop(0, td // ISSUE_UNROLL, drain, 0)


def _dispatch(dest, pad_start, counts, padded, x2, n_rows):
    K, T = dest.shape
    D = x2.shape[1]
    td = min(DISPATCH_TOKENS, T)
    smem_tok = pl.BlockSpec((K, td), lambda i: (0, i), memory_space=pltpu.SMEM)
    smem_full = pl.BlockSpec(memory_space=pltpu.SMEM)
    return pl.pallas_call(
        _dispatch_kernel,
        grid=(T // td,),
        in_specs=[smem_tok, smem_full, smem_full, smem_full,
                  pl.BlockSpec((td, D), lambda i: (i, 0))],
        out_specs=pl.BlockSpec(memory_space=pl.ANY),
        out_shape=jax.ShapeDtypeStruct((n_rows, D), F32),
        scratch_shapes=[pltpu.VMEM((1, D), F32), pltpu.SemaphoreType.DMA(()), pltpu.SemaphoreType.DMA(())],
        compiler_params=pltpu.CompilerParams(
            dimension_semantics=("arbitrary",), vmem_limit_bytes=VMEM_LIMIT),
    )(dest, pad_start, counts, padded, x2)


def _expert_kernel(bexp_ref, nused_ref, xs_ref, wgu_ref, bgu_ref, wd_ref, bd_ref, y_ref, wgu_bf, wd_bf):
    j = pl.program_id(0)
    d_ff = wd_ref.shape[1]

    @pl.when(j < nused_ref[0])
    def _():
        prev = bexp_ref[jnp.maximum(j - 1, 0)]

        @pl.when(jnp.logical_or(j == 0, bexp_ref[j] != prev))
        def _():
            wgu_bf[...] = wgu_ref[0].astype(BF16)
            wd_bf[...] = wd_ref[0].astype(BF16)

        xb = xs_ref[...].astype(BF16)
        gu = jnp.dot(xb, wgu_bf[...], preferred_element_type=F32) + bgu_ref[0]
        g = jnp.minimum(gu[:, 0:d_ff], SWIGLU_LIMIT)
        u = jnp.clip(gu[:, d_ff:2 * d_ff], -SWIGLU_LIMIT, SWIGLU_LIMIT)
        act = (u + 1.0) * g * (1.0 / (1.0 + jnp.exp(-SWIGLU_ALPHA * g)))
        y_ref[...] = jnp.dot(act.astype(BF16), wd_bf[...], preferred_element_type=F32) + bd_ref[0]


def _experts(block_exp, n_used, xs, w_gate_up, b_gate_up, w_down, b_down):
    P, D = xs.shape
    blk = EXPERT_ROWS
    nblk = P // blk
    E, _, F2 = w_gate_up.shape
    d_ff = w_down.shape[1]
    row = lambda j, be, nu: (jnp.minimum(j, nu[0] - 1), 0)
    exp3 = lambda j, be, nu: (be[j], 0, 0)
    grid_spec = pltpu.PrefetchScalarGridSpec(
        num_scalar_prefetch=2,
        grid=(nblk,),
        in_specs=[
            pl.BlockSpec((blk, D), row),
            pl.BlockSpec((1, D, F2), exp3),
            pl.BlockSpec((1, 1, F2), exp3),
            pl.BlockSpec((1, d_ff, D), exp3),
            pl.BlockSpec((1, 1, D), exp3),
        ],
        out_specs=pl.BlockSpec((blk, D), row),
        scratch_shapes=[pltpu.VMEM((D, F2), BF16), pltpu.VMEM((d_ff, D), BF16)],
    )
    return pl.pallas_call(
        _expert_kernel,
        grid_spec=grid_spec,
        out_shape=jax.ShapeDtypeStruct((P, D), F32),
        compiler_params=pltpu.CompilerParams(
            dimension_semantics=("arbitrary",), vmem_limit_bytes=VMEM_LIMIT),
    )(block_exp, n_used, xs, w_gate_up, b_gate_up.reshape(E, 1, F2), w_down, b_down.reshape(E, 1, D))


def _combine_kernel(dest_ref, dest_next_ref, x2_ref, gate_ref, g3_ref, b3_ref, y_hbm, o_ref, ybuf, sem):
    i = pl.program_id(0)
    n = pl.num_programs(0)
    tc = x2_ref.shape[0]
    slot = i % 2

    def row_copy(src_ref, s, j, k):
        return pltpu.make_async_copy(y_hbm.at[pl.ds(src_ref[k, j], 1), :], ybuf.at[s, k, pl.ds(j, 1), :], sem.at[s])

    def gather(src_ref, s):
        def issue(jj, c):
            for u in range(ISSUE_UNROLL):
                for k in range(TOP_K_EXPERTS):
                    row_copy(src_ref, s, jj * ISSUE_UNROLL + u, k).start()
            return c
        lax.fori_loop(0, tc // ISSUE_UNROLL, issue, 0)

    @pl.when(i == 0)
    def _():
        gather(dest_ref, 0)

    @pl.when(i + 1 < n)
    def _():
        gather(dest_next_ref, 1 - slot)

    def drain(jj, c):
        for _ in range(ISSUE_UNROLL * TOP_K_EXPERTS):
            pltpu.make_async_copy(y_hbm.at[pl.ds(0, 1), :], ybuf.at[slot, 0, pl.ds(0, 1), :], sem.at[slot]).wait()
        return c
    lax.fori_loop(0, tc // ISSUE_UNROLL, drain, 0)

    moe = None
    for k in range(TOP_K_EXPERTS):
        t = ybuf[slot, k] * gate_ref[:, k:k + 1]
        moe = t if moe is None else moe + t
    o_ref[...] = _layer_norm(DN_ALPHA * x2_ref[...] + moe, g3_ref[...], b3_ref[...])


def _combine(dest, x2, gates, g3, b3, y):
    K, T = dest.shape
    D = x2.shape[1]
    tc = min(COMBINE_TOKENS, T)
    n = T // tc
    row = lambda i: (i, 0)
    full = lambda i: (0, 0)
    return pl.pallas_call(
        _combine_kernel,
        grid=(n,),
        in_specs=[pl.BlockSpec((K, tc), lambda i: (0, i), memory_space=pltpu.SMEM),
                  pl.BlockSpec((K, tc), lambda i: (0, jnp.minimum(i + 1, n - 1)), memory_space=pltpu.SMEM),
                  pl.BlockSpec((tc, D), row), pl.BlockSpec((tc, LANES), row),
                  pl.BlockSpec(g3.shape, full), pl.BlockSpec(b3.shape, full),
                  pl.BlockSpec(memory_space=pl.ANY)],
        out_specs=pl.BlockSpec((tc, D), row),
        out_shape=jax.ShapeDtypeStruct((T, D), F32),
        scratch_shapes=[pltpu.VMEM((2, K, tc, D), F32), pltpu.SemaphoreType.DMA((2,))],
        compiler_params=pltpu.CompilerParams(
            dimension_semantics=("arbitrary",), vmem_limit_bytes=VMEM_LIMIT),
    )(dest, dest, x2, gates, g3, b3, y)


def _pad_cols(w, width):
    return jnp.pad(w, ((0, 0), (0, width - w.shape[1])))


def _layer(x, mem, w_in, w_pool, pool_scale, ik_g, ik_b, kv_g, w_uk, w_uv, w_o, ln1_g, ln1_b,
           w_mq, w_mkv, w_mo, ln2_g, ln2_b, w_router, b_router, w_gate_up, b_gate_up, w_down, b_down,
           ln3_g, ln3_b):
    B, L, D = x.shape
    T = B * L
    row = lambda v: v.reshape(1, -1)

    o = 0
    pieces = []
    for width in (POOL_WIDTH, ATT_WIDTH, KV_RANK, IDX_HEADS * IDX_HEAD_DIM, IDX_HEAD_DIM, IDX_HEADS):
        pieces.append(w_in[:, o:o + width])
        o += width
    pieces[4] = _pad_cols(pieces[4], LANES)
    pieces[5] = _pad_cols(pieces[5], LANES)
    w_in_pad = jnp.concatenate(pieces, axis=1).astype(BF16)
    wuv_pad = jnp.zeros((ATT_HEADS, KV_RANK, 2 * ATT_HEAD_DIM), F32)
    for h in range(ATT_HEADS):
        wuv_pad = wuv_pad.at[h, :, (h % 2) * ATT_HEAD_DIM:(h % 2 + 1) * ATT_HEAD_DIM].set(w_uv[h])

    ypool, qlat, ckv, qidx, kidx, widx = _inproj(
        x, w_in_pad, w_pool.astype(BF16), row(pool_scale), row(kv_g), row(ik_g), row(ik_b), w_uk.astype(BF16))
    yatt = _dsa(qidx, widx, qlat, kidx, ckv, wuv_pad.astype(BF16))
    k_mem, v_mem = _memkv(mem, w_mkv.astype(BF16))
    x2, tope, gates = _mix(
        x.reshape(T, D), ypool.reshape(T, -1), yatt.reshape(T, -1), w_o.astype(BF16), row(ln1_g), row(ln1_b),
        k_mem, v_mem, w_mq.astype(BF16), w_mo.astype(BF16), row(ln2_g), row(ln2_b),
        jnp.transpose(w_router), b_router.reshape(-1, 1), B, L)

    rank, cnt = _rank(tope)
    counts = cnt[:, 0]
    blk = EXPERT_ROWS
    padded = (counts + blk - 1) // blk * blk
    pad_end = jnp.cumsum(padded)
    pad_start = pad_end - padded
    n_rows = T * TOP_K_EXPERTS + N_EXPERTS * blk
    nblk = n_rows // blk
    n_used = (pad_end[-1] // blk).astype(I32)
    blk_first = jnp.minimum(jnp.arange(nblk, dtype=I32), n_used - 1) * blk
    n_ended = jnp.sum((pad_end[None, :] <= blk_first[:, None]).astype(I32), axis=1)
    block_exp = jnp.minimum(n_ended, N_EXPERTS - 1).astype(I32)

    pad_start = pad_start.astype(I32)
    dest = _dest(pad_start, tope, rank)
    xs = _dispatch(dest, pad_start, counts, padded.astype(I32), x2, n_rows)
    y = _experts(block_exp, n_used.reshape(1), xs, w_gate_up, b_gate_up, w_down, b_down)
    out = _combine(dest, x2, gates, row(ln3_g), row(ln3_b), y)
    return out.reshape(B, L, D)


def kernel(x, mem, w_in, w_pool, pool_scale, idx_k_norm_g, idx_k_norm_b, kv_norm_g, w_uk, w_uv, w_o, ln1_g, ln1_b, w_mq, w_mkv, w_mo, ln2_g, ln2_b, w_router, b_router, w_gate_up, b_gate_up, w_down, b_down, ln3_g, ln3_b):
    assert w_in.shape[0] == DEPTH
    return _layer(x, mem, w_in[0], w_pool[0], pool_scale[0], idx_k_norm_g[0], idx_k_norm_b[0], kv_norm_g[0],
                  w_uk[0], w_uv[0], w_o[0], ln1_g[0], ln1_b[0], w_mq[0], w_mkv[0], w_mo[0], ln2_g[0], ln2_b[0],
                  w_router[0], b_router[0], w_gate_up[0], b_gate_up[0], w_down[0], b_down[0], ln3_g[0], ln3_b[0])
```

```python
import functools

import jax
import jax.numpy as jnp
from jax import lax
from jax.experimental import pallas as pl
from jax.experimental.pallas import tpu as pltpu

F32 = jnp.float32
BF16 = jnp.bfloat16
I32 = jnp.int32

POOL_WINDOWS = (2, 4, 8, 16)
POOL_GROUP_DIM = 128
POOL_WIDTH = 512
ATT_HEADS = 8
ATT_HEAD_DIM = 64
ATT_WIDTH = 512
KV_RANK = 128
IDX_HEADS = 8
IDX_HEAD_DIM = 64
TOPK_MAX = 256
MEM_HEADS = 4
N_EXPERTS = 32
TOP_K_EXPERTS = 4
SWIGLU_LIMIT = 7.0
SWIGLU_ALPHA = 1.702
LN_EPS = 1e-5
RMS_EPS = 1e-6
DEPTH = 1
DN_ALPHA = (2 * DEPTH) ** 0.25

LANES = 128
MAX_POOL_WINDOW = 16
VMEM_LIMIT = 56 * 1024 * 1024

PROJ_ROWS = 512
DSA_QUERIES = 128
DSA_KEYS = 512
MIX_ROWS = 1024
MIX_ROW_GROUPS = 1
RANK_TOKENS = 512
DEST_TOKENS = 2048
DISPATCH_TOKENS = 512
ISSUE_UNROLL = 4
EXPERT_ROWS = 256
COMBINE_TOKENS = 256

NEG_BIG = -1e30
INT_MIN = -(2 ** 31)
NEG_INF_KEY = INT_MIN + 0x007FFFFF

_NT = (((1,), (1,)), ((), ()))


def _layer_norm(v, g, b):
    mu = jnp.mean(v, axis=-1, keepdims=True)
    d = v - mu
    var = jnp.mean(d * d, axis=-1, keepdims=True)
    return d * lax.rsqrt(var + LN_EPS) * g + b


_C_POOL = 0
_C_Q = 512
_C_CKV = 1024
_C_QIDX = 1152
_C_KIDX = 1664
_C_WIDX = 1792
_IN_PAD = 1920


def _inproj_kernel(x_ref, w_ref, wpool_ref, pscale_ref, kvg_ref, ikg_ref, ikb_ref, wuk_ref,
                   ypool_ref, qlat_ref, ckv_ref, qidx_ref, kidx_ref, widx_ref, ubuf):
    li = pl.program_id(1)
    tm = x_ref.shape[1]
    halo = MAX_POOL_WINDOW
    xb = x_ref[0].astype(BF16)

    u = jnp.dot(xb, w_ref[:, _C_POOL:_C_POOL + POOL_WIDTH], preferred_element_type=F32)

    @pl.when(li == 0)
    def _():
        ubuf[0:halo, :] = jnp.zeros((halo, POOL_WIDTH), F32)

    ubuf[halo:halo + tm, :] = u
    pos = li * tm + lax.broadcasted_iota(I32, (tm, 1), 0)
    for g, w in enumerate(POOL_WINDOWS):
        c0 = g * POOL_GROUP_DIM
        c1 = c0 + POOL_GROUP_DIM
        ug = ubuf[halo:halo + tm, c0:c1]
        s = ug
        for j in range(1, w):
            s = s + ubuf[halo - j:halo - j + tm, c0:c1]
        cnt = jnp.minimum(pos + 1, w).astype(F32)
        d = s / cnt - ug
        y = jnp.dot(d.astype(BF16), wpool_ref[g], preferred_element_type=F32) * pscale_ref[:, c0:c1]
        ypool_ref[0, :, c0:c1] = y.astype(BF16)
    ubuf[0:halo, :] = ubuf[tm:tm + halo, :]

    q = jnp.dot(xb, w_ref[:, _C_Q:_C_Q + ATT_WIDTH], preferred_element_type=F32)
    att_scale = ATT_HEAD_DIM ** -0.5
    for h in range(ATT_HEADS):
        qh = q[:, h * ATT_HEAD_DIM:(h + 1) * ATT_HEAD_DIM].astype(BF16)
        ql = jnp.dot(qh, wuk_ref[h], preferred_element_type=F32) * att_scale
        qlat_ref[0, h] = ql.astype(BF16)

    c = jnp.dot(xb, w_ref[:, _C_CKV:_C_CKV + KV_RANK], preferred_element_type=F32)
    c = c * lax.rsqrt(jnp.mean(c * c, axis=-1, keepdims=True) + RMS_EPS) * kvg_ref[...]
    ckv_ref[0, :, 0:KV_RANK] = c.astype(BF16)
    ckv_ref[0, :, KV_RANK:2 * KV_RANK] = jnp.ones((tm, KV_RANK), BF16)

    qi = jnp.dot(xb, w_ref[:, _C_QIDX:_C_QIDX + IDX_HEADS * IDX_HEAD_DIM], preferred_element_type=F32)
    for h in range(IDX_HEADS):
        qidx_ref[0, h] = qi[:, h * IDX_HEAD_DIM:(h + 1) * IDX_HEAD_DIM].astype(BF16)
    kw = jnp.dot(xb, w_ref[:, _C_KIDX:_C_KIDX + 2 * LANES], preferred_element_type=F32)
    kr = kw[:, 0:IDX_HEAD_DIM]
    kidx_ref[0] = _layer_norm(kr, ikg_ref[...], ikb_ref[...]).astype(BF16)
    widx_ref[0] = kw[:, LANES:2 * LANES]


def _inproj(x, w_in_pad, w_pool, pool_scale, kv_g, ik_g, ik_b, w_uk):
    B, L, D = x.shape
    tm = min(PROJ_ROWS, L)
    nl = L // tm
    full2 = lambda b, l: (0, 0)
    full3 = lambda b, l: (0, 0, 0)
    return pl.pallas_call(
        _inproj_kernel,
        grid=(B, nl),
        in_specs=[
            pl.BlockSpec((1, tm, D), lambda b, l: (b, l, 0)),
            pl.BlockSpec(w_in_pad.shape, full2),
            pl.BlockSpec(w_pool.shape, full3),
            pl.BlockSpec(pool_scale.shape, full2),
            pl.BlockSpec(kv_g.shape, full2),
            pl.BlockSpec(ik_g.shape, full2),
            pl.BlockSpec(ik_b.shape, full2),
            pl.BlockSpec(w_uk.shape, full3),
        ],
        out_specs=[
            pl.BlockSpec((1, tm, POOL_WIDTH), lambda b, l: (b, l, 0)),
            pl.BlockSpec((1, ATT_HEADS, tm, KV_RANK), lambda b, l: (b, 0, l, 0)),
            pl.BlockSpec((1, tm, 2 * KV_RANK), lambda b, l: (b, l, 0)),
            pl.BlockSpec((1, IDX_HEADS, tm, IDX_HEAD_DIM), lambda b, l: (b, 0, l, 0)),
            pl.BlockSpec((1, tm, IDX_HEAD_DIM), lambda b, l: (b, l, 0)),
            pl.BlockSpec((1, tm, LANES), lambda b, l: (b, l, 0)),
        ],
        out_shape=[
            jax.ShapeDtypeStruct((B, L, POOL_WIDTH), BF16),
            jax.ShapeDtypeStruct((B, ATT_HEADS, L, KV_RANK), BF16),
            jax.ShapeDtypeStruct((B, L, 2 * KV_RANK), BF16),
            jax.ShapeDtypeStruct((B, IDX_HEADS, L, IDX_HEAD_DIM), BF16),
            jax.ShapeDtypeStruct((B, L, IDX_HEAD_DIM), BF16),
            jax.ShapeDtypeStruct((B, L, LANES), F32),
        ],
        scratch_shapes=[pltpu.VMEM((MAX_POOL_WINDOW + tm, POOL_WIDTH), F32)],
        compiler_params=pltpu.CompilerParams(
            dimension_semantics=("arbitrary", "arbitrary"), vmem_limit_bytes=VMEM_LIMIT),
    )(x, w_in_pad, w_pool, pool_scale, kv_g, ik_g, ik_b, w_uk)


def _sortable(score):
    bits = pltpu.bitcast(score, I32)
    return bits ^ ((bits >> 31) & 0x7FFFFFFF)


def _dsa_kernel(qidx_ref, widx_ref, qlat_ref, kidx_ref, ckv_ref, wuv_ref, o_ref,
                key_ref, keyt_ref, wb_ref, s_ref, s2_ref, p_ref, p2_ref, a_ref, a2_ref, m_ref, acc_ref, sel_ref,
                *, top_k, idx_bits):
    qi = pl.program_id(1)
    H = ATT_HEADS
    qb = qidx_ref.shape[2]
    kc = s_ref.shape[1]
    qstart = qi * qb
    n_ch = (qstart + qb + kc - 1) // kc
    idx_scale = (IDX_HEAD_DIM ** -0.5) * (IDX_HEADS ** -0.5)

    qpos = qstart + lax.broadcasted_iota(I32, (qb, 1), 0)
    qpos_t = qstart + lax.broadcasted_iota(I32, (1, qb), 1)
    lane_pos = lax.broadcasted_iota(I32, (1, kc), 1)
    row_pos = lax.broadcasted_iota(I32, (kc, 1), 0)

    for h in range(H):
        wb_ref[h * qb:(h + 1) * qb, :] = jnp.broadcast_to(widx_ref[0, :, h:h + 1], (qb, LANES))

    def pipelined(produce, consume, buf_a, buf_b):
        produce(0, buf_a)
        n_pairs = (n_ch - 1) // 2

        def pair_body(i, carry):
            c = 2 * i
            consume(c, buf_a)
            produce(c + 1, buf_b)
            consume(c + 1, buf_b)
            produce(c + 2, buf_a)
            return carry

        lax.fori_loop(0, n_pairs, pair_body, 0)
        last = 2 * n_pairs

        @pl.when(n_ch - last == 1)
        def _():
            consume(last, buf_a)

        @pl.when(n_ch - last == 2)
        def _():
            consume(last, buf_a)
            produce(last + 1, buf_b)
            consume(last + 1, buf_b)

    def head_scores(c, buf):
        off = pl.multiple_of(c * kc, kc)
        kk = kidx_ref[0, pl.ds(off, kc), :]
        qs = qidx_ref[0].reshape(H * qb, IDX_HEAD_DIM)
        buf[...] = lax.dot_general(qs, kk, _NT, preferred_element_type=F32)

    def chunk_keys(c, buf):
        for j in range(kc // LANES):
            cs = slice(j * LANES, (j + 1) * LANES)
            acc = None
            for h in range(H):
                r = slice(h * qb, (h + 1) * qb)
                term = jnp.maximum(buf[r, cs], 0.0) * wb_ref[r, 0:LANES]
                acc = term if acc is None else acc + term
            score = acc * idx_scale
            score = jnp.where(c * kc + j * LANES + lane_pos[:, 0:LANES] <= qpos, score, -jnp.inf)
            key_ref[c, :, cs] = _sortable(score)
            keyt_ref[c, cs, :] = _sortable(jnp.transpose(score))

    pipelined(head_scores, chunk_keys, s_ref, s2_ref)

    def count(indicator):
        def body(c, cnt):
            m = indicator(keyt_ref[c], c * kc + row_pos)
            parts = [m[j * 8:(j + 1) * 8, :] for j in range(kc // 8)]
            while len(parts) > 1:
                parts = [parts[j] + parts[j + 1] for j in range(0, len(parts), 2)]
            return cnt + parts[0]
        cnt = lax.fori_loop(0, n_ch, body, jnp.zeros((8, qb), I32))
        return jnp.sum(cnt, axis=0, keepdims=True)

    def bit_body(i, t):
        cand = t + lax.shift_left(jnp.int32(1), 31 - i)
        n_ge = count(lambda k, _: jnp.where(k >= cand, 1, 0))
        return jnp.where(n_ge >= top_k, cand, t)

    thr_t = lax.fori_loop(0, 32, bit_body, jnp.full((1, qb), INT_MIN, I32))

    n_gt = count(lambda k, _: jnp.where(k > thr_t, 1, 0))
    n_ge = count(lambda k, _: jnp.where(k >= thr_t, 1, 0))
    need = top_k - n_gt
    surplus = jnp.where(thr_t > NEG_INF_KEY, jnp.where(n_ge > top_k, 1, 0), 0)
    sel_ref[0:1, :] = thr_t
    sel_ref[1:2, :] = qpos_t

    @pl.when(jnp.max(surplus) > 0)
    def _():
        def tie_body(i, m):
            cand = m + lax.shift_left(jnp.int32(1), idx_bits - 1 - i)
            n_before = count(lambda k, p: jnp.where(k == thr_t, jnp.where(p < cand, 1, 0), 0))
            return jnp.where(n_before < need, cand, m)
        last = lax.fori_loop(0, idx_bits, tie_body, jnp.zeros((1, qb), I32))
        sel_ref[1:2, :] = jnp.where(surplus > 0, jnp.minimum(last, qpos_t), qpos_t)

    thr = jnp.transpose(jnp.broadcast_to(sel_ref[0:1, :], (qb, qb)))[:, 0:1]
    tie_last = jnp.transpose(jnp.broadcast_to(sel_ref[1:2, :], (qb, qb)))[:, 0:1]

    def chunk_bias(c):
        key = key_ref[c]
        kpos = c * kc + lane_pos
        tie_bias = jnp.where(key == thr, jnp.where(kpos <= tie_last, 0.0, NEG_BIG), NEG_BIG)
        return jnp.where(key > thr, 0.0, tie_bias)

    n_lt = kc // LANES

    def head_logits(h, ck):
        return lax.dot_general(qlat_ref[0, h], ck, _NT, preferred_element_type=F32)

    m_ref[...] = jnp.full(m_ref.shape, NEG_BIG, F32)
    acc_ref[...] = jnp.zeros(acc_ref.shape, F32)

    def numerators(c, bufs):
        pbuf, abuf = bufs
        off = pl.multiple_of(c * kc, kc)
        ck = ckv_ref[0, pl.ds(off, kc), 0:KV_RANK]
        bias = chunk_bias(c)
        for h in range(H):
            r = slice(h * qb, (h + 1) * qb)
            lg = head_logits(h, ck)
            z = [lg[:, j * LANES:(j + 1) * LANES] + bias[:, j * LANES:(j + 1) * LANES] for j in range(n_lt)]
            zm = z[0]
            for j in range(1, n_lt):
                zm = jnp.maximum(zm, z[j])
            m_old = m_ref[r, :]
            m_new = jnp.maximum(m_old, jnp.max(zm, axis=1, keepdims=True))
            m_ref[r, :] = m_new
            abuf[r, :] = jnp.exp(m_old - m_new)
            for j in range(n_lt):
                pbuf[r, j * LANES:(j + 1) * LANES] = jnp.exp(z[j] - m_new).astype(BF16)

    def accumulate(c, bufs):
        pbuf, abuf = bufs
        off = pl.multiple_of(c * kc, kc)
        ckx = ckv_ref[0, pl.ds(off, kc), :]
        pv = jnp.dot(pbuf[...], ckx, preferred_element_type=F32)
        a = abuf[...]
        acc_ref[:, 0:KV_RANK] = a * acc_ref[:, 0:KV_RANK] + pv[:, 0:KV_RANK]
        acc_ref[:, KV_RANK:2 * KV_RANK] = a * acc_ref[:, KV_RANK:2 * KV_RANK] + pv[:, KV_RANK:2 * KV_RANK]

    pipelined(numerators, accumulate, (p_ref, a_ref), (p2_ref, a2_ref))

    o_lat = (acc_ref[:, 0:KV_RANK] / acc_ref[:, KV_RANK:2 * KV_RANK]).astype(BF16)
    for h in range(0, H, 2):
        t = jnp.dot(o_lat[h * qb:(h + 1) * qb, :], wuv_ref[h], preferred_element_type=F32)
        t = t + jnp.dot(o_lat[(h + 1) * qb:(h + 2) * qb, :], wuv_ref[h + 1], preferred_element_type=F32)
        o_ref[0, :, h * ATT_HEAD_DIM:(h + 2) * ATT_HEAD_DIM] = t.astype(BF16)


def _dsa(qidx, widx, qlat, kidx, ckv, wuv_pad):
    B, H, L, _ = qidx.shape
    qb = min(DSA_QUERIES, L)
    kc = min(DSA_KEYS, L)
    top_k = min(TOPK_MAX, L // 4)
    idx_bits = max(1, (L - 1).bit_length())
    kern = functools.partial(_dsa_kernel, top_k=top_k, idx_bits=idx_bits)
    return pl.pallas_call(
        kern,
        grid=(B, L // qb),
        in_specs=[
            pl.BlockSpec((1, H, qb, IDX_HEAD_DIM), lambda b, q: (b, 0, q, 0)),
            pl.BlockSpec((1, qb, LANES), lambda b, q: (b, q, 0)),
            pl.BlockSpec((1, H, qb, KV_RANK), lambda b, q: (b, 0, q, 0)),
            pl.BlockSpec((1, L, IDX_HEAD_DIM), lambda b, q: (b, 0, 0)),
            pl.BlockSpec((1, L, 2 * KV_RANK), lambda b, q: (b, 0, 0)),
            pl.BlockSpec(wuv_pad.shape, lambda b, q: (0, 0, 0)),
        ],
        out_specs=pl.BlockSpec((1, qb, ATT_WIDTH), lambda b, q: (b, q, 0)),
        out_shape=jax.ShapeDtypeStruct((B, L, ATT_WIDTH), BF16),
        scratch_shapes=[
            pltpu.VMEM((L // kc, qb, kc), I32),
            pltpu.VMEM((L // kc, kc, qb), I32),
            pltpu.VMEM((H * qb, LANES), F32),
            pltpu.VMEM((H * qb, kc), F32),
            pltpu.VMEM((H * qb, kc), F32),
            pltpu.VMEM((H * qb, kc), BF16),
            pltpu.VMEM((H * qb, kc), BF16),
            pltpu.VMEM((H * qb, LANES), F32),
            pltpu.VMEM((H * qb, LANES), F32),
            pltpu.VMEM((H * qb, LANES), F32),
            pltpu.VMEM((H * qb, 2 * KV_RANK), F32),
            pltpu.VMEM((8, qb), I32),
        ],
        compiler_params=pltpu.CompilerParams(
            dimension_semantics=("arbitrary", "arbitrary"), vmem_limit_bytes=VMEM_LIMIT),
    )(qidx, widx, qlat, kidx, ckv, wuv_pad)


def _memkv_kernel(mem_ref, w_ref, k_ref, v_ref):
    d = k_ref.shape[2]
    kv = jnp.dot(mem_ref[0].astype(BF16), w_ref[...], preferred_element_type=F32)
    k_ref[0] = kv[:, 0:d].astype(BF16)
    v_ref[0] = kv[:, d:2 * d].astype(BF16)


def _memkv(mem, w_mkv):
    B, M, D = mem.shape
    return pl.pallas_call(
        _memkv_kernel,
        grid=(B,),
        in_specs=[pl.BlockSpec((1, M, D), lambda b: (b, 0, 0)),
                  pl.BlockSpec(w_mkv.shape, lambda b: (0, 0))],
        out_specs=[pl.BlockSpec((1, M, D), lambda b: (b, 0, 0)),
                   pl.BlockSpec((1, M, D), lambda b: (b, 0, 0))],
        out_shape=[jax.ShapeDtypeStruct((B, M, D), BF16), jax.ShapeDtypeStruct((B, M, D), BF16)],
        compiler_params=pltpu.CompilerParams(
            dimension_semantics=("arbitrary",), vmem_limit_bytes=VMEM_LIMIT),
    )(mem, w_mkv)


def _split3(v):
    hi = v.astype(BF16)
    r1 = v - hi.astype(F32)
    mid = r1.astype(BF16)
    lo = (r1 - mid.astype(F32)).astype(BF16)
    return hi, mid, lo


def _mix_kernel(x_ref, yp_ref, ya_ref, wo_ref, g1_ref, b1_ref, km_ref, vm_ref, wq_ref, wmo_ref,
                g2_ref, b2_ref, wr_ref, br_ref, x2_ref, tope_ref, gate_ref):
    tm, d = x_ref.shape
    hd = d // MEM_HEADS
    pw = yp_ref.shape[1]
    ws = _split3(wr_ref[...])

    def stage_mix(rs):
        mix = jnp.dot(yp_ref[rs, :], wo_ref[0:pw, :], preferred_element_type=F32)
        mix = mix + jnp.dot(ya_ref[rs, :], wo_ref[pw:, :], preferred_element_type=F32)
        return _layer_norm(DN_ALPHA * x_ref[rs, :] + mix, g1_ref[...], b1_ref[...])

    def stage_mem(x1):
        q = jnp.dot(x1.astype(BF16), wq_ref[...], preferred_element_type=F32).astype(BF16)
        scale = hd ** -0.5
        att = None
        for h in range(MEM_HEADS):
            c = slice(h * hd, (h + 1) * hd)
            lg = lax.dot_general(q[:, c], km_ref[0, :, c], _NT, preferred_element_type=F32) * scale
            p = jnp.exp(lg - jnp.max(lg, axis=-1, keepdims=True))
            p = p / jnp.sum(p, axis=-1, keepdims=True)
            oh = jnp.dot(p.astype(BF16), vm_ref[0, :, c], preferred_element_type=F32).astype(BF16)
            t = jnp.dot(oh, wmo_ref[c, :], preferred_element_type=F32)
            att = t if att is None else att + t
        return att

    def stage_route(rs, x1, att):
        n = rs.stop - rs.start
        x2 = _layer_norm(DN_ALPHA * x1 + att, g2_ref[...], b2_ref[...])
        x2_ref[rs, :] = x2

        xs = _split3(x2)
        lt = None
        for i, j in ((0, 0), (0, 1), (1, 0)):
            t = lax.dot_general(ws[j], xs[i], _NT, preferred_element_type=F32)
            lt = t if lt is None else lt + t
        lt = lt + br_ref[...]
        n_e = lt.shape[0]
        eidx = lax.broadcasted_iota(I32, lt.shape, 0)
        vals, idxs = [], []
        for _ in range(TOP_K_EXPERTS):
            mx = jnp.max(lt, axis=0, keepdims=True)
            ix = jnp.min(jnp.where(lt == mx, eidx, n_e), axis=0, keepdims=True)
            vals.append(mx)
            idxs.append(ix)
            lt = jnp.where(eidx == ix, -jnp.inf, lt)
        tope_ref[:, rs] = jnp.concatenate(idxs, axis=0)
        ex = [jnp.exp(v - vals[0]) for v in vals]
        den = ex[0]
        for e_ in ex[1:]:
            den = den + e_
        gates = jnp.concatenate([e_ / den for e_ in ex] + [jnp.zeros((LANES - TOP_K_EXPERTS, n), F32)], axis=0)
        gate_ref[rs, :] = jnp.transpose(gates)

    n_groups = MIX_ROW_GROUPS if tm % (MIX_ROW_GROUPS * LANES) == 0 else 1
    groups = [slice(g * (tm // n_groups), (g + 1) * (tm // n_groups)) for g in range(n_groups)]
    x1s = [stage_mix(rs) for rs in groups]
    atts = [stage_mem(x1) for x1 in x1s]
    for rs, x1, att in zip(groups, x1s, atts):
        stage_route(rs, x1, att)


def _mix(x2d, ypool, yatt, w_o, g1, b1, k_mem, v_mem, w_mq, w_mo, g2, b2, w_rt, b_r, B, L):
    T, D = x2d.shape
    tm = min(MIX_ROWS, L)
    nl = L // tm
    M = k_mem.shape[1]
    row = lambda i: (i, 0)
    full = lambda i: (0, 0)
    return pl.pallas_call(
        _mix_kernel,
        grid=(T // tm,),
        in_specs=[
            pl.BlockSpec((tm, D), row),
            pl.BlockSpec((tm, ypool.shape[1]), row),
            pl.BlockSpec((tm, yatt.shape[1]), row),
            pl.BlockSpec(w_o.shape, full),
            pl.BlockSpec(g1.shape, full),
            pl.BlockSpec(b1.shape, full),
            pl.BlockSpec((1, M, D), lambda i: (i // nl, 0, 0)),
            pl.BlockSpec((1, M, D), lambda i: (i // nl, 0, 0)),
            pl.BlockSpec(w_mq.shape, full),
            pl.BlockSpec(w_mo.shape, full),
            pl.BlockSpec(g2.shape, full),
            pl.BlockSpec(b2.shape, full),
            pl.BlockSpec(w_rt.shape, full),
            pl.BlockSpec(b_r.shape, full),
        ],
        out_specs=[
            pl.BlockSpec((tm, D), row),
            pl.BlockSpec((TOP_K_EXPERTS, tm), lambda i: (0, i)),
            pl.BlockSpec((tm, LANES), row),
        ],
        out_shape=[
            jax.ShapeDtypeStruct((T, D), F32),
            jax.ShapeDtypeStruct((TOP_K_EXPERTS, T), I32),
            jax.ShapeDtypeStruct((T, LANES), F32),
        ],
        compiler_params=pltpu.CompilerParams(
            dimension_semantics=("arbitrary",), vmem_limit_bytes=VMEM_LIMIT),
    )(x2d, ypool, yatt, w_o, g1, b1, k_mem, v_mem, w_mq, w_mo, g2, b2, w_rt, b_r)


def _rank_kernel(tope_ref, rank_ref, cnt_ref, carry_ref):
    i = pl.program_id(0)
    tr = tope_ref.shape[1]

    @pl.when(i == 0)
    def _():
        carry_ref[...] = jnp.zeros(carry_ref.shape, F32)

    eidx = lax.broadcasted_iota(I32, (N_EXPERTS, tr), 0)
    onehot = jnp.zeros((N_EXPERTS, tr), F32)
    for k in range(TOP_K_EXPERTS):
        onehot = onehot + jnp.where(eidx == tope_ref[k:k + 1, :], 1.0, 0.0)
    before = jnp.where(lax.broadcasted_iota(I32, (tr, tr), 0) < lax.broadcasted_iota(I32, (tr, tr), 1), 1.0, 0.0)
    excl = jnp.dot(onehot.astype(BF16), before.astype(BF16), preferred_element_type=F32)
    rank_full = excl + carry_ref[:, 0:1]
    rows = []
    for k in range(TOP_K_EXPERTS):
        rows.append(jnp.sum(jnp.where(eidx == tope_ref[k:k + 1, :], rank_full, 0.0), axis=0, keepdims=True))
    rank_ref[...] = jnp.concatenate(rows, axis=0).astype(I32)
    carry_ref[...] = carry_ref[...] + jnp.sum(onehot, axis=1, keepdims=True)
    cnt_ref[...] = carry_ref[...].astype(I32)


def _rank(tope):
    K, T = tope.shape
    tr = min(RANK_TOKENS, T)
    return pl.pallas_call(
        _rank_kernel,
        grid=(T // tr,),
        in_specs=[pl.BlockSpec((K, tr), lambda i: (0, i))],
        out_specs=[pl.BlockSpec((K, tr), lambda i: (0, i)),
                   pl.BlockSpec((N_EXPERTS, LANES), lambda i: (0, 0))],
        out_shape=[jax.ShapeDtypeStruct((K, T), I32), jax.ShapeDtypeStruct((N_EXPERTS, LANES), I32)],
        scratch_shapes=[pltpu.VMEM((N_EXPERTS, LANES), F32)],
        compiler_params=pltpu.CompilerParams(dimension_semantics=("arbitrary",)),
    )(tope)


def _dest_kernel(start_ref, tope_ref, rank_ref, dest_ref):
    tope = tope_ref[...]
    base = jnp.zeros(tope.shape, I32)
    for e in range(N_EXPERTS):
        base = jnp.where(tope == e, start_ref[e], base)
    dest_ref[...] = base + rank_ref[...]


def _dest(pad_start, tope, rank):
    K, T = tope.shape
    tt = min(DEST_TOKENS, T)
    blk = pl.BlockSpec((K, tt), lambda i: (0, i))
    return pl.pallas_call(
        _dest_kernel,
        grid=(T // tt,),
        in_specs=[pl.BlockSpec(memory_space=pltpu.SMEM), blk, blk],
        out_specs=blk,
        out_shape=jax.ShapeDtypeStruct((K, T), I32),
        compiler_params=pltpu.CompilerParams(dimension_semantics=("arbitrary",)),
    )(pad_start, tope, rank)


def _dispatch_kernel(dest_ref, start_ref, cnt_ref, padded_ref, x_ref, xs_hbm, zrow, sem, zsem):
    i = pl.program_id(0)
    td = dest_ref.shape[1]

    def row_copy(j, dst):
        return pltpu.make_async_copy(x_ref.at[pl.ds(j, 1), :], xs_hbm.at[pl.ds(dst, 1), :], sem)

    def zero_copy(dst):
        return pltpu.make_async_copy(zrow, xs_hbm.at[pl.ds(dst, 1), :], zsem)

    @pl.when(i == 0)
    def _():
        zrow[...] = jnp.zeros(zrow.shape, F32)
        for e in range(N_EXPERTS):
            first = start_ref[e] + cnt_ref[e]
            n_pad = padded_ref[e] - cnt_ref[e]

            def zstart(r, c):
                zero_copy(first + r).start()
                return c
            lax.fori_loop(0, n_pad, zstart, 0)

            def zwait(r, c):
                zero_copy(first + r).wait()
                return c
            lax.fori_loop(0, n_pad, zwait, 0)

    def issue(jj, c):
        for u in range(ISSUE_UNROLL):
            j = jj * ISSUE_UNROLL + u
            for k in range(TOP_K_EXPERTS):
                row_copy(j, dest_ref[k, j]).start()
        return c
    lax.fori_loop(0, td // ISSUE_UNROLL, issue, 0)

    def drain(jj, c):
        for _ in range(ISSUE_UNROLL * TOP_K_EXPERTS):
            row_copy(0, 0).wait()
        return c
    lax.fori_loop(0, td // ISSUE_UNROLL, drain, 0)


def _dispatch(dest, pad_start, counts, padded, x2, n_rows):
    K, T = dest.shape
    D = x2.shape[1]
    td = min(DISPATCH_TOKENS, T)
    smem_tok = pl.BlockSpec((K, td), lambda i: (0, i), memory_space=pltpu.SMEM)
    smem_full = pl.BlockSpec(memory_space=pltpu.SMEM)
    return pl.pallas_call(
        _dispatch_kernel,
        grid=(T // td,),
        in_specs=[smem_tok, smem_full, smem_full, smem_full,
                  pl.BlockSpec((td, D), lambda i: (i, 0))],
        out_specs=pl.BlockSpec(memory_space=pl.ANY),
        out_shape=jax.ShapeDtypeStruct((n_rows, D), F32),
        scratch_shapes=[pltpu.VMEM((1, D), F32), pltpu.SemaphoreType.DMA(()), pltpu.SemaphoreType.DMA(())],
        compiler_params=pltpu.CompilerParams(
            dimension_semantics=("arbitrary",), vmem_limit_bytes=VMEM_LIMIT),
    )(dest, pad_start, counts, padded, x2)


def _expert_kernel(bexp_ref, nused_ref, xs_ref, wgu_ref, bgu_ref, wd_ref, bd_ref, y_ref, wgu_bf, wd_bf):
    j = pl.program_id(0)
    d_ff = wd_ref.shape[1]

    @pl.when(j < nused_ref[0])
    def _():
        prev = bexp_ref[jnp.maximum(j - 1, 0)]

        @pl.when(jnp.logical_or(j == 0, bexp_ref[j] != prev))
        def _():
            wgu_bf[...] = wgu_ref[0].astype(BF16)
            wd_bf[...] = wd_ref[0].astype(BF16)

        xb = xs_ref[...].astype(BF16)
        gu = jnp.dot(xb, wgu_bf[...], preferred_element_type=F32) + bgu_ref[0]
        g = jnp.minimum(gu[:, 0:d_ff], SWIGLU_LIMIT)
        u = jnp.clip(gu[:, d_ff:2 * d_ff], -SWIGLU_LIMIT, SWIGLU_LIMIT)
        act = (u + 1.0) * g * (1.0 / (1.0 + jnp.exp(-SWIGLU_ALPHA * g)))
        y_ref[...] = jnp.dot(act.astype(BF16), wd_bf[...], preferred_element_type=F32) + bd_ref[0]


def _experts(block_exp, n_used, xs, w_gate_up, b_gate_up, w_down, b_down):
    P, D = xs.shape
    blk = EXPERT_ROWS
    nblk = P // blk
    E, _, F2 = w_gate_up.shape
    d_ff = w_down.shape[1]
    row = lambda j, be, nu: (jnp.minimum(j, nu[0] - 1), 0)
    exp3 = lambda j, be, nu: (be[j], 0, 0)
    grid_spec = pltpu.PrefetchScalarGridSpec(
        num_scalar_prefetch=2,
        grid=(nblk,),
        in_specs=[
            pl.BlockSpec((blk, D), row),
            pl.BlockSpec((1, D, F2), exp3),
            pl.BlockSpec((1, 1, F2), exp3),
            pl.BlockSpec((1, d_ff, D), exp3),
            pl.BlockSpec((1, 1, D), exp3),
        ],
        out_specs=pl.BlockSpec((blk, D), row),
        scratch_shapes=[pltpu.VMEM((D, F2), BF16), pltpu.VMEM((d_ff, D), BF16)],
    )
    return pl.pallas_call(
        _expert_kernel,
        grid_spec=grid_spec,
        out_shape=jax.ShapeDtypeStruct((P, D), F32),
        compiler_params=pltpu.CompilerParams(
            dimension_semantics=("arbitrary",), vmem_limit_bytes=VMEM_LIMIT),
    )(block_exp, n_used, xs, w_gate_up, b_gate_up.reshape(E, 1, F2), w_down, b_down.reshape(E, 1, D))


def _combine_kernel(dest_ref, dest_next_ref, x2_ref, gate_ref, g3_ref, b3_ref, y_hbm, o_ref, ybuf, sem):
    i = pl.program_id(0)
    n = pl.num_programs(0)
    tc = x2_ref.shape[0]
    slot = i % 2

    def row_copy(src_ref, s, j, k):
        return pltpu.make_async_copy(y_hbm.at[pl.ds(src_ref[k, j], 1), :], ybuf.at[s, k, pl.ds(j, 1), :], sem.at[s])

    def gather(src_ref, s):
        def issue(jj, c):
            for u in range(ISSUE_UNROLL):
                for k in range(TOP_K_EXPERTS):
                    row_copy(src_ref, s, jj * ISSUE_UNROLL + u, k).start()
            return c
        lax.fori_loop(0, tc // ISSUE_UNROLL, issue, 0)

    @pl.when(i == 0)
    def _():
        gather(dest_ref, 0)

    @pl.when(i + 1 < n)
    def _():
        gather(dest_next_ref, 1 - slot)

    def drain(jj, c):
        for _ in range(ISSUE_UNROLL * TOP_K_EXPERTS):
            pltpu.make_async_copy(y_hbm.at[pl.ds(0, 1), :], ybuf.at[slot, 0, pl.ds(0, 1), :], sem.at[slot]).wait()
        return c
    lax.fori_loop(0, tc // ISSUE_UNROLL, drain, 0)

    moe = None
    for k in range(TOP_K_EXPERTS):
        t = ybuf[slot, k] * gate_ref[:, k:k + 1]
        moe = t if moe is None else moe + t
    o_ref[...] = _layer_norm(DN_ALPHA * x2_ref[...] + moe, g3_ref[...], b3_ref[...])


def _combine(dest, x2, gates, g3, b3, y):
    K, T = dest.shape
    D = x2.shape[1]
    tc = min(COMBINE_TOKENS, T)
    n = T // tc
    row = lambda i: (i, 0)
    full = lambda i: (0, 0)
    return pl.pallas_call(
        _combine_kernel,
        grid=(n,),
        in_specs=[pl.BlockSpec((K, tc), lambda i: (0, i), memory_space=pltpu.SMEM),
                  pl.BlockSpec((K, tc), lambda i: (0, jnp.minimum(i + 1, n - 1)), memory_space=pltpu.SMEM),
                  pl.BlockSpec((tc, D), row), pl.BlockSpec((tc, LANES), row),
                  pl.BlockSpec(g3.shape, full), pl.BlockSpec(b3.shape, full),
                  pl.BlockSpec(memory_space=pl.ANY)],
        out_specs=pl.BlockSpec((tc, D), row),
        out_shape=jax.ShapeDtypeStruct((T, D), F32),
        scratch_shapes=[pltpu.VMEM((2, K, tc, D), F32), pltpu.SemaphoreType.DMA((2,))],
        compiler_params=pltpu.CompilerParams(
            dimension_semantics=("arbitrary",), vmem_limit_bytes=VMEM_LIMIT),
    )(dest, dest, x2, gates, g3, b3, y)


def _pad_cols(w, width):
    return jnp.pad(w, ((0, 0), (0, width - w.shape[1])))


def _layer(x, mem, w_in, w_pool, pool_scale, ik_g, ik_b, kv_g, w_uk, w_uv, w_o, ln1_g, ln1_b,
           w_mq, w_mkv, w_mo, ln2_g, ln2_b, w_router, b_router, w_gate_up, b_gate_up, w_down, b_down,
           ln3_g, ln3_b):
    B, L, D = x.shape
    T = B * L
    row = lambda v: v.reshape(1, -1)

    o = 0
    pieces = []
    for width in (POOL_WIDTH, ATT_WIDTH, KV_RANK, IDX_HEADS * IDX_HEAD_DIM, IDX_HEAD_DIM, IDX_HEADS):
        pieces.append(w_in[:, o:o + width])
        o += width
    pieces[4] = _pad_cols(pieces[4], LANES)
    pieces[5] = _pad_cols(pieces[5], LANES)
    w_in_pad = jnp.concatenate(pieces, axis=1).astype(BF16)
    wuv_pad = jnp.zeros((ATT_HEADS, KV_RANK, 2 * ATT_HEAD_DIM), F32)
    for h in range(ATT_HEADS):
        wuv_pad = wuv_pad.at[h, :, (h % 2) * ATT_HEAD_DIM:(h % 2 + 1) * ATT_HEAD_DIM].set(w_uv[h])

    ypool, qlat, ckv, qidx, kidx, widx = _inproj(
        x, w_in_pad, w_pool.astype(BF16), row(pool_scale), row(kv_g), row(ik_g), row(ik_b), w_uk.astype(BF16))
    yatt = _dsa(qidx, widx, qlat, kidx, ckv, wuv_pad.astype(BF16))
    k_mem, v_mem = _memkv(mem, w_mkv.astype(BF16))
    x2, tope, gates = _mix(
        x.reshape(T, D), ypool.reshape(T, -1), yatt.reshape(T, -1), w_o.astype(BF16), row(ln1_g), row(ln1_b),
        k_mem, v_mem, w_mq.astype(BF16), w_mo.astype(BF16), row(ln2_g), row(ln2_b),
        jnp.transpose(w_router), b_router.reshape(-1, 1), B, L)

    rank, cnt = _rank(tope)
    counts = cnt[:, 0]
    blk = EXPERT_ROWS
    padded = (counts + blk - 1) // blk * blk
    pad_end = jnp.cumsum(padded)
    pad_start = pad_end - padded
    n_rows = T * TOP_K_EXPERTS + N_EXPERTS * blk
    nblk = n_rows // blk
    n_used = (pad_end[-1] // blk).astype(I32)
    blk_first = jnp.minimum(jnp.arange(nblk, dtype=I32), n_used - 1) * blk
    n_ended = jnp.sum((pad_end[None, :] <= blk_first[:, None]).astype(I32), axis=1)
    block_exp = jnp.minimum(n_ended, N_EXPERTS - 1).astype(I32)

    pad_start = pad_start.astype(I32)
    dest = _dest(pad_start, tope, rank)
    xs = _dispatch(dest, pad_start, counts, padded.astype(I32), x2, n_rows)
    y = _experts(block_exp, n_used.reshape(1), xs, w_gate_up, b_gate_up, w_down, b_down)
    out = _combine(dest, x2, gates, row(ln3_g), row(ln3_b), y)
    return out.reshape(B, L, D)


def kernel(x, mem, w_in, w_pool, pool_scale, idx_k_norm_g, idx_k_norm_b, kv_norm_g, w_uk, w_uv, w_o, ln1_g, ln1_b, w_mq, w_mkv, w_mo, ln2_g, ln2_b, w_router, b_router, w_gate_up, b_gate_up, w_down, b_down, ln3_g, ln3_b):
    assert w_in.shape[0] == DEPTH
    return _layer(x, mem, w_in[0], w_pool[0], pool_scale[0], idx_k_norm_g[0], idx_k_norm_b[0], kv_norm_g[0],
                  w_uk[0], w_uv[0], w_o[0], ln1_g[0], ln1_b[0], w_mq[0], w_mkv[0], w_mo[0], ln2_g[0], ln2_b[0],
                  w_router[0], b_router[0], w_gate_up[0], b_gate_up[0], w_down[0], b_down[0], ln3_g[0], ln3_b[0])
```

```python
import functools

import jax
import jax.numpy as jnp
from jax import lax
from jax.experimental import pallas as pl
from jax.experimental.pallas import tpu as pltpu

F32 = jnp.float32
BF16 = jnp.bfloat16
I32 = jnp.int32

POOL_WINDOWS = (2, 4, 8, 16)
POOL_GROUP_DIM = 128
POOL_WIDTH = 512
ATT_HEADS = 8
ATT_HEAD_DIM = 64
ATT_WIDTH = 512
KV_RANK = 128
IDX_HEADS = 8
IDX_HEAD_DIM = 64
TOPK_MAX = 256
MEM_HEADS = 4
N_EXPERTS = 32
TOP_K_EXPERTS = 4
SWIGLU_LIMIT = 7.0
SWIGLU_ALPHA = 1.702
LN_EPS = 1e-5
RMS_EPS = 1e-6
DEPTH = 1
DN_ALPHA = (2 * DEPTH) ** 0.25

LANES = 128
MAX_POOL_WINDOW = 16
VMEM_LIMIT = 56 * 1024 * 1024

PROJ_ROWS = 512
DSA_QUERIES = 128
DSA_KEYS = 512
SEARCH_CHECK_BIT = 24
MIX_ROWS = 1024
MIX_ROW_GROUPS = 1
RANK_TOKENS = 512
DEST_TOKENS = 2048
DISPATCH_TOKENS = 512
ISSUE_UNROLL = 4
EXPERT_ROWS = 256
COMBINE_TOKENS = 256

LOG2_E = 1.4426950408889634
NEG_BIG = -1e30
INT_MIN = -(2 ** 31)
NEG_INF_KEY = INT_MIN + 0x007FFFFF

_NT = (((1,), (1,)), ((), ()))


def _layer_norm(v, g, b):
    mu = jnp.mean(v, axis=-1, keepdims=True)
    d = v - mu
    var = jnp.mean(d * d, axis=-1, keepdims=True)
    return d * lax.rsqrt(var + LN_EPS) * g + b


_C_POOL = 0
_C_Q = 512
_C_CKV = 1024
_C_QIDX = 1152
_C_KIDX = 1664
_C_WIDX = 1792
_IN_PAD = 1920


def _inproj_kernel(x_ref, w_ref, wpool_ref, pscale_ref, kvg_ref, ikg_ref, ikb_ref, wuk_ref,
                   ypool_ref, qlat_ref, ckv_ref, qidx_ref, kidx_ref, widx_ref, ubuf):
    li = pl.program_id(1)
    tm = x_ref.shape[1]
    halo = MAX_POOL_WINDOW
    xb = x_ref[0].astype(BF16)

    u = jnp.dot(xb, w_ref[:, _C_POOL:_C_POOL + POOL_WIDTH], preferred_element_type=F32)

    @pl.when(li == 0)
    def _():
        ubuf[0:halo, :] = jnp.zeros((halo, POOL_WIDTH), F32)

    ubuf[halo:halo + tm, :] = u
    pos = li * tm + lax.broadcasted_iota(I32, (tm, 1), 0)
    for g, w in enumerate(POOL_WINDOWS):
        c0 = g * POOL_GROUP_DIM
        c1 = c0 + POOL_GROUP_DIM
        ug = ubuf[halo:halo + tm, c0:c1]
        s = ug
        for j in range(1, w):
            s = s + ubuf[halo - j:halo - j + tm, c0:c1]
        cnt = jnp.minimum(pos + 1, w).astype(F32)
        d = s / cnt - ug
        y = jnp.dot(d.astype(BF16), wpool_ref[g], preferred_element_type=F32) * pscale_ref[:, c0:c1]
        ypool_ref[0, :, c0:c1] = y.astype(BF16)
    ubuf[0:halo, :] = ubuf[tm:tm + halo, :]

    q = jnp.dot(xb, w_ref[:, _C_Q:_C_Q + ATT_WIDTH], preferred_element_type=F32)
    att_scale = (ATT_HEAD_DIM ** -0.5) * LOG2_E
    for h in range(ATT_HEADS):
        qh = q[:, h * ATT_HEAD_DIM:(h + 1) * ATT_HEAD_DIM].astype(BF16)
        ql = jnp.dot(qh, wuk_ref[h], preferred_element_type=F32) * att_scale
        qlat_ref[0, h] = ql.astype(BF16)

    c = jnp.dot(xb, w_ref[:, _C_CKV:_C_CKV + KV_RANK], preferred_element_type=F32)
    c = c * lax.rsqrt(jnp.mean(c * c, axis=-1, keepdims=True) + RMS_EPS) * kvg_ref[...]
    ckv_ref[0, :, 0:KV_RANK] = c.astype(BF16)
    ckv_ref[0, :, KV_RANK:2 * KV_RANK] = jnp.ones((tm, KV_RANK), BF16)

    qi = jnp.dot(xb, w_ref[:, _C_QIDX:_C_QIDX + IDX_HEADS * IDX_HEAD_DIM], preferred_element_type=F32)
    for h in range(IDX_HEADS):
        qidx_ref[0, h] = qi[:, h * IDX_HEAD_DIM:(h + 1) * IDX_HEAD_DIM].astype(BF16)
    kw = jnp.dot(xb, w_ref[:, _C_KIDX:_C_KIDX + 2 * LANES], preferred_element_type=F32)
    kr = kw[:, 0:IDX_HEAD_DIM]
    kidx_ref[0] = _layer_norm(kr, ikg_ref[...], ikb_ref[...]).astype(BF16)
    widx_ref[0] = kw[:, LANES:2 * LANES]


def _inproj(x, w_in_pad, w_pool, pool_scale, kv_g, ik_g, ik_b, w_uk):
    B, L, D = x.shape
    tm = min(PROJ_ROWS, L)
    nl = L // tm
    full2 = lambda b, l: (0, 0)
    full3 = lambda b, l: (0, 0, 0)
    return pl.pallas_call(
        _inproj_kernel,
        grid=(B, nl),
        in_specs=[
            pl.BlockSpec((1, tm, D), lambda b, l: (b, l, 0)),
            pl.BlockSpec(w_in_pad.shape, full2),
            pl.BlockSpec(w_pool.shape, full3),
            pl.BlockSpec(pool_scale.shape, full2),
            pl.BlockSpec(kv_g.shape, full2),
            pl.BlockSpec(ik_g.shape, full2),
            pl.BlockSpec(ik_b.shape, full2),
            pl.BlockSpec(w_uk.shape, full3),
        ],
        out_specs=[
            pl.BlockSpec((1, tm, POOL_WIDTH), lambda b, l: (b, l, 0)),
            pl.BlockSpec((1, ATT_HEADS, tm, KV_RANK), lambda b, l: (b, 0, l, 0)),
            pl.BlockSpec((1, tm, 2 * KV_RANK), lambda b, l: (b, l, 0)),
            pl.BlockSpec((1, IDX_HEADS, tm, IDX_HEAD_DIM), lambda b, l: (b, 0, l, 0)),
            pl.BlockSpec((1, tm, IDX_HEAD_DIM), lambda b, l: (b, l, 0)),
            pl.BlockSpec((1, tm, LANES), lambda b, l: (b, l, 0)),
        ],
        out_shape=[
            jax.ShapeDtypeStruct((B, L, POOL_WIDTH), BF16),
            jax.ShapeDtypeStruct((B, ATT_HEADS, L, KV_RANK), BF16),
            jax.ShapeDtypeStruct((B, L, 2 * KV_RANK), BF16),
            jax.ShapeDtypeStruct((B, IDX_HEADS, L, IDX_HEAD_DIM), BF16),
            jax.ShapeDtypeStruct((B, L, IDX_HEAD_DIM), BF16),
            jax.ShapeDtypeStruct((B, L, LANES), F32),
        ],
        scratch_shapes=[pltpu.VMEM((MAX_POOL_WINDOW + tm, POOL_WIDTH), F32)],
        compiler_params=pltpu.CompilerParams(
            dimension_semantics=("arbitrary", "arbitrary"), vmem_limit_bytes=VMEM_LIMIT),
    )(x, w_in_pad, w_pool, pool_scale, kv_g, ik_g, ik_b, w_uk)


def _sortable(score):
    bits = pltpu.bitcast(score, I32)
    return bits ^ ((bits >> 31) & 0x7FFFFFFF)


def _dsa_kernel(qidx_ref, widx_ref, qlat_ref, kidx_ref, ckv_ref, wuv_ref, o_ref,
                key_ref, keyt_ref, wb_ref, s_ref, s2_ref, p_ref, p2_ref, a_ref, a2_ref, m_ref, acc_ref, sel_ref,
                *, top_k, idx_bits):
    qi = pl.program_id(1)
    H = ATT_HEADS
    qb = qidx_ref.shape[2]
    kc = s_ref.shape[1]
    qstart = qi * qb
    n_ch = (qstart + qb + kc - 1) // kc
    idx_scale = (IDX_HEAD_DIM ** -0.5) * (IDX_HEADS ** -0.5)

    qpos = qstart + lax.broadcasted_iota(I32, (qb, 1), 0)
    qpos_t = qstart + lax.broadcasted_iota(I32, (1, qb), 1)
    lane_pos = lax.broadcasted_iota(I32, (1, kc), 1)
    row_pos = lax.broadcasted_iota(I32, (kc, 1), 0)

    for h in range(H):
        wb_ref[h * qb:(h + 1) * qb, :] = jnp.broadcast_to(widx_ref[0, :, h:h + 1] * idx_scale, (qb, LANES))

    def pipelined(produce, consume, buf_a, buf_b):
        produce(0, buf_a)
        n_pairs = (n_ch - 1) // 2

        def pair_body(i, carry):
            c = 2 * i
            consume(c, buf_a)
            produce(c + 1, buf_b)
            consume(c + 1, buf_b)
            produce(c + 2, buf_a)
            return carry

        lax.fori_loop(0, n_pairs, pair_body, 0)
        last = 2 * n_pairs

        @pl.when(n_ch - last == 1)
        def _():
            consume(last, buf_a)

        @pl.when(n_ch - last == 2)
        def _():
            consume(last, buf_a)
            produce(last + 1, buf_b)
            consume(last + 1, buf_b)

    def head_scores(c, buf):
        off = pl.multiple_of(c * kc, kc)
        kk = kidx_ref[0, pl.ds(off, kc), :]
        qs = qidx_ref[0].reshape(H * qb, IDX_HEAD_DIM)
        buf[...] = lax.dot_general(qs, kk, _NT, preferred_element_type=F32)

    def chunk_keys(c, buf):
        for j in range(kc // LANES):
            cs = slice(j * LANES, (j + 1) * LANES)
            acc = None
            for h in range(H):
                r = slice(h * qb, (h + 1) * qb)
                term = jnp.maximum(buf[r, cs], 0.0) * wb_ref[r, 0:LANES]
                acc = term if acc is None else acc + term
            score = jnp.where(c * kc + j * LANES + lane_pos[:, 0:LANES] <= qpos, acc, -jnp.inf)
            key = _sortable(score)
            key_ref[c, :, cs] = key
            keyt_ref[c, cs, :] = jnp.transpose(key)

    pipelined(head_scores, chunk_keys, s_ref, s2_ref)

    def count(indicator):
        def body(c, cnt):
            m = indicator(keyt_ref[c], c * kc + row_pos)
            parts = [m[j * 8:(j + 1) * 8, :] for j in range(kc // 8)]
            while len(parts) > 1:
                parts = [parts[j] + parts[j + 1] for j in range(0, len(parts), 2)]
            return cnt + parts[0]
        cnt = lax.fori_loop(0, n_ch, body, jnp.zeros((8, qb), I32))
        return jnp.sum(cnt, axis=0, keepdims=True)

    def bit_body(i, carry):
        t, n_t = carry
        cand = t + lax.shift_left(jnp.int32(1), 31 - i)
        n_ge = count(lambda k, _: jnp.where(k >= cand, 1, 0))
        ok = n_ge >= top_k
        return jnp.where(ok, cand, t), jnp.where(ok, n_ge, n_t)

    def run_bits(first, last, t, n_t):
        t, n_t = lax.fori_loop(first, last, bit_body, (t, n_t))
        sel_ref[0:1, :] = t
        sel_ref[2:3, :] = n_t

    run_bits(0, SEARCH_CHECK_BIT, jnp.full((1, qb), INT_MIN, I32), jnp.full((1, qb), n_ch * kc, I32))

    @pl.when(jnp.max(jnp.where(sel_ref[2:3, :] != top_k, 1, 0)) > 0)
    def _():
        run_bits(SEARCH_CHECK_BIT, 32, sel_ref[0:1, :], sel_ref[2:3, :])

    thr_t = sel_ref[0:1, :]
    n_ge = sel_ref[2:3, :]

    surplus = jnp.where(thr_t > NEG_INF_KEY, jnp.where(n_ge > top_k, 1, 0), 0)
    sel_ref[1:2, :] = qpos_t

    @pl.when(jnp.max(surplus) > 0)
    def _():
        need = top_k - count(lambda k, _: jnp.where(k > thr_t, 1, 0))

        def tie_body(i, m):
            cand = m + lax.shift_left(jnp.int32(1), idx_bits - 1 - i)
            n_before = count(lambda k, p: jnp.where(k == thr_t, jnp.where(p < cand, 1, 0), 0))
            return jnp.where(n_before < need, cand, m)
        last = lax.fori_loop(0, idx_bits, tie_body, jnp.zeros((1, qb), I32))
        sel_ref[1:2, :] = jnp.where(surplus > 0, jnp.minimum(last, qpos_t), qpos_t)

    thr = jnp.transpose(jnp.broadcast_to(sel_ref[0:1, :], (qb, qb)))[:, 0:1]
    tie_last = jnp.transpose(jnp.broadcast_to(sel_ref[1:2, :], (qb, qb)))[:, 0:1]

    def chunk_bias(c):
        key = key_ref[c]
        kpos = c * kc + lane_pos
        tie_bias = jnp.where(key == thr, jnp.where(kpos <= tie_last, 0.0, NEG_BIG), NEG_BIG)
        return jnp.where(key > thr, 0.0, tie_bias)

    n_lt = kc // LANES

    def head_logits(h, ck):
        return lax.dot_general(qlat_ref[0, h], ck, _NT, preferred_element_type=F32)

    m_ref[...] = jnp.full(m_ref.shape, NEG_BIG, F32)
    acc_ref[...] = jnp.zeros(acc_ref.shape, F32)

    def numerators(c, bufs):
        pbuf, abuf = bufs
        off = pl.multiple_of(c * kc, kc)
        ck = ckv_ref[0, pl.ds(off, kc), 0:KV_RANK]
        bias = chunk_bias(c)
        for h in range(H):
            r = slice(h * qb, (h + 1) * qb)
            lg = head_logits(h, ck)
            z = [lg[:, j * LANES:(j + 1) * LANES] + bias[:, j * LANES:(j + 1) * LANES] for j in range(n_lt)]
            zm = z[0]
            for j in range(1, n_lt):
                zm = jnp.maximum(zm, z[j])
            m_old = m_ref[r, :]
            m_new = jnp.maximum(m_old, jnp.max(zm, axis=1, keepdims=True))
            m_ref[r, :] = m_new
            abuf[r, :] = jnp.exp2(m_old - m_new)
            for j in range(n_lt):
                pbuf[r, j * LANES:(j + 1) * LANES] = jnp.exp2(z[j] - m_new).astype(BF16)

    def accumulate(c, bufs):
        pbuf, abuf = bufs
        off = pl.multiple_of(c * kc, kc)
        ckx = ckv_ref[0, pl.ds(off, kc), :]
        pv = jnp.dot(pbuf[...], ckx, preferred_element_type=F32)
        a = abuf[...]
        acc_ref[:, 0:KV_RANK] = a * acc_ref[:, 0:KV_RANK] + pv[:, 0:KV_RANK]
        acc_ref[:, KV_RANK:2 * KV_RANK] = a * acc_ref[:, KV_RANK:2 * KV_RANK] + pv[:, KV_RANK:2 * KV_RANK]

    pipelined(numerators, accumulate, (p_ref, a_ref), (p2_ref, a2_ref))

    o_lat = (acc_ref[:, 0:KV_RANK] / acc_ref[:, KV_RANK:2 * KV_RANK]).astype(BF16)
    for h in range(0, H, 2):
        t = jnp.dot(o_lat[h * qb:(h + 1) * qb, :], wuv_ref[h], preferred_element_type=F32)
        t = t + jnp.dot(o_lat[(h + 1) * qb:(h + 2) * qb, :], wuv_ref[h + 1], preferred_element_type=F32)
        o_ref[0, :, h * ATT_HEAD_DIM:(h + 2) * ATT_HEAD_DIM] = t.astype(BF16)


def _dsa(qidx, widx, qlat, kidx, ckv, wuv_pad):
    B, H, L, _ = qidx.shape
    qb = min(DSA_QUERIES, L)
    kc = min(DSA_KEYS, L)
    top_k = min(TOPK_MAX, L // 4)
    idx_bits = max(1, (L - 1).bit_length())
    kern = functools.partial(_dsa_kernel, top_k=top_k, idx_bits=idx_bits)
    return pl.pallas_call(
        kern,
        grid=(B, L // qb),
        in_specs=[
            pl.BlockSpec((1, H, qb, IDX_HEAD_DIM), lambda b, q: (b, 0, q, 0)),
            pl.BlockSpec((1, qb, LANES), lambda b, q: (b, q, 0)),
            pl.BlockSpec((1, H, qb, KV_RANK), lambda b, q: (b, 0, q, 0)),
            pl.BlockSpec((1, L, IDX_HEAD_DIM), lambda b, q: (b, 0, 0)),
            pl.BlockSpec((1, L, 2 * KV_RANK), lambda b, q: (b, 0, 0)),
            pl.BlockSpec(wuv_pad.shape, lambda b, q: (0, 0, 0)),
        ],
        out_specs=pl.BlockSpec((1, qb, ATT_WIDTH), lambda b, q: (b, q, 0)),
        out_shape=jax.ShapeDtypeStruct((B, L, ATT_WIDTH), BF16),
        scratch_shapes=[
            pltpu.VMEM((L // kc, qb, kc), I32),
            pltpu.VMEM((L // kc, kc, qb), I32),
            pltpu.VMEM((H * qb, LANES), F32),
            pltpu.VMEM((H * qb, kc), F32),
            pltpu.VMEM((H * qb, kc), F32),
            pltpu.VMEM((H * qb, kc), BF16),
            pltpu.VMEM((H * qb, kc), BF16),
            pltpu.VMEM((H * qb, LANES), F32),
            pltpu.VMEM((H * qb, LANES), F32),
            pltpu.VMEM((H * qb, LANES), F32),
            pltpu.VMEM((H * qb, 2 * KV_RANK), F32),
            pltpu.VMEM((8, qb), I32),
        ],
        compiler_params=pltpu.CompilerParams(
            dimension_semantics=("arbitrary", "arbitrary"), vmem_limit_bytes=VMEM_LIMIT),
    )(qidx, widx, qlat, kidx, ckv, wuv_pad)


def _memkv_kernel(mem_ref, w_ref, k_ref, v_ref):
    d = k_ref.shape[2]
    kv = jnp.dot(mem_ref[0].astype(BF16), w_ref[...], preferred_element_type=F32)
    k_ref[0] = kv[:, 0:d].astype(BF16)
    v_ref[0] = kv[:, d:2 * d].astype(BF16)


def _memkv(mem, w_mkv):
    B, M, D = mem.shape
    return pl.pallas_call(
        _memkv_kernel,
        grid=(B,),
        in_specs=[pl.BlockSpec((1, M, D), lambda b: (b, 0, 0)),
                  pl.BlockSpec(w_mkv.shape, lambda b: (0, 0))],
        out_specs=[pl.BlockSpec((1, M, D), lambda b: (b, 0, 0)),
                   pl.BlockSpec((1, M, D), lambda b: (b, 0, 0))],
        out_shape=[jax.ShapeDtypeStruct((B, M, D), BF16), jax.ShapeDtypeStruct((B, M, D), BF16)],
        compiler_params=pltpu.CompilerParams(
            dimension_semantics=("arbitrary",), vmem_limit_bytes=VMEM_LIMIT),
    )(mem, w_mkv)


def _split3(v):
    hi = v.astype(BF16)
    r1 = v - hi.astype(F32)
    mid = r1.astype(BF16)
    lo = (r1 - mid.astype(F32)).astype(BF16)
    return hi, mid, lo


def _mix_kernel(x_ref, yp_ref, ya_ref, wo_ref, g1_ref, b1_ref, km_ref, vm_ref, wq_ref, wmo_ref,
                g2_ref, b2_ref, wr_ref, br_ref, x2_ref, tope_ref, gate_ref):
    tm, d = x_ref.shape
    hd = d // MEM_HEADS
    pw = yp_ref.shape[1]
    ws = _split3(wr_ref[...])

    def stage_mix(rs):
        mix = jnp.dot(yp_ref[rs, :], wo_ref[0:pw, :], preferred_element_type=F32)
        mix = mix + jnp.dot(ya_ref[rs, :], wo_ref[pw:, :], preferred_element_type=F32)
        return _layer_norm(DN_ALPHA * x_ref[rs, :] + mix, g1_ref[...], b1_ref[...])

    def stage_mem(x1):
        q = jnp.dot(x1.astype(BF16), wq_ref[...], preferred_element_type=F32).astype(BF16)
        scale = hd ** -0.5
        att = None
        for h in range(MEM_HEADS):
            c = slice(h * hd, (h + 1) * hd)
            lg = lax.dot_general(q[:, c], km_ref[0, :, c], _NT, preferred_element_type=F32) * scale
            p = jnp.exp(lg - jnp.max(lg, axis=-1, keepdims=True))
            p = p / jnp.sum(p, axis=-1, keepdims=True)
            oh = jnp.dot(p.astype(BF16), vm_ref[0, :, c], preferred_element_type=F32).astype(BF16)
            t = jnp.dot(oh, wmo_ref[c, :], preferred_element_type=F32)
            att = t if att is None else att + t
        return att

    def stage_route(rs, x1, att):
        n = rs.stop - rs.start
        x2 = _layer_norm(DN_ALPHA * x1 + att, g2_ref[...], b2_ref[...])
        x2_ref[rs, :] = x2

        xs = _split3(x2)
        lt = None
        for i, j in ((0, 0), (0, 1), (1, 0)):
            t = lax.dot_general(ws[j], xs[i], _NT, preferred_element_type=F32)
            lt = t if lt is None else lt + t
        lt = lt + br_ref[...]
        n_e = lt.shape[0]
        eidx = lax.broadcasted_iota(I32, lt.shape, 0)
        vals, idxs = [], []
        for _ in range(TOP_K_EXPERTS):
            mx = jnp.max(lt, axis=0, keepdims=True)
            ix = jnp.min(jnp.where(lt == mx, eidx, n_e), axis=0, keepdims=True)
            vals.append(mx)
            idxs.append(ix)
            lt = jnp.where(eidx == ix, -jnp.inf, lt)
        tope_ref[:, rs] = jnp.concatenate(idxs, axis=0)
        ex = [jnp.exp(v - vals[0]) for v in vals]
        den = ex[0]
        for e_ in ex[1:]:
            den = den + e_
        gates = jnp.concatenate([e_ / den for e_ in ex] + [jnp.zeros((LANES - TOP_K_EXPERTS, n), F32)], axis=0)
        gate_ref[rs, :] = jnp.transpose(gates)

    n_groups = MIX_ROW_GROUPS if tm % (MIX_ROW_GROUPS * LANES) == 0 else 1
    groups = [slice(g * (tm // n_groups), (g + 1) * (tm // n_groups)) for g in range(n_groups)]
    x1s = [stage_mix(rs) for rs in groups]
    atts = [stage_mem(x1) for x1 in x1s]
    for rs, x1, att in zip(groups, x1s, atts):
        stage_route(rs, x1, att)


def _mix(x2d, ypool, yatt, w_o, g1, b1, k_mem, v_mem, w_mq, w_mo, g2, b2, w_rt, b_r, B, L):
    T, D = x2d.shape
    tm = min(MIX_ROWS, L)
    nl = L // tm
    M = k_mem.shape[1]
    row = lambda i: (i, 0)
    full = lambda i: (0, 0)
    return pl.pallas_call(
        _mix_kernel,
        grid=(T // tm,),
        in_specs=[
            pl.BlockSpec((tm, D), row),
            pl.BlockSpec((tm, ypool.shape[1]), row),
            pl.BlockSpec((tm, yatt.shape[1]), row),
            pl.BlockSpec(w_o.shape, full),
            pl.BlockSpec(g1.shape, full),
            pl.BlockSpec(b1.shape, full),
            pl.BlockSpec((1, M, D), lambda i: (i // nl, 0, 0)),
            pl.BlockSpec((1, M, D), lambda i: (i // nl, 0, 0)),
            pl.BlockSpec(w_mq.shape, full),
            pl.BlockSpec(w_mo.shape, full),
            pl.BlockSpec(g2.shape, full),
            pl.BlockSpec(b2.shape, full),
            pl.BlockSpec(w_rt.shape, full),
            pl.BlockSpec(b_r.shape, full),
        ],
        out_specs=[
            pl.BlockSpec((tm, D), row),
            pl.BlockSpec((TOP_K_EXPERTS, tm), lambda i: (0, i)),
            pl.BlockSpec((tm, LANES), row),
        ],
        out_shape=[
            jax.ShapeDtypeStruct((T, D), F32),
            jax.ShapeDtypeStruct((TOP_K_EXPERTS, T), I32),
            jax.ShapeDtypeStruct((T, LANES), F32),
        ],
        compiler_params=pltpu.CompilerParams(
            dimension_semantics=("arbitrary",), vmem_limit_bytes=VMEM_LIMIT),
    )(x2d, ypool, yatt, w_o, g1, b1, k_mem, v_mem, w_mq, w_mo, g2, b2, w_rt, b_r)


def _rank_kernel(tope_ref, rank_ref, cnt_ref, carry_ref):
    i = pl.program_id(0)
    tr = tope_ref.shape[1]

    @pl.when(i == 0)
    def _():
        carry_ref[...] = jnp.zeros(carry_ref.shape, F32)

    eidx = lax.broadcasted_iota(I32, (N_EXPERTS, tr), 0)
    onehot = jnp.zeros((N_EXPERTS, tr), F32)
    for k in range(TOP_K_EXPERTS):
        onehot = onehot + jnp.where(eidx == tope_ref[k:k + 1, :], 1.0, 0.0)
    before = jnp.where(lax.broadcasted_iota(I32, (tr, tr), 0) < lax.broadcasted_iota(I32, (tr, tr), 1), 1.0, 0.0)
    excl = jnp.dot(onehot.astype(BF16), before.astype(BF16), preferred_element_type=F32)
    rank_full = excl + carry_ref[:, 0:1]
    rows = []
    for k in range(TOP_K_EXPERTS):
        rows.append(jnp.sum(jnp.where(eidx == tope_ref[k:k + 1, :], rank_full, 0.0), axis=0, keepdims=True))
    rank_ref[...] = jnp.concatenate(rows, axis=0).astype(I32)
    carry_ref[...] = carry_ref[...] + jnp.sum(onehot, axis=1, keepdims=True)
    cnt_ref[...] = carry_ref[...].astype(I32)


def _rank(tope):
    K, T = tope.shape
    tr = min(RANK_TOKENS, T)
    return pl.pallas_call(
        _rank_kernel,
        grid=(T // tr,),
        in_specs=[pl.BlockSpec((K, tr), lambda i: (0, i))],
        out_specs=[pl.BlockSpec((K, tr), lambda i: (0, i)),
                   pl.BlockSpec((N_EXPERTS, LANES), lambda i: (0, 0))],
        out_shape=[jax.ShapeDtypeStruct((K, T), I32), jax.ShapeDtypeStruct((N_EXPERTS, LANES), I32)],
        scratch_shapes=[pltpu.VMEM((N_EXPERTS, LANES), F32)],
        compiler_params=pltpu.CompilerParams(dimension_semantics=("arbitrary",)),
    )(tope)


def _dest_kernel(start_ref, tope_ref, rank_ref, dest_ref):
    tope = tope_ref[...]
    base = jnp.zeros(tope.shape, I32)
    for e in range(N_EXPERTS):
        base = jnp.where(tope == e, start_ref[e], base)
    dest_ref[...] = base + rank_ref[...]


def _dest(pad_start, tope, rank):
    K, T = tope.shape
    tt = min(DEST_TOKENS, T)
    blk = pl.BlockSpec((K, tt), lambda i: (0, i))
    return pl.pallas_call(
        _dest_kernel,
        grid=(T // tt,),
        in_specs=[pl.BlockSpec(memory_space=pltpu.SMEM), blk, blk],
        out_specs=blk,
        out_shape=jax.ShapeDtypeStruct((K, T), I32),
        compiler_params=pltpu.CompilerParams(dimension_semantics=("arbitrary",)),
    )(pad_start, tope, rank)


def _dispatch_kernel(dest_ref, start_ref, cnt_ref, padded_ref, x_ref, xs_hbm, zrow, sem, zsem):
    i = pl.program_id(0)
    td = dest_ref.shape[1]

    def row_copy(j, dst):
        return pltpu.make_async_copy(x_ref.at[pl.ds(j, 1), :], xs_hbm.at[pl.ds(dst, 1), :], sem)

    def zero_copy(dst):
        return pltpu.make_async_copy(zrow, xs_hbm.at[pl.ds(dst, 1), :], zsem)

    @pl.when(i == 0)
    def _():
        zrow[...] = jnp.zeros(zrow.shape, F32)
        for e in range(N_EXPERTS):
            first = start_ref[e] + cnt_ref[e]
            n_pad = padded_ref[e] - cnt_ref[e]

            def zstart(r, c):
                zero_copy(first + r).start()
                return c
            lax.fori_loop(0, n_pad, zstart, 0)

            def zwait(r, c):
                zero_copy(first + r).wait()
                return c
            lax.fori_loop(0, n_pad, zwait, 0)

    def issue(jj, c):
        for u in range(ISSUE_UNROLL):
            j = jj * ISSUE_UNROLL + u
            for k in range(TOP_K_EXPERTS):
                row_copy(j, dest_ref[k, j]).start()
        return c
    lax.fori_loop(0, td // ISSUE_UNROLL, issue, 0)

    def drain(jj, c):
        for _ in range(ISSUE_UNROLL * TOP_K_EXPERTS):
            row_copy(0, 0).wait()
        return c
    lax.fori_loop(0, td // ISSUE_UNROLL, drain, 0)


def _dispatch(dest, pad_start, counts, padded, x2, n_rows):
    K, T = dest.shape
    D = x2.shape[1]
    td = min(DISPATCH_TOKENS, T)
    smem_tok = pl.BlockSpec((K, td), lambda i: (0, i), memory_space=pltpu.SMEM)
    smem_full = pl.BlockSpec(memory_space=pltpu.SMEM)
    return pl.pallas_call(
        _dispatch_kernel,
        grid=(T // td,),
        in_specs=[smem_tok, smem_full, smem_full, smem_full,
                  pl.BlockSpec((td, D), lambda i: (i, 0))],
        out_specs=pl.BlockSpec(memory_space=pl.ANY),
        out_shape=jax.ShapeDtypeStruct((n_rows, D), F32),
        scratch_shapes=[pltpu.VMEM((1, D), F32), pltpu.SemaphoreType.DMA(()), pltpu.SemaphoreType.DMA(())],
        compiler_params=pltpu.CompilerParams(
            dimension_semantics=("arbitrary",), vmem_limit_bytes=VMEM_LIMIT),
    )(dest, pad_start, counts, padded, x2)


def _expert_kernel(bexp_ref, nused_ref, xs_ref, wgu_ref, bgu_ref, wd_ref, bd_ref, y_ref, wgu_bf, wd_bf):
    j = pl.program_id(0)
    d_ff = wd_ref.shape[1]

    @pl.when(j < nused_ref[0])
    def _():
        prev = bexp_ref[jnp.maximum(j - 1, 0)]

        @pl.when(jnp.logical_or(j == 0, bexp_ref[j] != prev))
        def _():
            wgu_bf[...] = wgu_ref[0].astype(BF16)
            wd_bf[...] = wd_ref[0].astype(BF16)

        xb = xs_ref[...].astype(BF16)
        gu = jnp.dot(xb, wgu_bf[...], preferred_element_type=F32) + bgu_ref[0]
        g = jnp.minimum(gu[:, 0:d_ff], SWIGLU_LIMIT)
        u = jnp.clip(gu[:, d_ff:2 * d_ff], -SWIGLU_LIMIT, SWIGLU_LIMIT)
        act = (u + 1.0) * g * (1.0 / (1.0 + jnp.exp(-SWIGLU_ALPHA * g)))
        y_ref[...] = jnp.dot(act.astype(BF16), wd_bf[...], preferred_element_type=F32) + bd_ref[0]


def _experts(block_exp, n_used, xs, w_gate_up, b_gate_up, w_down, b_down):
    P, D = xs.shape
    blk = EXPERT_ROWS
    nblk = P // blk
    E, _, F2 = w_gate_up.shape
    d_ff = w_down.shape[1]
    row = lambda j, be, nu: (jnp.minimum(j, nu[0] - 1), 0)
    exp3 = lambda j, be, nu: (be[j], 0, 0)
    grid_spec = pltpu.PrefetchScalarGridSpec(
        num_scalar_prefetch=2,
        grid=(nblk,),
        in_specs=[
            pl.BlockSpec((blk, D), row),
            pl.BlockSpec((1, D, F2), exp3),
            pl.BlockSpec((1, 1, F2), exp3),
            pl.BlockSpec((1, d_ff, D), exp3),
            pl.BlockSpec((1, 1, D), exp3),
        ],
        out_specs=pl.BlockSpec((blk, D), row),
        scratch_shapes=[pltpu.VMEM((D, F2), BF16), pltpu.VMEM((d_ff, D), BF16)],
    )
    return pl.pallas_call(
        _expert_kernel,
        grid_spec=grid_spec,
        out_shape=jax.ShapeDtypeStruct((P, D), F32),
        compiler_params=pltpu.CompilerParams(
            dimension_semantics=("arbitrary",), vmem_limit_bytes=VMEM_LIMIT),
    )(block_exp, n_used, xs, w_gate_up, b_gate_up.reshape(E, 1, F2), w_down, b_down.reshape(E, 1, D))


def _combine_kernel(dest_ref, dest_next_ref, x2_ref, gate_ref, g3_ref, b3_ref, y_hbm, o_ref, ybuf, sem):
    i = pl.program_id(0)
    n = pl.num_programs(0)
    tc = x2_ref.shape[0]
    slot = i % 2

    def row_copy(src_ref, s, j, k):
        return pltpu.make_async_copy(y_hbm.at[pl.ds(src_ref[k, j], 1), :], ybuf.at[s, k, pl.ds(j, 1), :], sem.at[s])

    def gather(src_ref, s):
        def issue(jj, c):
            for u in range(ISSUE_UNROLL):
                for k in range(TOP_K_EXPERTS):
                    row_copy(src_ref, s, jj * ISSUE_UNROLL + u, k).start()
            return c
        lax.fori_loop(0, tc // ISSUE_UNROLL, issue, 0)

    @pl.when(i == 0)
    def _():
        gather(dest_ref, 0)

    @pl.when(i + 1 < n)
    def _():
        gather(dest_next_ref, 1 - slot)

    def drain(jj, c):
        for _ in range(ISSUE_UNROLL * TOP_K_EXPERTS):
            pltpu.make_async_copy(y_hbm.at[pl.ds(0, 1), :], ybuf.at[slot, 0, pl.ds(0, 1), :], sem.at[slot]).wait()
        return c
    lax.fori_loop(0, tc // ISSUE_UNROLL, drain, 0)

    moe = None
    for k in range(TOP_K_EXPERTS):
        t = ybuf[slot, k] * gate_ref[:, k:k + 1]
        moe = t if moe is None else moe + t
    o_ref[...] = _layer_norm(DN_ALPHA * x2_ref[...] + moe, g3_ref[...], b3_ref[...])


def _combine(dest, x2, gates, g3, b3, y):
    K, T = dest.shape
    D = x2.shape[1]
    tc = min(COMBINE_TOKENS, T)
    n = T // tc
    row = lambda i: (i, 0)
    full = lambda i: (0, 0)
    return pl.pallas_call(
        _combine_kernel,
        grid=(n,),
        in_specs=[pl.BlockSpec((K, tc), lambda i: (0, i), memory_space=pltpu.SMEM),
                  pl.BlockSpec((K, tc), lambda i: (0, jnp.minimum(i + 1, n - 1)), memory_space=pltpu.SMEM),
                  pl.BlockSpec((tc, D), row), pl.BlockSpec((tc, LANES), row),
                  pl.BlockSpec(g3.shape, full), pl.BlockSpec(b3.shape, full),
                  pl.BlockSpec(memory_space=pl.ANY)],
        out_specs=pl.BlockSpec((tc, D), row),
        out_shape=jax.ShapeDtypeStruct((T, D), F32),
        scratch_shapes=[pltpu.VMEM((2, K, tc, D), F32), pltpu.SemaphoreType.DMA((2,))],
        compiler_params=pltpu.CompilerParams(
            dimension_semantics=("arbitrary",), vmem_limit_bytes=VMEM_LIMIT),
    )(dest, dest, x2, gates, g3, b3, y)


def _pad_cols(w, width):
    return jnp.pad(w, ((0, 0), (0, width - w.shape[1])))


def _layer(x, mem, w_in, w_pool, pool_scale, ik_g, ik_b, kv_g, w_uk, w_uv, w_o, ln1_g, ln1_b,
           w_mq, w_mkv, w_mo, ln2_g, ln2_b, w_router, b_router, w_gate_up, b_gate_up, w_down, b_down,
           ln3_g, ln3_b):
    B, L, D = x.shape
    T = B * L
    row = lambda v: v.reshape(1, -1)

    o = 0
    pieces = []
    for width in (POOL_WIDTH, ATT_WIDTH, KV_RANK, IDX_HEADS * IDX_HEAD_DIM, IDX_HEAD_DIM, IDX_HEADS):
        pieces.append(w_in[:, o:o + width])
        o += width
    pieces[4] = _pad_cols(pieces[4], LANES)
    pieces[5] = _pad_cols(pieces[5], LANES)
    w_in_pad = jnp.concatenate(pieces, axis=1).astype(BF16)
    wuv_pad = jnp.zeros((ATT_HEADS, KV_RANK, 2 * ATT_HEAD_DIM), F32)
    for h in range(ATT_HEADS):
        wuv_pad = wuv_pad.at[h, :, (h % 2) * ATT_HEAD_DIM:(h % 2 + 1) * ATT_HEAD_DIM].set(w_uv[h])

    ypool, qlat, ckv, qidx, kidx, widx = _inproj(
        x, w_in_pad, w_pool.astype(BF16), row(pool_scale), row(kv_g), row(ik_g), row(ik_b), w_uk.astype(BF16))
    yatt = _dsa(qidx, widx, qlat, kidx, ckv, wuv_pad.astype(BF16))
    k_mem, v_mem = _memkv(mem, w_mkv.astype(BF16))
    x2, tope, gates = _mix(
        x.reshape(T, D), ypool.reshape(T, -1), yatt.reshape(T, -1), w_o.astype(BF16), row(ln1_g), row(ln1_b),
        k_mem, v_mem, w_mq.astype(BF16), w_mo.astype(BF16), row(ln2_g), row(ln2_b),
        jnp.transpose(w_router), b_router.reshape(-1, 1), B, L)

    rank, cnt = _rank(tope)
    counts = cnt[:, 0]
    blk = EXPERT_ROWS
    padded = (counts + blk - 1) // blk * blk
    pad_end = jnp.cumsum(padded)
    pad_start = pad_end - padded
    n_rows = T * TOP_K_EXPERTS + N_EXPERTS * blk
    nblk = n_rows // blk
    n_used = (pad_end[-1] // blk).astype(I32)
    blk_first = jnp.minimum(jnp.arange(nblk, dtype=I32), n_used - 1) * blk
    n_ended = jnp.sum((pad_end[None, :] <= blk_first[:, None]).astype(I32), axis=1)
    block_exp = jnp.minimum(n_ended, N_EXPERTS - 1).astype(I32)

    pad_start = pad_start.astype(I32)
    dest = _dest(pad_start, tope, rank)
    xs = _dispatch(dest, pad_start, counts, padded.astype(I32), x2, n_rows)
    y = _experts(block_exp, n_used.reshape(1), xs, w_gate_up, b_gate_up, w_down, b_down)
    out = _combine(dest, x2, gates, row(ln3_g), row(ln3_b), y)
    return out.reshape(B, L, D)


def kernel(x, mem, w_in, w_pool, pool_scale, idx_k_norm_g, idx_k_norm_b, kv_norm_g, w_uk, w_uv, w_o, ln1_g, ln1_b, w_mq, w_mkv, w_mo, ln2_g, ln2_b, w_router, b_router, w_gate_up, b_gate_up, w_down, b_down, ln3_g, ln3_b):
    assert w_in.shape[0] == DEPTH
    return _layer(x, mem, w_in[0], w_pool[0], pool_scale[0], idx_k_norm_g[0], idx_k_norm_b[0], kv_norm_g[0],
                  w_uk[0], w_uv[0], w_o[0], ln1_g[0], ln1_b[0], w_mq[0], w_mkv[0], w_mo[0], ln2_g[0], ln2_b[0],
                  w_router[0], b_router[0], w_gate_up[0], b_gate_up[0], w_down[0], b_down[0], ln3_g[0], ln3_b[0])
```

```python
import functools

import jax
import jax.numpy as jnp
from jax import lax
from jax.experimental import pallas as pl
from jax.experimental.pallas import tpu as pltpu

F32 = jnp.float32
BF16 = jnp.bfloat16
I32 = jnp.int32

POOL_WINDOWS = (2, 4, 8, 16)
POOL_GROUP_DIM = 128
POOL_WIDTH = 512
ATT_HEADS = 8
ATT_HEAD_DIM = 64
ATT_WIDTH = 512
KV_RANK = 128
IDX_HEADS = 8
IDX_HEAD_DIM = 64
TOPK_MAX = 256
MEM_HEADS = 4
N_EXPERTS = 32
TOP_K_EXPERTS = 4
SWIGLU_LIMIT = 7.0
SWIGLU_ALPHA = 1.702
LN_EPS = 1e-5
RMS_EPS = 1e-6
DEPTH = 1
DN_ALPHA = (2 * DEPTH) ** 0.25

LANES = 128
MAX_POOL_WINDOW = 16
VMEM_LIMIT = 56 * 1024 * 1024

PROJ_ROWS = 512
DSA_QUERIES = 128
DSA_KEYS = 512
FIELD_BITS = 15
SEARCH_CHECK_BIT = 6
MIX_ROWS = 1024
RANK_TOKENS = 512
DEST_TOKENS = 2048
DISPATCH_TOKENS = 512
ISSUE_UNROLL = 8
EXPERT_ROWS = 256
COMBINE_TOKENS = 256

LOG2_E = 1.4426950408889634
NEG_BIG = -1e30
INT_MIN = -(2 ** 31)
NEG_INF_KEY = INT_MIN + 0x007FFFFF

_NT = (((1,), (1,)), ((), ()))


def _layer_norm(v, g, b):
    mu = jnp.mean(v, axis=-1, keepdims=True)
    d = v - mu
    var = jnp.mean(d * d, axis=-1, keepdims=True)
    return d * lax.rsqrt(var + LN_EPS) * g + b


_C_POOL = 0
_C_Q = 512
_C_CKV = 1024
_C_QIDX = 1152
_C_KIDX = 1664
_C_WIDX = 1792
_IN_PAD = 1920


def _inproj_kernel(x_ref, w_ref, wpool_ref, pscale_ref, kvg_ref, ikg_ref, ikb_ref, wuk_ref,
                   ypool_ref, qlat_ref, ckv_ref, qidx_ref, kidx_ref, widx_ref, ubuf):
    li = pl.program_id(1)
    tm = x_ref.shape[1]
    halo = MAX_POOL_WINDOW
    xb = x_ref[0].astype(BF16)

    u = jnp.dot(xb, w_ref[:, _C_POOL:_C_POOL + POOL_WIDTH], preferred_element_type=F32)

    @pl.when(li == 0)
    def _():
        ubuf[0:halo, :] = jnp.zeros((halo, POOL_WIDTH), F32)

    ubuf[halo:halo + tm, :] = u
    pos = li * tm + lax.broadcasted_iota(I32, (tm, 1), 0)
    for g, w in enumerate(POOL_WINDOWS):
        c0 = g * POOL_GROUP_DIM
        c1 = c0 + POOL_GROUP_DIM
        ug = ubuf[halo:halo + tm, c0:c1]
        s = ug
        for j in range(1, w):
            s = s + ubuf[halo - j:halo - j + tm, c0:c1]
        cnt = jnp.minimum(pos + 1, w).astype(F32)
        d = s / cnt - ug
        y = jnp.dot(d.astype(BF16), wpool_ref[g], preferred_element_type=F32) * pscale_ref[:, c0:c1]
        ypool_ref[0, :, c0:c1] = y.astype(BF16)
    ubuf[0:halo, :] = ubuf[tm:tm + halo, :]

    q = jnp.dot(xb, w_ref[:, _C_Q:_C_Q + ATT_WIDTH], preferred_element_type=F32)
    att_scale = (ATT_HEAD_DIM ** -0.5) * LOG2_E
    for h in range(ATT_HEADS):
        qh = q[:, h * ATT_HEAD_DIM:(h + 1) * ATT_HEAD_DIM].astype(BF16)
        ql = jnp.dot(qh, wuk_ref[h], preferred_element_type=F32) * att_scale
        qlat_ref[0, h] = ql.astype(BF16)

    c = jnp.dot(xb, w_ref[:, _C_CKV:_C_CKV + KV_RANK], preferred_element_type=F32)
    c = c * lax.rsqrt(jnp.mean(c * c, axis=-1, keepdims=True) + RMS_EPS) * kvg_ref[...]
    ckv_ref[0, :, 0:KV_RANK] = c.astype(BF16)
    ckv_ref[0, :, KV_RANK:2 * KV_RANK] = jnp.ones((tm, KV_RANK), BF16)

    qi = jnp.dot(xb, w_ref[:, _C_QIDX:_C_QIDX + IDX_HEADS * IDX_HEAD_DIM], preferred_element_type=F32)
    for h in range(IDX_HEADS):
        qidx_ref[0, h] = qi[:, h * IDX_HEAD_DIM:(h + 1) * IDX_HEAD_DIM].astype(BF16)
    kw = jnp.dot(xb, w_ref[:, _C_KIDX:_C_KIDX + 2 * LANES], preferred_element_type=F32)
    kr = kw[:, 0:IDX_HEAD_DIM]
    kidx_ref[0] = _layer_norm(kr, ikg_ref[...], ikb_ref[...]).astype(BF16)
    widx_ref[0] = kw[:, LANES:2 * LANES]


def _inproj(x, w_in_pad, w_pool, pool_scale, kv_g, ik_g, ik_b, w_uk):
    B, L, D = x.shape
    tm = min(PROJ_ROWS, L)
    nl = L // tm
    full2 = lambda b, l: (0, 0)
    full3 = lambda b, l: (0, 0, 0)
    return pl.pallas_call(
        _inproj_kernel,
        grid=(B, nl),
        in_specs=[
            pl.BlockSpec((1, tm, D), lambda b, l: (b, l, 0)),
            pl.BlockSpec(w_in_pad.shape, full2),
            pl.BlockSpec(w_pool.shape, full3),
            pl.BlockSpec(pool_scale.shape, full2),
            pl.BlockSpec(kv_g.shape, full2),
            pl.BlockSpec(ik_g.shape, full2),
            pl.BlockSpec(ik_b.shape, full2),
            pl.BlockSpec(w_uk.shape, full3),
        ],
        out_specs=[
            pl.BlockSpec((1, tm, POOL_WIDTH), lambda b, l: (b, l, 0)),
            pl.BlockSpec((1, ATT_HEADS, tm, KV_RANK), lambda b, l: (b, 0, l, 0)),
            pl.BlockSpec((1, tm, 2 * KV_RANK), lambda b, l: (b, l, 0)),
            pl.BlockSpec((1, IDX_HEADS, tm, IDX_HEAD_DIM), lambda b, l: (b, 0, l, 0)),
            pl.BlockSpec((1, tm, IDX_HEAD_DIM), lambda b, l: (b, l, 0)),
            pl.BlockSpec((1, tm, LANES), lambda b, l: (b, l, 0)),
        ],
        out_shape=[
            jax.ShapeDtypeStruct((B, L, POOL_WIDTH), BF16),
            jax.ShapeDtypeStruct((B, ATT_HEADS, L, KV_RANK), BF16),
            jax.ShapeDtypeStruct((B, L, 2 * KV_RANK), BF16),
            jax.ShapeDtypeStruct((B, IDX_HEADS, L, IDX_HEAD_DIM), BF16),
            jax.ShapeDtypeStruct((B, L, IDX_HEAD_DIM), BF16),
            jax.ShapeDtypeStruct((B, L, LANES), F32),
        ],
        scratch_shapes=[pltpu.VMEM((MAX_POOL_WINDOW + tm, POOL_WIDTH), F32)],
        compiler_params=pltpu.CompilerParams(
            dimension_semantics=("arbitrary", "arbitrary"), vmem_limit_bytes=VMEM_LIMIT),
    )(x, w_in_pad, w_pool, pool_scale, kv_g, ik_g, ik_b, w_uk)


def _sortable(score):
    bits = pltpu.bitcast(score, I32)
    return bits ^ ((bits >> 31) & 0x7FFFFFFF)


def _dsa_kernel(qidx_ref, widx_ref, qlat_ref, kidx_ref, ckv_ref, wuv_ref, o_ref,
                key_ref, keyt_ref, w_ref, wb_ref, s_ref, s2_ref, p_ref, p2_ref, a_ref, a2_ref, m_ref, acc_ref, sel_ref,
                *, top_k, idx_bits):
    qi = pl.program_id(1)
    H = ATT_HEADS
    qb = qidx_ref.shape[2]
    kc = s_ref.shape[1]
    qstart = qi * qb
    n_ch = (qstart + qb + kc - 1) // kc
    idx_scale = (IDX_HEAD_DIM ** -0.5) * (IDX_HEADS ** -0.5)

    qpos = qstart + lax.broadcasted_iota(I32, (qb, 1), 0)
    qpos_t = qstart + lax.broadcasted_iota(I32, (1, qb), 1)
    lane_pos = lax.broadcasted_iota(I32, (1, kc), 1)
    row_pos = lax.broadcasted_iota(I32, (kc, 1), 0)

    for h in range(H):
        wb_ref[h * qb:(h + 1) * qb, :] = jnp.broadcast_to(widx_ref[0, :, h:h + 1] * idx_scale, (qb, LANES))

    def pipelined(produce, consume, buf_a, buf_b):
        produce(0, buf_a)
        n_pairs = (n_ch - 1) // 2

        def pair_body(i, carry):
            c = 2 * i
            consume(c, buf_a)
            produce(c + 1, buf_b)
            consume(c + 1, buf_b)
            produce(c + 2, buf_a)
            return carry

        lax.fori_loop(0, n_pairs, pair_body, 0)
        last = 2 * n_pairs

        @pl.when(n_ch - last == 1)
        def _():
            consume(last, buf_a)

        @pl.when(n_ch - last == 2)
        def _():
            consume(last, buf_a)
            produce(last + 1, buf_b)
            consume(last + 1, buf_b)

    def head_scores(c, buf):
        off = pl.multiple_of(c * kc, kc)
        kk = kidx_ref[0, pl.ds(off, kc), :]
        qs = qidx_ref[0].reshape(H * qb, IDX_HEAD_DIM)
        buf[...] = lax.dot_general(qs, kk, _NT, preferred_element_type=F32)

    def chunk_keys(c, buf):
        for j in range(kc // LANES):
            cs = slice(j * LANES, (j + 1) * LANES)
            acc = None
            for h in range(H):
                r = slice(h * qb, (h + 1) * qb)
                term = jnp.maximum(buf[r, cs], 0.0) * wb_ref[r, 0:LANES]
                acc = term if acc is None else acc + term
            score = jnp.where(c * kc + j * LANES + lane_pos[:, 0:LANES] <= qpos, acc, -jnp.inf)
            key = _sortable(score)
            key_ref[c, :, cs] = key
            keyt_ref[c, cs, :] = jnp.transpose(key)

    pipelined(head_scores, chunk_keys, s_ref, s2_ref)

    def count(indicator):
        def body(c, cnt):
            m = indicator(keyt_ref[c], c * kc + row_pos)
            parts = [m[j * 8:(j + 1) * 8, :] for j in range(kc // 8)]
            while len(parts) > 1:
                parts = [parts[j] + parts[j + 1] for j in range(0, len(parts), 2)]
            return cnt + parts[0]
        cnt = lax.fori_loop(0, n_ch, body, jnp.zeros((8, qb), I32))
        return jnp.sum(cnt, axis=0, keepdims=True)

    half = kc // 2
    f_max = (1 << FIELD_BITS) - 1
    guards = jnp.int32(-(1 << 31) + (1 << FIELD_BITS))
    units = jnp.int32((1 << 16) + 1)
    n_all = jnp.full((1, qb), n_ch * kc, I32)

    def field1(k):
        return (k >> (32 - FIELD_BITS)) + (1 << (FIELD_BITS - 1))

    def field2(k):
        return (k >> (32 - 2 * FIELD_BITS)) & f_max

    def pack_fields(field):
        def body(c, carry):
            fa = field(keyt_ref[c, 0:half, :])
            fb = field(keyt_ref[c, half:kc, :])
            w_ref[c] = lax.shift_left(fb, 16) | fa | guards
            return carry
        lax.fori_loop(0, n_ch, body, 0)

    def count_fields(cand):
        cc = cand | lax.shift_left(cand, 16)

        def body(c, cnt):
            g = lax.shift_right_logical(w_ref[c] - cc, FIELD_BITS) & units
            parts = [g[j * 8:(j + 1) * 8, :] for j in range(half // 8)]
            while len(parts) > 1:
                parts = [parts[j] + parts[j + 1] for j in range(0, len(parts), 2)]
            return cnt + parts[0]
        s = jnp.sum(lax.fori_loop(0, n_ch, body, jnp.zeros((8, qb), I32)), axis=0, keepdims=True)
        return (s & 0xFFFF) + lax.shift_right_logical(s, 16)

    def search_bits(first, last, t, n_t, n_above, counter):
        def body(i, carry):
            t, n_t = carry
            cand = t + lax.shift_left(jnp.int32(1), first - i)
            n = n_above + counter(cand)
            ok = n >= top_k
            return jnp.where(ok, cand, t), jnp.where(ok, n, n_t)
        return lax.fori_loop(0, first - last, body, (t, n_t))

    zero = jnp.zeros((1, qb), I32)
    pack_fields(field1)
    t1, n_t = search_bits(FIELD_BITS - 1, -1, zero, n_all, zero, count_fields)
    n_above = jnp.where(t1 < f_max, count_fields(jnp.minimum(t1 + 1, f_max)), 0)
    pack_fields(lambda k: jnp.where(field1(k) == t1, field2(k), 0))
    t2, n_t = search_bits(FIELD_BITS - 1, SEARCH_CHECK_BIT - 1, zero, n_t, n_above, count_fields)

    def prefix(t2):
        return lax.shift_left(t1 - (1 << (FIELD_BITS - 1)), 32 - FIELD_BITS) | lax.shift_left(t2, 32 - 2 * FIELD_BITS)

    sel_ref[0:1, :] = prefix(t2)
    sel_ref[2:3, :] = n_t
    sel_ref[3:4, :] = t2

    @pl.when(jnp.max(jnp.where(sel_ref[2:3, :] != top_k, 1, 0)) > 0)
    def _():
        t2, n_t = search_bits(SEARCH_CHECK_BIT - 1, -1, sel_ref[3:4, :], sel_ref[2:3, :], n_above, count_fields)
        t, n_t = search_bits(32 - 2 * FIELD_BITS - 1, -1, prefix(t2), n_t, zero,
                             lambda cand: count(lambda k, _: jnp.where(k >= cand, 1, 0)))
        sel_ref[0:1, :] = t
        sel_ref[2:3, :] = n_t

    thr_t = sel_ref[0:1, :]
    n_ge = sel_ref[2:3, :]

    surplus = jnp.where(thr_t > NEG_INF_KEY, jnp.where(n_ge > top_k, 1, 0), 0)
    sel_ref[1:2, :] = qpos_t

    @pl.when(jnp.max(surplus) > 0)
    def _():
        need = top_k - count(lambda k, _: jnp.where(k > thr_t, 1, 0))

        def tie_body(i, m):
            cand = m + lax.shift_left(jnp.int32(1), idx_bits - 1 - i)
            n_before = count(lambda k, p: jnp.where(k == thr_t, jnp.where(p < cand, 1, 0), 0))
            return jnp.where(n_before < need, cand, m)
        last = lax.fori_loop(0, idx_bits, tie_body, jnp.zeros((1, qb), I32))
        sel_ref[1:2, :] = jnp.where(surplus > 0, jnp.minimum(last, qpos_t), qpos_t)

    thr = jnp.transpose(jnp.broadcast_to(sel_ref[0:1, :], (qb, qb)))[:, 0:1]
    tie_last = jnp.transpose(jnp.broadcast_to(sel_ref[1:2, :], (qb, qb)))[:, 0:1]

    def chunk_bias(c):
        key = key_ref[c]
        kpos = c * kc + lane_pos
        tie_bias = jnp.where(key == thr, jnp.where(kpos <= tie_last, 0.0, NEG_BIG), NEG_BIG)
        return jnp.where(key > thr, 0.0, tie_bias)

    n_lt = kc // LANES

    def head_logits(h, ck):
        return lax.dot_general(qlat_ref[0, h], ck, _NT, preferred_element_type=F32)

    m_ref[...] = jnp.full(m_ref.shape, NEG_BIG, F32)
    acc_ref[...] = jnp.zeros(acc_ref.shape, F32)

    def numerators(c, bufs):
        pbuf, abuf = bufs
        off = pl.multiple_of(c * kc, kc)
        ck = ckv_ref[0, pl.ds(off, kc), 0:KV_RANK]
        bias = chunk_bias(c)
        for h in range(H):
            r = slice(h * qb, (h + 1) * qb)
            lg = head_logits(h, ck)
            z = [lg[:, j * LANES:(j + 1) * LANES] + bias[:, j * LANES:(j + 1) * LANES] for j in range(n_lt)]
            zm = z[0]
            for j in range(1, n_lt):
                zm = jnp.maximum(zm, z[j])
            m_old = m_ref[r, :]
            m_new = jnp.maximum(m_old, jnp.max(zm, axis=1, keepdims=True))
            m_ref[r, :] = m_new
            abuf[r, :] = jnp.exp2(m_old - m_new)
            for j in range(n_lt):
                pbuf[r, j * LANES:(j + 1) * LANES] = jnp.exp2(z[j] - m_new).astype(BF16)

    def accumulate(c, bufs):
        pbuf, abuf = bufs
        off = pl.multiple_of(c * kc, kc)
        ckx = ckv_ref[0, pl.ds(off, kc), :]
        pv = jnp.dot(pbuf[...], ckx, preferred_element_type=F32)
        a = abuf[...]
        acc_ref[:, 0:KV_RANK] = a * acc_ref[:, 0:KV_RANK] + pv[:, 0:KV_RANK]
        acc_ref[:, KV_RANK:2 * KV_RANK] = a * acc_ref[:, KV_RANK:2 * KV_RANK] + pv[:, KV_RANK:2 * KV_RANK]

    pipelined(numerators, accumulate, (p_ref, a_ref), (p2_ref, a2_ref))

    o_lat = (acc_ref[:, 0:KV_RANK] / acc_ref[:, KV_RANK:2 * KV_RANK]).astype(BF16)
    for h in range(0, H, 2):
        t = jnp.dot(o_lat[h * qb:(h + 1) * qb, :], wuv_ref[h], preferred_element_type=F32)
        t = t + jnp.dot(o_lat[(h + 1) * qb:(h + 2) * qb, :], wuv_ref[h + 1], preferred_element_type=F32)
        o_ref[0, :, h * ATT_HEAD_DIM:(h + 2) * ATT_HEAD_DIM] = t.astype(BF16)


def _dsa(qidx, widx, qlat, kidx, ckv, wuv_pad):
    B, H, L, _ = qidx.shape
    qb = min(DSA_QUERIES, L)
    kc = min(DSA_KEYS, L)
    top_k = min(TOPK_MAX, L // 4)
    idx_bits = max(1, (L - 1).bit_length())
    kern = functools.partial(_dsa_kernel, top_k=top_k, idx_bits=idx_bits)
    return pl.pallas_call(
        kern,
        grid=(B, L // qb),
        in_specs=[
            pl.BlockSpec((1, H, qb, IDX_HEAD_DIM), lambda b, q: (b, 0, q, 0)),
            pl.BlockSpec((1, qb, LANES), lambda b, q: (b, q, 0)),
            pl.BlockSpec((1, H, qb, KV_RANK), lambda b, q: (b, 0, q, 0)),
            pl.BlockSpec((1, L, IDX_HEAD_DIM), lambda b, q: (b, 0, 0)),
            pl.BlockSpec((1, L, 2 * KV_RANK), lambda b, q: (b, 0, 0)),
            pl.BlockSpec(wuv_pad.shape, lambda b, q: (0, 0, 0)),
        ],
        out_specs=pl.BlockSpec((1, qb, ATT_WIDTH), lambda b, q: (b, q, 0)),
        out_shape=jax.ShapeDtypeStruct((B, L, ATT_WIDTH), BF16),
        scratch_shapes=[
            pltpu.VMEM((L // kc, qb, kc), I32),
            pltpu.VMEM((L // kc, kc, qb), I32),
            pltpu.VMEM((L // kc, kc // 2, qb), I32),
            pltpu.VMEM((H * qb, LANES), F32),
            pltpu.VMEM((H * qb, kc), F32),
            pltpu.VMEM((H * qb, kc), F32),
            pltpu.VMEM((H * qb, kc), BF16),
            pltpu.VMEM((H * qb, kc), BF16),
            pltpu.VMEM((H * qb, LANES), F32),
            pltpu.VMEM((H * qb, LANES), F32),
            pltpu.VMEM((H * qb, LANES), F32),
            pltpu.VMEM((H * qb, 2 * KV_RANK), F32),
            pltpu.VMEM((8, qb), I32),
        ],
        compiler_params=pltpu.CompilerParams(
            dimension_semantics=("arbitrary", "arbitrary"), vmem_limit_bytes=VMEM_LIMIT),
    )(qidx, widx, qlat, kidx, ckv, wuv_pad)


def _memkv_kernel(mem_ref, w_ref, k_ref, v_ref):
    d = k_ref.shape[2]
    kv = jnp.dot(mem_ref[0].astype(BF16), w_ref[...], preferred_element_type=F32)
    k_ref[0] = kv[:, 0:d].astype(BF16)
    v_ref[0] = kv[:, d:2 * d].astype(BF16)


def _memkv(mem, w_mkv):
    B, M, D = mem.shape
    return pl.pallas_call(
        _memkv_kernel,
        grid=(B,),
        in_specs=[pl.BlockSpec((1, M, D), lambda b: (b, 0, 0)),
                  pl.BlockSpec(w_mkv.shape, lambda b: (0, 0))],
        out_specs=[pl.BlockSpec((1, M, D), lambda b: (b, 0, 0)),
                   pl.BlockSpec((1, M, D), lambda b: (b, 0, 0))],
        out_shape=[jax.ShapeDtypeStruct((B, M, D), BF16), jax.ShapeDtypeStruct((B, M, D), BF16)],
        compiler_params=pltpu.CompilerParams(
            dimension_semantics=("arbitrary",), vmem_limit_bytes=VMEM_LIMIT),
    )(mem, w_mkv)


def _split3(v):
    hi = v.astype(BF16)
    r1 = v - hi.astype(F32)
    mid = r1.astype(BF16)
    lo = (r1 - mid.astype(F32)).astype(BF16)
    return hi, mid, lo


def _mix_kernel(x_ref, yp_ref, ya_ref, wo_ref, g1_ref, b1_ref, km_ref, vm_ref, wq_ref, wmo_ref,
                g2_ref, b2_ref, wr_ref, br_ref, x2_ref, tope_ref, gate_ref):
    tm, d = x_ref.shape
    hd = d // MEM_HEADS
    pw = yp_ref.shape[1]
    ws = _split3(wr_ref[...])

    def stage_mix(rs):
        mix = jnp.dot(yp_ref[rs, :], wo_ref[0:pw, :], preferred_element_type=F32)
        mix = mix + jnp.dot(ya_ref[rs, :], wo_ref[pw:, :], preferred_element_type=F32)
        return _layer_norm(DN_ALPHA * x_ref[rs, :] + mix, g1_ref[...], b1_ref[...])

    def stage_mem(x1):
        q = jnp.dot(x1.astype(BF16), wq_ref[...], preferred_element_type=F32).astype(BF16)
        scale = hd ** -0.5
        att = None
        for h in range(MEM_HEADS):
            c = slice(h * hd, (h + 1) * hd)
            lg = lax.dot_general(q[:, c], km_ref[0, :, c], _NT, preferred_element_type=F32) * scale
            p = jnp.exp(lg - jnp.max(lg, axis=-1, keepdims=True))
            p = p / jnp.sum(p, axis=-1, keepdims=True)
            oh = jnp.dot(p.astype(BF16), vm_ref[0, :, c], preferred_element_type=F32).astype(BF16)
            t = jnp.dot(oh, wmo_ref[c, :], preferred_element_type=F32)
            att = t if att is None else att + t
        return att

    def stage_route(rs, x1, att):
        n = rs.stop - rs.start
        x2 = _layer_norm(DN_ALPHA * x1 + att, g2_ref[...], b2_ref[...])
        x2_ref[rs, :] = x2

        xs = _split3(x2)
        lt = None
        for i, j in ((0, 0), (0, 1), (1, 0)):
            t = lax.dot_general(ws[j], xs[i], _NT, preferred_element_type=F32)
            lt = t if lt is None else lt + t
        lt = lt + br_ref[...]
        n_e = lt.shape[0]
        eidx = lax.broadcasted_iota(I32, lt.shape, 0)
        vals, idxs = [], []
        for _ in range(TOP_K_EXPERTS):
            mx = jnp.max(lt, axis=0, keepdims=True)
            ix = jnp.min(jnp.where(lt == mx, eidx, n_e), axis=0, keepdims=True)
            vals.append(mx)
            idxs.append(ix)
            lt = jnp.where(eidx == ix, -jnp.inf, lt)
        tope_ref[:, rs] = jnp.concatenate(idxs, axis=0)
        ex = [jnp.exp(v - vals[0]) for v in vals]
        den = ex[0]
        for e_ in ex[1:]:
            den = den + e_
        gates = jnp.concatenate([e_ / den for e_ in ex] + [jnp.zeros((LANES - TOP_K_EXPERTS, n), F32)], axis=0)
        gate_ref[rs, :] = jnp.transpose(gates)

    all_rows = slice(0, tm)
    x1 = stage_mix(all_rows)
    stage_route(all_rows, x1, stage_mem(x1))


def _mix(x2d, ypool, yatt, w_o, g1, b1, k_mem, v_mem, w_mq, w_mo, g2, b2, w_rt, b_r, B, L):
    T, D = x2d.shape
    tm = min(MIX_ROWS, L)
    nl = L // tm
    M = k_mem.shape[1]
    row = lambda i: (i, 0)
    full = lambda i: (0, 0)
    return pl.pallas_call(
        _mix_kernel,
        grid=(T // tm,),
        in_specs=[
            pl.BlockSpec((tm, D), row),
            pl.BlockSpec((tm, ypool.shape[1]), row),
            pl.BlockSpec((tm, yatt.shape[1]), row),
            pl.BlockSpec(w_o.shape, full),
            pl.BlockSpec(g1.shape, full),
            pl.BlockSpec(b1.shape, full),
            pl.BlockSpec((1, M, D), lambda i: (i // nl, 0, 0)),
            pl.BlockSpec((1, M, D), lambda i: (i // nl, 0, 0)),
            pl.BlockSpec(w_mq.shape, full),
            pl.BlockSpec(w_mo.shape, full),
            pl.BlockSpec(g2.shape, full),
            pl.BlockSpec(b2.shape, full),
            pl.BlockSpec(w_rt.shape, full),
            pl.BlockSpec(b_r.shape, full),
        ],
        out_specs=[
            pl.BlockSpec((tm, D), row),
            pl.BlockSpec((TOP_K_EXPERTS, tm), lambda i: (0, i)),
            pl.BlockSpec((tm, LANES), row),
        ],
        out_shape=[
            jax.ShapeDtypeStruct((T, D), F32),
            jax.ShapeDtypeStruct((TOP_K_EXPERTS, T), I32),
            jax.ShapeDtypeStruct((T, LANES), F32),
        ],
        compiler_params=pltpu.CompilerParams(
            dimension_semantics=("arbitrary",), vmem_limit_bytes=VMEM_LIMIT),
    )(x2d, ypool, yatt, w_o, g1, b1, k_mem, v_mem, w_mq, w_mo, g2, b2, w_rt, b_r)


def _rank_kernel(tope_ref, rank_ref, cnt_ref, carry_ref):
    i = pl.program_id(0)
    tr = tope_ref.shape[1]

    @pl.when(i == 0)
    def _():
        carry_ref[...] = jnp.zeros(carry_ref.shape, F32)

    eidx = lax.broadcasted_iota(I32, (N_EXPERTS, tr), 0)
    onehot = jnp.zeros((N_EXPERTS, tr), F32)
    for k in range(TOP_K_EXPERTS):
        onehot = onehot + jnp.where(eidx == tope_ref[k:k + 1, :], 1.0, 0.0)
    before = jnp.where(lax.broadcasted_iota(I32, (tr, tr), 0) < lax.broadcasted_iota(I32, (tr, tr), 1), 1.0, 0.0)
    excl = jnp.dot(onehot.astype(BF16), before.astype(BF16), preferred_element_type=F32)
    rank_full = excl + carry_ref[:, 0:1]
    rows = []
    for k in range(TOP_K_EXPERTS):
        rows.append(jnp.sum(jnp.where(eidx == tope_ref[k:k + 1, :], rank_full, 0.0), axis=0, keepdims=True))
    rank_ref[...] = jnp.concatenate(rows, axis=0).astype(I32)
    carry_ref[...] = carry_ref[...] + jnp.sum(onehot, axis=1, keepdims=True)
    cnt_ref[...] = carry_ref[...].astype(I32)


def _rank(tope):
    K, T = tope.shape
    tr = min(RANK_TOKENS, T)
    return pl.pallas_call(
        _rank_kernel,
        grid=(T // tr,),
        in_specs=[pl.BlockSpec((K, tr), lambda i: (0, i))],
        out_specs=[pl.BlockSpec((K, tr), lambda i: (0, i)),
                   pl.BlockSpec((N_EXPERTS, LANES), lambda i: (0, 0))],
        out_shape=[jax.ShapeDtypeStruct((K, T), I32), jax.ShapeDtypeStruct((N_EXPERTS, LANES), I32)],
        scratch_shapes=[pltpu.VMEM((N_EXPERTS, LANES), F32)],
        compiler_params=pltpu.CompilerParams(dimension_semantics=("arbitrary",)),
    )(tope)


def _dest_kernel(start_ref, tope_ref, rank_ref, dest_ref):
    tope = tope_ref[...]
    base = jnp.zeros(tope.shape, I32)
    for e in range(N_EXPERTS):
        base = jnp.where(tope == e, start_ref[e], base)
    dest_ref[...] = base + rank_ref[...]


def _dest(pad_start, tope, rank):
    K, T = tope.shape
    tt = min(DEST_TOKENS, T)
    blk = pl.BlockSpec((K, tt), lambda i: (0, i))
    return pl.pallas_call(
        _dest_kernel,
        grid=(T // tt,),
        in_specs=[pl.BlockSpec(memory_space=pltpu.SMEM), blk, blk],
        out_specs=blk,
        out_shape=jax.ShapeDtypeStruct((K, T), I32),
        compiler_params=pltpu.CompilerParams(dimension_semantics=("arbitrary",)),
    )(pad_start, tope, rank)


def _dispatch_kernel(dest_ref, start_ref, cnt_ref, padded_ref, x_ref, xs_hbm, zrow, sem, zsem):
    i = pl.program_id(0)
    td = dest_ref.shape[1]

    def row_copy(j, dst):
        return pltpu.make_async_copy(x_ref.at[pl.ds(j, 1), :], xs_hbm.at[pl.ds(dst, 1), :], sem)

    def zero_copy(dst):
        return pltpu.make_async_copy(zrow, xs_hbm.at[pl.ds(dst, 1), :], zsem)

    @pl.when(i == 0)
    def _():
        zrow[...] = jnp.zeros(zrow.shape, F32)
        for e in range(N_EXPERTS):
            first = start_ref[e] + cnt_ref[e]
            n_pad = padded_ref[e] - cnt_ref[e]

            def zstart(r, c):
                zero_copy(first + r).start()
                return c
            lax.fori_loop(0, n_pad, zstart, 0)

            def zwait(r, c):
                zero_copy(first + r).wait()
                return c
            lax.fori_loop(0, n_pad, zwait, 0)

    def issue(jj, c):
        for u in range(ISSUE_UNROLL):
            j = jj * ISSUE_UNROLL + u
            for k in range(TOP_K_EXPERTS):
                row_copy(j, dest_ref[k, j]).start()
        return c
    lax.fori_loop(0, td // ISSUE_UNROLL, issue, 0)

    def drain(jj, c):
        for _ in range(ISSUE_UNROLL * TOP_K_EXPERTS):
            row_copy(0, 0).wait()
        return c
    lax.fori_loop(0, td // ISSUE_UNROLL, drain, 0)


def _dispatch(dest, pad_start, counts, padded, x2, n_rows):
    K, T = dest.shape
    D = x2.shape[1]
    td = min(DISPATCH_TOKENS, T)
    smem_tok = pl.BlockSpec((K, td), lambda i: (0, i), memory_space=pltpu.SMEM)
    smem_full = pl.BlockSpec(memory_space=pltpu.SMEM)
    return pl.pallas_call(
        _dispatch_kernel,
        grid=(T // td,),
        in_specs=[smem_tok, smem_full, smem_full, smem_full,
                  pl.BlockSpec((td, D), lambda i: (i, 0))],
        out_specs=pl.BlockSpec(memory_space=pl.ANY),
        out_shape=jax.ShapeDtypeStruct((n_rows, D), F32),
        scratch_shapes=[pltpu.VMEM((1, D), F32), pltpu.SemaphoreType.DMA(()), pltpu.SemaphoreType.DMA(())],
        compiler_params=pltpu.CompilerParams(
            dimension_semantics=("arbitrary",), vmem_limit_bytes=VMEM_LIMIT),
    )(dest, pad_start, counts, padded, x2)


def _expert_kernel(bexp_ref, nused_ref, xs_ref, wgu_ref, bgu_ref, wd_ref, bd_ref, y_ref, wgu_bf, wd_bf):
    j = pl.program_id(0)
    d_ff = wd_ref.shape[1]

    @pl.when(j < nused_ref[0])
    def _():
        prev = bexp_ref[jnp.maximum(j - 1, 0)]

        @pl.when(jnp.logical_or(j == 0, bexp_ref[j] != prev))
        def _():
            wgu_bf[...] = wgu_ref[0].astype(BF16)
            wd_bf[...] = wd_ref[0].astype(BF16)

        xb = xs_ref[...].astype(BF16)
        gu = jnp.dot(xb, wgu_bf[...], preferred_element_type=F32) + bgu_ref[0]
        g = jnp.minimum(gu[:, 0:d_ff], SWIGLU_LIMIT)
        u = jnp.clip(gu[:, d_ff:2 * d_ff], -SWIGLU_LIMIT, SWIGLU_LIMIT)
        act = (u + 1.0) * g * (1.0 / (1.0 + jnp.exp(-SWIGLU_ALPHA * g)))
        y_ref[...] = jnp.dot(act.astype(BF16), wd_bf[...], preferred_element_type=F32) + bd_ref[0]


def _experts(block_exp, n_used, xs, w_gate_up, b_gate_up, w_down, b_down):
    P, D = xs.shape
    blk = EXPERT_ROWS
    nblk = P // blk
    E, _, F2 = w_gate_up.shape
    d_ff = w_down.shape[1]
    row = lambda j, be, nu: (jnp.minimum(j, nu[0] - 1), 0)
    exp3 = lambda j, be, nu: (be[j], 0, 0)
    grid_spec = pltpu.PrefetchScalarGridSpec(
        num_scalar_prefetch=2,
        grid=(nblk,),
        in_specs=[
            pl.BlockSpec((blk, D), row),
            pl.BlockSpec((1, D, F2), exp3),
            pl.BlockSpec((1, 1, F2), exp3),
            pl.BlockSpec((1, d_ff, D), exp3),
            pl.BlockSpec((1, 1, D), exp3),
        ],
        out_specs=pl.BlockSpec((blk, D), row),
        scratch_shapes=[pltpu.VMEM((D, F2), BF16), pltpu.VMEM((d_ff, D), BF16)],
    )
    return pl.pallas_call(
        _expert_kernel,
        grid_spec=grid_spec,
        out_shape=jax.ShapeDtypeStruct((P, D), F32),
        compiler_params=pltpu.CompilerParams(
            dimension_semantics=("arbitrary",), vmem_limit_bytes=VMEM_LIMIT),
    )(block_exp, n_used, xs, w_gate_up, b_gate_up.reshape(E, 1, F2), w_down, b_down.reshape(E, 1, D))


def _combine_kernel(dest_ref, dest_next_ref, x2_ref, gate_ref, g3_ref, b3_ref, y_hbm, o_ref, ybuf, sem):
    i = pl.program_id(0)
    n = pl.num_programs(0)
    tc = x2_ref.shape[0]
    slot = i % 2

    def row_copy(src_ref, s, j, k):
        return pltpu.make_async_copy(y_hbm.at[pl.ds(src_ref[k, j], 1), :], ybuf.at[s, k, pl.ds(j, 1), :], sem.at[s])

    def gather(src_ref, s):
        def issue(jj, c):
            for u in range(ISSUE_UNROLL):
                for k in range(TOP_K_EXPERTS):
                    row_copy(src_ref, s, jj * ISSUE_UNROLL + u, k).start()
            return c
        lax.fori_loop(0, tc // ISSUE_UNROLL, issue, 0)

    @pl.when(i == 0)
    def _():
        gather(dest_ref, 0)

    @pl.when(i + 1 < n)
    def _():
        gather(dest_next_ref, 1 - slot)

    def drain(jj, c):
        for _ in range(ISSUE_UNROLL * TOP_K_EXPERTS):
            pltpu.make_async_copy(y_hbm.at[pl.ds(0, 1), :], ybuf.at[slot, 0, pl.ds(0, 1), :], sem.at[slot]).wait()
        return c
    lax.fori_loop(0, tc // ISSUE_UNROLL, drain, 0)

    moe = None
    for k in range(TOP_K_EXPERTS):
        t = ybuf[slot, k] * gate_ref[:, k:k + 1]
        moe = t if moe is None else moe + t
    o_ref[...] = _layer_norm(DN_ALPHA * x2_ref[...] + moe, g3_ref[...], b3_ref[...])


def _combine(dest, x2, gates, g3, b3, y):
    K, T = dest.shape
    D = x2.shape[1]
    tc = min(COMBINE_TOKENS, T)
    n = T // tc
    row = lambda i: (i, 0)
    full = lambda i: (0, 0)
    return pl.pallas_call(
        _combine_kernel,
        grid=(n,),
        in_specs=[pl.BlockSpec((K, tc), lambda i: (0, i), memory_space=pltpu.SMEM),
                  pl.BlockSpec((K, tc), lambda i: (0, jnp.minimum(i + 1, n - 1)), memory_space=pltpu.SMEM),
                  pl.BlockSpec((tc, D), row), pl.BlockSpec((tc, LANES), row),
                  pl.BlockSpec(g3.shape, full), pl.BlockSpec(b3.shape, full),
                  pl.BlockSpec(memory_space=pl.ANY)],
        out_specs=pl.BlockSpec((tc, D), row),
        out_shape=jax.ShapeDtypeStruct((T, D), F32),
        scratch_shapes=[pltpu.VMEM((2, K, tc, D), F32), pltpu.SemaphoreType.DMA((2,))],
        compiler_params=pltpu.CompilerParams(
            dimension_semantics=("arbitrary",), vmem_limit_bytes=VMEM_LIMIT),
    )(dest, dest, x2, gates, g3, b3, y)


def _pad_cols(w, width):
    return jnp.pad(w, ((0, 0), (0, width - w.shape[1])))


def _layer(x, mem, w_in, w_pool, pool_scale, ik_g, ik_b, kv_g, w_uk, w_uv, w_o, ln1_g, ln1_b,
           w_mq, w_mkv, w_mo, ln2_g, ln2_b, w_router, b_router, w_gate_up, b_gate_up, w_down, b_down,
           ln3_g, ln3_b):
    B, L, D = x.shape
    T = B * L
    row = lambda v: v.reshape(1, -1)

    o = 0
    pieces = []
    for width in (POOL_WIDTH, ATT_WIDTH, KV_RANK, IDX_HEADS * IDX_HEAD_DIM, IDX_HEAD_DIM, IDX_HEADS):
        pieces.append(w_in[:, o:o + width])
        o += width
    pieces[4] = _pad_cols(pieces[4], LANES)
    pieces[5] = _pad_cols(pieces[5], LANES)
    w_in_pad = jnp.concatenate(pieces, axis=1).astype(BF16)
    wuv_pad = jnp.zeros((ATT_HEADS, KV_RANK, 2 * ATT_HEAD_DIM), F32)
    for h in range(ATT_HEADS):
        wuv_pad = wuv_pad.at[h, :, (h % 2) * ATT_HEAD_DIM:(h % 2 + 1) * ATT_HEAD_DIM].set(w_uv[h])

    ypool, qlat, ckv, qidx, kidx, widx = _inproj(
        x, w_in_pad, w_pool.astype(BF16), row(pool_scale), row(kv_g), row(ik_g), row(ik_b), w_uk.astype(BF16))
    yatt = _dsa(qidx, widx, qlat, kidx, ckv, wuv_pad.astype(BF16))
    k_mem, v_mem = _memkv(mem, w_mkv.astype(BF16))
    x2, tope, gates = _mix(
        x.reshape(T, D), ypool.reshape(T, -1), yatt.reshape(T, -1), w_o.astype(BF16), row(ln1_g), row(ln1_b),
        k_mem, v_mem, w_mq.astype(BF16), w_mo.astype(BF16), row(ln2_g), row(ln2_b),
        jnp.transpose(w_router), b_router.reshape(-1, 1), B, L)

    rank, cnt = _rank(tope)
    counts = cnt[:, 0]
    blk = EXPERT_ROWS
    padded = (counts + blk - 1) // blk * blk
    pad_end = jnp.cumsum(padded)
    pad_start = pad_end - padded
    n_rows = T * TOP_K_EXPERTS + N_EXPERTS * blk
    nblk = n_rows // blk
    n_used = (pad_end[-1] // blk).astype(I32)
    blk_first = jnp.minimum(jnp.arange(nblk, dtype=I32), n_used - 1) * blk
    n_ended = jnp.sum((pad_end[None, :] <= blk_first[:, None]).astype(I32), axis=1)
    block_exp = jnp.minimum(n_ended, N_EXPERTS - 1).astype(I32)

    pad_start = pad_start.astype(I32)
    dest = _dest(pad_start, tope, rank)
    xs = _dispatch(dest, pad_start, counts, padded.astype(I32), x2, n_rows)
    y = _experts(block_exp, n_used.reshape(1), xs, w_gate_up, b_gate_up, w_down, b_down)
    out = _combine(dest, x2, gates, row(ln3_g), row(ln3_b), y)
    return out.reshape(B, L, D)


def kernel(x, mem, w_in, w_pool, pool_scale, idx_k_norm_g, idx_k_norm_b, kv_norm_g, w_uk, w_uv, w_o, ln1_g, ln1_b, w_mq, w_mkv, w_mo, ln2_g, ln2_b, w_router, b_router, w_gate_up, b_gate_up, w_down, b_down, ln3_g, ln3_b):
    assert w_in.shape[0] == DEPTH
    return _layer(x, mem, w_in[0], w_pool[0], pool_scale[0], idx_k_norm_g[0], idx_k_norm_b[0], kv_norm_g[0],
                  w_uk[0], w_uv[0], w_o[0], ln1_g[0], ln1_b[0], w_mq[0], w_mkv[0], w_mo[0], ln2_g[0], ln2_b[0],
                  w_router[0], b_router[0], w_gate_up[0], b_gate_up[0], w_down[0], b_down[0], ln3_g[0], ln3_b[0])
```

```python
import functools

import jax
import jax.numpy as jnp
from jax import lax
from jax.experimental import pallas as pl
from jax.experimental.pallas import tpu as pltpu

F32 = jnp.float32
BF16 = jnp.bfloat16
I32 = jnp.int32

POOL_WINDOWS = (2, 4, 8, 16)
POOL_GROUP_DIM = 128
POOL_WIDTH = 512
ATT_HEADS = 8
ATT_HEAD_DIM = 64
ATT_WIDTH = 512
KV_RANK = 128
IDX_HEADS = 8
IDX_HEAD_DIM = 64
TOPK_MAX = 256
MEM_HEADS = 4
N_EXPERTS = 32
TOP_K_EXPERTS = 4
SWIGLU_LIMIT = 7.0
SWIGLU_ALPHA = 1.702
LN_EPS = 1e-5
RMS_EPS = 1e-6
DEPTH = 1
DN_ALPHA = (2 * DEPTH) ** 0.25

LANES = 128
MAX_POOL_WINDOW = 16
VMEM_LIMIT = 56 * 1024 * 1024

PROJ_ROWS = 512
DSA_QUERIES = 128
DSA_KEYS = 512
FIELD_BITS = 15
SEARCH_CHECK_BIT = 6
MIX_ROWS = 1024
RANK_TOKENS = 512
DEST_TOKENS = 2048
DISPATCH_TOKENS = 512
ISSUE_UNROLL = 256
EXPERT_ROWS = 256
COMBINE_TOKENS = 256

LOG2_E = 1.4426950408889634
NEG_BIG = -1e30
INT_MIN = -(2 ** 31)
NEG_INF_KEY = INT_MIN + 0x007FFFFF

_NT = (((1,), (1,)), ((), ()))


def _layer_norm(v, g, b):
    mu = jnp.mean(v, axis=-1, keepdims=True)
    d = v - mu
    var = jnp.mean(d * d, axis=-1, keepdims=True)
    return d * lax.rsqrt(var + LN_EPS) * g + b


_C_POOL = 0
_C_Q = 512
_C_CKV = 1024
_C_QIDX = 1152
_C_KIDX = 1664
_C_WIDX = 1792
_IN_PAD = 1920


def _inproj_kernel(x_ref, w_ref, wpool_ref, pscale_ref, kvg_ref, ikg_ref, ikb_ref, wuk_ref,
                   ypool_ref, qlat_ref, ckv_ref, qidx_ref, kidx_ref, widx_ref, ubuf):
    li = pl.program_id(1)
    tm = x_ref.shape[1]
    halo = MAX_POOL_WINDOW
    xb = x_ref[0].astype(BF16)

    u = jnp.dot(xb, w_ref[:, _C_POOL:_C_POOL + POOL_WIDTH], preferred_element_type=F32)

    @pl.when(li == 0)
    def _():
        ubuf[0:halo, :] = jnp.zeros((halo, POOL_WIDTH), F32)

    ubuf[halo:halo + tm, :] = u
    pos = li * tm + lax.broadcasted_iota(I32, (tm, 1), 0)
    for g, w in enumerate(POOL_WINDOWS):
        c0 = g * POOL_GROUP_DIM
        c1 = c0 + POOL_GROUP_DIM
        ug = ubuf[halo:halo + tm, c0:c1]
        s = ug
        for j in range(1, w):
            s = s + ubuf[halo - j:halo - j + tm, c0:c1]
        cnt = jnp.minimum(pos + 1, w).astype(F32)
        d = s / cnt - ug
        y = jnp.dot(d.astype(BF16), wpool_ref[g], preferred_element_type=F32) * pscale_ref[:, c0:c1]
        ypool_ref[0, :, c0:c1] = y.astype(BF16)
    ubuf[0:halo, :] = ubuf[tm:tm + halo, :]

    q = jnp.dot(xb, w_ref[:, _C_Q:_C_Q + ATT_WIDTH], preferred_element_type=F32)
    att_scale = (ATT_HEAD_DIM ** -0.5) * LOG2_E
    for h in range(ATT_HEADS):
        qh = q[:, h * ATT_HEAD_DIM:(h + 1) * ATT_HEAD_DIM].astype(BF16)
        ql = jnp.dot(qh, wuk_ref[h], preferred_element_type=F32) * att_scale
        qlat_ref[0, h] = ql.astype(BF16)

    c = jnp.dot(xb, w_ref[:, _C_CKV:_C_CKV + KV_RANK], preferred_element_type=F32)
    c = c * lax.rsqrt(jnp.mean(c * c, axis=-1, keepdims=True) + RMS_EPS) * kvg_ref[...]
    ckv_ref[0, :, 0:KV_RANK] = c.astype(BF16)
    ckv_ref[0, :, KV_RANK:2 * KV_RANK] = jnp.ones((tm, KV_RANK), BF16)

    qi = jnp.dot(xb, w_ref[:, _C_QIDX:_C_QIDX + IDX_HEADS * IDX_HEAD_DIM], preferred_element_type=F32)
    for h in range(IDX_HEADS):
        qidx_ref[0, h] = qi[:, h * IDX_HEAD_DIM:(h + 1) * IDX_HEAD_DIM].astype(BF16)
    kw = jnp.dot(xb, w_ref[:, _C_KIDX:_C_KIDX + 2 * LANES], preferred_element_type=F32)
    kr = kw[:, 0:IDX_HEAD_DIM]
    kidx_ref[0] = _layer_norm(kr, ikg_ref[...], ikb_ref[...]).astype(BF16)
    widx_ref[0] = kw[:, LANES:2 * LANES]


def _inproj(x, w_in_pad, w_pool, pool_scale, kv_g, ik_g, ik_b, w_uk):
    B, L, D = x.shape
    tm = min(PROJ_ROWS, L)
    nl = L // tm
    full2 = lambda b, l: (0, 0)
    full3 = lambda b, l: (0, 0, 0)
    return pl.pallas_call(
        _inproj_kernel,
        grid=(B, nl),
        in_specs=[
            pl.BlockSpec((1, tm, D), lambda b, l: (b, l, 0)),
            pl.BlockSpec(w_in_pad.shape, full2),
            pl.BlockSpec(w_pool.shape, full3),
            pl.BlockSpec(pool_scale.shape, full2),
            pl.BlockSpec(kv_g.shape, full2),
            pl.BlockSpec(ik_g.shape, full2),
            pl.BlockSpec(ik_b.shape, full2),
            pl.BlockSpec(w_uk.shape, full3),
        ],
        out_specs=[
            pl.BlockSpec((1, tm, POOL_WIDTH), lambda b, l: (b, l, 0)),
            pl.BlockSpec((1, ATT_HEADS, tm, KV_RANK), lambda b, l: (b, 0, l, 0)),
            pl.BlockSpec((1, tm, 2 * KV_RANK), lambda b, l: (b, l, 0)),
            pl.BlockSpec((1, IDX_HEADS, tm, IDX_HEAD_DIM), lambda b, l: (b, 0, l, 0)),
            pl.BlockSpec((1, tm, IDX_HEAD_DIM), lambda b, l: (b, l, 0)),
            pl.BlockSpec((1, tm, LANES), lambda b, l: (b, l, 0)),
        ],
        out_shape=[
            jax.ShapeDtypeStruct((B, L, POOL_WIDTH), BF16),
            jax.ShapeDtypeStruct((B, ATT_HEADS, L, KV_RANK), BF16),
            jax.ShapeDtypeStruct((B, L, 2 * KV_RANK), BF16),
            jax.ShapeDtypeStruct((B, IDX_HEADS, L, IDX_HEAD_DIM), BF16),
            jax.ShapeDtypeStruct((B, L, IDX_HEAD_DIM), BF16),
            jax.ShapeDtypeStruct((B, L, LANES), F32),
        ],
        scratch_shapes=[pltpu.VMEM((MAX_POOL_WINDOW + tm, POOL_WIDTH), F32)],
        compiler_params=pltpu.CompilerParams(
            dimension_semantics=("arbitrary", "arbitrary"), vmem_limit_bytes=VMEM_LIMIT),
    )(x, w_in_pad, w_pool, pool_scale, kv_g, ik_g, ik_b, w_uk)


def _sortable(score):
    bits = pltpu.bitcast(score, I32)
    return bits ^ ((bits >> 31) & 0x7FFFFFFF)


def _dsa_kernel(qidx_ref, widx_ref, qlat_ref, kidx_ref, ckv_ref, wuv_ref, o_ref,
                key_ref, keyt_ref, w_ref, wb_ref, s_ref, s2_ref, p_ref, p2_ref, a_ref, a2_ref, m_ref, acc_ref, sel_ref,
                *, top_k, idx_bits):
    qi = pl.program_id(1)
    H = ATT_HEADS
    qb = qidx_ref.shape[2]
    kc = s_ref.shape[1]
    qstart = qi * qb
    n_ch = (qstart + qb + kc - 1) // kc
    idx_scale = (IDX_HEAD_DIM ** -0.5) * (IDX_HEADS ** -0.5)

    qpos = qstart + lax.broadcasted_iota(I32, (qb, 1), 0)
    qpos_t = qstart + lax.broadcasted_iota(I32, (1, qb), 1)
    lane_pos = lax.broadcasted_iota(I32, (1, kc), 1)
    row_pos = lax.broadcasted_iota(I32, (kc, 1), 0)

    for h in range(H):
        wb_ref[h * qb:(h + 1) * qb, :] = jnp.broadcast_to(widx_ref[0, :, h:h + 1] * idx_scale, (qb, LANES))

    def pipelined(produce, consume, buf_a, buf_b):
        produce(0, buf_a)
        n_pairs = (n_ch - 1) // 2

        def pair_body(i, carry):
            c = 2 * i
            consume(c, buf_a)
            produce(c + 1, buf_b)
            consume(c + 1, buf_b)
            produce(c + 2, buf_a)
            return carry

        lax.fori_loop(0, n_pairs, pair_body, 0)
        last = 2 * n_pairs

        @pl.when(n_ch - last == 1)
        def _():
            consume(last, buf_a)

        @pl.when(n_ch - last == 2)
        def _():
            consume(last, buf_a)
            produce(last + 1, buf_b)
            consume(last + 1, buf_b)

    def head_scores(c, buf):
        off = pl.multiple_of(c * kc, kc)
        kk = kidx_ref[0, pl.ds(off, kc), :]
        qs = qidx_ref[0].reshape(H * qb, IDX_HEAD_DIM)
        buf[...] = lax.dot_general(qs, kk, _NT, preferred_element_type=F32)

    def chunk_keys(c, buf):
        for j in range(kc // LANES):
            cs = slice(j * LANES, (j + 1) * LANES)
            acc = None
            for h in range(H):
                r = slice(h * qb, (h + 1) * qb)
                term = jnp.maximum(buf[r, cs], 0.0) * wb_ref[r, 0:LANES]
                acc = term if acc is None else acc + term
            score = jnp.where(c * kc + j * LANES + lane_pos[:, 0:LANES] <= qpos, acc, -jnp.inf)
            key = _sortable(score)
            key_ref[c, :, cs] = key
            keyt_ref[c, cs, :] = jnp.transpose(key)

    pipelined(head_scores, chunk_keys, s_ref, s2_ref)

    def count(indicator):
        def body(c, cnt):
            m = indicator(keyt_ref[c], c * kc + row_pos)
            parts = [m[j * 8:(j + 1) * 8, :] for j in range(kc // 8)]
            while len(parts) > 1:
                parts = [parts[j] + parts[j + 1] for j in range(0, len(parts), 2)]
            return cnt + parts[0]
        cnt = lax.fori_loop(0, n_ch, body, jnp.zeros((8, qb), I32))
        return jnp.sum(cnt, axis=0, keepdims=True)

    half = kc // 2
    f_max = (1 << FIELD_BITS) - 1
    guards = jnp.int32(-(1 << 31) + (1 << FIELD_BITS))
    units = jnp.int32((1 << 16) + 1)
    n_all = jnp.full((1, qb), n_ch * kc, I32)

    def field1(k):
        return (k >> (32 - FIELD_BITS)) + (1 << (FIELD_BITS - 1))

    def field2(k):
        return (k >> (32 - 2 * FIELD_BITS)) & f_max

    def pack_fields(field):
        def body(c, carry):
            fa = field(keyt_ref[c, 0:half, :])
            fb = field(keyt_ref[c, half:kc, :])
            w_ref[c] = lax.shift_left(fb, 16) | fa | guards
            return carry
        lax.fori_loop(0, n_ch, body, 0)

    def count_fields(cand):
        cc = cand | lax.shift_left(cand, 16)

        def body(c, cnt):
            g = lax.shift_right_logical(w_ref[c] - cc, FIELD_BITS) & units
            parts = [g[j * 8:(j + 1) * 8, :] for j in range(half // 8)]
            while len(parts) > 1:
                parts = [parts[j] + parts[j + 1] for j in range(0, len(parts), 2)]
            return cnt + parts[0]
        s = jnp.sum(lax.fori_loop(0, n_ch, body, jnp.zeros((8, qb), I32)), axis=0, keepdims=True)
        return (s & 0xFFFF) + lax.shift_right_logical(s, 16)

    def search_bits(first, last, t, n_t, n_above, counter):
        def body(i, carry):
            t, n_t = carry
            cand = t + lax.shift_left(jnp.int32(1), first - i)
            n = n_above + counter(cand)
            ok = n >= top_k
            return jnp.where(ok, cand, t), jnp.where(ok, n, n_t)
        return lax.fori_loop(0, first - last, body, (t, n_t))

    zero = jnp.zeros((1, qb), I32)
    pack_fields(field1)
    t1, n_t = search_bits(FIELD_BITS - 1, -1, zero, n_all, zero, count_fields)
    n_above = jnp.where(t1 < f_max, count_fields(jnp.minimum(t1 + 1, f_max)), 0)
    pack_fields(lambda k: jnp.where(field1(k) == t1, field2(k), 0))
    t2, n_t = search_bits(FIELD_BITS - 1, SEARCH_CHECK_BIT - 1, zero, n_t, n_above, count_fields)

    def prefix(t2):
        return lax.shift_left(t1 - (1 << (FIELD_BITS - 1)), 32 - FIELD_BITS) | lax.shift_left(t2, 32 - 2 * FIELD_BITS)

    sel_ref[0:1, :] = prefix(t2)
    sel_ref[2:3, :] = n_t
    sel_ref[3:4, :] = t2

    @pl.when(jnp.max(jnp.where(sel_ref[2:3, :] != top_k, 1, 0)) > 0)
    def _():
        t2, n_t = search_bits(SEARCH_CHECK_BIT - 1, -1, sel_ref[3:4, :], sel_ref[2:3, :], n_above, count_fields)
        t, n_t = search_bits(32 - 2 * FIELD_BITS - 1, -1, prefix(t2), n_t, zero,
                             lambda cand: count(lambda k, _: jnp.where(k >= cand, 1, 0)))
        sel_ref[0:1, :] = t
        sel_ref[2:3, :] = n_t

    thr_t = sel_ref[0:1, :]
    n_ge = sel_ref[2:3, :]

    surplus = jnp.where(thr_t > NEG_INF_KEY, jnp.where(n_ge > top_k, 1, 0), 0)
    sel_ref[1:2, :] = qpos_t

    @pl.when(jnp.max(surplus) > 0)
    def _():
        need = top_k - count(lambda k, _: jnp.where(k > thr_t, 1, 0))

        def tie_body(i, m):
            cand = m + lax.shift_left(jnp.int32(1), idx_bits - 1 - i)
            n_before = count(lambda k, p: jnp.where(k == thr_t, jnp.where(p < cand, 1, 0), 0))
            return jnp.where(n_before < need, cand, m)
        last = lax.fori_loop(0, idx_bits, tie_body, jnp.zeros((1, qb), I32))
        sel_ref[1:2, :] = jnp.where(surplus > 0, jnp.minimum(last, qpos_t), qpos_t)

    thr = jnp.transpose(jnp.broadcast_to(sel_ref[0:1, :], (qb, qb)))[:, 0:1]
    tie_last = jnp.transpose(jnp.broadcast_to(sel_ref[1:2, :], (qb, qb)))[:, 0:1]

    def chunk_bias(c):
        key = key_ref[c]
        kpos = c * kc + lane_pos
        tie_bias = jnp.where(key == thr, jnp.where(kpos <= tie_last, 0.0, NEG_BIG), NEG_BIG)
        return jnp.where(key > thr, 0.0, tie_bias)

    n_lt = kc // LANES

    def head_logits(h, ck):
        return lax.dot_general(qlat_ref[0, h], ck, _NT, preferred_element_type=F32)

    m_ref[...] = jnp.full(m_ref.shape, NEG_BIG, F32)
    acc_ref[...] = jnp.zeros(acc_ref.shape, F32)

    def numerators(c, bufs):
        pbuf, abuf = bufs
        off = pl.multiple_of(c * kc, kc)
        ck = ckv_ref[0, pl.ds(off, kc), 0:KV_RANK]
        bias = chunk_bias(c)
        for h in range(H):
            r = slice(h * qb, (h + 1) * qb)
            lg = head_logits(h, ck)
            z = [lg[:, j * LANES:(j + 1) * LANES] + bias[:, j * LANES:(j + 1) * LANES] for j in range(n_lt)]
            zm = z[0]
            for j in range(1, n_lt):
                zm = jnp.maximum(zm, z[j])
            m_old = m_ref[r, :]
            m_new = jnp.maximum(m_old, jnp.max(zm, axis=1, keepdims=True))
            m_ref[r, :] = m_new
            abuf[r, :] = jnp.exp2(m_old - m_new)
            for j in range(n_lt):
                pbuf[r, j * LANES:(j + 1) * LANES] = jnp.exp2(z[j] - m_new).astype(BF16)

    def accumulate(c, bufs):
        pbuf, abuf = bufs
        off = pl.multiple_of(c * kc, kc)
        ckx = ckv_ref[0, pl.ds(off, kc), :]
        pv = jnp.dot(pbuf[...], ckx, preferred_element_type=F32)
        a = abuf[...]
        acc_ref[:, 0:KV_RANK] = a * acc_ref[:, 0:KV_RANK] + pv[:, 0:KV_RANK]
        acc_ref[:, KV_RANK:2 * KV_RANK] = a * acc_ref[:, KV_RANK:2 * KV_RANK] + pv[:, KV_RANK:2 * KV_RANK]

    pipelined(numerators, accumulate, (p_ref, a_ref), (p2_ref, a2_ref))

    o_lat = (acc_ref[:, 0:KV_RANK] / acc_ref[:, KV_RANK:2 * KV_RANK]).astype(BF16)
    for h in range(0, H, 2):
        t = jnp.dot(o_lat[h * qb:(h + 1) * qb, :], wuv_ref[h], preferred_element_type=F32)
        t = t + jnp.dot(o_lat[(h + 1) * qb:(h + 2) * qb, :], wuv_ref[h + 1], preferred_element_type=F32)
        o_ref[0, :, h * ATT_HEAD_DIM:(h + 2) * ATT_HEAD_DIM] = t.astype(BF16)


def _dsa(qidx, widx, qlat, kidx, ckv, wuv_pad):
    B, H, L, _ = qidx.shape
    qb = min(DSA_QUERIES, L)
    kc = min(DSA_KEYS, L)
    top_k = min(TOPK_MAX, L // 4)
    idx_bits = max(1, (L - 1).bit_length())
    kern = functools.partial(_dsa_kernel, top_k=top_k, idx_bits=idx_bits)
    return pl.pallas_call(
        kern,
        grid=(B, L // qb),
        in_specs=[
            pl.BlockSpec((1, H, qb, IDX_HEAD_DIM), lambda b, q: (b, 0, q, 0)),
            pl.BlockSpec((1, qb, LANES), lambda b, q: (b, q, 0)),
            pl.BlockSpec((1, H, qb, KV_RANK), lambda b, q: (b, 0, q, 0)),
            pl.BlockSpec((1, L, IDX_HEAD_DIM), lambda b, q: (b, 0, 0)),
            pl.BlockSpec((1, L, 2 * KV_RANK), lambda b, q: (b, 0, 0)),
            pl.BlockSpec(wuv_pad.shape, lambda b, q: (0, 0, 0)),
        ],
        out_specs=pl.BlockSpec((1, qb, ATT_WIDTH), lambda b, q: (b, q, 0)),
        out_shape=jax.ShapeDtypeStruct((B, L, ATT_WIDTH), BF16),
        scratch_shapes=[
            pltpu.VMEM((L // kc, qb, kc), I32),
            pltpu.VMEM((L // kc, kc, qb), I32),
            pltpu.VMEM((L // kc, kc // 2, qb), I32),
            pltpu.VMEM((H * qb, LANES), F32),
            pltpu.VMEM((H * qb, kc), F32),
            pltpu.VMEM((H * qb, kc), F32),
            pltpu.VMEM((H * qb, kc), BF16),
            pltpu.VMEM((H * qb, kc), BF16),
            pltpu.VMEM((H * qb, LANES), F32),
            pltpu.VMEM((H * qb, LANES), F32),
            pltpu.VMEM((H * qb, LANES), F32),
            pltpu.VMEM((H * qb, 2 * KV_RANK), F32),
            pltpu.VMEM((8, qb), I32),
        ],
        compiler_params=pltpu.CompilerParams(
            dimension_semantics=("arbitrary", "arbitrary"), vmem_limit_bytes=VMEM_LIMIT),
    )(qidx, widx, qlat, kidx, ckv, wuv_pad)


def _memkv_kernel(mem_ref, w_ref, k_ref, v_ref):
    d = k_ref.shape[2]
    kv = jnp.dot(mem_ref[0].astype(BF16), w_ref[...], preferred_element_type=F32)
    k_ref[0] = kv[:, 0:d].astype(BF16)
    v_ref[0] = kv[:, d:2 * d].astype(BF16)


def _memkv(mem, w_mkv):
    B, M, D = mem.shape
    return pl.pallas_call(
        _memkv_kernel,
        grid=(B,),
        in_specs=[pl.BlockSpec((1, M, D), lambda b: (b, 0, 0)),
                  pl.BlockSpec(w_mkv.shape, lambda b: (0, 0))],
        out_specs=[pl.BlockSpec((1, M, D), lambda b: (b, 0, 0)),
                   pl.BlockSpec((1, M, D), lambda b: (b, 0, 0))],
        out_shape=[jax.ShapeDtypeStruct((B, M, D), BF16), jax.ShapeDtypeStruct((B, M, D), BF16)],
        compiler_params=pltpu.CompilerParams(
            dimension_semantics=("arbitrary",), vmem_limit_bytes=VMEM_LIMIT),
    )(mem, w_mkv)


def _split3(v):
    hi = v.astype(BF16)
    r1 = v - hi.astype(F32)
    mid = r1.astype(BF16)
    lo = (r1 - mid.astype(F32)).astype(BF16)
    return hi, mid, lo


def _mix_kernel(x_ref, yp_ref, ya_ref, wo_ref, g1_ref, b1_ref, km_ref, vm_ref, wq_ref, wmo_ref,
                g2_ref, b2_ref, wr_ref, br_ref, x2_ref, tope_ref, gate_ref):
    tm, d = x_ref.shape
    hd = d // MEM_HEADS
    pw = yp_ref.shape[1]
    ws = _split3(wr_ref[...])

    def stage_mix(rs):
        mix = jnp.dot(yp_ref[rs, :], wo_ref[0:pw, :], preferred_element_type=F32)
        mix = mix + jnp.dot(ya_ref[rs, :], wo_ref[pw:, :], preferred_element_type=F32)
        return _layer_norm(DN_ALPHA * x_ref[rs, :] + mix, g1_ref[...], b1_ref[...])

    def stage_mem(x1):
        q = jnp.dot(x1.astype(BF16), wq_ref[...], preferred_element_type=F32).astype(BF16)
        scale = hd ** -0.5
        att = None
        for h in range(MEM_HEADS):
            c = slice(h * hd, (h + 1) * hd)
            lg = lax.dot_general(q[:, c], km_ref[0, :, c], _NT, preferred_element_type=F32) * scale
            p = jnp.exp(lg - jnp.max(lg, axis=-1, keepdims=True))
            p = p / jnp.sum(p, axis=-1, keepdims=True)
            oh = jnp.dot(p.astype(BF16), vm_ref[0, :, c], preferred_element_type=F32).astype(BF16)
            t = jnp.dot(oh, wmo_ref[c, :], preferred_element_type=F32)
            att = t if att is None else att + t
        return att

    def stage_route(rs, x1, att):
        n = rs.stop - rs.start
        x2 = _layer_norm(DN_ALPHA * x1 + att, g2_ref[...], b2_ref[...])
        x2_ref[rs, :] = x2

        xs = _split3(x2)
        lt = None
        for i, j in ((0, 0), (0, 1), (1, 0)):
            t = lax.dot_general(ws[j], xs[i], _NT, preferred_element_type=F32)
            lt = t if lt is None else lt + t
        lt = lt + br_ref[...]
        n_e = lt.shape[0]
        eidx = lax.broadcasted_iota(I32, lt.shape, 0)
        vals, idxs = [], []
        for _ in range(TOP_K_EXPERTS):
            mx = jnp.max(lt, axis=0, keepdims=True)
            ix = jnp.min(jnp.where(lt == mx, eidx, n_e), axis=0, keepdims=True)
            vals.append(mx)
            idxs.append(ix)
            lt = jnp.where(eidx == ix, -jnp.inf, lt)
        tope_ref[:, rs] = jnp.concatenate(idxs, axis=0)
        ex = [jnp.exp(v - vals[0]) for v in vals]
        den = ex[0]
        for e_ in ex[1:]:
            den = den + e_
        gates = jnp.concatenate([e_ / den for e_ in ex] + [jnp.zeros((LANES - TOP_K_EXPERTS, n), F32)], axis=0)
        gate_ref[rs, :] = jnp.transpose(gates)

    all_rows = slice(0, tm)
    x1 = stage_mix(all_rows)
    stage_route(all_rows, x1, stage_mem(x1))


def _mix(x2d, ypool, yatt, w_o, g1, b1, k_mem, v_mem, w_mq, w_mo, g2, b2, w_rt, b_r, B, L):
    T, D = x2d.shape
    tm = min(MIX_ROWS, L)
    nl = L // tm
    M = k_mem.shape[1]
    row = lambda i: (i, 0)
    full = lambda i: (0, 0)
    return pl.pallas_call(
        _mix_kernel,
        grid=(T // tm,),
        in_specs=[
            pl.BlockSpec((tm, D), row),
            pl.BlockSpec((tm, ypool.shape[1]), row),
            pl.BlockSpec((tm, yatt.shape[1]), row),
            pl.BlockSpec(w_o.shape, full),
            pl.BlockSpec(g1.shape, full),
            pl.BlockSpec(b1.shape, full),
            pl.BlockSpec((1, M, D), lambda i: (i // nl, 0, 0)),
            pl.BlockSpec((1, M, D), lambda i: (i // nl, 0, 0)),
            pl.BlockSpec(w_mq.shape, full),
            pl.BlockSpec(w_mo.shape, full),
            pl.BlockSpec(g2.shape, full),
            pl.BlockSpec(b2.shape, full),
            pl.BlockSpec(w_rt.shape, full),
            pl.BlockSpec(b_r.shape, full),
        ],
        out_specs=[
            pl.BlockSpec((tm, D), row),
            pl.BlockSpec((TOP_K_EXPERTS, tm), lambda i: (0, i)),
            pl.BlockSpec((tm, LANES), row),
        ],
        out_shape=[
            jax.ShapeDtypeStruct((T, D), F32),
            jax.ShapeDtypeStruct((TOP_K_EXPERTS, T), I32),
            jax.ShapeDtypeStruct((T, LANES), F32),
        ],
        compiler_params=pltpu.CompilerParams(
            dimension_semantics=("arbitrary",), vmem_limit_bytes=VMEM_LIMIT),
    )(x2d, ypool, yatt, w_o, g1, b1, k_mem, v_mem, w_mq, w_mo, g2, b2, w_rt, b_r)


def _rank_kernel(tope_ref, rank_ref, cnt_ref, carry_ref):
    i = pl.program_id(0)
    tr = tope_ref.shape[1]

    @pl.when(i == 0)
    def _():
        carry_ref[...] = jnp.zeros(carry_ref.shape, F32)

    eidx = lax.broadcasted_iota(I32, (N_EXPERTS, tr), 0)
    onehot = jnp.zeros((N_EXPERTS, tr), F32)
    for k in range(TOP_K_EXPERTS):
        onehot = onehot + jnp.where(eidx == tope_ref[k:k + 1, :], 1.0, 0.0)
    before = jnp.where(lax.broadcasted_iota(I32, (tr, tr), 0) < lax.broadcasted_iota(I32, (tr, tr), 1), 1.0, 0.0)
    excl = jnp.dot(onehot.astype(BF16), before.astype(BF16), preferred_element_type=F32)
    rank_full = excl + carry_ref[:, 0:1]
    rows = []
    for k in range(TOP_K_EXPERTS):
        rows.append(jnp.sum(jnp.where(eidx == tope_ref[k:k + 1, :], rank_full, 0.0), axis=0, keepdims=True))
    rank_ref[...] = jnp.concatenate(rows, axis=0).astype(I32)
    carry_ref[...] = carry_ref[...] + jnp.sum(onehot, axis=1, keepdims=True)
    cnt_ref[...] = carry_ref[...].astype(I32)


def _rank(tope):
    K, T = tope.shape
    tr = min(RANK_TOKENS, T)
    return pl.pallas_call(
        _rank_kernel,
        grid=(T // tr,),
        in_specs=[pl.BlockSpec((K, tr), lambda i: (0, i))],
        out_specs=[pl.BlockSpec((K, tr), lambda i: (0, i)),
                   pl.BlockSpec((N_EXPERTS, LANES), lambda i: (0, 0))],
        out_shape=[jax.ShapeDtypeStruct((K, T), I32), jax.ShapeDtypeStruct((N_EXPERTS, LANES), I32)],
        scratch_shapes=[pltpu.VMEM((N_EXPERTS, LANES), F32)],
        compiler_params=pltpu.CompilerParams(dimension_semantics=("arbitrary",)),
    )(tope)


def _dest_kernel(start_ref, tope_ref, rank_ref, dest_ref):
    tope = tope_ref[...]
    base = jnp.zeros(tope.shape, I32)
    for e in range(N_EXPERTS):
        base = jnp.where(tope == e, start_ref[e], base)
    dest_ref[...] = base + rank_ref[...]


def _dest(pad_start, tope, rank):
    K, T = tope.shape
    tt = min(DEST_TOKENS, T)
    blk = pl.BlockSpec((K, tt), lambda i: (0, i))
    return pl.pallas_call(
        _dest_kernel,
        grid=(T // tt,),
        in_specs=[pl.BlockSpec(memory_space=pltpu.SMEM), blk, blk],
        out_specs=blk,
        out_shape=jax.ShapeDtypeStruct((K, T), I32),
        compiler_params=pltpu.CompilerParams(dimension_semantics=("arbitrary",)),
    )(pad_start, tope, rank)


def _dispatch_kernel(dest_ref, start_ref, cnt_ref, padded_ref, x_ref, xs_hbm, zrow, sem, zsem):
    i = pl.program_id(0)
    td = dest_ref.shape[1]

    def row_copy(j, dst):
        return pltpu.make_async_copy(x_ref.at[pl.ds(j, 1), :], xs_hbm.at[pl.ds(dst, 1), :], sem)

    def zero_copy(dst):
        return pltpu.make_async_copy(zrow, xs_hbm.at[pl.ds(dst, 1), :], zsem)

    @pl.when(i == 0)
    def _():
        zrow[...] = jnp.zeros(zrow.shape, F32)
        for e in range(N_EXPERTS):
            first = start_ref[e] + cnt_ref[e]
            n_pad = padded_ref[e] - cnt_ref[e]

            def zstart(r, c):
                zero_copy(first + r).start()
                return c
            lax.fori_loop(0, n_pad, zstart, 0)

            def zwait(r, c):
                zero_copy(first + r).wait()
                return c
            lax.fori_loop(0, n_pad, zwait, 0)

    def issue(jj, c):
        for u in range(ISSUE_UNROLL):
            j = jj * ISSUE_UNROLL + u
            for k in range(TOP_K_EXPERTS):
                row_copy(j, dest_ref[k, j]).start()
        return c
    lax.fori_loop(0, td // ISSUE_UNROLL, issue, 0)

    def drain(jj, c):
        for _ in range(ISSUE_UNROLL * TOP_K_EXPERTS):
            row_copy(0, 0).wait()
        return c
    lax.fori_loop(0, td // ISSUE_UNROLL, drain, 0)


def _dispatch(dest, pad_start, counts, padded, x2, n_rows):
    K, T = dest.shape
    D = x2.shape[1]
    td = min(DISPATCH_TOKENS, T)
    smem_tok = pl.BlockSpec((K, td), lambda i: (0, i), memory_space=pltpu.SMEM)
    smem_full = pl.BlockSpec(memory_space=pltpu.SMEM)
    return pl.pallas_call(
        _dispatch_kernel,
        grid=(T // td,),
        in_specs=[smem_tok, smem_full, smem_full, smem_full,
                  pl.BlockSpec((td, D), lambda i: (i, 0))],
        out_specs=pl.BlockSpec(memory_space=pl.ANY),
        out_shape=jax.ShapeDtypeStruct((n_rows, D), F32),
        scratch_shapes=[pltpu.VMEM((1, D), F32), pltpu.SemaphoreType.DMA(()), pltpu.SemaphoreType.DMA(())],
        compiler_params=pltpu.CompilerParams(
            dimension_semantics=("arbitrary",), vmem_limit_bytes=VMEM_LIMIT),
    )(dest, pad_start, counts, padded, x2)


def _expert_kernel(bexp_ref, nused_ref, xs_ref, wgu_ref, bgu_ref, wd_ref, bd_ref, y_ref, wgu_bf, wd_bf):
    j = pl.program_id(0)
    d_ff = wd_ref.shape[1]

    @pl.when(j < nused_ref[0])
    def _():
        prev = bexp_ref[jnp.maximum(j - 1, 0)]

        @pl.when(jnp.logical_or(j == 0, bexp_ref[j] != prev))
        def _():
            wgu_bf[...] = wgu_ref[0].astype(BF16)
            wd_bf[...] = wd_ref[0].astype(BF16)

        xb = xs_ref[...].astype(BF16)
        gu = jnp.dot(xb, wgu_bf[...], preferred_element_type=F32) + bgu_ref[0]
        g = jnp.minimum(gu[:, 0:d_ff], SWIGLU_LIMIT)
        u = jnp.clip(gu[:, d_ff:2 * d_ff], -SWIGLU_LIMIT, SWIGLU_LIMIT)
        act = (u + 1.0) * g * (1.0 / (1.0 + jnp.exp(-SWIGLU_ALPHA * g)))
        y_ref[...] = jnp.dot(act.astype(BF16), wd_bf[...], preferred_element_type=F32) + bd_ref[0]


def _experts(block_exp, n_used, xs, w_gate_up, b_gate_up, w_down, b_down):
    P, D = xs.shape
    blk = EXPERT_ROWS
    nblk = P // blk
    E, _, F2 = w_gate_up.shape
    d_ff = w_down.shape[1]
    row = lambda j, be, nu: (jnp.minimum(j, nu[0] - 1), 0)
    exp3 = lambda j, be, nu: (be[j], 0, 0)
    grid_spec = pltpu.PrefetchScalarGridSpec(
        num_scalar_prefetch=2,
        grid=(nblk,),
        in_specs=[
            pl.BlockSpec((blk, D), row),
            pl.BlockSpec((1, D, F2), exp3),
            pl.BlockSpec((1, 1, F2), exp3),
            pl.BlockSpec((1, d_ff, D), exp3),
            pl.BlockSpec((1, 1, D), exp3),
        ],
        out_specs=pl.BlockSpec((blk, D), row),
        scratch_shapes=[pltpu.VMEM((D, F2), BF16), pltpu.VMEM((d_ff, D), BF16)],
    )
    return pl.pallas_call(
        _expert_kernel,
        grid_spec=grid_spec,
        out_shape=jax.ShapeDtypeStruct((P, D), F32),
        compiler_params=pltpu.CompilerParams(
            dimension_semantics=("arbitrary",), vmem_limit_bytes=VMEM_LIMIT),
    )(block_exp, n_used, xs, w_gate_up, b_gate_up.reshape(E, 1, F2), w_down, b_down.reshape(E, 1, D))


def _combine_kernel(dest_ref, dest_next_ref, x2_ref, gate_ref, g3_ref, b3_ref, y_hbm, o_ref, ybuf, sem):
    i = pl.program_id(0)
    n = pl.num_programs(0)
    tc = x2_ref.shape[0]
    slot = i % 2

    def row_copy(src_ref, s, j, k):
        return pltpu.make_async_copy(y_hbm.at[pl.ds(src_ref[k, j], 1), :], ybuf.at[s, k, pl.ds(j, 1), :], sem.at[s])

    def gather(src_ref, s):
        def issue(jj, c):
            for u in range(ISSUE_UNROLL):
                for k in range(TOP_K_EXPERTS):
                    row_copy(src_ref, s, jj * ISSUE_UNROLL + u, k).start()
            return c
        lax.fori_loop(0, tc // ISSUE_UNROLL, issue, 0)

    @pl.when(i == 0)
    def _():
        gather(dest_ref, 0)

    @pl.when(i + 1 < n)
    def _():
        gather(dest_next_ref, 1 - slot)

    def drain(jj, c):
        for _ in range(ISSUE_UNROLL * TOP_K_EXPERTS):
            pltpu.make_async_copy(y_hbm.at[pl.ds(0, 1), :], ybuf.at[slot, 0, pl.ds(0, 1), :], sem.at[slot]).wait()
        return c
    lax.fori_loop(0, tc // ISSUE_UNROLL, drain, 0)

    moe = None
    for k in range(TOP_K_EXPERTS):
        t = ybuf[slot, k] * gate_ref[:, k:k + 1]
        moe = t if moe is None else moe + t
    o_ref[...] = _layer_norm(DN_ALPHA * x2_ref[...] + moe, g3_ref[...], b3_ref[...])


def _combine(dest, x2, gates, g3, b3, y):
    K, T = dest.shape
    D = x2.shape[1]
    tc = min(COMBINE_TOKENS, T)
    n = T // tc
    row = lambda i: (i, 0)
    full = lambda i: (0, 0)
    return pl.pallas_call(
        _combine_kernel,
        grid=(n,),
        in_specs=[pl.BlockSpec((K, tc), lambda i: (0, i), memory_space=pltpu.SMEM),
                  pl.BlockSpec((K, tc), lambda i: (0, jnp.minimum(i + 1, n - 1)), memory_space=pltpu.SMEM),
                  pl.BlockSpec((tc, D), row), pl.BlockSpec((tc, LANES), row),
                  pl.BlockSpec(g3.shape, full), pl.BlockSpec(b3.shape, full),
                  pl.BlockSpec(memory_space=pl.ANY)],
        out_specs=pl.BlockSpec((tc, D), row),
        out_shape=jax.ShapeDtypeStruct((T, D), F32),
        scratch_shapes=[pltpu.VMEM((2, K, tc, D), F32), pltpu.SemaphoreType.DMA((2,))],
        compiler_params=pltpu.CompilerParams(
            dimension_semantics=("arbitrary",), vmem_limit_bytes=VMEM_LIMIT),
    )(dest, dest, x2, gates, g3, b3, y)


def _pad_cols(w, width):
    return jnp.pad(w, ((0, 0), (0, width - w.shape[1])))


def _layer(x, mem, w_in, w_pool, pool_scale, ik_g, ik_b, kv_g, w_uk, w_uv, w_o, ln1_g, ln1_b,
           w_mq, w_mkv, w_mo, ln2_g, ln2_b, w_router, b_router, w_gate_up, b_gate_up, w_down, b_down,
           ln3_g, ln3_b):
    B, L, D = x.shape
    T = B * L
    row = lambda v: v.reshape(1, -1)

    o = 0
    pieces = []
    for width in (POOL_WIDTH, ATT_WIDTH, KV_RANK, IDX_HEADS * IDX_HEAD_DIM, IDX_HEAD_DIM, IDX_HEADS):
        pieces.append(w_in[:, o:o + width])
        o += width
    pieces[4] = _pad_cols(pieces[4], LANES)
    pieces[5] = _pad_cols(pieces[5], LANES)
    w_in_pad = jnp.concatenate(pieces, axis=1).astype(BF16)
    wuv_pad = jnp.zeros((ATT_HEADS, KV_RANK, 2 * ATT_HEAD_DIM), F32)
    for h in range(ATT_HEADS):
        wuv_pad = wuv_pad.at[h, :, (h % 2) * ATT_HEAD_DIM:(h % 2 + 1) * ATT_HEAD_DIM].set(w_uv[h])

    ypool, qlat, ckv, qidx, kidx, widx = _inproj(
        x, w_in_pad, w_pool.astype(BF16), row(pool_scale), row(kv_g), row(ik_g), row(ik_b), w_uk.astype(BF16))
    yatt = _dsa(qidx, widx, qlat, kidx, ckv, wuv_pad.astype(BF16))
    k_mem, v_mem = _memkv(mem, w_mkv.astype(BF16))
    x2, tope, gates = _mix(
        x.reshape(T, D), ypool.reshape(T, -1), yatt.reshape(T, -1), w_o.astype(BF16), row(ln1_g), row(ln1_b),
        k_mem, v_mem, w_mq.astype(BF16), w_mo.astype(BF16), row(ln2_g), row(ln2_b),
        jnp.transpose(w_router), b_router.reshape(-1, 1), B, L)

    rank, cnt = _rank(tope)
    counts = cnt[:, 0]
    blk = EXPERT_ROWS
    padded = (counts + blk - 1) // blk * blk
    pad_end = jnp.cumsum(padded)
    pad_start = pad_end - padded
    n_rows = T * TOP_K_EXPERTS + N_EXPERTS * blk
    nblk = n_rows // blk
    n_used = (pad_end[-1] // blk).astype(I32)
    blk_first = jnp.minimum(jnp.arange(nblk, dtype=I32), n_used - 1) * blk
    n_ended = jnp.sum((pad_end[None, :] <= blk_first[:, None]).astype(I32), axis=1)
    block_exp = jnp.minimum(n_ended, N_EXPERTS - 1).astype(I32)

    pad_start = pad_start.astype(I32)
    dest = _dest(pad_start, tope, rank)
    xs = _dispatch(dest, pad_start, counts, padded.astype(I32), x2, n_rows)
    y = _experts(block_exp, n_used.reshape(1), xs, w_gate_up, b_gate_up, w_down, b_down)
    out = _combine(dest, x2, gates, row(ln3_g), row(ln3_b), y)
    return out.reshape(B, L, D)


def kernel(x, mem, w_in, w_pool, pool_scale, idx_k_norm_g, idx_k_norm_b, kv_norm_g, w_uk, w_uv, w_o, ln1_g, ln1_b, w_mq, w_mkv, w_mo, ln2_g, ln2_b, w_router, b_router, w_gate_up, b_gate_up, w_down, b_down, ln3_g, ln3_b):
    assert w_in.shape[0] == DEPTH
    return _layer(x, mem, w_in[0], w_pool[0], pool_scale[0], idx_k_norm_g[0], idx_k_norm_b[0], kv_norm_g[0],
                  w_uk[0], w_uv[0], w_o[0], ln1_g[0], ln1_b[0], w_mq[0], w_mkv[0], w_mo[0], ln2_g[0], ln2_b[0],
                  w_router[0], b_router[0], w_gate_up[0], b_gate_up[0], w_down[0], b_down[0], ln3_g[0], ln3_b[0])
```

```python
import functools

import jax
import jax.numpy as jnp
from jax import lax
from jax.experimental import pallas as pl
from jax.experimental.pallas import tpu as pltpu

F32 = jnp.float32
BF16 = jnp.bfloat16
I32 = jnp.int32

POOL_WINDOWS = (2, 4, 8, 16)
POOL_GROUP_DIM = 128
POOL_WIDTH = 512
ATT_HEADS = 8
ATT_HEAD_DIM = 64
ATT_WIDTH = 512
KV_RANK = 128
IDX_HEADS = 8
IDX_HEAD_DIM = 64
TOPK_MAX = 256
MEM_HEADS = 4
N_EXPERTS = 32
TOP_K_EXPERTS = 4
SWIGLU_LIMIT = 7.0
SWIGLU_ALPHA = 1.702
LN_EPS = 1e-5
RMS_EPS = 1e-6
DEPTH = 1
DN_ALPHA = (2 * DEPTH) ** 0.25
MAX_POOL_WINDOW = max(POOL_WINDOWS)

LANES = 128
VMEM_LIMIT = 56 * 1024 * 1024

PROJ_ROWS = 512
DSA_QUERIES = 128
DSA_KEYS = 512
FIELD_BITS = 15
SEARCH_CHECK_BIT = 6
MIX_ROWS = 1024
RANK_TOKENS = 512
DEST_TOKENS = 2048
DISPATCH_TOKENS = 512
ISSUE_UNROLL = 256
EXPERT_ROWS = 256
COMBINE_TOKENS = 256

LOG2_E = 1.4426950408889634
NEG_BIG = -1e30
INT_MIN = -(2 ** 31)
NEG_INF_KEY = INT_MIN + 0x007FFFFF

_NT = (((1,), (1,)), ((), ()))


def _layer_norm(v, g, b):
    mu = jnp.mean(v, axis=-1, keepdims=True)
    d = v - mu
    var = jnp.mean(d * d, axis=-1, keepdims=True)
    return d * lax.rsqrt(var + LN_EPS) * g + b


_C_POOL = 0
_C_Q = 512
_C_CKV = 1024
_C_QIDX = 1152
_C_KIDX = 1664
_C_WIDX = 1792
_IN_PAD = 1920


def _inproj_kernel(x_ref, w_ref, wpool_ref, pscale_ref, kvg_ref, ikg_ref, ikb_ref, wuk_ref,
                   ypool_ref, qlat_ref, ckv_ref, qidx_ref, kidx_ref, widx_ref, ubuf):
    li = pl.program_id(1)
    tm = x_ref.shape[1]
    halo = MAX_POOL_WINDOW
    xb = x_ref[0].astype(BF16)

    u = jnp.dot(xb, w_ref[:, _C_POOL:_C_POOL + POOL_WIDTH], preferred_element_type=F32)

    @pl.when(li == 0)
    def _():
        ubuf[0:halo, :] = jnp.zeros((halo, POOL_WIDTH), F32)

    ubuf[halo:halo + tm, :] = u
    pos = li * tm + lax.broadcasted_iota(I32, (tm, 1), 0)
    for g, w in enumerate(POOL_WINDOWS):
        c0 = g * POOL_GROUP_DIM
        c1 = c0 + POOL_GROUP_DIM
        ug = ubuf[halo:halo + tm, c0:c1]
        s = ug
        for j in range(1, w):
            s = s + ubuf[halo - j:halo - j + tm, c0:c1]
        cnt = jnp.minimum(pos + 1, w).astype(F32)
        d = s / cnt - ug
        y = jnp.dot(d.astype(BF16), wpool_ref[g], preferred_element_type=F32) * pscale_ref[:, c0:c1]
        ypool_ref[0, :, c0:c1] = y.astype(BF16)
    ubuf[0:halo, :] = ubuf[tm:tm + halo, :]

    q = jnp.dot(xb, w_ref[:, _C_Q:_C_Q + ATT_WIDTH], preferred_element_type=F32)
    att_scale = (ATT_HEAD_DIM ** -0.5) * LOG2_E
    for h in range(ATT_HEADS):
        qh = q[:, h * ATT_HEAD_DIM:(h + 1) * ATT_HEAD_DIM].astype(BF16)
        ql = jnp.dot(qh, wuk_ref[h], preferred_element_type=F32) * att_scale
        qlat_ref[0, h] = ql.astype(BF16)

    c = jnp.dot(xb, w_ref[:, _C_CKV:_C_CKV + KV_RANK], preferred_element_type=F32)
    c = c * lax.rsqrt(jnp.mean(c * c, axis=-1, keepdims=True) + RMS_EPS) * kvg_ref[...]
    ckv_ref[0, :, 0:KV_RANK] = c.astype(BF16)
    ckv_ref[0, :, KV_RANK:2 * KV_RANK] = jnp.ones((tm, KV_RANK), BF16)

    qi = jnp.dot(xb, w_ref[:, _C_QIDX:_C_QIDX + IDX_HEADS * IDX_HEAD_DIM], preferred_element_type=F32)
    for h in range(IDX_HEADS):
        qidx_ref[0, h] = qi[:, h * IDX_HEAD_DIM:(h + 1) * IDX_HEAD_DIM].astype(BF16)
    kw = jnp.dot(xb, w_ref[:, _C_KIDX:_C_KIDX + 2 * LANES], preferred_element_type=F32)
    kr = kw[:, 0:IDX_HEAD_DIM]
    kidx_ref[0] = _layer_norm(kr, ikg_ref[...], ikb_ref[...]).astype(BF16)
    widx_ref[0] = kw[:, LANES:2 * LANES]


def _inproj(x, w_in_pad, w_pool, pool_scale, kv_g, ik_g, ik_b, w_uk):
    B, L, D = x.shape
    tm = min(PROJ_ROWS, L)
    nl = L // tm
    full2 = lambda b, l: (0, 0)
    full3 = lambda b, l: (0, 0, 0)
    return pl.pallas_call(
        _inproj_kernel,
        grid=(B, nl),
        in_specs=[
            pl.BlockSpec((1, tm, D), lambda b, l: (b, l, 0)),
            pl.BlockSpec(w_in_pad.shape, full2),
            pl.BlockSpec(w_pool.shape, full3),
            pl.BlockSpec(pool_scale.shape, full2),
            pl.BlockSpec(kv_g.shape, full2),
            pl.BlockSpec(ik_g.shape, full2),
            pl.BlockSpec(ik_b.shape, full2),
            pl.BlockSpec(w_uk.shape, full3),
        ],
        out_specs=[
            pl.BlockSpec((1, tm, POOL_WIDTH), lambda b, l: (b, l, 0)),
            pl.BlockSpec((1, ATT_HEADS, tm, KV_RANK), lambda b, l: (b, 0, l, 0)),
            pl.BlockSpec((1, tm, 2 * KV_RANK), lambda b, l: (b, l, 0)),
            pl.BlockSpec((1, IDX_HEADS, tm, IDX_HEAD_DIM), lambda b, l: (b, 0, l, 0)),
            pl.BlockSpec((1, tm, IDX_HEAD_DIM), lambda b, l: (b, l, 0)),
            pl.BlockSpec((1, tm, LANES), lambda b, l: (b, l, 0)),
        ],
        out_shape=[
            jax.ShapeDtypeStruct((B, L, POOL_WIDTH), BF16),
            jax.ShapeDtypeStruct((B, ATT_HEADS, L, KV_RANK), BF16),
            jax.ShapeDtypeStruct((B, L, 2 * KV_RANK), BF16),
            jax.ShapeDtypeStruct((B, IDX_HEADS, L, IDX_HEAD_DIM), BF16),
            jax.ShapeDtypeStruct((B, L, IDX_HEAD_DIM), BF16),
            jax.ShapeDtypeStruct((B, L, LANES), F32),
        ],
        scratch_shapes=[pltpu.VMEM((MAX_POOL_WINDOW + tm, POOL_WIDTH), F32)],
        compiler_params=pltpu.CompilerParams(
            dimension_semantics=("arbitrary", "arbitrary"), vmem_limit_bytes=VMEM_LIMIT),
    )(x, w_in_pad, w_pool, pool_scale, kv_g, ik_g, ik_b, w_uk)


def _sortable(score):
    bits = pltpu.bitcast(score, I32)
    return bits ^ ((bits >> 31) & 0x7FFFFFFF)


def _dsa_kernel(qidx_ref, widx_ref, qlat_ref, kidx_ref, ckv_ref, wuv_ref, o_ref,
                key_ref, keyt_ref, w_ref, wb_ref, s_ref, s2_ref, p_ref, p2_ref, a_ref, a2_ref, m_ref, acc_ref, sel_ref,
                *, top_k, idx_bits):
    qi = pl.program_id(1)
    H = ATT_HEADS
    qb = qidx_ref.shape[2]
    kc = s_ref.shape[1]
    qstart = qi * qb
    n_ch = (qstart + qb + kc - 1) // kc
    idx_scale = (IDX_HEAD_DIM ** -0.5) * (IDX_HEADS ** -0.5)

    qpos = qstart + lax.broadcasted_iota(I32, (qb, 1), 0)
    qpos_t = qstart + lax.broadcasted_iota(I32, (1, qb), 1)
    lane_pos = lax.broadcasted_iota(I32, (1, kc), 1)
    row_pos = lax.broadcasted_iota(I32, (kc, 1), 0)

    for h in range(H):
        wb_ref[h * qb:(h + 1) * qb, :] = jnp.broadcast_to(widx_ref[0, :, h:h + 1] * idx_scale, (qb, LANES))

    def pipelined(produce, consume, buf_a, buf_b):
        produce(0, buf_a)
        n_pairs = (n_ch - 1) // 2

        def pair_body(i, carry):
            c = 2 * i
            consume(c, buf_a)
            produce(c + 1, buf_b)
            consume(c + 1, buf_b)
            produce(c + 2, buf_a)
            return carry

        lax.fori_loop(0, n_pairs, pair_body, 0)
        last = 2 * n_pairs

        @pl.when(n_ch - last == 1)
        def _():
            consume(last, buf_a)

        @pl.when(n_ch - last == 2)
        def _():
            consume(last, buf_a)
            produce(last + 1, buf_b)
            consume(last + 1, buf_b)

    def head_scores(c, buf):
        off = pl.multiple_of(c * kc, kc)
        kk = kidx_ref[0, pl.ds(off, kc), :]
        qs = qidx_ref[0].reshape(H * qb, IDX_HEAD_DIM)
        buf[...] = lax.dot_general(qs, kk, _NT, preferred_element_type=F32)

    half = kc // 2
    f_max = (1 << FIELD_BITS) - 1
    guards = jnp.int32(-(1 << 31) + (1 << FIELD_BITS))

    def field1(k):
        return (k >> (32 - FIELD_BITS)) + (1 << (FIELD_BITS - 1))

    def field2(k):
        return (k >> (32 - 2 * FIELD_BITS)) & f_max

    def pack(fa, fb):
        return lax.shift_left(fb, 16) | fa | guards

    def chunk_keys(c, buf):
        keyt = []
        for j in range(kc // LANES):
            cs = slice(j * LANES, (j + 1) * LANES)
            acc = None
            for h in range(H):
                r = slice(h * qb, (h + 1) * qb)
                term = jnp.maximum(buf[r, cs], 0.0) * wb_ref[r, :]
                acc = term if acc is None else acc + term
            score = jnp.where(c * kc + j * LANES + lane_pos[:, 0:LANES] <= qpos, acc, -jnp.inf)
            key = _sortable(score)
            key_ref[c, :, cs] = key
            keyt.append(jnp.transpose(key))
            keyt_ref[c, cs, :] = keyt[-1]
        kt = jnp.concatenate(keyt, axis=0)
        w_ref[c] = pack(field1(kt[0:half, :]), field1(kt[half:kc, :]))

    pipelined(head_scores, chunk_keys, s_ref, s2_ref)

    def count(indicator):
        def body(c, cnt):
            m = indicator(keyt_ref[c], c * kc + row_pos)
            parts = [m[j * 8:(j + 1) * 8, :] for j in range(kc // 8)]
            while len(parts) > 1:
                parts = [parts[j] + parts[j + 1] for j in range(0, len(parts), 2)]
            return cnt + parts[0]
        cnt = lax.fori_loop(0, n_ch, body, jnp.zeros((8, qb), I32))
        return jnp.sum(cnt, axis=0, keepdims=True)

    units = jnp.int32((1 << 16) + 1)
    n_all = jnp.full((1, qb), n_ch * kc, I32)

    def pack_fields(field):
        def body(c, carry):
            w_ref[c] = pack(field(keyt_ref[c, 0:half, :]), field(keyt_ref[c, half:kc, :]))
            return carry
        lax.fori_loop(0, n_ch, body, 0)

    def count_fields(cand):
        cc = cand | lax.shift_left(cand, 16)

        def body(c, cnt):
            g = lax.shift_right_logical(w_ref[c] - cc, FIELD_BITS) & units
            parts = [g[j * 8:(j + 1) * 8, :] for j in range(half // 8)]
            while len(parts) > 1:
                parts = [parts[j] + parts[j + 1] for j in range(0, len(parts), 2)]
            return cnt + parts[0]
        s = jnp.sum(lax.fori_loop(0, n_ch, body, jnp.zeros((8, qb), I32)), axis=0, keepdims=True)
        return (s & 0xFFFF) + lax.shift_right_logical(s, 16)

    def search_bits(first, last, t, n_t, n_above, counter):
        def body(i, carry):
            t, n_t = carry
            cand = t + lax.shift_left(jnp.int32(1), first - i)
            n = n_above + counter(cand)
            ok = n >= top_k
            return jnp.where(ok, cand, t), jnp.where(ok, n, n_t)
        return lax.fori_loop(0, first - last, body, (t, n_t))

    zero = jnp.zeros((1, qb), I32)
    t1, n_t = search_bits(FIELD_BITS - 1, -1, zero, n_all, zero, count_fields)
    n_above = jnp.where(t1 < f_max, count_fields(jnp.minimum(t1 + 1, f_max)), 0)
    pack_fields(lambda k: jnp.where(field1(k) == t1, field2(k), 0))
    t2, n_t = search_bits(FIELD_BITS - 1, SEARCH_CHECK_BIT - 1, zero, n_t, n_above, count_fields)

    def prefix(t2):
        return lax.shift_left(t1 - (1 << (FIELD_BITS - 1)), 32 - FIELD_BITS) | lax.shift_left(t2, 32 - 2 * FIELD_BITS)

    sel_ref[0:1, :] = prefix(t2)
    sel_ref[2:3, :] = n_t
    sel_ref[3:4, :] = t2

    @pl.when(jnp.max(jnp.where(sel_ref[2:3, :] != top_k, 1, 0)) > 0)
    def _():
        t2, n_t = search_bits(SEARCH_CHECK_BIT - 1, -1, sel_ref[3:4, :], sel_ref[2:3, :], n_above, count_fields)
        t, n_t = search_bits(32 - 2 * FIELD_BITS - 1, -1, prefix(t2), n_t, zero,
                             lambda cand: count(lambda k, _: jnp.where(k >= cand, 1, 0)))
        sel_ref[0:1, :] = t
        sel_ref[2:3, :] = n_t

    thr_t = sel_ref[0:1, :]
    n_ge = sel_ref[2:3, :]

    surplus = jnp.where(thr_t > NEG_INF_KEY, jnp.where(n_ge > top_k, 1, 0), 0)
    sel_ref[1:2, :] = qpos_t

    @pl.when(jnp.max(surplus) > 0)
    def _():
        need = top_k - count(lambda k, _: jnp.where(k > thr_t, 1, 0))

        def tie_body(i, m):
            cand = m + lax.shift_left(jnp.int32(1), idx_bits - 1 - i)
            n_before = count(lambda k, p: jnp.where(k == thr_t, jnp.where(p < cand, 1, 0), 0))
            return jnp.where(n_before < need, cand, m)
        last = lax.fori_loop(0, idx_bits, tie_body, jnp.zeros((1, qb), I32))
        sel_ref[1:2, :] = jnp.where(surplus > 0, jnp.minimum(last, qpos_t), qpos_t)

    thr = jnp.transpose(jnp.broadcast_to(sel_ref[0:1, :], (qb, qb)))[:, 0:1]
    tie_last = jnp.transpose(jnp.broadcast_to(sel_ref[1:2, :], (qb, qb)))[:, 0:1]

    def chunk_bias(c):
        key = key_ref[c]
        kpos = c * kc + lane_pos
        tie_bias = jnp.where(key == thr, jnp.where(kpos <= tie_last, 0.0, NEG_BIG), NEG_BIG)
        return jnp.where(key > thr, 0.0, tie_bias)

    n_lt = kc // LANES

    def head_logits(h, ck):
        return lax.dot_general(qlat_ref[0, h], ck, _NT, preferred_element_type=F32)

    m_ref[...] = jnp.full(m_ref.shape, NEG_BIG, F32)
    acc_ref[...] = jnp.zeros(acc_ref.shape, F32)

    def numerators(c, bufs):
        pbuf, abuf = bufs
        off = pl.multiple_of(c * kc, kc)
        ck = ckv_ref[0, pl.ds(off, kc), 0:KV_RANK]
        bias = chunk_bias(c)
        for h in range(H):
            r = slice(h * qb, (h + 1) * qb)
            lg = head_logits(h, ck)
            z = [lg[:, j * LANES:(j + 1) * LANES] + bias[:, j * LANES:(j + 1) * LANES] for j in range(n_lt)]
            zm = z[0]
            for j in range(1, n_lt):
                zm = jnp.maximum(zm, z[j])
            m_old = m_ref[r, :]
            m_new = jnp.maximum(m_old, jnp.max(zm, axis=1, keepdims=True))
            m_ref[r, :] = m_new
            abuf[r, :] = jnp.exp2(m_old - m_new)
            for j in range(n_lt):
                pbuf[r, j * LANES:(j + 1) * LANES] = jnp.exp2(z[j] - m_new).astype(BF16)

    def accumulate(c, bufs):
        pbuf, abuf = bufs
        off = pl.multiple_of(c * kc, kc)
        ckx = ckv_ref[0, pl.ds(off, kc), :]
        pv = jnp.dot(pbuf[...], ckx, preferred_element_type=F32)
        a = abuf[...]
        acc_ref[:, 0:KV_RANK] = a * acc_ref[:, 0:KV_RANK] + pv[:, 0:KV_RANK]
        acc_ref[:, KV_RANK:2 * KV_RANK] = a * acc_ref[:, KV_RANK:2 * KV_RANK] + pv[:, KV_RANK:2 * KV_RANK]

    pipelined(numerators, accumulate, (p_ref, a_ref), (p2_ref, a2_ref))

    o_lat = (acc_ref[:, 0:KV_RANK] / acc_ref[:, KV_RANK:2 * KV_RANK]).astype(BF16)
    for h in range(0, H, 2):
        t = jnp.dot(o_lat[h * qb:(h + 1) * qb, :], wuv_ref[h], preferred_element_type=F32)
        t = t + jnp.dot(o_lat[(h + 1) * qb:(h + 2) * qb, :], wuv_ref[h + 1], preferred_element_type=F32)
        o_ref[0, :, h * ATT_HEAD_DIM:(h + 2) * ATT_HEAD_DIM] = t.astype(BF16)


def _dsa(qidx, widx, qlat, kidx, ckv, wuv_pad):
    B, H, L, _ = qidx.shape
    qb = min(DSA_QUERIES, L)
    kc = min(DSA_KEYS, L)
    top_k = min(TOPK_MAX, L // 4)
    idx_bits = max(1, (L - 1).bit_length())
    kern = functools.partial(_dsa_kernel, top_k=top_k, idx_bits=idx_bits)
    return pl.pallas_call(
        kern,
        grid=(B, L // qb),
        in_specs=[
            pl.BlockSpec((1, H, qb, IDX_HEAD_DIM), lambda b, q: (b, 0, q, 0)),
            pl.BlockSpec((1, qb, LANES), lambda b, q: (b, q, 0)),
            pl.BlockSpec((1, H, qb, KV_RANK), lambda b, q: (b, 0, q, 0)),
            pl.BlockSpec((1, L, IDX_HEAD_DIM), lambda b, q: (b, 0, 0)),
            pl.BlockSpec((1, L, 2 * KV_RANK), lambda b, q: (b, 0, 0)),
            pl.BlockSpec(wuv_pad.shape, lambda b, q: (0, 0, 0)),
        ],
        out_specs=pl.BlockSpec((1, qb, ATT_WIDTH), lambda b, q: (b, q, 0)),
        out_shape=jax.ShapeDtypeStruct((B, L, ATT_WIDTH), BF16),
        scratch_shapes=[
            pltpu.VMEM((L // kc, qb, kc), I32),
            pltpu.VMEM((L // kc, kc, qb), I32),
            pltpu.VMEM((L // kc, kc // 2, qb), I32),
            pltpu.VMEM((H * qb, LANES), F32),
            pltpu.VMEM((H * qb, kc), F32),
            pltpu.VMEM((H * qb, kc), F32),
            pltpu.VMEM((H * qb, kc), BF16),
            pltpu.VMEM((H * qb, kc), BF16),
            pltpu.VMEM((H * qb, LANES), F32),
            pltpu.VMEM((H * qb, LANES), F32),
            pltpu.VMEM((H * qb, LANES), F32),
            pltpu.VMEM((H * qb, 2 * KV_RANK), F32),
            pltpu.VMEM((8, qb), I32),
        ],
        compiler_params=pltpu.CompilerParams(
            dimension_semantics=("arbitrary", "arbitrary"), vmem_limit_bytes=VMEM_LIMIT),
    )(qidx, widx, qlat, kidx, ckv, wuv_pad)


def _memkv_kernel(mem_ref, w_ref, k_ref, v_ref):
    d = k_ref.shape[2]
    kv = jnp.dot(mem_ref[0].astype(BF16), w_ref[...], preferred_element_type=F32)
    k_ref[0] = kv[:, 0:d].astype(BF16)
    v_ref[0] = kv[:, d:2 * d].astype(BF16)


def _memkv(mem, w_mkv):
    B, M, D = mem.shape
    return pl.pallas_call(
        _memkv_kernel,
        grid=(B,),
        in_specs=[pl.BlockSpec((1, M, D), lambda b: (b, 0, 0)),
                  pl.BlockSpec(w_mkv.shape, lambda b: (0, 0))],
        out_specs=[pl.BlockSpec((1, M, D), lambda b: (b, 0, 0)),
                   pl.BlockSpec((1, M, D), lambda b: (b, 0, 0))],
        out_shape=[jax.ShapeDtypeStruct((B, M, D), BF16), jax.ShapeDtypeStruct((B, M, D), BF16)],
        compiler_params=pltpu.CompilerParams(
            dimension_semantics=("arbitrary",), vmem_limit_bytes=VMEM_LIMIT),
    )(mem, w_mkv)


def _split3(v):
    hi = v.astype(BF16)
    r1 = v - hi.astype(F32)
    mid = r1.astype(BF16)
    lo = (r1 - mid.astype(F32)).astype(BF16)
    return hi, mid, lo


def _mix_kernel(x_ref, yp_ref, ya_ref, wo_ref, g1_ref, b1_ref, km_ref, vm_ref, wq_ref, wmo_ref,
                g2_ref, b2_ref, wr_ref, br_ref, x2_ref, tope_ref, gate_ref):
    tm, d = x_ref.shape
    hd = d // MEM_HEADS
    pw = yp_ref.shape[1]
    ws = _split3(wr_ref[...])

    def stage_mix(rs):
        mix = jnp.dot(yp_ref[rs, :], wo_ref[0:pw, :], preferred_element_type=F32)
        mix = mix + jnp.dot(ya_ref[rs, :], wo_ref[pw:, :], preferred_element_type=F32)
        return _layer_norm(DN_ALPHA * x_ref[rs, :] + mix, g1_ref[...], b1_ref[...])

    def stage_mem(x1):
        q = jnp.dot(x1.astype(BF16), wq_ref[...], preferred_element_type=F32).astype(BF16)
        scale = hd ** -0.5
        att = None
        for h in range(MEM_HEADS):
            c = slice(h * hd, (h + 1) * hd)
            lg = lax.dot_general(q[:, c], km_ref[0, :, c], _NT, preferred_element_type=F32) * scale
            p = jnp.exp(lg - jnp.max(lg, axis=-1, keepdims=True))
            p = p / jnp.sum(p, axis=-1, keepdims=True)
            oh = jnp.dot(p.astype(BF16), vm_ref[0, :, c], preferred_element_type=F32).astype(BF16)
            t = jnp.dot(oh, wmo_ref[c, :], preferred_element_type=F32)
            att = t if att is None else att + t
        return att

    def stage_route(rs, x1, att):
        n = rs.stop - rs.start
        x2 = _layer_norm(DN_ALPHA * x1 + att, g2_ref[...], b2_ref[...])
        x2_ref[rs, :] = x2

        xs = _split3(x2)
        lt = None
        for i, j in ((0, 0), (0, 1), (1, 0)):
            t = lax.dot_general(ws[j], xs[i], _NT, preferred_element_type=F32)
            lt = t if lt is None else lt + t
        lt = lt + br_ref[...]
        n_e = lt.shape[0]
        eidx = lax.broadcasted_iota(I32, lt.shape, 0)
        vals, idxs = [], []
        for _ in range(TOP_K_EXPERTS):
            mx = jnp.max(lt, axis=0, keepdims=True)
            ix = jnp.min(jnp.where(lt == mx, eidx, n_e), axis=0, keepdims=True)
            vals.append(mx)
            idxs.append(ix)
            lt = jnp.where(eidx == ix, -jnp.inf, lt)
        tope_ref[:, rs] = jnp.concatenate(idxs, axis=0)
        ex = [jnp.exp(v - vals[0]) for v in vals]
        den = ex[0]
        for e_ in ex[1:]:
            den = den + e_
        gates = jnp.concatenate([e_ / den for e_ in ex] + [jnp.zeros((LANES - TOP_K_EXPERTS, n), F32)], axis=0)
        gate_ref[rs, :] = jnp.transpose(gates)

    all_rows = slice(0, tm)
    x1 = stage_mix(all_rows)
    stage_route(all_rows, x1, stage_mem(x1))


def _mix(x2d, ypool, yatt, w_o, g1, b1, k_mem, v_mem, w_mq, w_mo, g2, b2, w_rt, b_r, B, L):
    T, D = x2d.shape
    tm = min(MIX_ROWS, L)
    nl = L // tm
    M = k_mem.shape[1]
    row = lambda i: (i, 0)
    full = lambda i: (0, 0)
    return pl.pallas_call(
        _mix_kernel,
        grid=(T // tm,),
        in_specs=[
            pl.BlockSpec((tm, D), row),
            pl.BlockSpec((tm, ypool.shape[1]), row),
            pl.BlockSpec((tm, yatt.shape[1]), row),
            pl.BlockSpec(w_o.shape, full),
            pl.BlockSpec(g1.shape, full),
            pl.BlockSpec(b1.shape, full),
            pl.BlockSpec((1, M, D), lambda i: (i // nl, 0, 0)),
            pl.BlockSpec((1, M, D), lambda i: (i // nl, 0, 0)),
            pl.BlockSpec(w_mq.shape, full),
            pl.BlockSpec(w_mo.shape, full),
            pl.BlockSpec(g2.shape, full),
            pl.BlockSpec(b2.shape, full),
            pl.BlockSpec(w_rt.shape, full),
            pl.BlockSpec(b_r.shape, full),
        ],
        out_specs=[
            pl.BlockSpec((tm, D), row),
            pl.BlockSpec((TOP_K_EXPERTS, tm), lambda i: (0, i)),
            pl.BlockSpec((tm, LANES), row),
        ],
        out_shape=[
            jax.ShapeDtypeStruct((T, D), F32),
            jax.ShapeDtypeStruct((TOP_K_EXPERTS, T), I32),
            jax.ShapeDtypeStruct((T, LANES), F32),
        ],
        compiler_params=pltpu.CompilerParams(
            dimension_semantics=("arbitrary",), vmem_limit_bytes=VMEM_LIMIT),
    )(x2d, ypool, yatt, w_o, g1, b1, k_mem, v_mem, w_mq, w_mo, g2, b2, w_rt, b_r)


def _rank_kernel(tope_ref, rank_ref, cnt_ref, carry_ref):
    i = pl.program_id(0)
    tr = tope_ref.shape[1]

    @pl.when(i == 0)
    def _():
        carry_ref[...] = jnp.zeros(carry_ref.shape, F32)

    eidx = lax.broadcasted_iota(I32, (N_EXPERTS, tr), 0)
    onehot = jnp.zeros((N_EXPERTS, tr), F32)
    for k in range(TOP_K_EXPERTS):
        onehot = onehot + jnp.where(eidx == tope_ref[k:k + 1, :], 1.0, 0.0)
    before = jnp.where(lax.broadcasted_iota(I32, (tr, tr), 0) < lax.broadcasted_iota(I32, (tr, tr), 1), 1.0, 0.0)
    excl = jnp.dot(onehot.astype(BF16), before.astype(BF16), preferred_element_type=F32)
    rank_full = excl + carry_ref[:, 0:1]
    rows = []
    for k in range(TOP_K_EXPERTS):
        rows.append(jnp.sum(jnp.where(eidx == tope_ref[k:k + 1, :], rank_full, 0.0), axis=0, keepdims=True))
    rank_ref[...] = jnp.concatenate(rows, axis=0).astype(I32)
    carry_ref[...] = carry_ref[...] + jnp.sum(onehot, axis=1, keepdims=True)
    cnt_ref[...] = carry_ref[...].astype(I32)


def _rank(tope):
    K, T = tope.shape
    tr = min(RANK_TOKENS, T)
    return pl.pallas_call(
        _rank_kernel,
        grid=(T // tr,),
        in_specs=[pl.BlockSpec((K, tr), lambda i: (0, i))],
        out_specs=[pl.BlockSpec((K, tr), lambda i: (0, i)),
                   pl.BlockSpec((N_EXPERTS, LANES), lambda i: (0, 0))],
        out_shape=[jax.ShapeDtypeStruct((K, T), I32), jax.ShapeDtypeStruct((N_EXPERTS, LANES), I32)],
        scratch_shapes=[pltpu.VMEM((N_EXPERTS, LANES), F32)],
        compiler_params=pltpu.CompilerParams(dimension_semantics=("arbitrary",)),
    )(tope)


def _dest_kernel(start_ref, tope_ref, rank_ref, dest_ref):
    tope = tope_ref[...]
    base = jnp.zeros(tope.shape, I32)
    for e in range(N_EXPERTS):
        base = jnp.where(tope == e, start_ref[e], base)
    dest_ref[...] = base + rank_ref[...]


def _dest(pad_start, tope, rank):
    K, T = tope.shape
    tt = min(DEST_TOKENS, T)
    blk = pl.BlockSpec((K, tt), lambda i: (0, i))
    return pl.pallas_call(
        _dest_kernel,
        grid=(T // tt,),
        in_specs=[pl.BlockSpec(memory_space=pltpu.SMEM), blk, blk],
        out_specs=blk,
        out_shape=jax.ShapeDtypeStruct((K, T), I32),
        compiler_params=pltpu.CompilerParams(dimension_semantics=("arbitrary",)),
    )(pad_start, tope, rank)


def _dispatch_kernel(dest_ref, start_ref, cnt_ref, padded_ref, x_ref, xs_hbm, zrow, sem, zsem):
    i = pl.program_id(0)
    td = dest_ref.shape[1]

    def row_copy(j, dst):
        return pltpu.make_async_copy(x_ref.at[pl.ds(j, 1), :], xs_hbm.at[pl.ds(dst, 1), :], sem)

    def zero_copy(dst):
        return pltpu.make_async_copy(zrow, xs_hbm.at[pl.ds(dst, 1), :], zsem)

    @pl.when(i == 0)
    def _():
        zrow[...] = jnp.zeros(zrow.shape, F32)
        for e in range(N_EXPERTS):
            first = start_ref[e] + cnt_ref[e]
            n_pad = padded_ref[e] - cnt_ref[e]

            def zstart(r, c):
                zero_copy(first + r).start()
                return c
            lax.fori_loop(0, n_pad, zstart, 0)

            def zwait(r, c):
                zero_copy(first + r).wait()
                return c
            lax.fori_loop(0, n_pad, zwait, 0)

    def issue(jj, c):
        for u in range(ISSUE_UNROLL):
            j = jj * ISSUE_UNROLL + u
            for k in range(TOP_K_EXPERTS):
                row_copy(j, dest_ref[k, j]).start()
        return c
    lax.fori_loop(0, td // ISSUE_UNROLL, issue, 0)

    def drain(jj, c):
        for _ in range(ISSUE_UNROLL * TOP_K_EXPERTS):
            row_copy(0, 0).wait()
        return c
    lax.fori_loop(0, td // ISSUE_UNROLL, drain, 0)


def _dispatch(dest, pad_start, counts, padded, x2, n_rows):
    K, T = dest.shape
    D = x2.shape[1]
    td = min(DISPATCH_TOKENS, T)
    smem_tok = pl.BlockSpec((K, td), lambda i: (0, i), memory_space=pltpu.SMEM)
    smem_full = pl.BlockSpec(memory_space=pltpu.SMEM)
    return pl.pallas_call(
        _dispatch_kernel,
        grid=(T // td,),
        in_specs=[smem_tok, smem_full, smem_full, smem_full,
                  pl.BlockSpec((td, D), lambda i: (i, 0))],
        out_specs=pl.BlockSpec(memory_space=pl.ANY),
        out_shape=jax.ShapeDtypeStruct((n_rows, D), F32),
        scratch_shapes=[pltpu.VMEM((1, D), F32), pltpu.SemaphoreType.DMA(()), pltpu.SemaphoreType.DMA(())],
        compiler_params=pltpu.CompilerParams(
            dimension_semantics=("arbitrary",), vmem_limit_bytes=VMEM_LIMIT),
    )(dest, pad_start, counts, padded, x2)


def _expert_kernel(bexp_ref, nused_ref, xs_ref, wgu_ref, bgu_ref, wd_ref, bd_ref, y_ref, wgu_bf, wd_bf):
    j = pl.program_id(0)
    d_ff = wd_ref.shape[1]

    @pl.when(j < nused_ref[0])
    def _():
        prev = bexp_ref[jnp.maximum(j - 1, 0)]

        @pl.when(jnp.logical_or(j == 0, bexp_ref[j] != prev))
        def _():
            wgu_bf[...] = wgu_ref[0].astype(BF16)
            wd_bf[...] = wd_ref[0].astype(BF16)

        xb = xs_ref[...].astype(BF16)
        gu = jnp.dot(xb, wgu_bf[...], preferred_element_type=F32) + bgu_ref[0]
        g = jnp.minimum(gu[:, 0:d_ff], SWIGLU_LIMIT)
        u = jnp.clip(gu[:, d_ff:2 * d_ff], -SWIGLU_LIMIT, SWIGLU_LIMIT)
        act = (u + 1.0) * g * (1.0 / (1.0 + jnp.exp(-SWIGLU_ALPHA * g)))
        y_ref[...] = jnp.dot(act.astype(BF16), wd_bf[...], preferred_element_type=F32) + bd_ref[0]


def _experts(block_exp, n_used, xs, w_gate_up, b_gate_up, w_down, b_down):
    P, D = xs.shape
    blk = EXPERT_ROWS
    nblk = P // blk
    E, _, F2 = w_gate_up.shape
    d_ff = w_down.shape[1]
    row = lambda j, be, nu: (jnp.minimum(j, nu[0] - 1), 0)
    exp3 = lambda j, be, nu: (be[j], 0, 0)
    grid_spec = pltpu.PrefetchScalarGridSpec(
        num_scalar_prefetch=2,
        grid=(nblk,),
        in_specs=[
            pl.BlockSpec((blk, D), row),
            pl.BlockSpec((1, D, F2), exp3),
            pl.BlockSpec((1, 1, F2), exp3),
            pl.BlockSpec((1, d_ff, D), exp3),
            pl.BlockSpec((1, 1, D), exp3),
        ],
        out_specs=pl.BlockSpec((blk, D), row),
        scratch_shapes=[pltpu.VMEM((D, F2), BF16), pltpu.VMEM((d_ff, D), BF16)],
    )
    return pl.pallas_call(
        _expert_kernel,
        grid_spec=grid_spec,
        out_shape=jax.ShapeDtypeStruct((P, D), F32),
        compiler_params=pltpu.CompilerParams(
            dimension_semantics=("arbitrary",), vmem_limit_bytes=VMEM_LIMIT),
    )(block_exp, n_used, xs, w_gate_up, b_gate_up.reshape(E, 1, F2), w_down, b_down.reshape(E, 1, D))


def _combine_kernel(dest_ref, dest_next_ref, x2_ref, gate_ref, g3_ref, b3_ref, y_hbm, o_ref, ybuf, sem):
    i = pl.program_id(0)
    n = pl.num_programs(0)
    tc = x2_ref.shape[0]
    slot = i % 2

    def row_copy(src_ref, s, j, k):
        return pltpu.make_async_copy(y_hbm.at[pl.ds(src_ref[k, j], 1), :], ybuf.at[s, k, pl.ds(j, 1), :], sem.at[s])

    def gather(src_ref, s):
        def issue(jj, c):
            for u in range(ISSUE_UNROLL):
                for k in range(TOP_K_EXPERTS):
                    row_copy(src_ref, s, jj * ISSUE_UNROLL + u, k).start()
            return c
        lax.fori_loop(0, tc // ISSUE_UNROLL, issue, 0)

    @pl.when(i == 0)
    def _():
        gather(dest_ref, 0)

    @pl.when(i + 1 < n)
    def _():
        gather(dest_next_ref, 1 - slot)

    def drain(jj, c):
        for _ in range(ISSUE_UNROLL * TOP_K_EXPERTS):
            pltpu.make_async_copy(y_hbm.at[pl.ds(0, 1), :], ybuf.at[slot, 0, pl.ds(0, 1), :], sem.at[slot]).wait()
        return c
    lax.fori_loop(0, tc // ISSUE_UNROLL, drain, 0)

    moe = None
    for k in range(TOP_K_EXPERTS):
        t = ybuf[slot, k] * gate_ref[:, k:k + 1]
        moe = t if moe is None else moe + t
    o_ref[...] = _layer_norm(DN_ALPHA * x2_ref[...] + moe, g3_ref[...], b3_ref[...])


def _combine(dest, x2, gates, g3, b3, y):
    K, T = dest.shape
    D = x2.shape[1]
    tc = min(COMBINE_TOKENS, T)
    n = T // tc
    row = lambda i: (i, 0)
    full = lambda i: (0, 0)
    return pl.pallas_call(
        _combine_kernel,
        grid=(n,),
        in_specs=[pl.BlockSpec((K, tc), lambda i: (0, i), memory_space=pltpu.SMEM),
                  pl.BlockSpec((K, tc), lambda i: (0, jnp.minimum(i + 1, n - 1)), memory_space=pltpu.SMEM),
                  pl.BlockSpec((tc, D), row), pl.BlockSpec((tc, LANES), row),
                  pl.BlockSpec(g3.shape, full), pl.BlockSpec(b3.shape, full),
                  pl.BlockSpec(memory_space=pl.ANY)],
        out_specs=pl.BlockSpec((tc, D), row),
        out_shape=jax.ShapeDtypeStruct((T, D), F32),
        scratch_shapes=[pltpu.VMEM((2, K, tc, D), F32), pltpu.SemaphoreType.DMA((2,))],
        compiler_params=pltpu.CompilerParams(
            dimension_semantics=("arbitrary",), vmem_limit_bytes=VMEM_LIMIT),
    )(dest, dest, x2, gates, g3, b3, y)


def _pad_cols(w, width):
    return jnp.pad(w, ((0, 0), (0, width - w.shape[1])))


def _layer(x, mem, w_in, w_pool, pool_scale, ik_g, ik_b, kv_g, w_uk, w_uv, w_o, ln1_g, ln1_b,
           w_mq, w_mkv, w_mo, ln2_g, ln2_b, w_router, b_router, w_gate_up, b_gate_up, w_down, b_down,
           ln3_g, ln3_b):
    B, L, D = x.shape
    T = B * L
    row = lambda v: v.reshape(1, -1)

    o = 0
    pieces = []
    for width in (POOL_WIDTH, ATT_WIDTH, KV_RANK, IDX_HEADS * IDX_HEAD_DIM, IDX_HEAD_DIM, IDX_HEADS):
        pieces.append(w_in[:, o:o + width])
        o += width
    pieces[4] = _pad_cols(pieces[4], LANES)
    pieces[5] = _pad_cols(pieces[5], LANES)
    w_in_pad = jnp.concatenate(pieces, axis=1).astype(BF16)
    wuv_pad = jnp.zeros((ATT_HEADS, KV_RANK, 2 * ATT_HEAD_DIM), F32)
    for h in range(ATT_HEADS):
        wuv_pad = wuv_pad.at[h, :, (h % 2) * ATT_HEAD_DIM:(h % 2 + 1) * ATT_HEAD_DIM].set(w_uv[h])

    ypool, qlat, ckv, qidx, kidx, widx = _inproj(
        x, w_in_pad, w_pool.astype(BF16), row(pool_scale), row(kv_g), row(ik_g), row(ik_b), w_uk.astype(BF16))
    yatt = _dsa(qidx, widx, qlat, kidx, ckv, wuv_pad.astype(BF16))
    k_mem, v_mem = _memkv(mem, w_mkv.astype(BF16))
    x2, tope, gates = _mix(
        x.reshape(T, D), ypool.reshape(T, -1), yatt.reshape(T, -1), w_o.astype(BF16), row(ln1_g), row(ln1_b),
        k_mem, v_mem, w_mq.astype(BF16), w_mo.astype(BF16), row(ln2_g), row(ln2_b),
        jnp.transpose(w_router), b_router.reshape(-1, 1), B, L)

    rank, cnt = _rank(tope)
    counts = cnt[:, 0]
    blk = EXPERT_ROWS
    padded = (counts + blk - 1) // blk * blk
    pad_end = jnp.cumsum(padded)
    pad_start = pad_end - padded
    n_rows = T * TOP_K_EXPERTS + N_EXPERTS * blk
    nblk = n_rows // blk
    n_used = (pad_end[-1] // blk).astype(I32)
    blk_first = jnp.minimum(jnp.arange(nblk, dtype=I32), n_used - 1) * blk
    n_ended = jnp.sum((pad_end[None, :] <= blk_first[:, None]).astype(I32), axis=1)
    block_exp = jnp.minimum(n_ended, N_EXPERTS - 1).astype(I32)

    pad_start = pad_start.astype(I32)
    dest = _dest(pad_start, tope, rank)
    xs = _dispatch(dest, pad_start, counts, padded.astype(I32), x2, n_rows)
    y = _experts(block_exp, n_used.reshape(1), xs, w_gate_up, b_gate_up, w_down, b_down)
    out = _combine(dest, x2, gates, row(ln3_g), row(ln3_b), y)
    return out.reshape(B, L, D)


def kernel(x, mem, w_in, w_pool, pool_scale, idx_k_norm_g, idx_k_norm_b, kv_norm_g, w_uk, w_uv, w_o, ln1_g, ln1_b, w_mq, w_mkv, w_mo, ln2_g, ln2_b, w_router, b_router, w_gate_up, b_gate_up, w_down, b_down, ln3_g, ln3_b):
    assert w_in.shape[0] == DEPTH
    return _layer(x, mem, w_in[0], w_pool[0], pool_scale[0], idx_k_norm_g[0], idx_k_norm_b[0], kv_norm_g[0],
                  w_uk[0], w_uv[0], w_o[0], ln1_g[0], ln1_b[0], w_mq[0], w_mkv[0], w_mo[0], ln2_g[0], ln2_b[0],
                  w_router[0], b_router[0], w_gate_up[0], b_gate_up[0], w_down[0], b_down[0], ln3_g[0], ln3_b[0])
```

```python
import functools

import jax
import jax.numpy as jnp
from jax import lax
from jax.experimental import pallas as pl
from jax.experimental.pallas import tpu as pltpu

F32 = jnp.float32
BF16 = jnp.bfloat16
I32 = jnp.int32

POOL_WINDOWS = (2, 4, 8, 16)
POOL_GROUP_DIM = 128
POOL_WIDTH = 512
ATT_HEADS = 8
ATT_HEAD_DIM = 64
ATT_WIDTH = 512
KV_RANK = 128
IDX_HEADS = 8
IDX_HEAD_DIM = 64
TOPK_MAX = 256
MEM_HEADS = 4
N_EXPERTS = 32
TOP_K_EXPERTS = 4
SWIGLU_LIMIT = 7.0
SWIGLU_ALPHA = 1.702
LN_EPS = 1e-5
RMS_EPS = 1e-6
DEPTH = 1
DN_ALPHA = (2 * DEPTH) ** 0.25
MAX_POOL_WINDOW = max(POOL_WINDOWS)

LANES = 128
VMEM_LIMIT = 56 * 1024 * 1024

PROJ_ROWS = 512
DSA_QUERIES = 128
DSA_KEYS = 512
FIELD_BITS = 15
SEARCH_CHECK_BIT = 6
MIX_ROWS = 1024
RANK_TOKENS = 512
DEST_TOKENS = 2048
DISPATCH_TOKENS = 512
ISSUE_UNROLL = 256
EXPERT_ROWS = 256
COMBINE_TOKENS = 256

LOG2_E = 1.4426950408889634
NEG_BIG = -1e30
INT_MIN = -(2 ** 31)
NEG_INF_KEY = INT_MIN + 0x007FFFFF

_NT = (((1,), (1,)), ((), ()))


def _layer_norm(v, g, b):
    mu = jnp.mean(v, axis=-1, keepdims=True)
    d = v - mu
    var = jnp.mean(d * d, axis=-1, keepdims=True)
    return d * lax.rsqrt(var + LN_EPS) * g + b


_C_POOL = 0
_C_Q = 512
_C_CKV = 1024
_C_QIDX = 1152
_C_KIDX = 1664
_C_WIDX = 1792
_IN_PAD = 1920


def _inproj_kernel(x_ref, w_ref, wpool_ref, pscale_ref, kvg_ref, ikg_ref, ikb_ref, wuk_ref,
                   ypool_ref, qlat_ref, ckv_ref, qidx_ref, kidx_ref, widx_ref, ubuf):
    li = pl.program_id(1)
    tm = x_ref.shape[1]
    halo = MAX_POOL_WINDOW
    xb = x_ref[0].astype(BF16)

    u = jnp.dot(xb, w_ref[:, _C_POOL:_C_POOL + POOL_WIDTH], preferred_element_type=F32)

    @pl.when(li == 0)
    def _():
        ubuf[0:halo, :] = jnp.zeros((halo, POOL_WIDTH), F32)

    ubuf[halo:halo + tm, :] = u
    pos = li * tm + lax.broadcasted_iota(I32, (tm, 1), 0)
    for g, w in enumerate(POOL_WINDOWS):
        c0 = g * POOL_GROUP_DIM
        c1 = c0 + POOL_GROUP_DIM
        ug = ubuf[halo:halo + tm, c0:c1]
        s = ug
        for j in range(1, w):
            s = s + ubuf[halo - j:halo - j + tm, c0:c1]
        cnt = jnp.minimum(pos + 1, w).astype(F32)
        d = s / cnt - ug
        y = jnp.dot(d.astype(BF16), wpool_ref[g], preferred_element_type=F32) * pscale_ref[:, c0:c1]
        ypool_ref[0, :, c0:c1] = y.astype(BF16)
    ubuf[0:halo, :] = ubuf[tm:tm + halo, :]

    q = jnp.dot(xb, w_ref[:, _C_Q:_C_Q + ATT_WIDTH], preferred_element_type=F32)
    att_scale = (ATT_HEAD_DIM ** -0.5) * LOG2_E
    for h in range(ATT_HEADS):
        qh = q[:, h * ATT_HEAD_DIM:(h + 1) * ATT_HEAD_DIM].astype(BF16)
        ql = jnp.dot(qh, wuk_ref[h], preferred_element_type=F32) * att_scale
        qlat_ref[0, h] = ql.astype(BF16)

    c = jnp.dot(xb, w_ref[:, _C_CKV:_C_CKV + KV_RANK], preferred_element_type=F32)
    c = c * lax.rsqrt(jnp.mean(c * c, axis=-1, keepdims=True) + RMS_EPS) * kvg_ref[...]
    ckv_ref[0, :, 0:KV_RANK] = c.astype(BF16)
    ckv_ref[0, :, KV_RANK:2 * KV_RANK] = jnp.ones((tm, KV_RANK), BF16)

    qi = jnp.dot(xb, w_ref[:, _C_QIDX:_C_QIDX + IDX_HEADS * IDX_HEAD_DIM], preferred_element_type=F32)
    for h in range(IDX_HEADS):
        qidx_ref[0, h] = qi[:, h * IDX_HEAD_DIM:(h + 1) * IDX_HEAD_DIM].astype(BF16)
    kw = jnp.dot(xb, w_ref[:, _C_KIDX:_C_KIDX + 2 * LANES], preferred_element_type=F32)
    kr = kw[:, 0:IDX_HEAD_DIM]
    kidx_ref[0] = _layer_norm(kr, ikg_ref[...], ikb_ref[...]).astype(BF16)
    widx_ref[0] = kw[:, LANES:2 * LANES]


def _inproj(x, w_in_pad, w_pool, pool_scale, kv_g, ik_g, ik_b, w_uk):
    B, L, D = x.shape
    tm = min(PROJ_ROWS, L)
    nl = L // tm
    full2 = lambda b, l: (0, 0)
    full3 = lambda b, l: (0, 0, 0)
    return pl.pallas_call(
        _inproj_kernel,
        grid=(B, nl),
        in_specs=[
            pl.BlockSpec((1, tm, D), lambda b, l: (b, l, 0)),
            pl.BlockSpec(w_in_pad.shape, full2),
            pl.BlockSpec(w_pool.shape, full3),
            pl.BlockSpec(pool_scale.shape, full2),
            pl.BlockSpec(kv_g.shape, full2),
            pl.BlockSpec(ik_g.shape, full2),
            pl.BlockSpec(ik_b.shape, full2),
            pl.BlockSpec(w_uk.shape, full3),
        ],
        out_specs=[
            pl.BlockSpec((1, tm, POOL_WIDTH), lambda b, l: (b, l, 0)),
            pl.BlockSpec((1, ATT_HEADS, tm, KV_RANK), lambda b, l: (b, 0, l, 0)),
            pl.BlockSpec((1, tm, 2 * KV_RANK), lambda b, l: (b, l, 0)),
            pl.BlockSpec((1, IDX_HEADS, tm, IDX_HEAD_DIM), lambda b, l: (b, 0, l, 0)),
            pl.BlockSpec((1, tm, IDX_HEAD_DIM), lambda b, l: (b, l, 0)),
            pl.BlockSpec((1, tm, LANES), lambda b, l: (b, l, 0)),
        ],
        out_shape=[
            jax.ShapeDtypeStruct((B, L, POOL_WIDTH), BF16),
            jax.ShapeDtypeStruct((B, ATT_HEADS, L, KV_RANK), BF16),
            jax.ShapeDtypeStruct((B, L, 2 * KV_RANK), BF16),
            jax.ShapeDtypeStruct((B, IDX_HEADS, L, IDX_HEAD_DIM), BF16),
            jax.ShapeDtypeStruct((B, L, IDX_HEAD_DIM), BF16),
            jax.ShapeDtypeStruct((B, L, LANES), F32),
        ],
        scratch_shapes=[pltpu.VMEM((MAX_POOL_WINDOW + tm, POOL_WIDTH), F32)],
        compiler_params=pltpu.CompilerParams(
            dimension_semantics=("arbitrary", "arbitrary"), vmem_limit_bytes=VMEM_LIMIT),
    )(x, w_in_pad, w_pool, pool_scale, kv_g, ik_g, ik_b, w_uk)


def _sortable(score):
    bits = pltpu.bitcast(score, I32)
    return bits ^ ((bits >> 31) & 0x7FFFFFFF)


def _dsa_kernel(qidx_ref, widx_ref, qlat_ref, kidx_ref, ckv_ref, wuv_ref, o_ref,
                key_ref, keyt_ref, w_ref, wb_ref, s_ref, s2_ref, p_ref, p2_ref, a_ref, a2_ref, m_ref, acc_ref, sel_ref,
                *, top_k, idx_bits):
    qi = pl.program_id(1)
    H = ATT_HEADS
    qb = qidx_ref.shape[2]
    kc = s_ref.shape[1]
    qstart = qi * qb
    n_ch = (qstart + qb + kc - 1) // kc
    idx_scale = (IDX_HEAD_DIM ** -0.5) * (IDX_HEADS ** -0.5)

    qpos = qstart + lax.broadcasted_iota(I32, (qb, 1), 0)
    qpos_t = qstart + lax.broadcasted_iota(I32, (1, qb), 1)
    lane_pos = lax.broadcasted_iota(I32, (1, kc), 1)
    row_pos = lax.broadcasted_iota(I32, (kc, 1), 0)

    for h in range(H):
        wb_ref[h * qb:(h + 1) * qb, :] = jnp.broadcast_to(widx_ref[0, :, h:h + 1] * idx_scale, (qb, LANES))

    def pipelined(produce, consume, buf_a, buf_b):
        produce(0, buf_a)
        n_pairs = (n_ch - 1) // 2

        def pair_body(i, carry):
            c = 2 * i
            consume(c, buf_a)
            produce(c + 1, buf_b)
            consume(c + 1, buf_b)
            produce(c + 2, buf_a)
            return carry

        lax.fori_loop(0, n_pairs, pair_body, 0)
        last = 2 * n_pairs

        @pl.when(n_ch - last == 1)
        def _():
            consume(last, buf_a)

        @pl.when(n_ch - last == 2)
        def _():
            consume(last, buf_a)
            produce(last + 1, buf_b)
            consume(last + 1, buf_b)

    def head_scores(c, buf):
        off = pl.multiple_of(c * kc, kc)
        kk = kidx_ref[0, pl.ds(off, kc), :]
        qs = qidx_ref[0].reshape(H * qb, IDX_HEAD_DIM)
        buf[...] = lax.dot_general(qs, kk, _NT, preferred_element_type=F32)

    half = kc // 2
    f_max = (1 << FIELD_BITS) - 1
    guards = jnp.int32(-(1 << 31) + (1 << FIELD_BITS))

    def field1(k):
        return (k >> (32 - FIELD_BITS)) + (1 << (FIELD_BITS - 1))

    def field2(k):
        return (k >> (32 - 2 * FIELD_BITS)) & f_max

    def pack(fa, fb):
        return lax.shift_left(fb, 16) | fa | guards

    def chunk_keys(c, buf):
        keyt = []
        for j in range(kc // LANES):
            cs = slice(j * LANES, (j + 1) * LANES)
            acc = None
            for h in range(H):
                r = slice(h * qb, (h + 1) * qb)
                term = jnp.maximum(buf[r, cs], 0.0) * wb_ref[r, :]
                acc = term if acc is None else acc + term
            score = jnp.where(c * kc + j * LANES + lane_pos[:, 0:LANES] <= qpos, acc, -jnp.inf)
            key = _sortable(score)
            key_ref[c, :, cs] = key
            keyt.append(jnp.transpose(key))
            keyt_ref[c, cs, :] = keyt[-1]
        kt = jnp.concatenate(keyt, axis=0)
        w_ref[c] = pack(field1(kt[0:half, :]), field1(kt[half:kc, :]))

    pipelined(head_scores, chunk_keys, s_ref, s2_ref)

    def count(indicator):
        def body(c, cnt):
            m = indicator(keyt_ref[c], c * kc + row_pos)
            parts = [m[j * 8:(j + 1) * 8, :] for j in range(kc // 8)]
            while len(parts) > 1:
                parts = [parts[j] + parts[j + 1] for j in range(0, len(parts), 2)]
            return cnt + parts[0]
        cnt = lax.fori_loop(0, n_ch, body, jnp.zeros((8, qb), I32))
        return jnp.sum(cnt, axis=0, keepdims=True)

    units = jnp.int32((1 << 16) + 1)
    n_all = jnp.full((1, qb), n_ch * kc, I32)

    def pack_fields(field):
        def body(c, carry):
            w_ref[c] = pack(field(keyt_ref[c, 0:half, :]), field(keyt_ref[c, half:kc, :]))
            return carry
        lax.fori_loop(0, n_ch, body, 0)

    def count_fields(cand):
        cc = cand | lax.shift_left(cand, 16)

        def body(c, cnt):
            g = lax.shift_right_logical(w_ref[c] - cc, FIELD_BITS) & units
            parts = [g[j * 8:(j + 1) * 8, :] for j in range(half // 8)]
            while len(parts) > 1:
                parts = [parts[j] + parts[j + 1] for j in range(0, len(parts), 2)]
            return cnt + parts[0]
        s = jnp.sum(lax.fori_loop(0, n_ch, body, jnp.zeros((8, qb), I32)), axis=0, keepdims=True)
        return (s & 0xFFFF) + lax.shift_right_logical(s, 16)

    def search_bits(first, last, t, n_t, n_above, counter):
        def body(i, carry):
            t, n_t = carry
            cand = t + lax.shift_left(jnp.int32(1), first - i)
            n = n_above + counter(cand)
            ok = n >= top_k
            return jnp.where(ok, cand, t), jnp.where(ok, n, n_t)
        return lax.fori_loop(0, first - last, body, (t, n_t))

    zero = jnp.zeros((1, qb), I32)
    t1, n_t = search_bits(FIELD_BITS - 1, -1, zero, n_all, zero, count_fields)
    n_above = jnp.where(t1 < f_max, count_fields(jnp.minimum(t1 + 1, f_max)), 0)
    pack_fields(lambda k: jnp.where(field1(k) == t1, field2(k), 0))
    t2, n_t = search_bits(FIELD_BITS - 1, SEARCH_CHECK_BIT - 1, zero, n_t, n_above, count_fields)

    def prefix(t2):
        return lax.shift_left(t1 - (1 << (FIELD_BITS - 1)), 32 - FIELD_BITS) | lax.shift_left(t2, 32 - 2 * FIELD_BITS)

    sel_ref[0:1, :] = prefix(t2)
    sel_ref[2:3, :] = n_t
    sel_ref[3:4, :] = t2

    @pl.when(jnp.max(jnp.where(sel_ref[2:3, :] != top_k, 1, 0)) > 0)
    def _():
        t2, n_t = search_bits(SEARCH_CHECK_BIT - 1, -1, sel_ref[3:4, :], sel_ref[2:3, :], n_above, count_fields)
        t, n_t = search_bits(32 - 2 * FIELD_BITS - 1, -1, prefix(t2), n_t, zero,
                             lambda cand: count(lambda k, _: jnp.where(k >= cand, 1, 0)))
        sel_ref[0:1, :] = t
        sel_ref[2:3, :] = n_t

    thr_t = sel_ref[0:1, :]
    n_ge = sel_ref[2:3, :]

    surplus = jnp.where(thr_t > NEG_INF_KEY, jnp.where(n_ge > top_k, 1, 0), 0)
    sel_ref[1:2, :] = qpos_t

    @pl.when(jnp.max(surplus) > 0)
    def _():
        need = top_k - count(lambda k, _: jnp.where(k > thr_t, 1, 0))

        def tie_body(i, m):
            cand = m + lax.shift_left(jnp.int32(1), idx_bits - 1 - i)
            n_before = count(lambda k, p: jnp.where(k == thr_t, jnp.where(p < cand, 1, 0), 0))
            return jnp.where(n_before < need, cand, m)
        last = lax.fori_loop(0, idx_bits, tie_body, jnp.zeros((1, qb), I32))
        sel_ref[1:2, :] = jnp.where(surplus > 0, jnp.minimum(last, qpos_t), qpos_t)

    thr = jnp.transpose(jnp.broadcast_to(sel_ref[0:1, :], (qb, qb)))[:, 0:1]
    tie_last = jnp.transpose(jnp.broadcast_to(sel_ref[1:2, :], (qb, qb)))[:, 0:1]

    def chunk_bias(c):
        key = key_ref[c]
        kpos = c * kc + lane_pos
        tie_bias = jnp.where(key == thr, jnp.where(kpos <= tie_last, 0.0, NEG_BIG), NEG_BIG)
        return jnp.where(key > thr, 0.0, tie_bias)

    n_lt = kc // LANES

    def head_logits(h, ck):
        return lax.dot_general(qlat_ref[0, h], ck, _NT, preferred_element_type=F32)

    m_ref[...] = jnp.full(m_ref.shape, NEG_BIG, F32)
    acc_ref[...] = jnp.zeros(acc_ref.shape, F32)

    def numerators(c, bufs):
        pbuf, abuf = bufs
        off = pl.multiple_of(c * kc, kc)
        ck = ckv_ref[0, pl.ds(off, kc), 0:KV_RANK]
        bias = chunk_bias(c)
        for h in range(H):
            r = slice(h * qb, (h + 1) * qb)
            lg = head_logits(h, ck)
            z = [lg[:, j * LANES:(j + 1) * LANES] + bias[:, j * LANES:(j + 1) * LANES] for j in range(n_lt)]
            zm = z[0]
            for j in range(1, n_lt):
                zm = jnp.maximum(zm, z[j])
            m_old = m_ref[r, :]
            m_new = jnp.maximum(m_old, jnp.max(zm, axis=1, keepdims=True))
            m_ref[r, :] = m_new
            abuf[r, :] = jnp.exp2(m_old - m_new)
            for j in range(n_lt):
                pbuf[r, j * LANES:(j + 1) * LANES] = jnp.exp2(z[j] - m_new).astype(BF16)

    def accumulate(c, bufs):
        pbuf, abuf = bufs
        off = pl.multiple_of(c * kc, kc)
        ckx = ckv_ref[0, pl.ds(off, kc), :]
        pv = jnp.dot(pbuf[...], ckx, preferred_element_type=F32)
        a = abuf[...]
        acc_ref[:, 0:KV_RANK] = a * acc_ref[:, 0:KV_RANK] + pv[:, 0:KV_RANK]
        acc_ref[:, KV_RANK:2 * KV_RANK] = a * acc_ref[:, KV_RANK:2 * KV_RANK] + pv[:, KV_RANK:2 * KV_RANK]

    pipelined(numerators, accumulate, (p_ref, a_ref), (p2_ref, a2_ref))

    o_lat = (acc_ref[:, 0:KV_RANK] / acc_ref[:, KV_RANK:2 * KV_RANK]).astype(BF16)
    for h in range(0, H, 2):
        t = jnp.dot(o_lat[h * qb:(h + 1) * qb, :], wuv_ref[h], preferred_element_type=F32)
        t = t + jnp.dot(o_lat[(h + 1) * qb:(h + 2) * qb, :], wuv_ref[h + 1], preferred_element_type=F32)
        o_ref[0, :, h * ATT_HEAD_DIM:(h + 2) * ATT_HEAD_DIM] = t.astype(BF16)


def _dsa(qidx, widx, qlat, kidx, ckv, wuv_pad):
    B, H, L, _ = qidx.shape
    qb = min(DSA_QUERIES, L)
    kc = min(DSA_KEYS, L)
    top_k = min(TOPK_MAX, L // 4)
    idx_bits = max(1, (L - 1).bit_length())
    assert L % qb == 0 and L % kc == 0 and qb % LANES == 0 and kc % (2 * LANES) == 0
    assert top_k < kc and L < (1 << FIELD_BITS)
    kern = functools.partial(_dsa_kernel, top_k=top_k, idx_bits=idx_bits)
    return pl.pallas_call(
        kern,
        grid=(B, L // qb),
        in_specs=[
            pl.BlockSpec((1, H, qb, IDX_HEAD_DIM), lambda b, q: (b, 0, q, 0)),
            pl.BlockSpec((1, qb, LANES), lambda b, q: (b, q, 0)),
            pl.BlockSpec((1, H, qb, KV_RANK), lambda b, q: (b, 0, q, 0)),
            pl.BlockSpec((1, L, IDX_HEAD_DIM), lambda b, q: (b, 0, 0)),
            pl.BlockSpec((1, L, 2 * KV_RANK), lambda b, q: (b, 0, 0)),
            pl.BlockSpec(wuv_pad.shape, lambda b, q: (0, 0, 0)),
        ],
        out_specs=pl.BlockSpec((1, qb, ATT_WIDTH), lambda b, q: (b, q, 0)),
        out_shape=jax.ShapeDtypeStruct((B, L, ATT_WIDTH), BF16),
        scratch_shapes=[
            pltpu.VMEM((L // kc, qb, kc), I32),
            pltpu.VMEM((L // kc, kc, qb), I32),
            pltpu.VMEM((L // kc, kc // 2, qb), I32),
            pltpu.VMEM((H * qb, LANES), F32),
            pltpu.VMEM((H * qb, kc), F32),
            pltpu.VMEM((H * qb, kc), F32),
            pltpu.VMEM((H * qb, kc), BF16),
            pltpu.VMEM((H * qb, kc), BF16),
            pltpu.VMEM((H * qb, LANES), F32),
            pltpu.VMEM((H * qb, LANES), F32),
            pltpu.VMEM((H * qb, LANES), F32),
            pltpu.VMEM((H * qb, 2 * KV_RANK), F32),
            pltpu.VMEM((8, qb), I32),
        ],
        compiler_params=pltpu.CompilerParams(
            dimension_semantics=("arbitrary", "arbitrary"), vmem_limit_bytes=VMEM_LIMIT),
    )(qidx, widx, qlat, kidx, ckv, wuv_pad)


def _memkv_kernel(mem_ref, w_ref, k_ref, v_ref):
    d = k_ref.shape[2]
    kv = jnp.dot(mem_ref[0].astype(BF16), w_ref[...], preferred_element_type=F32)
    k_ref[0] = kv[:, 0:d].astype(BF16)
    v_ref[0] = kv[:, d:2 * d].astype(BF16)


def _memkv(mem, w_mkv):
    B, M, D = mem.shape
    return pl.pallas_call(
        _memkv_kernel,
        grid=(B,),
        in_specs=[pl.BlockSpec((1, M, D), lambda b: (b, 0, 0)),
                  pl.BlockSpec(w_mkv.shape, lambda b: (0, 0))],
        out_specs=[pl.BlockSpec((1, M, D), lambda b: (b, 0, 0)),
                   pl.BlockSpec((1, M, D), lambda b: (b, 0, 0))],
        out_shape=[jax.ShapeDtypeStruct((B, M, D), BF16), jax.ShapeDtypeStruct((B, M, D), BF16)],
        compiler_params=pltpu.CompilerParams(
            dimension_semantics=("arbitrary",), vmem_limit_bytes=VMEM_LIMIT),
    )(mem, w_mkv)


def _split3(v):
    hi = v.astype(BF16)
    r1 = v - hi.astype(F32)
    mid = r1.astype(BF16)
    lo = (r1 - mid.astype(F32)).astype(BF16)
    return hi, mid, lo


def _mix_kernel(x_ref, yp_ref, ya_ref, wo_ref, g1_ref, b1_ref, km_ref, vm_ref, wq_ref, wmo_ref,
                g2_ref, b2_ref, wr_ref, br_ref, x2_ref, tope_ref, gate_ref):
    tm, d = x_ref.shape
    hd = d // MEM_HEADS
    pw = yp_ref.shape[1]
    ws = _split3(wr_ref[...])

    def stage_mix(rs):
        mix = jnp.dot(yp_ref[rs, :], wo_ref[0:pw, :], preferred_element_type=F32)
        mix = mix + jnp.dot(ya_ref[rs, :], wo_ref[pw:, :], preferred_element_type=F32)
        return _layer_norm(DN_ALPHA * x_ref[rs, :] + mix, g1_ref[...], b1_ref[...])

    def stage_mem(x1):
        q = jnp.dot(x1.astype(BF16), wq_ref[...], preferred_element_type=F32).astype(BF16)
        scale = hd ** -0.5
        att = None
        for h in range(MEM_HEADS):
            c = slice(h * hd, (h + 1) * hd)
            lg = lax.dot_general(q[:, c], km_ref[0, :, c], _NT, preferred_element_type=F32) * scale
            p = jnp.exp(lg - jnp.max(lg, axis=-1, keepdims=True))
            p = p / jnp.sum(p, axis=-1, keepdims=True)
            oh = jnp.dot(p.astype(BF16), vm_ref[0, :, c], preferred_element_type=F32).astype(BF16)
            t = jnp.dot(oh, wmo_ref[c, :], preferred_element_type=F32)
            att = t if att is None else att + t
        return att

    def stage_route(rs, x1, att):
        n = rs.stop - rs.start
        x2 = _layer_norm(DN_ALPHA * x1 + att, g2_ref[...], b2_ref[...])
        x2_ref[rs, :] = x2

        xs = _split3(x2)
        lt = None
        for i, j in ((0, 0), (0, 1), (1, 0)):
            t = lax.dot_general(ws[j], xs[i], _NT, preferred_element_type=F32)
            lt = t if lt is None else lt + t
        lt = lt + br_ref[...]
        n_e = lt.shape[0]
        eidx = lax.broadcasted_iota(I32, lt.shape, 0)
        vals, idxs = [], []
        for _ in range(TOP_K_EXPERTS):
            mx = jnp.max(lt, axis=0, keepdims=True)
            ix = jnp.min(jnp.where(lt == mx, eidx, n_e), axis=0, keepdims=True)
            vals.append(mx)
            idxs.append(ix)
            lt = jnp.where(eidx == ix, -jnp.inf, lt)
        tope_ref[:, rs] = jnp.concatenate(idxs, axis=0)
        ex = [jnp.exp(v - vals[0]) for v in vals]
        den = ex[0]
        for e_ in ex[1:]:
            den = den + e_
        gates = jnp.concatenate([e_ / den for e_ in ex] + [jnp.zeros((LANES - TOP_K_EXPERTS, n), F32)], axis=0)
        gate_ref[rs, :] = jnp.transpose(gates)

    all_rows = slice(0, tm)
    x1 = stage_mix(all_rows)
    stage_route(all_rows, x1, stage_mem(x1))


def _mix(x2d, ypool, yatt, w_o, g1, b1, k_mem, v_mem, w_mq, w_mo, g2, b2, w_rt, b_r, B, L):
    T, D = x2d.shape
    tm = min(MIX_ROWS, L)
    nl = L // tm
    M = k_mem.shape[1]
    row = lambda i: (i, 0)
    full = lambda i: (0, 0)
    return pl.pallas_call(
        _mix_kernel,
        grid=(T // tm,),
        in_specs=[
            pl.BlockSpec((tm, D), row),
            pl.BlockSpec((tm, ypool.shape[1]), row),
            pl.BlockSpec((tm, yatt.shape[1]), row),
            pl.BlockSpec(w_o.shape, full),
            pl.BlockSpec(g1.shape, full),
            pl.BlockSpec(b1.shape, full),
            pl.BlockSpec((1, M, D), lambda i: (i // nl, 0, 0)),
            pl.BlockSpec((1, M, D), lambda i: (i // nl, 0, 0)),
            pl.BlockSpec(w_mq.shape, full),
            pl.BlockSpec(w_mo.shape, full),
            pl.BlockSpec(g2.shape, full),
            pl.BlockSpec(b2.shape, full),
            pl.BlockSpec(w_rt.shape, full),
            pl.BlockSpec(b_r.shape, full),
        ],
        out_specs=[
            pl.BlockSpec((tm, D), row),
            pl.BlockSpec((TOP_K_EXPERTS, tm), lambda i: (0, i)),
            pl.BlockSpec((tm, LANES), row),
        ],
        out_shape=[
            jax.ShapeDtypeStruct((T, D), F32),
            jax.ShapeDtypeStruct((TOP_K_EXPERTS, T), I32),
            jax.ShapeDtypeStruct((T, LANES), F32),
        ],
        compiler_params=pltpu.CompilerParams(
            dimension_semantics=("arbitrary",), vmem_limit_bytes=VMEM_LIMIT),
    )(x2d, ypool, yatt, w_o, g1, b1, k_mem, v_mem, w_mq, w_mo, g2, b2, w_rt, b_r)


def _rank_kernel(tope_ref, rank_ref, cnt_ref, carry_ref):
    i = pl.program_id(0)
    tr = tope_ref.shape[1]

    @pl.when(i == 0)
    def _():
        carry_ref[...] = jnp.zeros(carry_ref.shape, F32)

    eidx = lax.broadcasted_iota(I32, (N_EXPERTS, tr), 0)
    onehot = jnp.zeros((N_EXPERTS, tr), F32)
    for k in range(TOP_K_EXPERTS):
        onehot = onehot + jnp.where(eidx == tope_ref[k:k + 1, :], 1.0, 0.0)
    before = jnp.where(lax.broadcasted_iota(I32, (tr, tr), 0) < lax.broadcasted_iota(I32, (tr, tr), 1), 1.0, 0.0)
    excl = jnp.dot(onehot.astype(BF16), before.astype(BF16), preferred_element_type=F32)
    rank_full = excl + carry_ref[:, 0:1]
    rows = []
    for k in range(TOP_K_EXPERTS):
        rows.append(jnp.sum(jnp.where(eidx == tope_ref[k:k + 1, :], rank_full, 0.0), axis=0, keepdims=True))
    rank_ref[...] = jnp.concatenate(rows, axis=0).astype(I32)
    carry_ref[...] = carry_ref[...] + jnp.sum(onehot, axis=1, keepdims=True)
    cnt_ref[...] = carry_ref[...].astype(I32)


def _rank(tope):
    K, T = tope.shape
    tr = min(RANK_TOKENS, T)
    return pl.pallas_call(
        _rank_kernel,
        grid=(T // tr,),
        in_specs=[pl.BlockSpec((K, tr), lambda i: (0, i))],
        out_specs=[pl.BlockSpec((K, tr), lambda i: (0, i)),
                   pl.BlockSpec((N_EXPERTS, LANES), lambda i: (0, 0))],
        out_shape=[jax.ShapeDtypeStruct((K, T), I32), jax.ShapeDtypeStruct((N_EXPERTS, LANES), I32)],
        scratch_shapes=[pltpu.VMEM((N_EXPERTS, LANES), F32)],
        compiler_params=pltpu.CompilerParams(dimension_semantics=("arbitrary",)),
    )(tope)


def _dest_kernel(start_ref, tope_ref, rank_ref, dest_ref):
    tope = tope_ref[...]
    base = jnp.zeros(tope.shape, I32)
    for e in range(N_EXPERTS):
        base = jnp.where(tope == e, start_ref[e], base)
    dest_ref[...] = base + rank_ref[...]


def _dest(pad_start, tope, rank):
    K, T = tope.shape
    tt = min(DEST_TOKENS, T)
    blk = pl.BlockSpec((K, tt), lambda i: (0, i))
    return pl.pallas_call(
        _dest_kernel,
        grid=(T // tt,),
        in_specs=[pl.BlockSpec(memory_space=pltpu.SMEM), blk, blk],
        out_specs=blk,
        out_shape=jax.ShapeDtypeStruct((K, T), I32),
        compiler_params=pltpu.CompilerParams(dimension_semantics=("arbitrary",)),
    )(pad_start, tope, rank)


def _dispatch_kernel(dest_ref, start_ref, cnt_ref, padded_ref, x_ref, xs_hbm, zrow, sem, zsem):
    i = pl.program_id(0)
    td = dest_ref.shape[1]

    def row_copy(j, dst):
        return pltpu.make_async_copy(x_ref.at[pl.ds(j, 1), :], xs_hbm.at[pl.ds(dst, 1), :], sem)

    def zero_copy(dst):
        return pltpu.make_async_copy(zrow, xs_hbm.at[pl.ds(dst, 1), :], zsem)

    @pl.when(i == 0)
    def _():
        zrow[...] = jnp.zeros(zrow.shape, F32)
        for e in range(N_EXPERTS):
            first = start_ref[e] + cnt_ref[e]
            n_pad = padded_ref[e] - cnt_ref[e]

            def zstart(r, c):
                zero_copy(first + r).start()
                return c
            lax.fori_loop(0, n_pad, zstart, 0)

            def zwait(r, c):
                zero_copy(first + r).wait()
                return c
            lax.fori_loop(0, n_pad, zwait, 0)

    def issue(jj, c):
        for u in range(ISSUE_UNROLL):
            j = jj * ISSUE_UNROLL + u
            for k in range(TOP_K_EXPERTS):
                row_copy(j, dest_ref[k, j]).start()
        return c
    lax.fori_loop(0, td // ISSUE_UNROLL, issue, 0)

    def drain(jj, c):
        for _ in range(ISSUE_UNROLL * TOP_K_EXPERTS):
            row_copy(0, 0).wait()
        return c
    lax.fori_loop(0, td // ISSUE_UNROLL, drain, 0)


def _dispatch(dest, pad_start, counts, padded, x2, n_rows):
    K, T = dest.shape
    D = x2.shape[1]
    td = min(DISPATCH_TOKENS, T)
    smem_tok = pl.BlockSpec((K, td), lambda i: (0, i), memory_space=pltpu.SMEM)
    smem_full = pl.BlockSpec(memory_space=pltpu.SMEM)
    return pl.pallas_call(
        _dispatch_kernel,
        grid=(T // td,),
        in_specs=[smem_tok, smem_full, smem_full, smem_full,
                  pl.BlockSpec((td, D), lambda i: (i, 0))],
        out_specs=pl.BlockSpec(memory_space=pl.ANY),
        out_shape=jax.ShapeDtypeStruct((n_rows, D), F32),
        scratch_shapes=[pltpu.VMEM((1, D), F32), pltpu.SemaphoreType.DMA(()), pltpu.SemaphoreType.DMA(())],
        compiler_params=pltpu.CompilerParams(
            dimension_semantics=("arbitrary",), vmem_limit_bytes=VMEM_LIMIT),
    )(dest, pad_start, counts, padded, x2)


def _expert_kernel(bexp_ref, nused_ref, xs_ref, wgu_ref, bgu_ref, wd_ref, bd_ref, y_ref, wgu_bf, wd_bf):
    j = pl.program_id(0)
    d_ff = wd_ref.shape[1]

    @pl.when(j < nused_ref[0])
    def _():
        prev = bexp_ref[jnp.maximum(j - 1, 0)]

        @pl.when(jnp.logical_or(j == 0, bexp_ref[j] != prev))
        def _():
            wgu_bf[...] = wgu_ref[0].astype(BF16)
            wd_bf[...] = wd_ref[0].astype(BF16)

        xb = xs_ref[...].astype(BF16)
        gu = jnp.dot(xb, wgu_bf[...], preferred_element_type=F32) + bgu_ref[0]
        g = jnp.minimum(gu[:, 0:d_ff], SWIGLU_LIMIT)
        u = jnp.clip(gu[:, d_ff:2 * d_ff], -SWIGLU_LIMIT, SWIGLU_LIMIT)
        act = (u + 1.0) * g * (1.0 / (1.0 + jnp.exp(-SWIGLU_ALPHA * g)))
        y_ref[...] = jnp.dot(act.astype(BF16), wd_bf[...], preferred_element_type=F32) + bd_ref[0]


def _experts(block_exp, n_used, xs, w_gate_up, b_gate_up, w_down, b_down):
    P, D = xs.shape
    blk = EXPERT_ROWS
    nblk = P // blk
    E, _, F2 = w_gate_up.shape
    d_ff = w_down.shape[1]
    row = lambda j, be, nu: (jnp.minimum(j, nu[0] - 1), 0)
    exp3 = lambda j, be, nu: (be[j], 0, 0)
    grid_spec = pltpu.PrefetchScalarGridSpec(
        num_scalar_prefetch=2,
        grid=(nblk,),
        in_specs=[
            pl.BlockSpec((blk, D), row),
            pl.BlockSpec((1, D, F2), exp3),
            pl.BlockSpec((1, 1, F2), exp3),
            pl.BlockSpec((1, d_ff, D), exp3),
            pl.BlockSpec((1, 1, D), exp3),
        ],
        out_specs=pl.BlockSpec((blk, D), row),
        scratch_shapes=[pltpu.VMEM((D, F2), BF16), pltpu.VMEM((d_ff, D), BF16)],
    )
    return pl.pallas_call(
        _expert_kernel,
        grid_spec=grid_spec,
        out_shape=jax.ShapeDtypeStruct((P, D), F32),
        compiler_params=pltpu.CompilerParams(
            dimension_semantics=("arbitrary",), vmem_limit_bytes=VMEM_LIMIT),
    )(block_exp, n_used, xs, w_gate_up, b_gate_up.reshape(E, 1, F2), w_down, b_down.reshape(E, 1, D))


def _combine_kernel(dest_ref, dest_next_ref, x2_ref, gate_ref, g3_ref, b3_ref, y_hbm, o_ref, ybuf, sem):
    i = pl.program_id(0)
    n = pl.num_programs(0)
    tc = x2_ref.shape[0]
    slot = i % 2

    def row_copy(src_ref, s, j, k):
        return pltpu.make_async_copy(y_hbm.at[pl.ds(src_ref[k, j], 1), :], ybuf.at[s, k, pl.ds(j, 1), :], sem.at[s])

    def gather(src_ref, s):
        def issue(jj, c):
            for u in range(ISSUE_UNROLL):
                for k in range(TOP_K_EXPERTS):
                    row_copy(src_ref, s, jj * ISSUE_UNROLL + u, k).start()
            return c
        lax.fori_loop(0, tc // ISSUE_UNROLL, issue, 0)

    @pl.when(i == 0)
    def _():
        gather(dest_ref, 0)

    @pl.when(i + 1 < n)
    def _():
        gather(dest_next_ref, 1 - slot)

    def drain(jj, c):
        for _ in range(ISSUE_UNROLL * TOP_K_EXPERTS):
            pltpu.make_async_copy(y_hbm.at[pl.ds(0, 1), :], ybuf.at[slot, 0, pl.ds(0, 1), :], sem.at[slot]).wait()
        return c
    lax.fori_loop(0, tc // ISSUE_UNROLL, drain, 0)

    moe = None
    for k in range(TOP_K_EXPERTS):
        t = ybuf[slot, k] * gate_ref[:, k:k + 1]
        moe = t if moe is None else moe + t
    o_ref[...] = _layer_norm(DN_ALPHA * x2_ref[...] + moe, g3_ref[...], b3_ref[...])


def _combine(dest, x2, gates, g3, b3, y):
    K, T = dest.shape
    D = x2.shape[1]
    tc = min(COMBINE_TOKENS, T)
    n = T // tc
    row = lambda i: (i, 0)
    full = lambda i: (0, 0)
    return pl.pallas_call(
        _combine_kernel,
        grid=(n,),
        in_specs=[pl.BlockSpec((K, tc), lambda i: (0, i), memory_space=pltpu.SMEM),
                  pl.BlockSpec((K, tc), lambda i: (0, jnp.minimum(i + 1, n - 1)), memory_space=pltpu.SMEM),
                  pl.BlockSpec((tc, D), row), pl.BlockSpec((tc, LANES), row),
                  pl.BlockSpec(g3.shape, full), pl.BlockSpec(b3.shape, full),
                  pl.BlockSpec(memory_space=pl.ANY)],
        out_specs=pl.BlockSpec((tc, D), row),
        out_shape=jax.ShapeDtypeStruct((T, D), F32),
        scratch_shapes=[pltpu.VMEM((2, K, tc, D), F32), pltpu.SemaphoreType.DMA((2,))],
        compiler_params=pltpu.CompilerParams(
            dimension_semantics=("arbitrary",), vmem_limit_bytes=VMEM_LIMIT),
    )(dest, dest, x2, gates, g3, b3, y)


def _pad_cols(w, width):
    return jnp.pad(w, ((0, 0), (0, width - w.shape[1])))


def _layer(x, mem, w_in, w_pool, pool_scale, ik_g, ik_b, kv_g, w_uk, w_uv, w_o, ln1_g, ln1_b,
           w_mq, w_mkv, w_mo, ln2_g, ln2_b, w_router, b_router, w_gate_up, b_gate_up, w_down, b_down,
           ln3_g, ln3_b):
    B, L, D = x.shape
    T = B * L
    row = lambda v: v.reshape(1, -1)

    o = 0
    pieces = []
    for width in (POOL_WIDTH, ATT_WIDTH, KV_RANK, IDX_HEADS * IDX_HEAD_DIM, IDX_HEAD_DIM, IDX_HEADS):
        pieces.append(w_in[:, o:o + width])
        o += width
    pieces[4] = _pad_cols(pieces[4], LANES)
    pieces[5] = _pad_cols(pieces[5], LANES)
    w_in_pad = jnp.concatenate(pieces, axis=1).astype(BF16)
    wuv_pad = jnp.zeros((ATT_HEADS, KV_RANK, 2 * ATT_HEAD_DIM), F32)
    for h in range(ATT_HEADS):
        wuv_pad = wuv_pad.at[h, :, (h % 2) * ATT_HEAD_DIM:(h % 2 + 1) * ATT_HEAD_DIM].set(w_uv[h])

    ypool, qlat, ckv, qidx, kidx, widx = _inproj(
        x, w_in_pad, w_pool.astype(BF16), row(pool_scale), row(kv_g), row(ik_g), row(ik_b), w_uk.astype(BF16))
    yatt = _dsa(qidx, widx, qlat, kidx, ckv, wuv_pad.astype(BF16))
    k_mem, v_mem = _memkv(mem, w_mkv.astype(BF16))
    x2, tope, gates = _mix(
        x.reshape(T, D), ypool.reshape(T, -1), yatt.reshape(T, -1), w_o.astype(BF16), row(ln1_g), row(ln1_b),
        k_mem, v_mem, w_mq.astype(BF16), w_mo.astype(BF16), row(ln2_g), row(ln2_b),
        jnp.transpose(w_router), b_router.reshape(-1, 1), B, L)

    rank, cnt = _rank(tope)
    counts = cnt[:, 0]
    blk = EXPERT_ROWS
    padded = (counts + blk - 1) // blk * blk
    pad_end = jnp.cumsum(padded)
    pad_start = pad_end - padded
    n_rows = T * TOP_K_EXPERTS + N_EXPERTS * blk
    nblk = n_rows // blk
    n_used = (pad_end[-1] // blk).astype(I32)
    blk_first = jnp.minimum(jnp.arange(nblk, dtype=I32), n_used - 1) * blk
    n_ended = jnp.sum((pad_end[None, :] <= blk_first[:, None]).astype(I32), axis=1)
    block_exp = jnp.minimum(n_ended, N_EXPERTS - 1).astype(I32)

    pad_start = pad_start.astype(I32)
    dest = _dest(pad_start, tope, rank)
    xs = _dispatch(dest, pad_start, counts, padded.astype(I32), x2, n_rows)
    y = _experts(block_exp, n_used.reshape(1), xs, w_gate_up, b_gate_up, w_down, b_down)
    out = _combine(dest, x2, gates, row(ln3_g), row(ln3_b), y)
    return out.reshape(B, L, D)


def kernel(x, mem, w_in, w_pool, pool_scale, idx_k_norm_g, idx_k_norm_b, kv_norm_g, w_uk, w_uv, w_o, ln1_g, ln1_b, w_mq, w_mkv, w_mo, ln2_g, ln2_b, w_router, b_router, w_gate_up, b_gate_up, w_down, b_down, ln3_g, ln3_b):
    assert w_in.shape[0] == DEPTH
    return _layer(x, mem, w_in[0], w_pool[0], pool_scale[0], idx_k_norm_g[0], idx_k_norm_b[0], kv_norm_g[0],
                  w_uk[0], w_uv[0], w_o[0], ln1_g[0], ln1_b[0], w_mq[0], w_mkv[0], w_mo[0], ln2_g[0], ln2_b[0],
                  w_router[0], b_router[0], w_gate_up[0], b_gate_up[0], w_down[0], b_down[0], ln3_g[0], ln3_b[0])
```

```python
import functools

import jax
import jax.numpy as jnp
from jax import lax
from jax.experimental import pallas as pl
from jax.experimental.pallas import tpu as pltpu

F32 = jnp.float32
BF16 = jnp.bfloat16
I32 = jnp.int32

POOL_WINDOWS = (2, 4, 8, 16)
POOL_GROUP_DIM = 128
POOL_WIDTH = 512
ATT_HEADS = 8
ATT_HEAD_DIM = 64
ATT_WIDTH = 512
KV_RANK = 128
IDX_HEADS = 8
IDX_HEAD_DIM = 64
TOPK_MAX = 256
MEM_HEADS = 4
N_EXPERTS = 32
TOP_K_EXPERTS = 4
SWIGLU_LIMIT = 7.0
SWIGLU_ALPHA = 1.702
LN_EPS = 1e-5
RMS_EPS = 1e-6
DEPTH = 1
DN_ALPHA = (2 * DEPTH) ** 0.25
MAX_POOL_WINDOW = max(POOL_WINDOWS)

LANES = 128
VMEM_LIMIT = 56 * 1024 * 1024

PROJ_ROWS = 512
DSA_QUERIES = 128
DSA_KEYS = 512
FIELD_BITS = 15
SEARCH_CHECK_BIT = 6
MIX_ROWS = 1024
RANK_TOKENS = 512
DEST_TOKENS = 2048
DISPATCH_TOKENS = 1024
ISSUE_UNROLL = 256
EXPERT_ROWS = 256
COMBINE_TOKENS = 512

LOG2_E = 1.4426950408889634
NEG_BIG = -1e30
INT_MIN = -(2 ** 31)
NEG_INF_KEY = INT_MIN + 0x007FFFFF

_NT = (((1,), (1,)), ((), ()))


def _layer_norm(v, g, b):
    mu = jnp.mean(v, axis=-1, keepdims=True)
    d = v - mu
    var = jnp.mean(d * d, axis=-1, keepdims=True)
    return d * lax.rsqrt(var + LN_EPS) * g + b


_C_POOL = 0
_C_Q = 512
_C_CKV = 1024
_C_QIDX = 1152
_C_KIDX = 1664
_C_WIDX = 1792
_IN_PAD = 1920


def _inproj_kernel(x_ref, w_ref, wpool_ref, pscale_ref, kvg_ref, ikg_ref, ikb_ref, wuk_ref,
                   ypool_ref, qlat_ref, ckv_ref, qidx_ref, kidx_ref, widx_ref, ubuf):
    li = pl.program_id(1)
    tm = x_ref.shape[1]
    halo = MAX_POOL_WINDOW
    xb = x_ref[0].astype(BF16)

    u = jnp.dot(xb, w_ref[:, _C_POOL:_C_POOL + POOL_WIDTH], preferred_element_type=F32)

    @pl.when(li == 0)
    def _():
        ubuf[0:halo, :] = jnp.zeros((halo, POOL_WIDTH), F32)

    ubuf[halo:halo + tm, :] = u
    pos = li * tm + lax.broadcasted_iota(I32, (tm, 1), 0)
    for g, w in enumerate(POOL_WINDOWS):
        c0 = g * POOL_GROUP_DIM
        c1 = c0 + POOL_GROUP_DIM
        ug = ubuf[halo:halo + tm, c0:c1]
        s = ug
        for j in range(1, w):
            s = s + ubuf[halo - j:halo - j + tm, c0:c1]
        cnt = jnp.minimum(pos + 1, w).astype(F32)
        d = s / cnt - ug
        y = jnp.dot(d.astype(BF16), wpool_ref[g], preferred_element_type=F32) * pscale_ref[:, c0:c1]
        ypool_ref[0, :, c0:c1] = y.astype(BF16)
    ubuf[0:halo, :] = ubuf[tm:tm + halo, :]

    q = jnp.dot(xb, w_ref[:, _C_Q:_C_Q + ATT_WIDTH], preferred_element_type=F32)
    att_scale = (ATT_HEAD_DIM ** -0.5) * LOG2_E
    for h in range(ATT_HEADS):
        qh = q[:, h * ATT_HEAD_DIM:(h + 1) * ATT_HEAD_DIM].astype(BF16)
        ql = jnp.dot(qh, wuk_ref[h], preferred_element_type=F32) * att_scale
        qlat_ref[0, h] = ql.astype(BF16)

    c = jnp.dot(xb, w_ref[:, _C_CKV:_C_CKV + KV_RANK], preferred_element_type=F32)
    c = c * lax.rsqrt(jnp.mean(c * c, axis=-1, keepdims=True) + RMS_EPS) * kvg_ref[...]
    ckv_ref[0, :, 0:KV_RANK] = c.astype(BF16)
    ckv_ref[0, :, KV_RANK:2 * KV_RANK] = jnp.ones((tm, KV_RANK), BF16)

    qi = jnp.dot(xb, w_ref[:, _C_QIDX:_C_QIDX + IDX_HEADS * IDX_HEAD_DIM], preferred_element_type=F32)
    for h in range(IDX_HEADS):
        qidx_ref[0, h] = qi[:, h * IDX_HEAD_DIM:(h + 1) * IDX_HEAD_DIM].astype(BF16)
    kw = jnp.dot(xb, w_ref[:, _C_KIDX:_C_KIDX + 2 * LANES], preferred_element_type=F32)
    kr = kw[:, 0:IDX_HEAD_DIM]
    kidx_ref[0] = _layer_norm(kr, ikg_ref[...], ikb_ref[...]).astype(BF16)
    widx_ref[0] = kw[:, LANES:2 * LANES]


def _inproj(x, w_in_pad, w_pool, pool_scale, kv_g, ik_g, ik_b, w_uk):
    B, L, D = x.shape
    tm = min(PROJ_ROWS, L)
    nl = L // tm
    full2 = lambda b, l: (0, 0)
    full3 = lambda b, l: (0, 0, 0)
    return pl.pallas_call(
        _inproj_kernel,
        grid=(B, nl),
        in_specs=[
            pl.BlockSpec((1, tm, D), lambda b, l: (b, l, 0)),
            pl.BlockSpec(w_in_pad.shape, full2),
            pl.BlockSpec(w_pool.shape, full3),
            pl.BlockSpec(pool_scale.shape, full2),
            pl.BlockSpec(kv_g.shape, full2),
            pl.BlockSpec(ik_g.shape, full2),
            pl.BlockSpec(ik_b.shape, full2),
            pl.BlockSpec(w_uk.shape, full3),
        ],
        out_specs=[
            pl.BlockSpec((1, tm, POOL_WIDTH), lambda b, l: (b, l, 0)),
            pl.BlockSpec((1, ATT_HEADS, tm, KV_RANK), lambda b, l: (b, 0, l, 0)),
            pl.BlockSpec((1, tm, 2 * KV_RANK), lambda b, l: (b, l, 0)),
            pl.BlockSpec((1, IDX_HEADS, tm, IDX_HEAD_DIM), lambda b, l: (b, 0, l, 0)),
            pl.BlockSpec((1, tm, IDX_HEAD_DIM), lambda b, l: (b, l, 0)),
            pl.BlockSpec((1, tm, LANES), lambda b, l: (b, l, 0)),
        ],
        out_shape=[
            jax.ShapeDtypeStruct((B, L, POOL_WIDTH), BF16),
            jax.ShapeDtypeStruct((B, ATT_HEADS, L, KV_RANK), BF16),
            jax.ShapeDtypeStruct((B, L, 2 * KV_RANK), BF16),
            jax.ShapeDtypeStruct((B, IDX_HEADS, L, IDX_HEAD_DIM), BF16),
            jax.ShapeDtypeStruct((B, L, IDX_HEAD_DIM), BF16),
            jax.ShapeDtypeStruct((B, L, LANES), F32),
        ],
        scratch_shapes=[pltpu.VMEM((MAX_POOL_WINDOW + tm, POOL_WIDTH), F32)],
        compiler_params=pltpu.CompilerParams(
            dimension_semantics=("arbitrary", "arbitrary"), vmem_limit_bytes=VMEM_LIMIT),
    )(x, w_in_pad, w_pool, pool_scale, kv_g, ik_g, ik_b, w_uk)


def _sortable(score):
    bits = pltpu.bitcast(score, I32)
    return bits ^ ((bits >> 31) & 0x7FFFFFFF)


def _dsa_kernel(qidx_ref, widx_ref, qlat_ref, kidx_ref, ckv_ref, wuv_ref, o_ref,
                key_ref, keyt_ref, w_ref, wb_ref, s_ref, s2_ref, p_ref, p2_ref, a_ref, a2_ref, m_ref, acc_ref, sel_ref,
                *, top_k, idx_bits):
    qi = pl.program_id(1)
    H = ATT_HEADS
    qb = qidx_ref.shape[2]
    kc = s_ref.shape[1]
    qstart = qi * qb
    n_ch = (qstart + qb + kc - 1) // kc
    idx_scale = (IDX_HEAD_DIM ** -0.5) * (IDX_HEADS ** -0.5)

    qpos = qstart + lax.broadcasted_iota(I32, (qb, 1), 0)
    qpos_t = qstart + lax.broadcasted_iota(I32, (1, qb), 1)
    lane_pos = lax.broadcasted_iota(I32, (1, kc), 1)
    row_pos = lax.broadcasted_iota(I32, (kc, 1), 0)

    for h in range(H):
        wb_ref[h * qb:(h + 1) * qb, :] = jnp.broadcast_to(widx_ref[0, :, h:h + 1] * idx_scale, (qb, LANES))

    def pipelined(produce, consume, buf_a, buf_b):
        produce(0, buf_a)
        n_pairs = (n_ch - 1) // 2

        def pair_body(i, carry):
            c = 2 * i
            consume(c, buf_a)
            produce(c + 1, buf_b)
            consume(c + 1, buf_b)
            produce(c + 2, buf_a)
            return carry

        lax.fori_loop(0, n_pairs, pair_body, 0)
        last = 2 * n_pairs

        @pl.when(n_ch - last == 1)
        def _():
            consume(last, buf_a)

        @pl.when(n_ch - last == 2)
        def _():
            consume(last, buf_a)
            produce(last + 1, buf_b)
            consume(last + 1, buf_b)

    def head_scores(c, buf):
        off = pl.multiple_of(c * kc, kc)
        kk = kidx_ref[0, pl.ds(off, kc), :]
        qs = qidx_ref[0].reshape(H * qb, IDX_HEAD_DIM)
        buf[...] = lax.dot_general(qs, kk, _NT, preferred_element_type=F32)

    half = kc // 2
    f_max = (1 << FIELD_BITS) - 1
    guards = jnp.int32(-(1 << 31) + (1 << FIELD_BITS))

    def field1(k):
        return (k >> (32 - FIELD_BITS)) + (1 << (FIELD_BITS - 1))

    def field2(k):
        return (k >> (32 - 2 * FIELD_BITS)) & f_max

    def pack(fa, fb):
        return lax.shift_left(fb, 16) | fa | guards

    def chunk_keys(c, buf):
        keyt = []
        for j in range(kc // LANES):
            cs = slice(j * LANES, (j + 1) * LANES)
            acc = None
            for h in range(H):
                r = slice(h * qb, (h + 1) * qb)
                term = jnp.maximum(buf[r, cs], 0.0) * wb_ref[r, :]
                acc = term if acc is None else acc + term
            score = jnp.where(c * kc + j * LANES + lane_pos[:, 0:LANES] <= qpos, acc, -jnp.inf)
            key = _sortable(score)
            key_ref[c, :, cs] = key
            keyt.append(jnp.transpose(key))
            keyt_ref[c, cs, :] = keyt[-1]
        kt = jnp.concatenate(keyt, axis=0)
        w_ref[c] = pack(field1(kt[0:half, :]), field1(kt[half:kc, :]))

    pipelined(head_scores, chunk_keys, s_ref, s2_ref)

    def count(indicator):
        def body(c, cnt):
            m = indicator(keyt_ref[c], c * kc + row_pos)
            parts = [m[j * 8:(j + 1) * 8, :] for j in range(kc // 8)]
            while len(parts) > 1:
                parts = [parts[j] + parts[j + 1] for j in range(0, len(parts), 2)]
            return cnt + parts[0]
        cnt = lax.fori_loop(0, n_ch, body, jnp.zeros((8, qb), I32))
        return jnp.sum(cnt, axis=0, keepdims=True)

    units = jnp.int32((1 << 16) + 1)
    n_all = jnp.full((1, qb), n_ch * kc, I32)

    def pack_fields(field):
        def body(c, carry):
            w_ref[c] = pack(field(keyt_ref[c, 0:half, :]), field(keyt_ref[c, half:kc, :]))
            return carry
        lax.fori_loop(0, n_ch, body, 0)

    def count_fields(cand):
        cc = cand | lax.shift_left(cand, 16)

        def body(c, cnt):
            g = lax.shift_right_logical(w_ref[c] - cc, FIELD_BITS) & units
            parts = [g[j * 8:(j + 1) * 8, :] for j in range(half // 8)]
            while len(parts) > 1:
                parts = [parts[j] + parts[j + 1] for j in range(0, len(parts), 2)]
            return cnt + parts[0]
        s = jnp.sum(lax.fori_loop(0, n_ch, body, jnp.zeros((8, qb), I32)), axis=0, keepdims=True)
        return (s & 0xFFFF) + lax.shift_right_logical(s, 16)

    def search_bits(first, last, t, n_t, n_above, counter):
        def body(i, carry):
            t, n_t = carry
            cand = t + lax.shift_left(jnp.int32(1), first - i)
            n = n_above + counter(cand)
            ok = n >= top_k
            return jnp.where(ok, cand, t), jnp.where(ok, n, n_t)
        return lax.fori_loop(0, first - last, body, (t, n_t))

    zero = jnp.zeros((1, qb), I32)
    t1, n_t = search_bits(FIELD_BITS - 1, -1, zero, n_all, zero, count_fields)
    n_above = jnp.where(t1 < f_max, count_fields(jnp.minimum(t1 + 1, f_max)), 0)
    pack_fields(lambda k: jnp.where(field1(k) == t1, field2(k), 0))
    t2, n_t = search_bits(FIELD_BITS - 1, SEARCH_CHECK_BIT - 1, zero, n_t, n_above, count_fields)

    def prefix(t2):
        return lax.shift_left(t1 - (1 << (FIELD_BITS - 1)), 32 - FIELD_BITS) | lax.shift_left(t2, 32 - 2 * FIELD_BITS)

    sel_ref[0:1, :] = prefix(t2)
    sel_ref[2:3, :] = n_t
    sel_ref[3:4, :] = t2

    @pl.when(jnp.max(jnp.where(sel_ref[2:3, :] != top_k, 1, 0)) > 0)
    def _():
        t2, n_t = search_bits(SEARCH_CHECK_BIT - 1, -1, sel_ref[3:4, :], sel_ref[2:3, :], n_above, count_fields)
        t, n_t = search_bits(32 - 2 * FIELD_BITS - 1, -1, prefix(t2), n_t, zero,
                             lambda cand: count(lambda k, _: jnp.where(k >= cand, 1, 0)))
        sel_ref[0:1, :] = t
        sel_ref[2:3, :] = n_t

    thr_t = sel_ref[0:1, :]
    n_ge = sel_ref[2:3, :]

    surplus = jnp.where(thr_t > NEG_INF_KEY, jnp.where(n_ge > top_k, 1, 0), 0)
    sel_ref[1:2, :] = qpos_t

    @pl.when(jnp.max(surplus) > 0)
    def _():
        need = top_k - count(lambda k, _: jnp.where(k > thr_t, 1, 0))

        def tie_body(i, m):
            cand = m + lax.shift_left(jnp.int32(1), idx_bits - 1 - i)
            n_before = count(lambda k, p: jnp.where(k == thr_t, jnp.where(p < cand, 1, 0), 0))
            return jnp.where(n_before < need, cand, m)
        last = lax.fori_loop(0, idx_bits, tie_body, jnp.zeros((1, qb), I32))
        sel_ref[1:2, :] = jnp.where(surplus > 0, jnp.minimum(last, qpos_t), qpos_t)

    thr = jnp.transpose(jnp.broadcast_to(sel_ref[0:1, :], (qb, qb)))[:, 0:1]
    tie_last = jnp.transpose(jnp.broadcast_to(sel_ref[1:2, :], (qb, qb)))[:, 0:1]

    def chunk_bias(c):
        key = key_ref[c]
        kpos = c * kc + lane_pos
        tie_bias = jnp.where(key == thr, jnp.where(kpos <= tie_last, 0.0, NEG_BIG), NEG_BIG)
        return jnp.where(key > thr, 0.0, tie_bias)

    n_lt = kc // LANES

    def head_logits(h, ck):
        return lax.dot_general(qlat_ref[0, h], ck, _NT, preferred_element_type=F32)

    m_ref[...] = jnp.full(m_ref.shape, NEG_BIG, F32)
    acc_ref[...] = jnp.zeros(acc_ref.shape, F32)

    def numerators(c, bufs):
        pbuf, abuf = bufs
        off = pl.multiple_of(c * kc, kc)
        ck = ckv_ref[0, pl.ds(off, kc), 0:KV_RANK]
        bias = chunk_bias(c)
        for h in range(H):
            r = slice(h * qb, (h + 1) * qb)
            lg = head_logits(h, ck)
            z = [lg[:, j * LANES:(j + 1) * LANES] + bias[:, j * LANES:(j + 1) * LANES] for j in range(n_lt)]
            zm = z[0]
            for j in range(1, n_lt):
                zm = jnp.maximum(zm, z[j])
            m_old = m_ref[r, :]
            m_new = jnp.maximum(m_old, jnp.max(zm, axis=1, keepdims=True))
            m_ref[r, :] = m_new
            abuf[r, :] = jnp.exp2(m_old - m_new)
            for j in range(n_lt):
                pbuf[r, j * LANES:(j + 1) * LANES] = jnp.exp2(z[j] - m_new).astype(BF16)

    def accumulate(c, bufs):
        pbuf, abuf = bufs
        off = pl.multiple_of(c * kc, kc)
        ckx = ckv_ref[0, pl.ds(off, kc), :]
        pv = jnp.dot(pbuf[...], ckx, preferred_element_type=F32)
        a = abuf[...]
        acc_ref[:, 0:KV_RANK] = a * acc_ref[:, 0:KV_RANK] + pv[:, 0:KV_RANK]
        acc_ref[:, KV_RANK:2 * KV_RANK] = a * acc_ref[:, KV_RANK:2 * KV_RANK] + pv[:, KV_RANK:2 * KV_RANK]

    pipelined(numerators, accumulate, (p_ref, a_ref), (p2_ref, a2_ref))

    o_lat = (acc_ref[:, 0:KV_RANK] / acc_ref[:, KV_RANK:2 * KV_RANK]).astype(BF16)
    for h in range(0, H, 2):
        t = jnp.dot(o_lat[h * qb:(h + 1) * qb, :], wuv_ref[h], preferred_element_type=F32)
        t = t + jnp.dot(o_lat[(h + 1) * qb:(h + 2) * qb, :], wuv_ref[h + 1], preferred_element_type=F32)
        o_ref[0, :, h * ATT_HEAD_DIM:(h + 2) * ATT_HEAD_DIM] = t.astype(BF16)


def _dsa(qidx, widx, qlat, kidx, ckv, wuv_pad):
    B, H, L, _ = qidx.shape
    qb = min(DSA_QUERIES, L)
    kc = min(DSA_KEYS, L)
    top_k = min(TOPK_MAX, L // 4)
    idx_bits = max(1, (L - 1).bit_length())
    assert L % qb == 0 and L % kc == 0 and qb % LANES == 0 and kc % (2 * LANES) == 0
    assert top_k < kc and L < (1 << FIELD_BITS)
    kern = functools.partial(_dsa_kernel, top_k=top_k, idx_bits=idx_bits)
    return pl.pallas_call(
        kern,
        grid=(B, L // qb),
        in_specs=[
            pl.BlockSpec((1, H, qb, IDX_HEAD_DIM), lambda b, q: (b, 0, q, 0)),
            pl.BlockSpec((1, qb, LANES), lambda b, q: (b, q, 0)),
            pl.BlockSpec((1, H, qb, KV_RANK), lambda b, q: (b, 0, q, 0)),
            pl.BlockSpec((1, L, IDX_HEAD_DIM), lambda b, q: (b, 0, 0)),
            pl.BlockSpec((1, L, 2 * KV_RANK), lambda b, q: (b, 0, 0)),
            pl.BlockSpec(wuv_pad.shape, lambda b, q: (0, 0, 0)),
        ],
        out_specs=pl.BlockSpec((1, qb, ATT_WIDTH), lambda b, q: (b, q, 0)),
        out_shape=jax.ShapeDtypeStruct((B, L, ATT_WIDTH), BF16),
        scratch_shapes=[
            pltpu.VMEM((L // kc, qb, kc), I32),
            pltpu.VMEM((L // kc, kc, qb), I32),
            pltpu.VMEM((L // kc, kc // 2, qb), I32),
            pltpu.VMEM((H * qb, LANES), F32),
            pltpu.VMEM((H * qb, kc), F32),
            pltpu.VMEM((H * qb, kc), F32),
            pltpu.VMEM((H * qb, kc), BF16),
            pltpu.VMEM((H * qb, kc), BF16),
            pltpu.VMEM((H * qb, LANES), F32),
            pltpu.VMEM((H * qb, LANES), F32),
            pltpu.VMEM((H * qb, LANES), F32),
            pltpu.VMEM((H * qb, 2 * KV_RANK), F32),
            pltpu.VMEM((8, qb), I32),
        ],
        compiler_params=pltpu.CompilerParams(
            dimension_semantics=("arbitrary", "arbitrary"), vmem_limit_bytes=VMEM_LIMIT),
    )(qidx, widx, qlat, kidx, ckv, wuv_pad)


def _memkv_kernel(mem_ref, w_ref, k_ref, v_ref):
    d = k_ref.shape[2]
    kv = jnp.dot(mem_ref[0].astype(BF16), w_ref[...], preferred_element_type=F32)
    k_ref[0] = kv[:, 0:d].astype(BF16)
    v_ref[0] = kv[:, d:2 * d].astype(BF16)


def _memkv(mem, w_mkv):
    B, M, D = mem.shape
    return pl.pallas_call(
        _memkv_kernel,
        grid=(B,),
        in_specs=[pl.BlockSpec((1, M, D), lambda b: (b, 0, 0)),
                  pl.BlockSpec(w_mkv.shape, lambda b: (0, 0))],
        out_specs=[pl.BlockSpec((1, M, D), lambda b: (b, 0, 0)),
                   pl.BlockSpec((1, M, D), lambda b: (b, 0, 0))],
        out_shape=[jax.ShapeDtypeStruct((B, M, D), BF16), jax.ShapeDtypeStruct((B, M, D), BF16)],
        compiler_params=pltpu.CompilerParams(
            dimension_semantics=("arbitrary",), vmem_limit_bytes=VMEM_LIMIT),
    )(mem, w_mkv)


def _split3(v):
    hi = v.astype(BF16)
    r1 = v - hi.astype(F32)
    mid = r1.astype(BF16)
    lo = (r1 - mid.astype(F32)).astype(BF16)
    return hi, mid, lo


def _mix_kernel(x_ref, yp_ref, ya_ref, wo_ref, g1_ref, b1_ref, km_ref, vm_ref, wq_ref, wmo_ref,
                g2_ref, b2_ref, wr_ref, br_ref, x2_ref, tope_ref, gate_ref):
    tm, d = x_ref.shape
    hd = d // MEM_HEADS
    pw = yp_ref.shape[1]
    ws = _split3(wr_ref[...])

    def stage_mix(rs):
        mix = jnp.dot(yp_ref[rs, :], wo_ref[0:pw, :], preferred_element_type=F32)
        mix = mix + jnp.dot(ya_ref[rs, :], wo_ref[pw:, :], preferred_element_type=F32)
        return _layer_norm(DN_ALPHA * x_ref[rs, :] + mix, g1_ref[...], b1_ref[...])

    def stage_mem(x1):
        q = jnp.dot(x1.astype(BF16), wq_ref[...], preferred_element_type=F32).astype(BF16)
        scale = hd ** -0.5
        att = None
        for h in range(MEM_HEADS):
            c = slice(h * hd, (h + 1) * hd)
            lg = lax.dot_general(q[:, c], km_ref[0, :, c], _NT, preferred_element_type=F32) * scale
            p = jnp.exp(lg - jnp.max(lg, axis=-1, keepdims=True))
            p = p / jnp.sum(p, axis=-1, keepdims=True)
            oh = jnp.dot(p.astype(BF16), vm_ref[0, :, c], preferred_element_type=F32).astype(BF16)
            t = jnp.dot(oh, wmo_ref[c, :], preferred_element_type=F32)
            att = t if att is None else att + t
        return att

    def stage_route(rs, x1, att):
        n = rs.stop - rs.start
        x2 = _layer_norm(DN_ALPHA * x1 + att, g2_ref[...], b2_ref[...])
        x2_ref[rs, :] = x2

        xs = _split3(x2)
        lt = None
        for i, j in ((0, 0), (0, 1), (1, 0)):
            t = lax.dot_general(ws[j], xs[i], _NT, preferred_element_type=F32)
            lt = t if lt is None else lt + t
        lt = lt + br_ref[...]
        n_e = lt.shape[0]
        eidx = lax.broadcasted_iota(I32, lt.shape, 0)
        vals, idxs = [], []
        for _ in range(TOP_K_EXPERTS):
            mx = jnp.max(lt, axis=0, keepdims=True)
            ix = jnp.min(jnp.where(lt == mx, eidx, n_e), axis=0, keepdims=True)
            vals.append(mx)
            idxs.append(ix)
            lt = jnp.where(eidx == ix, -jnp.inf, lt)
        tope_ref[:, rs] = jnp.concatenate(idxs, axis=0)
        ex = [jnp.exp(v - vals[0]) for v in vals]
        den = ex[0]
        for e_ in ex[1:]:
            den = den + e_
        gates = jnp.concatenate([e_ / den for e_ in ex] + [jnp.zeros((LANES - TOP_K_EXPERTS, n), F32)], axis=0)
        gate_ref[rs, :] = jnp.transpose(gates)

    all_rows = slice(0, tm)
    x1 = stage_mix(all_rows)
    stage_route(all_rows, x1, stage_mem(x1))


def _mix(x2d, ypool, yatt, w_o, g1, b1, k_mem, v_mem, w_mq, w_mo, g2, b2, w_rt, b_r, B, L):
    T, D = x2d.shape
    tm = min(MIX_ROWS, L)
    nl = L // tm
    M = k_mem.shape[1]
    row = lambda i: (i, 0)
    full = lambda i: (0, 0)
    return pl.pallas_call(
        _mix_kernel,
        grid=(T // tm,),
        in_specs=[
            pl.BlockSpec((tm, D), row),
            pl.BlockSpec((tm, ypool.shape[1]), row),
            pl.BlockSpec((tm, yatt.shape[1]), row),
            pl.BlockSpec(w_o.shape, full),
            pl.BlockSpec(g1.shape, full),
            pl.BlockSpec(b1.shape, full),
            pl.BlockSpec((1, M, D), lambda i: (i // nl, 0, 0)),
            pl.BlockSpec((1, M, D), lambda i: (i // nl, 0, 0)),
            pl.BlockSpec(w_mq.shape, full),
            pl.BlockSpec(w_mo.shape, full),
            pl.BlockSpec(g2.shape, full),
            pl.BlockSpec(b2.shape, full),
            pl.BlockSpec(w_rt.shape, full),
            pl.BlockSpec(b_r.shape, full),
        ],
        out_specs=[
            pl.BlockSpec((tm, D), row),
            pl.BlockSpec((TOP_K_EXPERTS, tm), lambda i: (0, i)),
            pl.BlockSpec((tm, LANES), row),
        ],
        out_shape=[
            jax.ShapeDtypeStruct((T, D), F32),
            jax.ShapeDtypeStruct((TOP_K_EXPERTS, T), I32),
            jax.ShapeDtypeStruct((T, LANES), F32),
        ],
        compiler_params=pltpu.CompilerParams(
            dimension_semantics=("arbitrary",), vmem_limit_bytes=VMEM_LIMIT),
    )(x2d, ypool, yatt, w_o, g1, b1, k_mem, v_mem, w_mq, w_mo, g2, b2, w_rt, b_r)


def _rank_kernel(tope_ref, rank_ref, cnt_ref, carry_ref):
    i = pl.program_id(0)
    tr = tope_ref.shape[1]

    @pl.when(i == 0)
    def _():
        carry_ref[...] = jnp.zeros(carry_ref.shape, F32)

    eidx = lax.broadcasted_iota(I32, (N_EXPERTS, tr), 0)
    onehot = jnp.zeros((N_EXPERTS, tr), F32)
    for k in range(TOP_K_EXPERTS):
        onehot = onehot + jnp.where(eidx == tope_ref[k:k + 1, :], 1.0, 0.0)
    before = jnp.where(lax.broadcasted_iota(I32, (tr, tr), 0) < lax.broadcasted_iota(I32, (tr, tr), 1), 1.0, 0.0)
    excl = jnp.dot(onehot.astype(BF16), before.astype(BF16), preferred_element_type=F32)
    rank_full = excl + carry_ref[:, 0:1]
    rows = []
    for k in range(TOP_K_EXPERTS):
        rows.append(jnp.sum(jnp.where(eidx == tope_ref[k:k + 1, :], rank_full, 0.0), axis=0, keepdims=True))
    rank_ref[...] = jnp.concatenate(rows, axis=0).astype(I32)
    carry_ref[...] = carry_ref[...] + jnp.sum(onehot, axis=1, keepdims=True)
    cnt_ref[...] = carry_ref[...].astype(I32)


def _rank(tope):
    K, T = tope.shape
    tr = min(RANK_TOKENS, T)
    return pl.pallas_call(
        _rank_kernel,
        grid=(T // tr,),
        in_specs=[pl.BlockSpec((K, tr), lambda i: (0, i))],
        out_specs=[pl.BlockSpec((K, tr), lambda i: (0, i)),
                   pl.BlockSpec((N_EXPERTS, LANES), lambda i: (0, 0))],
        out_shape=[jax.ShapeDtypeStruct((K, T), I32), jax.ShapeDtypeStruct((N_EXPERTS, LANES), I32)],
        scratch_shapes=[pltpu.VMEM((N_EXPERTS, LANES), F32)],
        compiler_params=pltpu.CompilerParams(dimension_semantics=("arbitrary",)),
    )(tope)


def _dest_kernel(start_ref, tope_ref, rank_ref, dest_ref):
    tope = tope_ref[...]
    base = jnp.zeros(tope.shape, I32)
    for e in range(N_EXPERTS):
        base = jnp.where(tope == e, start_ref[e], base)
    dest_ref[...] = base + rank_ref[...]


def _dest(pad_start, tope, rank):
    K, T = tope.shape
    tt = min(DEST_TOKENS, T)
    blk = pl.BlockSpec((K, tt), lambda i: (0, i))
    return pl.pallas_call(
        _dest_kernel,
        grid=(T // tt,),
        in_specs=[pl.BlockSpec(memory_space=pltpu.SMEM), blk, blk],
        out_specs=blk,
        out_shape=jax.ShapeDtypeStruct((K, T), I32),
        compiler_params=pltpu.CompilerParams(dimension_semantics=("arbitrary",)),
    )(pad_start, tope, rank)


def _dispatch_kernel(dest_ref, start_ref, cnt_ref, padded_ref, x_ref, xs_hbm, zrow, sem, zsem):
    i = pl.program_id(0)
    td = dest_ref.shape[1]

    def row_copy(j, dst):
        return pltpu.make_async_copy(x_ref.at[pl.ds(j, 1), :], xs_hbm.at[pl.ds(dst, 1), :], sem)

    def zero_copy(dst):
        return pltpu.make_async_copy(zrow, xs_hbm.at[pl.ds(dst, 1), :], zsem)

    @pl.when(i == 0)
    def _():
        zrow[...] = jnp.zeros(zrow.shape, F32)
        for e in range(N_EXPERTS):
            first = start_ref[e] + cnt_ref[e]
            n_pad = padded_ref[e] - cnt_ref[e]

            def zstart(r, c):
                zero_copy(first + r).start()
                return c
            lax.fori_loop(0, n_pad, zstart, 0)

            def zwait(r, c):
                zero_copy(first + r).wait()
                return c
            lax.fori_loop(0, n_pad, zwait, 0)

    def issue(jj, c):
        for u in range(ISSUE_UNROLL):
            j = jj * ISSUE_UNROLL + u
            for k in range(TOP_K_EXPERTS):
                row_copy(j, dest_ref[k, j]).start()
        return c
    lax.fori_loop(0, td // ISSUE_UNROLL, issue, 0)

    def drain(jj, c):
        for _ in range(ISSUE_UNROLL * TOP_K_EXPERTS):
            row_copy(0, 0).wait()
        return c
    lax.fori_loop(0, td // ISSUE_UNROLL, drain, 0)


def _dispatch(dest, pad_start, counts, padded, x2, n_rows):
    K, T = dest.shape
    D = x2.shape[1]
    td = min(DISPATCH_TOKENS, T)
    smem_tok = pl.BlockSpec((K, td), lambda i: (0, i), memory_space=pltpu.SMEM)
    smem_full = pl.BlockSpec(memory_space=pltpu.SMEM)
    return pl.pallas_call(
        _dispatch_kernel,
        grid=(T // td,),
        in_specs=[smem_tok, smem_full, smem_full, smem_full,
                  pl.BlockSpec((td, D), lambda i: (i, 0))],
        out_specs=pl.BlockSpec(memory_space=pl.ANY),
        out_shape=jax.ShapeDtypeStruct((n_rows, D), F32),
        scratch_shapes=[pltpu.VMEM((1, D), F32), pltpu.SemaphoreType.DMA(()), pltpu.SemaphoreType.DMA(())],
        compiler_params=pltpu.CompilerParams(
            dimension_semantics=("arbitrary",), vmem_limit_bytes=VMEM_LIMIT),
    )(dest, pad_start, counts, padded, x2)


def _expert_kernel(bexp_ref, nused_ref, xs_ref, wgu_ref, bgu_ref, wd_ref, bd_ref, y_ref, wgu_bf, wd_bf):
    j = pl.program_id(0)
    d_ff = wd_ref.shape[1]

    @pl.when(j < nused_ref[0])
    def _():
        prev = bexp_ref[jnp.maximum(j - 1, 0)]

        @pl.when(jnp.logical_or(j == 0, bexp_ref[j] != prev))
        def _():
            wgu_bf[...] = wgu_ref[0].astype(BF16)
            wd_bf[...] = wd_ref[0].astype(BF16)

        xb = xs_ref[...].astype(BF16)
        gu = jnp.dot(xb, wgu_bf[...], preferred_element_type=F32) + bgu_ref[0]
        g = jnp.minimum(gu[:, 0:d_ff], SWIGLU_LIMIT)
        u = jnp.clip(gu[:, d_ff:2 * d_ff], -SWIGLU_LIMIT, SWIGLU_LIMIT)
        act = (u + 1.0) * g * (1.0 / (1.0 + jnp.exp(-SWIGLU_ALPHA * g)))
        y_ref[...] = jnp.dot(act.astype(BF16), wd_bf[...], preferred_element_type=F32) + bd_ref[0]


def _experts(block_exp, n_used, xs, w_gate_up, b_gate_up, w_down, b_down):
    P, D = xs.shape
    blk = EXPERT_ROWS
    nblk = P // blk
    E, _, F2 = w_gate_up.shape
    d_ff = w_down.shape[1]
    row = lambda j, be, nu: (jnp.minimum(j, nu[0] - 1), 0)
    exp3 = lambda j, be, nu: (be[j], 0, 0)
    grid_spec = pltpu.PrefetchScalarGridSpec(
        num_scalar_prefetch=2,
        grid=(nblk,),
        in_specs=[
            pl.BlockSpec((blk, D), row),
            pl.BlockSpec((1, D, F2), exp3),
            pl.BlockSpec((1, 1, F2), exp3),
            pl.BlockSpec((1, d_ff, D), exp3),
            pl.BlockSpec((1, 1, D), exp3),
        ],
        out_specs=pl.BlockSpec((blk, D), row),
        scratch_shapes=[pltpu.VMEM((D, F2), BF16), pltpu.VMEM((d_ff, D), BF16)],
    )
    return pl.pallas_call(
        _expert_kernel,
        grid_spec=grid_spec,
        out_shape=jax.ShapeDtypeStruct((P, D), F32),
        compiler_params=pltpu.CompilerParams(
            dimension_semantics=("arbitrary",), vmem_limit_bytes=VMEM_LIMIT),
    )(block_exp, n_used, xs, w_gate_up, b_gate_up.reshape(E, 1, F2), w_down, b_down.reshape(E, 1, D))


def _combine_kernel(dest_ref, dest_next_ref, x2_ref, gate_ref, g3_ref, b3_ref, y_hbm, o_ref, ybuf, sem):
    i = pl.program_id(0)
    n = pl.num_programs(0)
    tc = x2_ref.shape[0]
    slot = i % 2

    def row_copy(src_ref, s, j, k):
        return pltpu.make_async_copy(y_hbm.at[pl.ds(src_ref[k, j], 1), :], ybuf.at[s, k, pl.ds(j, 1), :], sem.at[s])

    def gather(src_ref, s):
        def issue(jj, c):
            for u in range(ISSUE_UNROLL):
                for k in range(TOP_K_EXPERTS):
                    row_copy(src_ref, s, jj * ISSUE_UNROLL + u, k).start()
            return c
        lax.fori_loop(0, tc // ISSUE_UNROLL, issue, 0)

    @pl.when(i == 0)
    def _():
        gather(dest_ref, 0)

    @pl.when(i + 1 < n)
    def _():
        gather(dest_next_ref, 1 - slot)

    def drain(jj, c):
        for _ in range(ISSUE_UNROLL * TOP_K_EXPERTS):
            pltpu.make_async_copy(y_hbm.at[pl.ds(0, 1), :], ybuf.at[slot, 0, pl.ds(0, 1), :], sem.at[slot]).wait()
        return c
    lax.fori_loop(0, tc // ISSUE_UNROLL, drain, 0)

    moe = None
    for k in range(TOP_K_EXPERTS):
        t = ybuf[slot, k] * gate_ref[:, k:k + 1]
        moe = t if moe is None else moe + t
    o_ref[...] = _layer_norm(DN_ALPHA * x2_ref[...] + moe, g3_ref[...], b3_ref[...])


def _combine(dest, x2, gates, g3, b3, y):
    K, T = dest.shape
    D = x2.shape[1]
    tc = min(COMBINE_TOKENS, T)
    n = T // tc
    row = lambda i: (i, 0)
    full = lambda i: (0, 0)
    return pl.pallas_call(
        _combine_kernel,
        grid=(n,),
        in_specs=[pl.BlockSpec((K, tc), lambda i: (0, i), memory_space=pltpu.SMEM),
                  pl.BlockSpec((K, tc), lambda i: (0, jnp.minimum(i + 1, n - 1)), memory_space=pltpu.SMEM),
                  pl.BlockSpec((tc, D), row), pl.BlockSpec((tc, LANES), row),
                  pl.BlockSpec(g3.shape, full), pl.BlockSpec(b3.shape, full),
                  pl.BlockSpec(memory_space=pl.ANY)],
        out_specs=pl.BlockSpec((tc, D), row),
        out_shape=jax.ShapeDtypeStruct((T, D), F32),
        scratch_shapes=[pltpu.VMEM((2, K, tc, D), F32), pltpu.SemaphoreType.DMA((2,))],
        compiler_params=pltpu.CompilerParams(
            dimension_semantics=("arbitrary",), vmem_limit_bytes=VMEM_LIMIT),
    )(dest, dest, x2, gates, g3, b3, y)


def _pad_cols(w, width):
    return jnp.pad(w, ((0, 0), (0, width - w.shape[1])))


def _layer(x, mem, w_in, w_pool, pool_scale, ik_g, ik_b, kv_g, w_uk, w_uv, w_o, ln1_g, ln1_b,
           w_mq, w_mkv, w_mo, ln2_g, ln2_b, w_router, b_router, w_gate_up, b_gate_up, w_down, b_down,
           ln3_g, ln3_b):
    B, L, D = x.shape
    T = B * L
    row = lambda v: v.reshape(1, -1)

    o = 0
    pieces = []
    for width in (POOL_WIDTH, ATT_WIDTH, KV_RANK, IDX_HEADS * IDX_HEAD_DIM, IDX_HEAD_DIM, IDX_HEADS):
        pieces.append(w_in[:, o:o + width])
        o += width
    pieces[4] = _pad_cols(pieces[4], LANES)
    pieces[5] = _pad_cols(pieces[5], LANES)
    w_in_pad = jnp.concatenate(pieces, axis=1).astype(BF16)
    wuv_pad = jnp.zeros((ATT_HEADS, KV_RANK, 2 * ATT_HEAD_DIM), F32)
    for h in range(ATT_HEADS):
        wuv_pad = wuv_pad.at[h, :, (h % 2) * ATT_HEAD_DIM:(h % 2 + 1) * ATT_HEAD_DIM].set(w_uv[h])

    ypool, qlat, ckv, qidx, kidx, widx = _inproj(
        x, w_in_pad, w_pool.astype(BF16), row(pool_scale), row(kv_g), row(ik_g), row(ik_b), w_uk.astype(BF16))
    yatt = _dsa(qidx, widx, qlat, kidx, ckv, wuv_pad.astype(BF16))
    k_mem, v_mem = _memkv(mem, w_mkv.astype(BF16))
    x2, tope, gates = _mix(
        x.reshape(T, D), ypool.reshape(T, -1), yatt.reshape(T, -1), w_o.astype(BF16), row(ln1_g), row(ln1_b),
        k_mem, v_mem, w_mq.astype(BF16), w_mo.astype(BF16), row(ln2_g), row(ln2_b),
        jnp.transpose(w_router), b_router.reshape(-1, 1), B, L)

    rank, cnt = _rank(tope)
    counts = cnt[:, 0]
    blk = EXPERT_ROWS
    padded = (counts + blk - 1) // blk * blk
    pad_end = jnp.cumsum(padded)
    pad_start = pad_end - padded
    n_rows = T * TOP_K_EXPERTS + N_EXPERTS * blk
    nblk = n_rows // blk
    n_used = (pad_end[-1] // blk).astype(I32)
    blk_first = jnp.minimum(jnp.arange(nblk, dtype=I32), n_used - 1) * blk
    n_ended = jnp.sum((pad_end[None, :] <= blk_first[:, None]).astype(I32), axis=1)
    block_exp = jnp.minimum(n_ended, N_EXPERTS - 1).astype(I32)

    pad_start = pad_start.astype(I32)
    dest = _dest(pad_start, tope, rank)
    xs = _dispatch(dest, pad_start, counts, padded.astype(I32), x2, n_rows)
    y = _experts(block_exp, n_used.reshape(1), xs, w_gate_up, b_gate_up, w_down, b_down)
    out = _combine(dest, x2, gates, row(ln3_g), row(ln3_b), y)
    return out.reshape(B, L, D)


def kernel(x, mem, w_in, w_pool, pool_scale, idx_k_norm_g, idx_k_norm_b, kv_norm_g, w_uk, w_uv, w_o, ln1_g, ln1_b, w_mq, w_mkv, w_mo, ln2_g, ln2_b, w_router, b_router, w_gate_up, b_gate_up, w_down, b_down, ln3_g, ln3_b):
    assert w_in.shape[0] == DEPTH
    return _layer(x, mem, w_in[0], w_pool[0], pool_scale[0], idx_k_norm_g[0], idx_k_norm_b[0], kv_norm_g[0],
                  w_uk[0], w_uv[0], w_o[0], ln1_g[0], ln1_b[0], w_mq[0], w_mkv[0], w_mo[0], ln2_g[0], ln2_b[0],
                  w_router[0], b_router[0], w_gate_up[0], b_gate_up[0], w_down[0], b_down[0], ln3_g[0], ln3_b[0])
```

```python
import functools

import jax
import jax.numpy as jnp
from jax import lax
from jax.experimental import pallas as pl
from jax.experimental.pallas import tpu as pltpu

F32 = jnp.float32
BF16 = jnp.bfloat16
I32 = jnp.int32

POOL_WINDOWS = (2, 4, 8, 16)
POOL_GROUP_DIM = 128
POOL_WIDTH = 512
ATT_HEADS = 8
ATT_HEAD_DIM = 64
ATT_WIDTH = 512
KV_RANK = 128
IDX_HEADS = 8
IDX_HEAD_DIM = 64
TOPK_MAX = 256
MEM_HEADS = 4
N_EXPERTS = 32
TOP_K_EXPERTS = 4
SWIGLU_LIMIT = 7.0
SWIGLU_ALPHA = 1.702
LN_EPS = 1e-5
RMS_EPS = 1e-6
DEPTH = 1
DN_ALPHA = (2 * DEPTH) ** 0.25
MAX_POOL_WINDOW = max(POOL_WINDOWS)

LANES = 128
VMEM_LIMIT = 56 * 1024 * 1024

PROJ_ROWS = 512
DSA_QUERIES = 128
DSA_KEYS = 512
FIELD_BITS = 15
SEARCH_CHECK_BIT = 5
MIX_ROWS = 1024
RANK_TOKENS = 512
DEST_TOKENS = 2048
DISPATCH_TOKENS = 512
ISSUE_UNROLL = 256
EXPERT_ROWS = 256
COMBINE_TOKENS = 256

LOG2_E = 1.4426950408889634
NEG_BIG = -1e30
INT_MIN = -(2 ** 31)
NEG_INF_KEY = INT_MIN + 0x007FFFFF

_NT = (((1,), (1,)), ((), ()))


def _layer_norm(v, g, b):
    mu = jnp.mean(v, axis=-1, keepdims=True)
    d = v - mu
    var = jnp.mean(d * d, axis=-1, keepdims=True)
    return d * lax.rsqrt(var + LN_EPS) * g + b


_C_POOL = 0
_C_Q = 512
_C_CKV = 1024
_C_QIDX = 1152
_C_KIDX = 1664
_C_WIDX = 1792
_IN_PAD = 1920


def _inproj_kernel(x_ref, w_ref, wpool_ref, pscale_ref, kvg_ref, ikg_ref, ikb_ref, wuk_ref,
                   ypool_ref, qlat_ref, ckv_ref, qidx_ref, kidx_ref, widx_ref, ubuf):
    li = pl.program_id(1)
    tm = x_ref.shape[1]
    halo = MAX_POOL_WINDOW
    xb = x_ref[0].astype(BF16)

    u = jnp.dot(xb, w_ref[:, _C_POOL:_C_POOL + POOL_WIDTH], preferred_element_type=F32)

    @pl.when(li == 0)
    def _():
        ubuf[0:halo, :] = jnp.zeros((halo, POOL_WIDTH), F32)

    ubuf[halo:halo + tm, :] = u
    pos = li * tm + lax.broadcasted_iota(I32, (tm, 1), 0)
    for g, w in enumerate(POOL_WINDOWS):
        c0 = g * POOL_GROUP_DIM
        c1 = c0 + POOL_GROUP_DIM
        ug = ubuf[halo:halo + tm, c0:c1]
        s = ug
        for j in range(1, w):
            s = s + ubuf[halo - j:halo - j + tm, c0:c1]
        cnt = jnp.minimum(pos + 1, w).astype(F32)
        d = s / cnt - ug
        y = jnp.dot(d.astype(BF16), wpool_ref[g], preferred_element_type=F32) * pscale_ref[:, c0:c1]
        ypool_ref[0, :, c0:c1] = y.astype(BF16)
    ubuf[0:halo, :] = ubuf[tm:tm + halo, :]

    q = jnp.dot(xb, w_ref[:, _C_Q:_C_Q + ATT_WIDTH], preferred_element_type=F32)
    att_scale = (ATT_HEAD_DIM ** -0.5) * LOG2_E
    for h in range(ATT_HEADS):
        qh = q[:, h * ATT_HEAD_DIM:(h + 1) * ATT_HEAD_DIM].astype(BF16)
        ql = jnp.dot(qh, wuk_ref[h], preferred_element_type=F32) * att_scale
        qlat_ref[0, h] = ql.astype(BF16)

    c = jnp.dot(xb, w_ref[:, _C_CKV:_C_CKV + KV_RANK], preferred_element_type=F32)
    c = c * lax.rsqrt(jnp.mean(c * c, axis=-1, keepdims=True) + RMS_EPS) * kvg_ref[...]
    ckv_ref[0, :, 0:KV_RANK] = c.astype(BF16)
    ckv_ref[0, :, KV_RANK:2 * KV_RANK] = jnp.ones((tm, KV_RANK), BF16)

    qi = jnp.dot(xb, w_ref[:, _C_QIDX:_C_QIDX + IDX_HEADS * IDX_HEAD_DIM], preferred_element_type=F32)
    for h in range(IDX_HEADS):
        qidx_ref[0, h] = qi[:, h * IDX_HEAD_DIM:(h + 1) * IDX_HEAD_DIM].astype(BF16)
    kw = jnp.dot(xb, w_ref[:, _C_KIDX:_C_KIDX + 2 * LANES], preferred_element_type=F32)
    kr = kw[:, 0:IDX_HEAD_DIM]
    kidx_ref[0] = _layer_norm(kr, ikg_ref[...], ikb_ref[...]).astype(BF16)
    widx_ref[0] = kw[:, LANES:2 * LANES]


def _inproj(x, w_in_pad, w_pool, pool_scale, kv_g, ik_g, ik_b, w_uk):
    B, L, D = x.shape
    tm = min(PROJ_ROWS, L)
    nl = L // tm
    full2 = lambda b, l: (0, 0)
    full3 = lambda b, l: (0, 0, 0)
    return pl.pallas_call(
        _inproj_kernel,
        grid=(B, nl),
        in_specs=[
            pl.BlockSpec((1, tm, D), lambda b, l: (b, l, 0)),
            pl.BlockSpec(w_in_pad.shape, full2),
            pl.BlockSpec(w_pool.shape, full3),
            pl.BlockSpec(pool_scale.shape, full2),
            pl.BlockSpec(kv_g.shape, full2),
            pl.BlockSpec(ik_g.shape, full2),
            pl.BlockSpec(ik_b.shape, full2),
            pl.BlockSpec(w_uk.shape, full3),
        ],
        out_specs=[
            pl.BlockSpec((1, tm, POOL_WIDTH), lambda b, l: (b, l, 0)),
            pl.BlockSpec((1, ATT_HEADS, tm, KV_RANK), lambda b, l: (b, 0, l, 0)),
            pl.BlockSpec((1, tm, 2 * KV_RANK), lambda b, l: (b, l, 0)),
            pl.BlockSpec((1, IDX_HEADS, tm, IDX_HEAD_DIM), lambda b, l: (b, 0, l, 0)),
            pl.BlockSpec((1, tm, IDX_HEAD_DIM), lambda b, l: (b, l, 0)),
            pl.BlockSpec((1, tm, LANES), lambda b, l: (b, l, 0)),
        ],
        out_shape=[
            jax.ShapeDtypeStruct((B, L, POOL_WIDTH), BF16),
            jax.ShapeDtypeStruct((B, ATT_HEADS, L, KV_RANK), BF16),
            jax.ShapeDtypeStruct((B, L, 2 * KV_RANK), BF16),
            jax.ShapeDtypeStruct((B, IDX_HEADS, L, IDX_HEAD_DIM), BF16),
            jax.ShapeDtypeStruct((B, L, IDX_HEAD_DIM), BF16),
            jax.ShapeDtypeStruct((B, L, LANES), F32),
        ],
        scratch_shapes=[pltpu.VMEM((MAX_POOL_WINDOW + tm, POOL_WIDTH), F32)],
        compiler_params=pltpu.CompilerParams(
            dimension_semantics=("arbitrary", "arbitrary"), vmem_limit_bytes=VMEM_LIMIT),
    )(x, w_in_pad, w_pool, pool_scale, kv_g, ik_g, ik_b, w_uk)


def _sortable(score):
    bits = pltpu.bitcast(score, I32)
    return bits ^ ((bits >> 31) & 0x7FFFFFFF)


def _dsa_kernel(qidx_ref, widx_ref, qlat_ref, kidx_ref, ckv_ref, wuv_ref, o_ref,
                key_ref, keyt_ref, w_ref, wb_ref, s_ref, s2_ref, p_ref, p2_ref, a_ref, a2_ref, m_ref, acc_ref, sel_ref,
                *, top_k, idx_bits):
    qi = pl.program_id(1)
    H = ATT_HEADS
    qb = qidx_ref.shape[2]
    kc = s_ref.shape[1]
    qstart = qi * qb
    n_ch = (qstart + qb + kc - 1) // kc
    idx_scale = (IDX_HEAD_DIM ** -0.5) * (IDX_HEADS ** -0.5)

    qpos = qstart + lax.broadcasted_iota(I32, (qb, 1), 0)
    qpos_t = qstart + lax.broadcasted_iota(I32, (1, qb), 1)
    lane_pos = lax.broadcasted_iota(I32, (1, kc), 1)
    row_pos = lax.broadcasted_iota(I32, (kc, 1), 0)

    for h in range(H):
        wb_ref[h * qb:(h + 1) * qb, :] = jnp.broadcast_to(widx_ref[0, :, h:h + 1] * idx_scale, (qb, LANES))

    def pipelined(produce, consume, buf_a, buf_b):
        produce(0, buf_a)
        n_pairs = (n_ch - 1) // 2

        def pair_body(i, carry):
            c = 2 * i
            consume(c, buf_a)
            produce(c + 1, buf_b)
            consume(c + 1, buf_b)
            produce(c + 2, buf_a)
            return carry

        lax.fori_loop(0, n_pairs, pair_body, 0)
        last = 2 * n_pairs

        @pl.when(n_ch - last == 1)
        def _():
            consume(last, buf_a)

        @pl.when(n_ch - last == 2)
        def _():
            consume(last, buf_a)
            produce(last + 1, buf_b)
            consume(last + 1, buf_b)

    def head_scores(c, buf):
        off = pl.multiple_of(c * kc, kc)
        kk = kidx_ref[0, pl.ds(off, kc), :]
        qs = qidx_ref[0].reshape(H * qb, IDX_HEAD_DIM)
        buf[...] = lax.dot_general(qs, kk, _NT, preferred_element_type=F32)

    half = kc // 2
    f_max = (1 << FIELD_BITS) - 1
    guards = jnp.int32(-(1 << 31) + (1 << FIELD_BITS))

    def field1(k):
        return (k >> (32 - FIELD_BITS)) + (1 << (FIELD_BITS - 1))

    def field2(k):
        return (k >> (32 - 2 * FIELD_BITS)) & f_max

    def pack(fa, fb):
        return lax.shift_left(fb, 16) | fa | guards

    def chunk_keys(c, buf):
        keyt = []
        for j in range(kc // LANES):
            cs = slice(j * LANES, (j + 1) * LANES)
            acc = None
            for h in range(H):
                r = slice(h * qb, (h + 1) * qb)
                term = jnp.maximum(buf[r, cs], 0.0) * wb_ref[r, :]
                acc = term if acc is None else acc + term
            score = jnp.where(c * kc + j * LANES + lane_pos[:, 0:LANES] <= qpos, acc, -jnp.inf)
            key = _sortable(score)
            key_ref[c, :, cs] = key
            keyt.append(jnp.transpose(key))
            keyt_ref[c, cs, :] = keyt[-1]
        kt = jnp.concatenate(keyt, axis=0)
        w_ref[c] = pack(field1(kt[0:half, :]), field1(kt[half:kc, :]))

    pipelined(head_scores, chunk_keys, s_ref, s2_ref)

    def count(indicator):
        def body(c, cnt):
            m = indicator(keyt_ref[c], c * kc + row_pos)
            parts = [m[j * 8:(j + 1) * 8, :] for j in range(kc // 8)]
            while len(parts) > 1:
                parts = [parts[j] + parts[j + 1] for j in range(0, len(parts), 2)]
            return cnt + parts[0]
        cnt = lax.fori_loop(0, n_ch, body, jnp.zeros((8, qb), I32))
        return jnp.sum(cnt, axis=0, keepdims=True)

    units = jnp.int32((1 << 16) + 1)
    n_all = jnp.full((1, qb), n_ch * kc, I32)

    def pack_fields(field):
        def body(c, carry):
            w_ref[c] = pack(field(keyt_ref[c, 0:half, :]), field(keyt_ref[c, half:kc, :]))
            return carry
        lax.fori_loop(0, n_ch, body, 0)

    def count_fields(cand):
        cc = cand | lax.shift_left(cand, 16)

        def body(c, cnt):
            g = lax.shift_right_logical(w_ref[c] - cc, FIELD_BITS) & units
            parts = [g[j * 8:(j + 1) * 8, :] for j in range(half // 8)]
            while len(parts) > 1:
                parts = [parts[j] + parts[j + 1] for j in range(0, len(parts), 2)]
            return cnt + parts[0]
        s = jnp.sum(lax.fori_loop(0, n_ch, body, jnp.zeros((8, qb), I32)), axis=0, keepdims=True)
        return (s & 0xFFFF) + lax.shift_right_logical(s, 16)

    def search_bits(first, last, t, n_t, n_above, counter):
        def body(i, carry):
            t, n_t = carry
            cand = t + lax.shift_left(jnp.int32(1), first - i)
            n = n_above + counter(cand)
            ok = n >= top_k
            return jnp.where(ok, cand, t), jnp.where(ok, n, n_t)
        return lax.fori_loop(0, first - last, body, (t, n_t))

    zero = jnp.zeros((1, qb), I32)
    t1, n_t = search_bits(FIELD_BITS - 1, -1, zero, n_all, zero, count_fields)
    n_above = jnp.where(t1 < f_max, count_fields(jnp.minimum(t1 + 1, f_max)), 0)
    pack_fields(lambda k: jnp.where(field1(k) == t1, field2(k), 0))
    t2, n_t = search_bits(FIELD_BITS - 1, SEARCH_CHECK_BIT - 1, zero, n_t, n_above, count_fields)

    def prefix(t2):
        return lax.shift_left(t1 - (1 << (FIELD_BITS - 1)), 32 - FIELD_BITS) | lax.shift_left(t2, 32 - 2 * FIELD_BITS)

    sel_ref[0:1, :] = prefix(t2)
    sel_ref[2:3, :] = n_t
    sel_ref[3:4, :] = t2

    @pl.when(jnp.max(jnp.where(sel_ref[2:3, :] != top_k, 1, 0)) > 0)
    def _():
        t2, n_t = search_bits(SEARCH_CHECK_BIT - 1, -1, sel_ref[3:4, :], sel_ref[2:3, :], n_above, count_fields)
        t, n_t = search_bits(32 - 2 * FIELD_BITS - 1, -1, prefix(t2), n_t, zero,
                             lambda cand: count(lambda k, _: jnp.where(k >= cand, 1, 0)))
        sel_ref[0:1, :] = t
        sel_ref[2:3, :] = n_t

    thr_t = sel_ref[0:1, :]
    n_ge = sel_ref[2:3, :]

    surplus = jnp.where(thr_t > NEG_INF_KEY, jnp.where(n_ge > top_k, 1, 0), 0)
    sel_ref[1:2, :] = qpos_t

    @pl.when(jnp.max(surplus) > 0)
    def _():
        need = top_k - count(lambda k, _: jnp.where(k > thr_t, 1, 0))

        def tie_body(i, m):
            cand = m + lax.shift_left(jnp.int32(1), idx_bits - 1 - i)
            n_before = count(lambda k, p: jnp.where(k == thr_t, jnp.where(p < cand, 1, 0), 0))
            return jnp.where(n_before < need, cand, m)
        last = lax.fori_loop(0, idx_bits, tie_body, jnp.zeros((1, qb), I32))
        sel_ref[1:2, :] = jnp.where(surplus > 0, jnp.minimum(last, qpos_t), qpos_t)

    thr = jnp.transpose(jnp.broadcast_to(sel_ref[0:1, :], (qb, qb)))[:, 0:1]
    tie_last = jnp.transpose(jnp.broadcast_to(sel_ref[1:2, :], (qb, qb)))[:, 0:1]

    def chunk_bias(c):
        key = key_ref[c]
        kpos = c * kc + lane_pos
        tie_bias = jnp.where(key == thr, jnp.where(kpos <= tie_last, 0.0, NEG_BIG), NEG_BIG)
        return jnp.where(key > thr, 0.0, tie_bias)

    n_lt = kc // LANES

    def head_logits(h, ck):
        return lax.dot_general(qlat_ref[0, h], ck, _NT, preferred_element_type=F32)

    m_ref[...] = jnp.full(m_ref.shape, NEG_BIG, F32)
    acc_ref[...] = jnp.zeros(acc_ref.shape, F32)

    def numerators(c, bufs):
        pbuf, abuf = bufs
        off = pl.multiple_of(c * kc, kc)
        ck = ckv_ref[0, pl.ds(off, kc), 0:KV_RANK]
        bias = chunk_bias(c)
        for h in range(H):
            r = slice(h * qb, (h + 1) * qb)
            lg = head_logits(h, ck)
            z = [lg[:, j * LANES:(j + 1) * LANES] + bias[:, j * LANES:(j + 1) * LANES] for j in range(n_lt)]
            zm = z[0]
            for j in range(1, n_lt):
                zm = jnp.maximum(zm, z[j])
            m_old = m_ref[r, :]
            m_new = jnp.maximum(m_old, jnp.max(zm, axis=1, keepdims=True))
            m_ref[r, :] = m_new
            abuf[r, :] = jnp.exp2(m_old - m_new)
            for j in range(n_lt):
                pbuf[r, j * LANES:(j + 1) * LANES] = jnp.exp2(z[j] - m_new).astype(BF16)

    def accumulate(c, bufs):
        pbuf, abuf = bufs
        off = pl.multiple_of(c * kc, kc)
        ckx = ckv_ref[0, pl.ds(off, kc), :]
        pv = jnp.dot(pbuf[...], ckx, preferred_element_type=F32)
        a = abuf[...]
        acc_ref[:, 0:KV_RANK] = a * acc_ref[:, 0:KV_RANK] + pv[:, 0:KV_RANK]
        acc_ref[:, KV_RANK:2 * KV_RANK] = a * acc_ref[:, KV_RANK:2 * KV_RANK] + pv[:, KV_RANK:2 * KV_RANK]

    pipelined(numerators, accumulate, (p_ref, a_ref), (p2_ref, a2_ref))

    o_lat = (acc_ref[:, 0:KV_RANK] / acc_ref[:, KV_RANK:2 * KV_RANK]).astype(BF16)
    for h in range(0, H, 2):
        t = jnp.dot(o_lat[h * qb:(h + 1) * qb, :], wuv_ref[h], preferred_element_type=F32)
        t = t + jnp.dot(o_lat[(h + 1) * qb:(h + 2) * qb, :], wuv_ref[h + 1], preferred_element_type=F32)
        o_ref[0, :, h * ATT_HEAD_DIM:(h + 2) * ATT_HEAD_DIM] = t.astype(BF16)


def _dsa(qidx, widx, qlat, kidx, ckv, wuv_pad):
    B, H, L, _ = qidx.shape
    qb = min(DSA_QUERIES, L)
    kc = min(DSA_KEYS, L)
    top_k = min(TOPK_MAX, L // 4)
    idx_bits = max(1, (L - 1).bit_length())
    assert L % qb == 0 and L % kc == 0 and qb % LANES == 0 and kc % (2 * LANES) == 0
    assert top_k < kc and L < (1 << FIELD_BITS)
    kern = functools.partial(_dsa_kernel, top_k=top_k, idx_bits=idx_bits)
    return pl.pallas_call(
        kern,
        grid=(B, L // qb),
        in_specs=[
            pl.BlockSpec((1, H, qb, IDX_HEAD_DIM), lambda b, q: (b, 0, q, 0)),
            pl.BlockSpec((1, qb, LANES), lambda b, q: (b, q, 0)),
            pl.BlockSpec((1, H, qb, KV_RANK), lambda b, q: (b, 0, q, 0)),
            pl.BlockSpec((1, L, IDX_HEAD_DIM), lambda b, q: (b, 0, 0)),
            pl.BlockSpec((1, L, 2 * KV_RANK), lambda b, q: (b, 0, 0)),
            pl.BlockSpec(wuv_pad.shape, lambda b, q: (0, 0, 0)),
        ],
        out_specs=pl.BlockSpec((1, qb, ATT_WIDTH), lambda b, q: (b, q, 0)),
        out_shape=jax.ShapeDtypeStruct((B, L, ATT_WIDTH), BF16),
        scratch_shapes=[
            pltpu.VMEM((L // kc, qb, kc), I32),
            pltpu.VMEM((L // kc, kc, qb), I32),
            pltpu.VMEM((L // kc, kc // 2, qb), I32),
            pltpu.VMEM((H * qb, LANES), F32),
            pltpu.VMEM((H * qb, kc), F32),
            pltpu.VMEM((H * qb, kc), F32),
            pltpu.VMEM((H * qb, kc), BF16),
            pltpu.VMEM((H * qb, kc), BF16),
            pltpu.VMEM((H * qb, LANES), F32),
            pltpu.VMEM((H * qb, LANES), F32),
            pltpu.VMEM((H * qb, LANES), F32),
            pltpu.VMEM((H * qb, 2 * KV_RANK), F32),
            pltpu.VMEM((8, qb), I32),
        ],
        compiler_params=pltpu.CompilerParams(
            dimension_semantics=("arbitrary", "arbitrary"), vmem_limit_bytes=VMEM_LIMIT),
    )(qidx, widx, qlat, kidx, ckv, wuv_pad)


def _memkv_kernel(mem_ref, w_ref, k_ref, v_ref):
    d = k_ref.shape[2]
    kv = jnp.dot(mem_ref[0].astype(BF16), w_ref[...], preferred_element_type=F32)
    k_ref[0] = kv[:, 0:d].astype(BF16)
    v_ref[0] = kv[:, d:2 * d].astype(BF16)


def _memkv(mem, w_mkv):
    B, M, D = mem.shape
    return pl.pallas_call(
        _memkv_kernel,
        grid=(B,),
        in_specs=[pl.BlockSpec((1, M, D), lambda b: (b, 0, 0)),
                  pl.BlockSpec(w_mkv.shape, lambda b: (0, 0))],
        out_specs=[pl.BlockSpec((1, M, D), lambda b: (b, 0, 0)),
                   pl.BlockSpec((1, M, D), lambda b: (b, 0, 0))],
        out_shape=[jax.ShapeDtypeStruct((B, M, D), BF16), jax.ShapeDtypeStruct((B, M, D), BF16)],
        compiler_params=pltpu.CompilerParams(
            dimension_semantics=("arbitrary",), vmem_limit_bytes=VMEM_LIMIT),
    )(mem, w_mkv)


def _split3(v):
    hi = v.astype(BF16)
    r1 = v - hi.astype(F32)
    mid = r1.astype(BF16)
    lo = (r1 - mid.astype(F32)).astype(BF16)
    return hi, mid, lo


def _mix_kernel(x_ref, yp_ref, ya_ref, wo_ref, g1_ref, b1_ref, km_ref, vm_ref, wq_ref, wmo_ref,
                g2_ref, b2_ref, wr_ref, br_ref, x2_ref, tope_ref, gate_ref):
    tm, d = x_ref.shape
    hd = d // MEM_HEADS
    pw = yp_ref.shape[1]
    ws = _split3(wr_ref[...])

    def stage_mix(rs):
        mix = jnp.dot(yp_ref[rs, :], wo_ref[0:pw, :], preferred_element_type=F32)
        mix = mix + jnp.dot(ya_ref[rs, :], wo_ref[pw:, :], preferred_element_type=F32)
        return _layer_norm(DN_ALPHA * x_ref[rs, :] + mix, g1_ref[...], b1_ref[...])

    def stage_mem(x1):
        q = jnp.dot(x1.astype(BF16), wq_ref[...], preferred_element_type=F32).astype(BF16)
        scale = hd ** -0.5
        att = None
        for h in range(MEM_HEADS):
            c = slice(h * hd, (h + 1) * hd)
            lg = lax.dot_general(q[:, c], km_ref[0, :, c], _NT, preferred_element_type=F32) * scale
            p = jnp.exp(lg - jnp.max(lg, axis=-1, keepdims=True))
            p = p / jnp.sum(p, axis=-1, keepdims=True)
            oh = jnp.dot(p.astype(BF16), vm_ref[0, :, c], preferred_element_type=F32).astype(BF16)
            t = jnp.dot(oh, wmo_ref[c, :], preferred_element_type=F32)
            att = t if att is None else att + t
        return att

    def stage_route(rs, x1, att):
        n = rs.stop - rs.start
        x2 = _layer_norm(DN_ALPHA * x1 + att, g2_ref[...], b2_ref[...])
        x2_ref[rs, :] = x2

        xs = _split3(x2)
        lt = None
        for i, j in ((0, 0), (0, 1), (1, 0)):
            t = lax.dot_general(ws[j], xs[i], _NT, preferred_element_type=F32)
            lt = t if lt is None else lt + t
        lt = lt + br_ref[...]
        n_e = lt.shape[0]
        eidx = lax.broadcasted_iota(I32, lt.shape, 0)
        vals, idxs = [], []
        for _ in range(TOP_K_EXPERTS):
            mx = jnp.max(lt, axis=0, keepdims=True)
            ix = jnp.min(jnp.where(lt == mx, eidx, n_e), axis=0, keepdims=True)
            vals.append(mx)
            idxs.append(ix)
            lt = jnp.where(eidx == ix, -jnp.inf, lt)
        tope_ref[:, rs] = jnp.concatenate(idxs, axis=0)
        ex = [jnp.exp(v - vals[0]) for v in vals]
        den = ex[0]
        for e_ in ex[1:]:
            den = den + e_
        gates = jnp.concatenate([e_ / den for e_ in ex] + [jnp.zeros((LANES - TOP_K_EXPERTS, n), F32)], axis=0)
        gate_ref[rs, :] = jnp.transpose(gates)

    all_rows = slice(0, tm)
    x1 = stage_mix(all_rows)
    stage_route(all_rows, x1, stage_mem(x1))


def _mix(x2d, ypool, yatt, w_o, g1, b1, k_mem, v_mem, w_mq, w_mo, g2, b2, w_rt, b_r, B, L):
    T, D = x2d.shape
    tm = min(MIX_ROWS, L)
    nl = L // tm
    M = k_mem.shape[1]
    row = lambda i: (i, 0)
    full = lambda i: (0, 0)
    return pl.pallas_call(
        _mix_kernel,
        grid=(T // tm,),
        in_specs=[
            pl.BlockSpec((tm, D), row),
            pl.BlockSpec((tm, ypool.shape[1]), row),
            pl.BlockSpec((tm, yatt.shape[1]), row),
            pl.BlockSpec(w_o.shape, full),
            pl.BlockSpec(g1.shape, full),
            pl.BlockSpec(b1.shape, full),
            pl.BlockSpec((1, M, D), lambda i: (i // nl, 0, 0)),
            pl.BlockSpec((1, M, D), lambda i: (i // nl, 0, 0)),
            pl.BlockSpec(w_mq.shape, full),
            pl.BlockSpec(w_mo.shape, full),
            pl.BlockSpec(g2.shape, full),
            pl.BlockSpec(b2.shape, full),
            pl.BlockSpec(w_rt.shape, full),
            pl.BlockSpec(b_r.shape, full),
        ],
        out_specs=[
            pl.BlockSpec((tm, D), row),
            pl.BlockSpec((TOP_K_EXPERTS, tm), lambda i: (0, i)),
            pl.BlockSpec((tm, LANES), row),
        ],
        out_shape=[
            jax.ShapeDtypeStruct((T, D), F32),
            jax.ShapeDtypeStruct((TOP_K_EXPERTS, T), I32),
            jax.ShapeDtypeStruct((T, LANES), F32),
        ],
        compiler_params=pltpu.CompilerParams(
            dimension_semantics=("arbitrary",), vmem_limit_bytes=VMEM_LIMIT),
    )(x2d, ypool, yatt, w_o, g1, b1, k_mem, v_mem, w_mq, w_mo, g2, b2, w_rt, b_r)


def _rank_kernel(tope_ref, rank_ref, cnt_ref, carry_ref):
    i = pl.program_id(0)
    tr = tope_ref.shape[1]

    @pl.when(i == 0)
    def _():
        carry_ref[...] = jnp.zeros(carry_ref.shape, F32)

    eidx = lax.broadcasted_iota(I32, (N_EXPERTS, tr), 0)
    onehot = jnp.zeros((N_EXPERTS, tr), F32)
    for k in range(TOP_K_EXPERTS):
        onehot = onehot + jnp.where(eidx == tope_ref[k:k + 1, :], 1.0, 0.0)
    before = jnp.where(lax.broadcasted_iota(I32, (tr, tr), 0) < lax.broadcasted_iota(I32, (tr, tr), 1), 1.0, 0.0)
    excl = jnp.dot(onehot.astype(BF16), before.astype(BF16), preferred_element_type=F32)
    rank_full = excl + carry_ref[:, 0:1]
    rows = []
    for k in range(TOP_K_EXPERTS):
        rows.append(jnp.sum(jnp.where(eidx == tope_ref[k:k + 1, :], rank_full, 0.0), axis=0, keepdims=True))
    rank_ref[...] = jnp.concatenate(rows, axis=0).astype(I32)
    carry_ref[...] = carry_ref[...] + jnp.sum(onehot, axis=1, keepdims=True)
    cnt_ref[...] = carry_ref[...].astype(I32)


def _rank(tope):
    K, T = tope.shape
    tr = min(RANK_TOKENS, T)
    return pl.pallas_call(
        _rank_kernel,
        grid=(T // tr,),
        in_specs=[pl.BlockSpec((K, tr), lambda i: (0, i))],
        out_specs=[pl.BlockSpec((K, tr), lambda i: (0, i)),
                   pl.BlockSpec((N_EXPERTS, LANES), lambda i: (0, 0))],
        out_shape=[jax.ShapeDtypeStruct((K, T), I32), jax.ShapeDtypeStruct((N_EXPERTS, LANES), I32)],
        scratch_shapes=[pltpu.VMEM((N_EXPERTS, LANES), F32)],
        compiler_params=pltpu.CompilerParams(dimension_semantics=("arbitrary",)),
    )(tope)


def _dest_kernel(start_ref, tope_ref, rank_ref, dest_ref):
    tope = tope_ref[...]
    base = jnp.zeros(tope.shape, I32)
    for e in range(N_EXPERTS):
        base = jnp.where(tope == e, start_ref[e], base)
    dest_ref[...] = base + rank_ref[...]


def _dest(pad_start, tope, rank):
    K, T = tope.shape
    tt = min(DEST_TOKENS, T)
    blk = pl.BlockSpec((K, tt), lambda i: (0, i))
    return pl.pallas_call(
        _dest_kernel,
        grid=(T // tt,),
        in_specs=[pl.BlockSpec(memory_space=pltpu.SMEM), blk, blk],
        out_specs=blk,
        out_shape=jax.ShapeDtypeStruct((K, T), I32),
        compiler_params=pltpu.CompilerParams(dimension_semantics=("arbitrary",)),
    )(pad_start, tope, rank)


def _dispatch_kernel(dest_ref, start_ref, cnt_ref, padded_ref, x_ref, xs_hbm, zrow, sem, zsem):
    i = pl.program_id(0)
    td = dest_ref.shape[1]

    def row_copy(j, dst):
        return pltpu.make_async_copy(x_ref.at[pl.ds(j, 1), :], xs_hbm.at[pl.ds(dst, 1), :], sem)

    def zero_copy(dst):
        return pltpu.make_async_copy(zrow, xs_hbm.at[pl.ds(dst, 1), :], zsem)

    @pl.when(i == 0)
    def _():
        zrow[...] = jnp.zeros(zrow.shape, F32)
        for e in range(N_EXPERTS):
            first = start_ref[e] + cnt_ref[e]
            n_pad = padded_ref[e] - cnt_ref[e]

            def zstart(r, c):
                zero_copy(first + r).start()
                return c
            lax.fori_loop(0, n_pad, zstart, 0)

            def zwait(r, c):
                zero_copy(first + r).wait()
                return c
            lax.fori_loop(0, n_pad, zwait, 0)

    def issue(jj, c):
        for u in range(ISSUE_UNROLL):
            j = jj * ISSUE_UNROLL + u
            for k in range(TOP_K_EXPERTS):
                row_copy(j, dest_ref[k, j]).start()
        return c
    lax.fori_loop(0, td // ISSUE_UNROLL, issue, 0)

    def drain(jj, c):
        for _ in range(ISSUE_UNROLL * TOP_K_EXPERTS):
            row_copy(0, 0).wait()
        return c
    lax.fori_loop(0, td // ISSUE_UNROLL, drain, 0)


def _dispatch(dest, pad_start, counts, padded, x2, n_rows):
    K, T = dest.shape
    D = x2.shape[1]
    td = min(DISPATCH_TOKENS, T)
    smem_tok = pl.BlockSpec((K, td), lambda i: (0, i), memory_space=pltpu.SMEM)
    smem_full = pl.BlockSpec(memory_space=pltpu.SMEM)
    return pl.pallas_call(
        _dispatch_kernel,
        grid=(T // td,),
        in_specs=[smem_tok, smem_full, smem_full, smem_full,
                  pl.BlockSpec((td, D), lambda i: (i, 0))],
        out_specs=pl.BlockSpec(memory_space=pl.ANY),
        out_shape=jax.ShapeDtypeStruct((n_rows, D), F32),
        scratch_shapes=[pltpu.VMEM((1, D), F32), pltpu.SemaphoreType.DMA(()), pltpu.SemaphoreType.DMA(())],
        compiler_params=pltpu.CompilerParams(
            dimension_semantics=("arbitrary",), vmem_limit_bytes=VMEM_LIMIT),
    )(dest, pad_start, counts, padded, x2)


def _expert_kernel(bexp_ref, nused_ref, xs_ref, wgu_ref, bgu_ref, wd_ref, bd_ref, y_ref, wgu_bf, wd_bf):
    j = pl.program_id(0)
    d_ff = wd_ref.shape[1]

    @pl.when(j < nused_ref[0])
    def _():
        prev = bexp_ref[jnp.maximum(j - 1, 0)]

        @pl.when(jnp.logical_or(j == 0, bexp_ref[j] != prev))
        def _():
            wgu_bf[...] = wgu_ref[0].astype(BF16)
            wd_bf[...] = wd_ref[0].astype(BF16)

        xb = xs_ref[...].astype(BF16)
        gu = jnp.dot(xb, wgu_bf[...], preferred_element_type=F32) + bgu_ref[0]
        g = jnp.minimum(gu[:, 0:d_ff], SWIGLU_LIMIT)
        u = jnp.clip(gu[:, d_ff:2 * d_ff], -SWIGLU_LIMIT, SWIGLU_LIMIT)
        act = (u + 1.0) * g * (1.0 / (1.0 + jnp.exp(-SWIGLU_ALPHA * g)))
        y_ref[...] = jnp.dot(act.astype(BF16), wd_bf[...], preferred_element_type=F32) + bd_ref[0]


def _experts(block_exp, n_used, xs, w_gate_up, b_gate_up, w_down, b_down):
    P, D = xs.shape
    blk = EXPERT_ROWS
    nblk = P // blk
    E, _, F2 = w_gate_up.shape
    d_ff = w_down.shape[1]
    row = lambda j, be, nu: (jnp.minimum(j, nu[0] - 1), 0)
    exp3 = lambda j, be, nu: (be[j], 0, 0)
    grid_spec = pltpu.PrefetchScalarGridSpec(
        num_scalar_prefetch=2,
        grid=(nblk,),
        in_specs=[
            pl.BlockSpec((blk, D), row),
            pl.BlockSpec((1, D, F2), exp3),
            pl.BlockSpec((1, 1, F2), exp3),
            pl.BlockSpec((1, d_ff, D), exp3),
            pl.BlockSpec((1, 1, D), exp3),
        ],
        out_specs=pl.BlockSpec((blk, D), row),
        scratch_shapes=[pltpu.VMEM((D, F2), BF16), pltpu.VMEM((d_ff, D), BF16)],
    )
    return pl.pallas_call(
        _expert_kernel,
        grid_spec=grid_spec,
        out_shape=jax.ShapeDtypeStruct((P, D), F32),
        compiler_params=pltpu.CompilerParams(
            dimension_semantics=("arbitrary",), vmem_limit_bytes=VMEM_LIMIT),
    )(block_exp, n_used, xs, w_gate_up, b_gate_up.reshape(E, 1, F2), w_down, b_down.reshape(E, 1, D))


def _combine_kernel(dest_ref, dest_next_ref, x2_ref, gate_ref, g3_ref, b3_ref, y_hbm, o_ref, ybuf, sem):
    i = pl.program_id(0)
    n = pl.num_programs(0)
    tc = x2_ref.shape[0]
    slot = i % 2

    def row_copy(src_ref, s, j, k):
        return pltpu.make_async_copy(y_hbm.at[pl.ds(src_ref[k, j], 1), :], ybuf.at[s, k, pl.ds(j, 1), :], sem.at[s])

    def gather(src_ref, s):
        def issue(jj, c):
            for u in range(ISSUE_UNROLL):
                for k in range(TOP_K_EXPERTS):
                    row_copy(src_ref, s, jj * ISSUE_UNROLL + u, k).start()
            return c
        lax.fori_loop(0, tc // ISSUE_UNROLL, issue, 0)

    @pl.when(i == 0)
    def _():
        gather(dest_ref, 0)

    @pl.when(i + 1 < n)
    def _():
        gather(dest_next_ref, 1 - slot)

    def drain(jj, c):
        for _ in range(ISSUE_UNROLL * TOP_K_EXPERTS):
            pltpu.make_async_copy(y_hbm.at[pl.ds(0, 1), :], ybuf.at[slot, 0, pl.ds(0, 1), :], sem.at[slot]).wait()
        return c
    lax.fori_loop(0, tc // ISSUE_UNROLL, drain, 0)

    moe = None
    for k in range(TOP_K_EXPERTS):
        t = ybuf[slot, k] * gate_ref[:, k:k + 1]
        moe = t if moe is None else moe + t
    o_ref[...] = _layer_norm(DN_ALPHA * x2_ref[...] + moe, g3_ref[...], b3_ref[...])


def _combine(dest, x2, gates, g3, b3, y):
    K, T = dest.shape
    D = x2.shape[1]
    tc = min(COMBINE_TOKENS, T)
    n = T // tc
    row = lambda i: (i, 0)
    full = lambda i: (0, 0)
    return pl.pallas_call(
        _combine_kernel,
        grid=(n,),
        in_specs=[pl.BlockSpec((K, tc), lambda i: (0, i), memory_space=pltpu.SMEM),
                  pl.BlockSpec((K, tc), lambda i: (0, jnp.minimum(i + 1, n - 1)), memory_space=pltpu.SMEM),
                  pl.BlockSpec((tc, D), row), pl.BlockSpec((tc, LANES), row),
                  pl.BlockSpec(g3.shape, full), pl.BlockSpec(b3.shape, full),
                  pl.BlockSpec(memory_space=pl.ANY)],
        out_specs=pl.BlockSpec((tc, D), row),
        out_shape=jax.ShapeDtypeStruct((T, D), F32),
        scratch_shapes=[pltpu.VMEM((2, K, tc, D), F32), pltpu.SemaphoreType.DMA((2,))],
        compiler_params=pltpu.CompilerParams(
            dimension_semantics=("arbitrary",), vmem_limit_bytes=VMEM_LIMIT),
    )(dest, dest, x2, gates, g3, b3, y)


def _pad_cols(w, width):
    return jnp.pad(w, ((0, 0), (0, width - w.shape[1])))


def _layer(x, mem, w_in, w_pool, pool_scale, ik_g, ik_b, kv_g, w_uk, w_uv, w_o, ln1_g, ln1_b,
           w_mq, w_mkv, w_mo, ln2_g, ln2_b, w_router, b_router, w_gate_up, b_gate_up, w_down, b_down,
           ln3_g, ln3_b):
    B, L, D = x.shape
    T = B * L
    row = lambda v: v.reshape(1, -1)

    o = 0
    pieces = []
    for width in (POOL_WIDTH, ATT_WIDTH, KV_RANK, IDX_HEADS * IDX_HEAD_DIM, IDX_HEAD_DIM, IDX_HEADS):
        pieces.append(w_in[:, o:o + width])
        o += width
    pieces[4] = _pad_cols(pieces[4], LANES)
    pieces[5] = _pad_cols(pieces[5], LANES)
    w_in_pad = jnp.concatenate(pieces, axis=1).astype(BF16)
    wuv_pad = jnp.zeros((ATT_HEADS, KV_RANK, 2 * ATT_HEAD_DIM), F32)
    for h in range(ATT_HEADS):
        wuv_pad = wuv_pad.at[h, :, (h % 2) * ATT_HEAD_DIM:(h % 2 + 1) * ATT_HEAD_DIM].set(w_uv[h])

    ypool, qlat, ckv, qidx, kidx, widx = _inproj(
        x, w_in_pad, w_pool.astype(BF16), row(pool_scale), row(kv_g), row(ik_g), row(ik_b), w_uk.astype(BF16))
    yatt = _dsa(qidx, widx, qlat, kidx, ckv, wuv_pad.astype(BF16))
    k_mem, v_mem = _memkv(mem, w_mkv.astype(BF16))
    x2, tope, gates = _mix(
        x.reshape(T, D), ypool.reshape(T, -1), yatt.reshape(T, -1), w_o.astype(BF16), row(ln1_g), row(ln1_b),
        k_mem, v_mem, w_mq.astype(BF16), w_mo.astype(BF16), row(ln2_g), row(ln2_b),
        jnp.transpose(w_router), b_router.reshape(-1, 1), B, L)

    rank, cnt = _rank(tope)
    counts = cnt[:, 0]
    blk = EXPERT_ROWS
    padded = (counts + blk - 1) // blk * blk
    pad_end = jnp.cumsum(padded)
    pad_start = pad_end - padded
    n_rows = T * TOP_K_EXPERTS + N_EXPERTS * blk
    nblk = n_rows // blk
    n_used = (pad_end[-1] // blk).astype(I32)
    blk_first = jnp.minimum(jnp.arange(nblk, dtype=I32), n_used - 1) * blk
    n_ended = jnp.sum((pad_end[None, :] <= blk_first[:, None]).astype(I32), axis=1)
    block_exp = jnp.minimum(n_ended, N_EXPERTS - 1).astype(I32)

    pad_start = pad_start.astype(I32)
    dest = _dest(pad_start, tope, rank)
    xs = _dispatch(dest, pad_start, counts, padded.astype(I32), x2, n_rows)
    y = _experts(block_exp, n_used.reshape(1), xs, w_gate_up, b_gate_up, w_down, b_down)
    out = _combine(dest, x2, gates, row(ln3_g), row(ln3_b), y)
    return out.reshape(B, L, D)


def kernel(x, mem, w_in, w_pool, pool_scale, idx_k_norm_g, idx_k_norm_b, kv_norm_g, w_uk, w_uv, w_o, ln1_g, ln1_b, w_mq, w_mkv, w_mo, ln2_g, ln2_b, w_router, b_router, w_gate_up, b_gate_up, w_down, b_down, ln3_g, ln3_b):
    assert w_in.shape[0] == DEPTH
    return _layer(x, mem, w_in[0], w_pool[0], pool_scale[0], idx_k_norm_g[0], idx_k_norm_b[0], kv_norm_g[0],
                  w_uk[0], w_uv[0], w_o[0], ln1_g[0], ln1_b[0], w_mq[0], w_mkv[0], w_mo[0], ln2_g[0], ln2_b[0],
                  w_router[0], b_router[0], w_gate_up[0], b_gate_up[0], w_down[0], b_down[0], ln3_g[0], ln3_b[0])
```

```python
import functools

import jax
import jax.numpy as jnp
from jax import lax
from jax.experimental import pallas as pl
from jax.experimental.pallas import tpu as pltpu

F32 = jnp.float32
BF16 = jnp.bfloat16
I32 = jnp.int32

POOL_WINDOWS = (2, 4, 8, 16)
POOL_GROUP_DIM = 128
POOL_WIDTH = 512
ATT_HEADS = 8
ATT_HEAD_DIM = 64
ATT_WIDTH = 512
KV_RANK = 128
IDX_HEADS = 8
IDX_HEAD_DIM = 64
TOPK_MAX = 256
MEM_HEADS = 4
N_EXPERTS = 32
TOP_K_EXPERTS = 4
SWIGLU_LIMIT = 7.0
SWIGLU_ALPHA = 1.702
LN_EPS = 1e-5
RMS_EPS = 1e-6
DEPTH = 1
DN_ALPHA = (2 * DEPTH) ** 0.25
MAX_POOL_WINDOW = max(POOL_WINDOWS)

LANES = 128
VMEM_LIMIT = 56 * 1024 * 1024

PROJ_ROWS = 512
DSA_QUERIES = 128
DSA_KEYS = 512
FIELD_BITS = 15
SEARCH_CHECK_BIT = 5
MIX_ROWS = 1024
RANK_TOKENS = 512
DEST_TOKENS = 2048
DISPATCH_TOKENS = 512
ISSUE_UNROLL = 256
EXPERT_ROWS = 256
COMBINE_TOKENS = 256

LOG2_E = 1.4426950408889634
NEG_BIG = -1e30
INT_MIN = -(2 ** 31)
NEG_INF_KEY = INT_MIN + 0x007FFFFF

_NT = (((1,), (1,)), ((), ()))


def _layer_norm(v, g, b):
    mu = jnp.mean(v, axis=-1, keepdims=True)
    d = v - mu
    var = jnp.mean(d * d, axis=-1, keepdims=True)
    return d * lax.rsqrt(var + LN_EPS) * g + b


_C_POOL = 0
_C_Q = 512
_C_CKV = 1024
_C_QIDX = 1152
_C_KIDX = 1664
_C_WIDX = 1792
_IN_PAD = 1920


def _inproj_kernel(x_ref, w_ref, wpool_ref, pscale_ref, kvg_ref, ikg_ref, ikb_ref, wuk_ref,
                   ypool_ref, qlat_ref, ckv_ref, qidx_ref, kidx_ref, widx_ref, ubuf):
    li = pl.program_id(1)
    tm = x_ref.shape[1]
    halo = MAX_POOL_WINDOW
    xb = x_ref[0].astype(BF16)

    u = jnp.dot(xb, w_ref[:, _C_POOL:_C_POOL + POOL_WIDTH], preferred_element_type=F32)

    @pl.when(li == 0)
    def _():
        ubuf[0:halo, :] = jnp.zeros((halo, POOL_WIDTH), F32)

    ubuf[halo:halo + tm, :] = u
    pos = li * tm + lax.broadcasted_iota(I32, (tm, 1), 0)
    for g, w in enumerate(POOL_WINDOWS):
        c0 = g * POOL_GROUP_DIM
        c1 = c0 + POOL_GROUP_DIM
        ug = ubuf[halo:halo + tm, c0:c1]
        s = ug
        for j in range(1, w):
            s = s + ubuf[halo - j:halo - j + tm, c0:c1]
        cnt = jnp.minimum(pos + 1, w).astype(F32)
        d = s / cnt - ug
        y = jnp.dot(d.astype(BF16), wpool_ref[g], preferred_element_type=F32) * pscale_ref[:, c0:c1]
        ypool_ref[0, :, c0:c1] = y.astype(BF16)
    ubuf[0:halo, :] = ubuf[tm:tm + halo, :]

    q = jnp.dot(xb, w_ref[:, _C_Q:_C_Q + ATT_WIDTH], preferred_element_type=F32)
    att_scale = (ATT_HEAD_DIM ** -0.5) * LOG2_E
    for h in range(ATT_HEADS):
        qh = q[:, h * ATT_HEAD_DIM:(h + 1) * ATT_HEAD_DIM].astype(BF16)
        ql = jnp.dot(qh, wuk_ref[h], preferred_element_type=F32) * att_scale
        qlat_ref[0, h] = ql.astype(BF16)

    c = jnp.dot(xb, w_ref[:, _C_CKV:_C_CKV + KV_RANK], preferred_element_type=F32)
    c = c * lax.rsqrt(jnp.mean(c * c, axis=-1, keepdims=True) + RMS_EPS) * kvg_ref[...]
    ckv_ref[0, :, 0:KV_RANK] = c.astype(BF16)
    ckv_ref[0, :, KV_RANK:2 * KV_RANK] = jnp.ones((tm, KV_RANK), BF16)

    qi = jnp.dot(xb, w_ref[:, _C_QIDX:_C_QIDX + IDX_HEADS * IDX_HEAD_DIM], preferred_element_type=F32)
    for h in range(IDX_HEADS):
        qidx_ref[0, h] = qi[:, h * IDX_HEAD_DIM:(h + 1) * IDX_HEAD_DIM].astype(BF16)
    kw = jnp.dot(xb, w_ref[:, _C_KIDX:_C_KIDX + 2 * LANES], preferred_element_type=F32)
    kr = kw[:, 0:IDX_HEAD_DIM]
    kidx_ref[0] = _layer_norm(kr, ikg_ref[...], ikb_ref[...]).astype(BF16)
    widx_ref[0] = kw[:, LANES:2 * LANES]


def _inproj(x, w_in_pad, w_pool, pool_scale, kv_g, ik_g, ik_b, w_uk):
    B, L, D = x.shape
    tm = min(PROJ_ROWS, L)
    nl = L // tm
    full2 = lambda b, l: (0, 0)
    full3 = lambda b, l: (0, 0, 0)
    return pl.pallas_call(
        _inproj_kernel,
        grid=(B, nl),
        in_specs=[
            pl.BlockSpec((1, tm, D), lambda b, l: (b, l, 0)),
            pl.BlockSpec(w_in_pad.shape, full2),
            pl.BlockSpec(w_pool.shape, full3),
            pl.BlockSpec(pool_scale.shape, full2),
            pl.BlockSpec(kv_g.shape, full2),
            pl.BlockSpec(ik_g.shape, full2),
            pl.BlockSpec(ik_b.shape, full2),
            pl.BlockSpec(w_uk.shape, full3),
        ],
        out_specs=[
            pl.BlockSpec((1, tm, POOL_WIDTH), lambda b, l: (b, l, 0)),
            pl.BlockSpec((1, ATT_HEADS, tm, KV_RANK), lambda b, l: (b, 0, l, 0)),
            pl.BlockSpec((1, tm, 2 * KV_RANK), lambda b, l: (b, l, 0)),
            pl.BlockSpec((1, IDX_HEADS, tm, IDX_HEAD_DIM), lambda b, l: (b, 0, l, 0)),
            pl.BlockSpec((1, tm, IDX_HEAD_DIM), lambda b, l: (b, l, 0)),
            pl.BlockSpec((1, tm, LANES), lambda b, l: (b, l, 0)),
        ],
        out_shape=[
            jax.ShapeDtypeStruct((B, L, POOL_WIDTH), BF16),
            jax.ShapeDtypeStruct((B, ATT_HEADS, L, KV_RANK), BF16),
            jax.ShapeDtypeStruct((B, L, 2 * KV_RANK), BF16),
            jax.ShapeDtypeStruct((B, IDX_HEADS, L, IDX_HEAD_DIM), BF16),
            jax.ShapeDtypeStruct((B, L, IDX_HEAD_DIM), BF16),
            jax.ShapeDtypeStruct((B, L, LANES), F32),
        ],
        scratch_shapes=[pltpu.VMEM((MAX_POOL_WINDOW + tm, POOL_WIDTH), F32)],
        compiler_params=pltpu.CompilerParams(
            dimension_semantics=("arbitrary", "arbitrary"), vmem_limit_bytes=VMEM_LIMIT),
    )(x, w_in_pad, w_pool, pool_scale, kv_g, ik_g, ik_b, w_uk)


def _sortable(score):
    bits = pltpu.bitcast(score, I32)
    return bits ^ ((bits >> 31) & 0x7FFFFFFF)


def _dsa_kernel(qidx_ref, widx_ref, qlat_ref, kidx_ref, ckv_ref, wuv_ref, o_ref,
                key_ref, keyt_ref, w_ref, wb_ref, s_ref, s2_ref, p_ref, p2_ref, a_ref, a2_ref, m_ref, acc_ref, sel_ref,
                *, top_k, idx_bits):
    qi = pl.program_id(1)
    H = ATT_HEADS
    qb = qidx_ref.shape[2]
    kc = s_ref.shape[1]
    qstart = qi * qb
    n_ch = (qstart + qb + kc - 1) // kc
    idx_scale = (IDX_HEAD_DIM ** -0.5) * (IDX_HEADS ** -0.5)

    qpos = qstart + lax.broadcasted_iota(I32, (qb, 1), 0)
    qpos_t = qstart + lax.broadcasted_iota(I32, (1, qb), 1)
    lane_pos = lax.broadcasted_iota(I32, (1, kc), 1)
    row_pos = lax.broadcasted_iota(I32, (kc, 1), 0)

    for h in range(H):
        wb_ref[h * qb:(h + 1) * qb, :] = jnp.broadcast_to(widx_ref[0, :, h:h + 1] * idx_scale, (qb, LANES))

    def pipelined(produce, consume, buf_a, buf_b):
        produce(0, buf_a)
        n_pairs = (n_ch - 1) // 2

        def pair_body(i, carry):
            c = 2 * i
            consume(c, buf_a)
            produce(c + 1, buf_b)
            consume(c + 1, buf_b)
            produce(c + 2, buf_a)
            return carry

        lax.fori_loop(0, n_pairs, pair_body, 0)
        last = 2 * n_pairs

        @pl.when(n_ch - last == 1)
        def _():
            consume(last, buf_a)

        @pl.when(n_ch - last == 2)
        def _():
            consume(last, buf_a)
            produce(last + 1, buf_b)
            consume(last + 1, buf_b)

    def head_scores(c, buf):
        off = pl.multiple_of(c * kc, kc)
        kk = kidx_ref[0, pl.ds(off, kc), :]
        qs = qidx_ref[0].reshape(H * qb, IDX_HEAD_DIM)
        buf[...] = lax.dot_general(qs, kk, _NT, preferred_element_type=F32)

    half = kc // 2
    f_max = (1 << FIELD_BITS) - 1
    guards = jnp.int32(-(1 << 31) + (1 << FIELD_BITS))

    def field1(k):
        return (k >> (32 - FIELD_BITS)) + (1 << (FIELD_BITS - 1))

    def field2(k):
        return (k >> (32 - 2 * FIELD_BITS)) & f_max

    def pack(fa, fb):
        return lax.shift_left(fb, 16) | fa | guards

    def chunk_keys(c, buf):
        keyt = []
        for j in range(kc // LANES):
            cs = slice(j * LANES, (j + 1) * LANES)
            acc = None
            for h in range(H):
                r = slice(h * qb, (h + 1) * qb)
                term = jnp.maximum(buf[r, cs], 0.0) * wb_ref[r, :]
                acc = term if acc is None else acc + term
            score = jnp.where(c * kc + j * LANES + lane_pos[:, 0:LANES] <= qpos, acc, -jnp.inf)
            key = _sortable(score)
            key_ref[c, :, cs] = key
            keyt.append(jnp.transpose(key))
            keyt_ref[c, cs, :] = keyt[-1]
        kt = jnp.concatenate(keyt, axis=0)
        w_ref[c] = pack(field1(kt[0:half, :]), field1(kt[half:kc, :]))

    pipelined(head_scores, chunk_keys, s_ref, s2_ref)

    def count(indicator):
        def body(c, cnt):
            m = indicator(keyt_ref[c], c * kc + row_pos)
            parts = [m[j * 8:(j + 1) * 8, :] for j in range(kc // 8)]
            while len(parts) > 1:
                parts = [parts[j] + parts[j + 1] for j in range(0, len(parts), 2)]
            return cnt + parts[0]
        cnt = lax.fori_loop(0, n_ch, body, jnp.zeros((8, qb), I32))
        return jnp.sum(cnt, axis=0, keepdims=True)

    units = jnp.int32((1 << 16) + 1)
    n_all = jnp.full((1, qb), n_ch * kc, I32)

    def pack_fields(field):
        def body(c, carry):
            w_ref[c] = pack(field(keyt_ref[c, 0:half, :]), field(keyt_ref[c, half:kc, :]))
            return carry
        lax.fori_loop(0, n_ch, body, 0)

    def count_fields(cand):
        cc = cand | lax.shift_left(cand, 16)

        def body(c, cnt):
            g = lax.shift_right_logical(w_ref[c] - cc, FIELD_BITS) & units
            parts = [g[j * 8:(j + 1) * 8, :] for j in range(half // 8)]
            while len(parts) > 1:
                parts = [parts[j] + parts[j + 1] for j in range(0, len(parts), 2)]
            return cnt + parts[0]
        s = jnp.sum(lax.fori_loop(0, n_ch, body, jnp.zeros((8, qb), I32)), axis=0, keepdims=True)
        return (s & 0xFFFF) + lax.shift_right_logical(s, 16)

    def search_bits(first, last, t, n_t, n_above, counter):
        def body(i, carry):
            t, n_t = carry
            cand = t + lax.shift_left(jnp.int32(1), first - i)
            n = n_above + counter(cand)
            ok = n >= top_k
            return jnp.where(ok, cand, t), jnp.where(ok, n, n_t)
        return lax.fori_loop(0, first - last, body, (t, n_t))

    zero = jnp.zeros((1, qb), I32)
    t1, n_t = search_bits(FIELD_BITS - 1, -1, zero, n_all, zero, count_fields)
    n_above = jnp.where(t1 < f_max, count_fields(jnp.minimum(t1 + 1, f_max)), 0)
    pack_fields(lambda k: jnp.where(field1(k) == t1, field2(k), 0))
    t2, n_t = search_bits(FIELD_BITS - 1, SEARCH_CHECK_BIT - 1, zero, n_t, n_above, count_fields)

    def prefix(t2):
        return lax.shift_left(t1 - (1 << (FIELD_BITS - 1)), 32 - FIELD_BITS) | lax.shift_left(t2, 32 - 2 * FIELD_BITS)

    sel_ref[0:1, :] = prefix(t2)
    sel_ref[2:3, :] = n_t
    sel_ref[3:4, :] = t2

    @pl.when(jnp.max(jnp.where(sel_ref[2:3, :] != top_k, 1, 0)) > 0)
    def _():
        t2, n_t = search_bits(SEARCH_CHECK_BIT - 1, -1, sel_ref[3:4, :], sel_ref[2:3, :], n_above, count_fields)
        t, n_t = search_bits(32 - 2 * FIELD_BITS - 1, -1, prefix(t2), n_t, zero,
                             lambda cand: count(lambda k, _: jnp.where(k >= cand, 1, 0)))
        sel_ref[0:1, :] = t
        sel_ref[2:3, :] = n_t

    thr_t = sel_ref[0:1, :]
    n_ge = sel_ref[2:3, :]

    surplus = jnp.where(thr_t > NEG_INF_KEY, jnp.where(n_ge > top_k, 1, 0), 0)
    sel_ref[1:2, :] = qpos_t

    @pl.when(jnp.max(surplus) > 0)
    def _():
        need = top_k - count(lambda k, _: jnp.where(k > thr_t, 1, 0))

        def tie_body(i, m):
            cand = m + lax.shift_left(jnp.int32(1), idx_bits - 1 - i)
            n_before = count(lambda k, p: jnp.where(k == thr_t, jnp.where(p < cand, 1, 0), 0))
            return jnp.where(n_before < need, cand, m)
        last = lax.fori_loop(0, idx_bits, tie_body, jnp.zeros((1, qb), I32))
        sel_ref[1:2, :] = jnp.where(surplus > 0, jnp.minimum(last, qpos_t), qpos_t)

    thr = jnp.transpose(jnp.broadcast_to(sel_ref[0:1, :], (qb, qb)))[:, 0:1]
    tie_last = jnp.transpose(jnp.broadcast_to(sel_ref[1:2, :], (qb, qb)))[:, 0:1]

    def chunk_bias(c):
        key = key_ref[c]
        kpos = c * kc + lane_pos
        tie_bias = jnp.where(key == thr, jnp.where(kpos <= tie_last, 0.0, NEG_BIG), NEG_BIG)
        return jnp.where(key > thr, 0.0, tie_bias)

    n_lt = kc // LANES

    def head_logits(h, ck):
        return lax.dot_general(qlat_ref[0, h], ck, _NT, preferred_element_type=F32)

    m_ref[...] = jnp.full(m_ref.shape, NEG_BIG, F32)
    acc_ref[...] = jnp.zeros(acc_ref.shape, F32)

    def numerators(c, bufs):
        pbuf, abuf = bufs
        off = pl.multiple_of(c * kc, kc)
        ck = ckv_ref[0, pl.ds(off, kc), 0:KV_RANK]
        bias = chunk_bias(c)
        for h in range(H):
            r = slice(h * qb, (h + 1) * qb)
            lg = head_logits(h, ck)
            z = [lg[:, j * LANES:(j + 1) * LANES] + bias[:, j * LANES:(j + 1) * LANES] for j in range(n_lt)]
            zm = z[0]
            for j in range(1, n_lt):
                zm = jnp.maximum(zm, z[j])
            m_old = m_ref[r, :]
            m_new = jnp.maximum(m_old, jnp.max(zm, axis=1, keepdims=True))
            m_ref[r, :] = m_new
            abuf[r, :] = jnp.exp2(m_old - m_new)
            for j in range(n_lt):
                pbuf[r, j * LANES:(j + 1) * LANES] = jnp.exp2(z[j] - m_new).astype(BF16)

    def accumulate(c, bufs):
        pbuf, abuf = bufs
        off = pl.multiple_of(c * kc, kc)
        ckx = ckv_ref[0, pl.ds(off, kc), :]
        pv = jnp.dot(pbuf[...], ckx, preferred_element_type=F32)
        a = abuf[...]
        acc_ref[:, 0:KV_RANK] = a * acc_ref[:, 0:KV_RANK] + pv[:, 0:KV_RANK]
        acc_ref[:, KV_RANK:2 * KV_RANK] = a * acc_ref[:, KV_RANK:2 * KV_RANK] + pv[:, KV_RANK:2 * KV_RANK]

    pipelined(numerators, accumulate, (p_ref, a_ref), (p2_ref, a2_ref))

    o_lat = (acc_ref[:, 0:KV_RANK] / acc_ref[:, KV_RANK:2 * KV_RANK]).astype(BF16)
    for h in range(0, H, 2):
        t = jnp.dot(o_lat[h * qb:(h + 1) * qb, :], wuv_ref[h], preferred_element_type=F32)
        t = t + jnp.dot(o_lat[(h + 1) * qb:(h + 2) * qb, :], wuv_ref[h + 1], preferred_element_type=F32)
        o_ref[0, :, h * ATT_HEAD_DIM:(h + 2) * ATT_HEAD_DIM] = t.astype(BF16)


def _dsa(qidx, widx, qlat, kidx, ckv, wuv_pad):
    B, H, L, _ = qidx.shape
    qb = min(DSA_QUERIES, L)
    kc = min(DSA_KEYS, L)
    top_k = min(TOPK_MAX, L // 4)
    idx_bits = max(1, (L - 1).bit_length())
    assert L % qb == 0 and L % kc == 0 and qb % LANES == 0 and kc % (2 * LANES) == 0
    assert top_k < kc and L < (1 << FIELD_BITS)
    kern = functools.partial(_dsa_kernel, top_k=top_k, idx_bits=idx_bits)
    return pl.pallas_call(
        kern,
        grid=(B, L // qb),
        in_specs=[
            pl.BlockSpec((1, H, qb, IDX_HEAD_DIM), lambda b, q: (b, 0, q, 0)),
            pl.BlockSpec((1, qb, LANES), lambda b, q: (b, q, 0)),
            pl.BlockSpec((1, H, qb, KV_RANK), lambda b, q: (b, 0, q, 0)),
            pl.BlockSpec((1, L, IDX_HEAD_DIM), lambda b, q: (b, 0, 0)),
            pl.BlockSpec((1, L, 2 * KV_RANK), lambda b, q: (b, 0, 0)),
            pl.BlockSpec(wuv_pad.shape, lambda b, q: (0, 0, 0)),
        ],
        out_specs=pl.BlockSpec((1, qb, ATT_WIDTH), lambda b, q: (b, q, 0)),
        out_shape=jax.ShapeDtypeStruct((B, L, ATT_WIDTH), BF16),
        scratch_shapes=[
            pltpu.VMEM((L // kc, qb, kc), I32),
            pltpu.VMEM((L // kc, kc, qb), I32),
            pltpu.VMEM((L // kc, kc // 2, qb), I32),
            pltpu.VMEM((H * qb, LANES), F32),
            pltpu.VMEM((H * qb, kc), F32),
            pltpu.VMEM((H * qb, kc), F32),
            pltpu.VMEM((H * qb, kc), BF16),
            pltpu.VMEM((H * qb, kc), BF16),
            pltpu.VMEM((H * qb, LANES), F32),
            pltpu.VMEM((H * qb, LANES), F32),
            pltpu.VMEM((H * qb, LANES), F32),
            pltpu.VMEM((H * qb, 2 * KV_RANK), F32),
            pltpu.VMEM((8, qb), I32),
        ],
        compiler_params=pltpu.CompilerParams(
            dimension_semantics=("arbitrary", "arbitrary"), vmem_limit_bytes=VMEM_LIMIT),
    )(qidx, widx, qlat, kidx, ckv, wuv_pad)


def _memkv_kernel(mem_ref, w_ref, k_ref, v_ref):
    d = k_ref.shape[2]
    kv = jnp.dot(mem_ref[0].astype(BF16), w_ref[...], preferred_element_type=F32)
    k_ref[0] = kv[:, 0:d].astype(BF16)
    v_ref[0] = kv[:, d:2 * d].astype(BF16)


def _memkv(mem, w_mkv):
    B, M, D = mem.shape
    return pl.pallas_call(
        _memkv_kernel,
        grid=(B,),
        in_specs=[pl.BlockSpec((1, M, D), lambda b: (b, 0, 0)),
                  pl.BlockSpec(w_mkv.shape, lambda b: (0, 0))],
        out_specs=[pl.BlockSpec((1, M, D), lambda b: (b, 0, 0)),
                   pl.BlockSpec((1, M, D), lambda b: (b, 0, 0))],
        out_shape=[jax.ShapeDtypeStruct((B, M, D), BF16), jax.ShapeDtypeStruct((B, M, D), BF16)],
        compiler_params=pltpu.CompilerParams(
            dimension_semantics=("arbitrary",), vmem_limit_bytes=VMEM_LIMIT),
    )(mem, w_mkv)


def _split3(v):
    hi = v.astype(BF16)
    r1 = v - hi.astype(F32)
    mid = r1.astype(BF16)
    lo = (r1 - mid.astype(F32)).astype(BF16)
    return hi, mid, lo


def _mix_kernel(x_ref, yp_ref, ya_ref, wo_ref, g1_ref, b1_ref, km_ref, vm_ref, wq_ref, wmo_ref,
                g2_ref, b2_ref, wr_ref, br_ref, x2_ref, tope_ref, gate_ref):
    tm, d = x_ref.shape
    hd = d // MEM_HEADS
    pw = yp_ref.shape[1]
    ws = _split3(wr_ref[...])

    def stage_mix(rs):
        mix = jnp.dot(yp_ref[rs, :], wo_ref[0:pw, :], preferred_element_type=F32)
        mix = mix + jnp.dot(ya_ref[rs, :], wo_ref[pw:, :], preferred_element_type=F32)
        return _layer_norm(DN_ALPHA * x_ref[rs, :] + mix, g1_ref[...], b1_ref[...])

    def stage_mem(x1):
        q = jnp.dot(x1.astype(BF16), wq_ref[...], preferred_element_type=F32).astype(BF16)
        scale = hd ** -0.5
        att = None
        for h in range(MEM_HEADS):
            c = slice(h * hd, (h + 1) * hd)
            lg = lax.dot_general(q[:, c], km_ref[0, :, c], _NT, preferred_element_type=F32) * scale
            p = jnp.exp(lg - jnp.max(lg, axis=-1, keepdims=True))
            p = p / jnp.sum(p, axis=-1, keepdims=True)
            oh = jnp.dot(p.astype(BF16), vm_ref[0, :, c], preferred_element_type=F32).astype(BF16)
            t = jnp.dot(oh, wmo_ref[c, :], preferred_element_type=F32)
            att = t if att is None else att + t
        return att

    def stage_route(rs, x1, att):
        n = rs.stop - rs.start
        x2 = _layer_norm(DN_ALPHA * x1 + att, g2_ref[...], b2_ref[...])
        x2_ref[rs, :] = x2

        xs = _split3(x2)
        lt = None
        for i, j in ((0, 0), (0, 1), (1, 0)):
            t = lax.dot_general(ws[j], xs[i], _NT, preferred_element_type=F32)
            lt = t if lt is None else lt + t
        lt = lt + br_ref[...]
        n_e = lt.shape[0]
        eidx = lax.broadcasted_iota(I32, lt.shape, 0)
        vals, idxs = [], []
        for _ in range(TOP_K_EXPERTS):
            mx = jnp.max(lt, axis=0, keepdims=True)
            ix = jnp.min(jnp.where(lt == mx, eidx, n_e), axis=0, keepdims=True)
            vals.append(mx)
            idxs.append(ix)
            lt = jnp.where(eidx == ix, -jnp.inf, lt)
        tope_ref[:, rs] = jnp.concatenate(idxs, axis=0)
        ex = [jnp.exp(v - vals[0]) for v in vals]
        den = ex[0]
        for e_ in ex[1:]:
            den = den + e_
        gates = jnp.concatenate([e_ / den for e_ in ex] + [jnp.zeros((LANES - TOP_K_EXPERTS, n), F32)], axis=0)
        gate_ref[rs, :] = jnp.transpose(gates)

    all_rows = slice(0, tm)
    x1 = stage_mix(all_rows)
    stage_route(all_rows, x1, stage_mem(x1))


def _mix(x2d, ypool, yatt, w_o, g1, b1, k_mem, v_mem, w_mq, w_mo, g2, b2, w_rt, b_r, B, L):
    T, D = x2d.shape
    tm = min(MIX_ROWS, L)
    nl = L // tm
    M = k_mem.shape[1]
    row = lambda i: (i, 0)
    full = lambda i: (0, 0)
    return pl.pallas_call(
        _mix_kernel,
        grid=(T // tm,),
        in_specs=[
            pl.BlockSpec((tm, D), row),
            pl.BlockSpec((tm, ypool.shape[1]), row),
            pl.BlockSpec((tm, yatt.shape[1]), row),
            pl.BlockSpec(w_o.shape, full),
            pl.BlockSpec(g1.shape, full),
            pl.BlockSpec(b1.shape, full),
            pl.BlockSpec((1, M, D), lambda i: (i // nl, 0, 0)),
            pl.BlockSpec((1, M, D), lambda i: (i // nl, 0, 0)),
            pl.BlockSpec(w_mq.shape, full),
            pl.BlockSpec(w_mo.shape, full),
            pl.BlockSpec(g2.shape, full),
            pl.BlockSpec(b2.shape, full),
            pl.BlockSpec(w_rt.shape, full),
            pl.BlockSpec(b_r.shape, full),
        ],
        out_specs=[
            pl.BlockSpec((tm, D), row),
            pl.BlockSpec((TOP_K_EXPERTS, tm), lambda i: (0, i)),
            pl.BlockSpec((tm, LANES), row),
        ],
        out_shape=[
            jax.ShapeDtypeStruct((T, D), F32),
            jax.ShapeDtypeStruct((TOP_K_EXPERTS, T), I32),
            jax.ShapeDtypeStruct((T, LANES), F32),
        ],
        compiler_params=pltpu.CompilerParams(
            dimension_semantics=("arbitrary",), vmem_limit_bytes=VMEM_LIMIT),
    )(x2d, ypool, yatt, w_o, g1, b1, k_mem, v_mem, w_mq, w_mo, g2, b2, w_rt, b_r)


def _rank_kernel(tope_ref, rank_ref, cnt_ref, carry_ref):
    i = pl.program_id(0)
    tr = tope_ref.shape[1]

    @pl.when(i == 0)
    def _():
        carry_ref[...] = jnp.zeros(carry_ref.shape, F32)

    eidx = lax.broadcasted_iota(I32, (N_EXPERTS, tr), 0)
    onehot = jnp.zeros((N_EXPERTS, tr), F32)
    for k in range(TOP_K_EXPERTS):
        onehot = onehot + jnp.where(eidx == tope_ref[k:k + 1, :], 1.0, 0.0)
    before = jnp.where(lax.broadcasted_iota(I32, (tr, tr), 0) < lax.broadcasted_iota(I32, (tr, tr), 1), 1.0, 0.0)
    excl = jnp.dot(onehot.astype(BF16), before.astype(BF16), preferred_element_type=F32)
    rank_full = excl + carry_ref[:, 0:1]
    rows = []
    for k in range(TOP_K_EXPERTS):
        rows.append(jnp.sum(jnp.where(eidx == tope_ref[k:k + 1, :], rank_full, 0.0), axis=0, keepdims=True))
    rank_ref[...] = jnp.concatenate(rows, axis=0).astype(I32)
    carry_ref[...] = carry_ref[...] + jnp.sum(onehot, axis=1, keepdims=True)
    cnt_ref[...] = carry_ref[...].astype(I32)


def _rank(tope):
    K, T = tope.shape
    tr = min(RANK_TOKENS, T)
    return pl.pallas_call(
        _rank_kernel,
        grid=(T // tr,),
        in_specs=[pl.BlockSpec((K, tr), lambda i: (0, i))],
        out_specs=[pl.BlockSpec((K, tr), lambda i: (0, i)),
                   pl.BlockSpec((N_EXPERTS, LANES), lambda i: (0, 0))],
        out_shape=[jax.ShapeDtypeStruct((K, T), I32), jax.ShapeDtypeStruct((N_EXPERTS, LANES), I32)],
        scratch_shapes=[pltpu.VMEM((N_EXPERTS, LANES), F32)],
        compiler_params=pltpu.CompilerParams(dimension_semantics=("arbitrary",)),
    )(tope)


def _dest_kernel(start_ref, tope_ref, rank_ref, dest_ref):
    tope = tope_ref[...]
    base = jnp.zeros(tope.shape, I32)
    for e in range(N_EXPERTS):
        base = jnp.where(tope == e, start_ref[e], base)
    dest_ref[...] = base + rank_ref[...]


def _dest(pad_start, tope, rank):
    K, T = tope.shape
    tt = min(DEST_TOKENS, T)
    blk = pl.BlockSpec((K, tt), lambda i: (0, i))
    return pl.pallas_call(
        _dest_kernel,
        grid=(T // tt,),
        in_specs=[pl.BlockSpec(memory_space=pltpu.SMEM), blk, blk],
        out_specs=blk,
        out_shape=jax.ShapeDtypeStruct((K, T), I32),
        compiler_params=pltpu.CompilerParams(dimension_semantics=("arbitrary",)),
    )(pad_start, tope, rank)


def _dispatch_kernel(dest_ref, start_ref, cnt_ref, padded_ref, x_ref, xs_hbm, zrow, sem, zsem):
    i = pl.program_id(0)
    td = dest_ref.shape[1]

    def row_copy(j, dst):
        return pltpu.make_async_copy(x_ref.at[pl.ds(j, 1), :], xs_hbm.at[pl.ds(dst, 1), :], sem)

    def zero_copy(dst):
        return pltpu.make_async_copy(zrow, xs_hbm.at[pl.ds(dst, 1), :], zsem)

    @pl.when(i == 0)
    def _():
        zrow[...] = jnp.zeros(zrow.shape, F32)
        for e in range(N_EXPERTS):
            first = start_ref[e] + cnt_ref[e]
            n_pad = padded_ref[e] - cnt_ref[e]

            def zstart(r, c):
                zero_copy(first + r).start()
                return c
            lax.fori_loop(0, n_pad, zstart, 0)

            def zwait(r, c):
                zero_copy(first + r).wait()
                return c
            lax.fori_loop(0, n_pad, zwait, 0)

    def issue(jj, c):
        for u in range(ISSUE_UNROLL):
            j = jj * ISSUE_UNROLL + u
            for k in range(TOP_K_EXPERTS):
                row_copy(j, dest_ref[k, j]).start(priority=k % 2)
        return c
    lax.fori_loop(0, td // ISSUE_UNROLL, issue, 0)

    def drain(jj, c):
        for _ in range(ISSUE_UNROLL * TOP_K_EXPERTS):
            row_copy(0, 0).wait()
        return c
    lax.fori_loop(0, td // ISSUE_UNROLL, drain, 0)


def _dispatch(dest, pad_start, counts, padded, x2, n_rows):
    K, T = dest.shape
    D = x2.shape[1]
    td = min(DISPATCH_TOKENS, T)
    smem_tok = pl.BlockSpec((K, td), lambda i: (0, i), memory_space=pltpu.SMEM)
    smem_full = pl.BlockSpec(memory_space=pltpu.SMEM)
    return pl.pallas_call(
        _dispatch_kernel,
        grid=(T // td,),
        in_specs=[smem_tok, smem_full, smem_full, smem_full,
                  pl.BlockSpec((td, D), lambda i: (i, 0))],
        out_specs=pl.BlockSpec(memory_space=pl.ANY),
        out_shape=jax.ShapeDtypeStruct((n_rows, D), F32),
        scratch_shapes=[pltpu.VMEM((1, D), F32), pltpu.SemaphoreType.DMA(()), pltpu.SemaphoreType.DMA(())],
        compiler_params=pltpu.CompilerParams(
            dimension_semantics=("arbitrary",), vmem_limit_bytes=VMEM_LIMIT),
    )(dest, pad_start, counts, padded, x2)


def _expert_kernel(bexp_ref, nused_ref, xs_ref, wgu_ref, bgu_ref, wd_ref, bd_ref, y_ref, wgu_bf, wd_bf):
    j = pl.program_id(0)
    d_ff = wd_ref.shape[1]

    @pl.when(j < nused_ref[0])
    def _():
        prev = bexp_ref[jnp.maximum(j - 1, 0)]

        @pl.when(jnp.logical_or(j == 0, bexp_ref[j] != prev))
        def _():
            wgu_bf[...] = wgu_ref[0].astype(BF16)
            wd_bf[...] = wd_ref[0].astype(BF16)

        xb = xs_ref[...].astype(BF16)
        gu = jnp.dot(xb, wgu_bf[...], preferred_element_type=F32) + bgu_ref[0]
        g = jnp.minimum(gu[:, 0:d_ff], SWIGLU_LIMIT)
        u = jnp.clip(gu[:, d_ff:2 * d_ff], -SWIGLU_LIMIT, SWIGLU_LIMIT)
        act = (u + 1.0) * g * (1.0 / (1.0 + jnp.exp(-SWIGLU_ALPHA * g)))
        y_ref[...] = jnp.dot(act.astype(BF16), wd_bf[...], preferred_element_type=F32) + bd_ref[0]


def _experts(block_exp, n_used, xs, w_gate_up, b_gate_up, w_down, b_down):
    P, D = xs.shape
    blk = EXPERT_ROWS
    nblk = P // blk
    E, _, F2 = w_gate_up.shape
    d_ff = w_down.shape[1]
    row = lambda j, be, nu: (jnp.minimum(j, nu[0] - 1), 0)
    exp3 = lambda j, be, nu: (be[j], 0, 0)
    grid_spec = pltpu.PrefetchScalarGridSpec(
        num_scalar_prefetch=2,
        grid=(nblk,),
        in_specs=[
            pl.BlockSpec((blk, D), row),
            pl.BlockSpec((1, D, F2), exp3),
            pl.BlockSpec((1, 1, F2), exp3),
            pl.BlockSpec((1, d_ff, D), exp3),
            pl.BlockSpec((1, 1, D), exp3),
        ],
        out_specs=pl.BlockSpec((blk, D), row),
        scratch_shapes=[pltpu.VMEM((D, F2), BF16), pltpu.VMEM((d_ff, D), BF16)],
    )
    return pl.pallas_call(
        _expert_kernel,
        grid_spec=grid_spec,
        out_shape=jax.ShapeDtypeStruct((P, D), F32),
        compiler_params=pltpu.CompilerParams(
            dimension_semantics=("arbitrary",), vmem_limit_bytes=VMEM_LIMIT),
    )(block_exp, n_used, xs, w_gate_up, b_gate_up.reshape(E, 1, F2), w_down, b_down.reshape(E, 1, D))


def _combine_kernel(dest_ref, dest_next_ref, x2_ref, gate_ref, g3_ref, b3_ref, y_hbm, o_ref, ybuf, sem):
    i = pl.program_id(0)
    n = pl.num_programs(0)
    tc = x2_ref.shape[0]
    slot = i % 2

    def row_copy(src_ref, s, j, k):
        return pltpu.make_async_copy(y_hbm.at[pl.ds(src_ref[k, j], 1), :], ybuf.at[s, k, pl.ds(j, 1), :], sem.at[s])

    def gather(src_ref, s):
        def issue(jj, c):
            for u in range(ISSUE_UNROLL):
                for k in range(TOP_K_EXPERTS):
                    row_copy(src_ref, s, jj * ISSUE_UNROLL + u, k).start(priority=k % 2)
            return c
        lax.fori_loop(0, tc // ISSUE_UNROLL, issue, 0)

    @pl.when(i == 0)
    def _():
        gather(dest_ref, 0)

    @pl.when(i + 1 < n)
    def _():
        gather(dest_next_ref, 1 - slot)

    def drain(jj, c):
        for _ in range(ISSUE_UNROLL * TOP_K_EXPERTS):
            pltpu.make_async_copy(y_hbm.at[pl.ds(0, 1), :], ybuf.at[slot, 0, pl.ds(0, 1), :], sem.at[slot]).wait()
        return c
    lax.fori_loop(0, tc // ISSUE_UNROLL, drain, 0)

    moe = None
    for k in range(TOP_K_EXPERTS):
        t = ybuf[slot, k] * gate_ref[:, k:k + 1]
        moe = t if moe is None else moe + t
    o_ref[...] = _layer_norm(DN_ALPHA * x2_ref[...] + moe, g3_ref[...], b3_ref[...])


def _combine(dest, x2, gates, g3, b3, y):
    K, T = dest.shape
    D = x2.shape[1]
    tc = min(COMBINE_TOKENS, T)
    n = T // tc
    row = lambda i: (i, 0)
    full = lambda i: (0, 0)
    return pl.pallas_call(
        _combine_kernel,
        grid=(n,),
        in_specs=[pl.BlockSpec((K, tc), lambda i: (0, i), memory_space=pltpu.SMEM),
                  pl.BlockSpec((K, tc), lambda i: (0, jnp.minimum(i + 1, n - 1)), memory_space=pltpu.SMEM),
                  pl.BlockSpec((tc, D), row), pl.BlockSpec((tc, LANES), row),
                  pl.BlockSpec(g3.shape, full), pl.BlockSpec(b3.shape, full),
                  pl.BlockSpec(memory_space=pl.ANY)],
        out_specs=pl.BlockSpec((tc, D), row),
        out_shape=jax.ShapeDtypeStruct((T, D), F32),
        scratch_shapes=[pltpu.VMEM((2, K, tc, D), F32), pltpu.SemaphoreType.DMA((2,))],
        compiler_params=pltpu.CompilerParams(
            dimension_semantics=("arbitrary",), vmem_limit_bytes=VMEM_LIMIT),
    )(dest, dest, x2, gates, g3, b3, y)


def _pad_cols(w, width):
    return jnp.pad(w, ((0, 0), (0, width - w.shape[1])))


def _layer(x, mem, w_in, w_pool, pool_scale, ik_g, ik_b, kv_g, w_uk, w_uv, w_o, ln1_g, ln1_b,
           w_mq, w_mkv, w_mo, ln2_g, ln2_b, w_router, b_router, w_gate_up, b_gate_up, w_down, b_down,
           ln3_g, ln3_b):
    B, L, D = x.shape
    T = B * L
    row = lambda v: v.reshape(1, -1)

    o = 0
    pieces = []
    for width in (POOL_WIDTH, ATT_WIDTH, KV_RANK, IDX_HEADS * IDX_HEAD_DIM, IDX_HEAD_DIM, IDX_HEADS):
        pieces.append(w_in[:, o:o + width])
        o += width
    pieces[4] = _pad_cols(pieces[4], LANES)
    pieces[5] = _pad_cols(pieces[5], LANES)
    w_in_pad = jnp.concatenate(pieces, axis=1).astype(BF16)
    wuv_pad = jnp.zeros((ATT_HEADS, KV_RANK, 2 * ATT_HEAD_DIM), F32)
    for h in range(ATT_HEADS):
        wuv_pad = wuv_pad.at[h, :, (h % 2) * ATT_HEAD_DIM:(h % 2 + 1) * ATT_HEAD_DIM].set(w_uv[h])

    ypool, qlat, ckv, qidx, kidx, widx = _inproj(
        x, w_in_pad, w_pool.astype(BF16), row(pool_scale), row(kv_g), row(ik_g), row(ik_b), w_uk.astype(BF16))
    yatt = _dsa(qidx, widx, qlat, kidx, ckv, wuv_pad.astype(BF16))
    k_mem, v_mem = _memkv(mem, w_mkv.astype(BF16))
    x2, tope, gates = _mix(
        x.reshape(T, D), ypool.reshape(T, -1), yatt.reshape(T, -1), w_o.astype(BF16), row(ln1_g), row(ln1_b),
        k_mem, v_mem, w_mq.astype(BF16), w_mo.astype(BF16), row(ln2_g), row(ln2_b),
        jnp.transpose(w_router), b_router.reshape(-1, 1), B, L)

    rank, cnt = _rank(tope)
    counts = cnt[:, 0]
    blk = EXPERT_ROWS
    padded = (counts + blk - 1) // blk * blk
    pad_end = jnp.cumsum(padded)
    pad_start = pad_end - padded
    n_rows = T * TOP_K_EXPERTS + N_EXPERTS * blk
    nblk = n_rows // blk
    n_used = (pad_end[-1] // blk).astype(I32)
    blk_first = jnp.minimum(jnp.arange(nblk, dtype=I32), n_used - 1) * blk
    n_ended = jnp.sum((pad_end[None, :] <= blk_first[:, None]).astype(I32), axis=1)
    block_exp = jnp.minimum(n_ended, N_EXPERTS - 1).astype(I32)

    pad_start = pad_start.astype(I32)
    dest = _dest(pad_start, tope, rank)
    xs = _dispatch(dest, pad_start, counts, padded.astype(I32), x2, n_rows)
    y = _experts(block_exp, n_used.reshape(1), xs, w_gate_up, b_gate_up, w_down, b_down)
    out = _combine(dest, x2, gates, row(ln3_g), row(ln3_b), y)
    return out.reshape(B, L, D)


def kernel(x, mem, w_in, w_pool, pool_scale, idx_k_norm_g, idx_k_norm_b, kv_norm_g, w_uk, w_uv, w_o, ln1_g, ln1_b, w_mq, w_mkv, w_mo, ln2_g, ln2_b, w_router, b_router, w_gate_up, b_gate_up, w_down, b_down, ln3_g, ln3_b):
    assert w_in.shape[0] == DEPTH
    return _layer(x, mem, w_in[0], w_pool[0], pool_scale[0], idx_k_norm_g[0], idx_k_norm_b[0], kv_norm_g[0],
                  w_uk[0], w_uv[0], w_o[0], ln1_g[0], ln1_b[0], w_mq[0], w_mkv[0], w_mo[0], ln2_g[0], ln2_b[0],
                  w_router[0], b_router[0], w_gate_up[0], b_gate_up[0], w_down[0], b_down[0], ln3_g[0], ln3_b[0])
```
